```python
import math
import jax, jax.numpy as jnp
from jax import lax
import numpy as np

D_MODEL = 1024
BATCH = 8
SEQ = 4096
DEPTH = 1

CHUNK = 64
PLE_DIM = 256
EPS = 1e-6

RW_HEADS = 8
RW_HEAD_DIM = 64
RW_DIM = RW_HEADS * RW_HEAD_DIM
RW_LORA_W = 64
RW_LORA_A = 64
RW_LORA_G = 128
RW_GN_EPS = 64e-5

MLA_HEADS = 8
QK_NOPE = 64
QK_ROPE = 32
V_HEAD = 64
Q_LORA = 384
KV_LORA = 256
MLA_DIM = MLA_HEADS * V_HEAD
ROPE_THETA = 10000.0
Q_BLOCK = 128
NEG_INF = -1e30

PEER_HEADS = 8
N_KEYS = 128
N_EXPERTS = N_KEYS * N_KEYS
PEER_TOPK = 16
D_QUERY = 256
HALF_Q = D_QUERY // 2
PEER_TOKEN_BLOCK = 128

RW_COLS = 3 * RW_DIM + RW_LORA_W + RW_LORA_A + RW_LORA_G
MLA_COLS = Q_LORA + KV_LORA + QK_ROPE
GATE_COLS = 2 * D_MODEL
IN_COLS = RW_COLS + MLA_COLS + GATE_COLS

kernel_name = 'hybrid_rwkv7_mla_peer_block'


def rms_norm(x, gain, eps=EPS):
    xf = x.astype(jnp.float32)
    y = xf * lax.rsqrt(jnp.mean(xf * xf, axis=-1, keepdims=True) + eps)
    return (y * gain.astype(jnp.float32)).astype(x.dtype)


def rwkv7_scan(r, decay, k, v, a, b):
    def step(state, inp):
        r_t, w_t, k_t, v_t, a_t, b_t = inp
        sa = jnp.einsum('bhvk,bhk->bhv', state, a_t)
        state = (state * w_t[:, :, None, :] + sa[..., None] * b_t[:, :, None, :]
                 + v_t[..., None] * k_t[:, :, None, :])
        return state, jnp.einsum('bhvk,bhk->bhv', state, r_t)
    xs = tuple(jnp.moveaxis(t.astype(jnp.float32), 1, 0) for t in (r, decay, k, v, a, b))
    bsz, _, heads, n = r.shape
    s0 = jnp.zeros((bsz, heads, n, n), jnp.float32)
    _, ys = lax.scan(step, s0, xs)
    return jnp.moveaxis(ys, 0, 1)


def rwkv7_branch(z, mu, w0, w2, a0, a2, g2, k_k, k_a, r_k, gn_w, gn_b, w_o):
    bsz, seq = z.shape[:2]
    z_prev = jnp.pad(z, ((0, 0), (1, 0), (0, 0)))[:, :-1]
    z = z + mu * (z_prev - z)
    o1, o2, o3 = RW_DIM, 2 * RW_DIM, 3 * RW_DIM
    o4, o5 = o3 + RW_LORA_W, o3 + RW_LORA_W + RW_LORA_A
    r, k, v = z[..., :o1], z[..., o1:o2], z[..., o2:o3]
    zw, za, zg = z[..., o3:o4], z[..., o4:o5], z[..., o5:]
    w = -jax.nn.softplus(-(w0 + jnp.tanh(zw) @ w2)) - 0.5
    iclr = jax.nn.sigmoid(a0 + za @ a2)
    g = jax.nn.sigmoid(zg) @ g2
    hd = lambda t: t.reshape(bsz, seq, RW_HEADS, RW_HEAD_DIM)
    kk = hd(k * k_k).astype(jnp.float32)
    kk = kk / jnp.maximum(jnp.sqrt(jnp.sum(kk * kk, axis=-1, keepdims=True)), 1e-12)
    k = k * (1 + (iclr - 1) * k_a)
    decay = jnp.exp(-jnp.exp(w.astype(jnp.float32)))
    r_h, k_h, v_h, iclr_h = hd(r), hd(k), hd(v), hd(iclr).astype(jnp.float32)
    y = rwkv7_scan(r_h, hd(decay), k_h, v_h, -kk, kk * iclr_h)
    mean = jnp.mean(y, axis=-1, keepdims=True)
    var = jnp.mean(jnp.square(y - mean), axis=-1, keepdims=True)
    y = ((y - mean) * lax.rsqrt(var + RW_GN_EPS)).reshape(bsz, seq, RW_DIM) * gn_w + gn_b
    bonus = jnp.sum(r_h * k_h * r_k, axis=-1, keepdims=True) * v_h
    y = y + bonus.reshape(bsz, seq, RW_DIM)
    return (y * g).astype(z.dtype) @ w_o


def rope_tables(positions):
    inv_freq = ROPE_THETA ** (-jnp.arange(0, QK_ROPE, 2, dtype=jnp.float32) / QK_ROPE)
    ang = positions.astype(jnp.float32)[..., None] * inv_freq
    return jnp.cos(ang), jnp.sin(ang)


def apply_rope(x, cos, sin):
    half = x.shape[-1] // 2
    xf = x.astype(jnp.float32)
    x1, x2 = xf[..., :half], xf[..., half:]
    return jnp.concatenate([x1 * cos - x2 * sin, x2 * cos + x1 * sin], axis=-1).astype(x.dtype)


def chunk_causal_attention(q_nope, q_rope, k_nope, k_rope, v):
    bsz, seq, heads, _ = q_nope.shape
    n_blk = seq // Q_BLOCK
    scale = 1.0 / math.sqrt(QK_NOPE + QK_ROPE)
    key_chunk = jnp.arange(seq) // CHUNK

    def to_blocks(t):
        return jnp.moveaxis(t.reshape(bsz, n_blk, Q_BLOCK, *t.shape[2:]), 1, 0)

    def one_block(args):
        qn, qr, blk = args
        s = (jnp.einsum('bqhd,bkhd->bhqk', qn, k_nope)
             + jnp.einsum('bqhr,bkr->bhqk', qr, k_rope)).astype(jnp.float32) * scale
        q_chunk = (blk * Q_BLOCK + jnp.arange(Q_BLOCK)) // CHUNK
        mask = key_chunk[None, :] <= q_chunk[:, None]
        s = jnp.where(mask, s, NEG_INF)
        prob = jax.nn.softmax(s, axis=-1).astype(v.dtype)
        return jnp.einsum('bhqk,bkhd->bqhd', prob, v)

    out = lax.map(one_block, (to_blocks(q_nope), to_blocks(q_rope), jnp.arange(n_blk)))
    return jnp.moveaxis(out, 0, 1).reshape(bsz, seq, heads, V_HEAD)


def mla_branch(z, positions, q_norm, w_uq, kv_norm, w_ukv, w_o):
    bsz, seq = z.shape[:2]
    c_q = rms_norm(z[..., :Q_LORA], q_norm)
    c_kv = rms_norm(z[..., Q_LORA:Q_LORA + KV_LORA], kv_norm)
    k_rope = z[..., Q_LORA + KV_LORA:]
    q = (c_q @ w_uq).reshape(bsz, seq, MLA_HEADS, QK_NOPE + QK_ROPE)
    kv = (c_kv @ w_ukv).reshape(bsz, seq, MLA_HEADS, QK_NOPE + V_HEAD)
    q_nope, q_rope = q[..., :QK_NOPE], q[..., QK_NOPE:]
    k_nope, v = kv[..., :QK_NOPE], kv[..., QK_NOPE:]
    cos, sin = rope_tables(positions)
    q_rope = apply_rope(q_rope, cos[:, :, None, :], sin[:, :, None, :])
    k_rope = apply_rope(k_rope, cos, sin)
    o = chunk_causal_attention(q_nope, q_rope, k_nope, k_rope, v)
    return o.reshape(bsz, seq, MLA_DIM) @ w_o


def peer_ffn(h, w_q, sub_keys, u_tab, v_tab):
    bsz, seq, d = h.shape
    hb = h.reshape(bsz * seq // PEER_TOKEN_BLOCK, PEER_TOKEN_BLOCK, d)

    def one_block(xb):
        t = xb.shape[0]
        q = (xb @ w_q).reshape(t, PEER_HEADS, 2, HALF_Q)
        s = jnp.einsum('thcd,hcnd->thcn', q, sub_keys).astype(jnp.float32)
        top_s, top_i = lax.top_k(s, PEER_TOPK)
        cand_s = (top_s[:, :, 0, :, None] + top_s[:, :, 1, None, :]).reshape(t, PEER_HEADS, PEER_TOPK * PEER_TOPK)
        cand_i = (top_i[:, :, 0, :, None] * N_KEYS + top_i[:, :, 1, None, :]).reshape(t, PEER_HEADS, PEER_TOPK * PEER_TOPK)
        best_s, best_pos = lax.top_k(cand_s, PEER_TOPK)
        idx = jnp.take_along_axis(cand_i, best_pos, axis=-1)
        gate = jax.nn.softmax(best_s, axis=-1).astype(xb.dtype)
        act = jax.nn.gelu(jnp.einsum('td,thkd->thk', xb, u_tab[idx]), approximate=False)
        return jnp.einsum('thk,thkd->td', gate * act, v_tab[idx])

    return lax.map(one_block, hb).reshape(bsz, seq, d)


def setup_inputs(seed: int = 0) -> dict:
    key = jax.random.key(seed)
    ks = iter(jax.random.split(key, 40))
    nrm = lambda shape, scale: jax.random.normal(next(ks), shape, jnp.float32) * scale
    gain = lambda shape: 1.0 + nrm(shape, 0.02)
    L = DEPTH
    offset = jax.random.randint(next(ks), (BATCH, 1), 0, 10000, jnp.int32)
    return {
        'x': nrm((BATCH, SEQ, D_MODEL), 1.0),
        'p': nrm((DEPTH, BATCH, SEQ, PLE_DIM), 1.0),
        'positions': offset + jnp.arange(SEQ, dtype=jnp.int32)[None, :],
        'norm_mix': gain((L, D_MODEL)),
        'w_in': nrm((L, D_MODEL, IN_COLS), D_MODEL ** -0.5),
        'rw_mu': jax.random.uniform(next(ks), (L, RW_COLS), jnp.float32),
        'rw_w0': jax.random.uniform(next(ks), (L, RW_DIM), jnp.float32, -6.0, -1.0),
        'rw_w2': nrm((L, RW_LORA_W, RW_DIM), 0.1),
        'rw_a0': nrm((L, RW_DIM), 0.1),
        'rw_a2': nrm((L, RW_LORA_A, RW_DIM), RW_LORA_A ** -0.5),
        'rw_g2': nrm((L, RW_LORA_G, RW_DIM), RW_LORA_G ** -0.5),
        'rw_k_k': 0.85 + nrm((L, RW_DIM), 0.02),
        'rw_k_a': gain((L, RW_DIM)),
        'rw_r_k': nrm((L, RW_HEADS, RW_HEAD_DIM), 0.1),
        'rw_gn_w': gain((L, RW_DIM)),
        'rw_gn_b': nrm((L, RW_DIM), 0.01),
        'rw_w_o': nrm((L, RW_DIM, D_MODEL), RW_DIM ** -0.5),
        'mla_q_norm': gain((L, Q_LORA)),
        'mla_w_uq': nrm((L, Q_LORA, MLA_HEADS * (QK_NOPE + QK_ROPE)), Q_LORA ** -0.5),
        'mla_kv_norm': gain((L, KV_LORA)),
        'mla_w_ukv': nrm((L, KV_LORA, MLA_HEADS * (QK_NOPE + V_HEAD)), KV_LORA ** -0.5),
        'mla_w_o': nrm((L, MLA_DIM, D_MODEL), MLA_DIM ** -0.5),
        'w_out': nrm((L, D_MODEL, D_MODEL), D_MODEL ** -0.5),
        'norm_ffn': gain((L, D_MODEL)),
        'peer_w_q': nrm((L, D_MODEL, PEER_HEADS * D_QUERY), D_MODEL ** -0.5),
        'peer_sub_keys': nrm((L, PEER_HEADS, 2, N_KEYS, HALF_Q), HALF_Q ** -0.5),
        'peer_u': nrm((L, N_EXPERTS, D_MODEL), D_MODEL ** -0.5),
        'peer_v': nrm((L, N_EXPERTS, D_MODEL), D_MODEL ** -0.5),
        'norm_ple': gain((L, D_MODEL)),
        'ple_w_gate': nrm((L, D_MODEL, D_MODEL), D_MODEL ** -0.5),
        'ple_w_proj': nrm((L, PLE_DIM, D_MODEL), 0.5 * PLE_DIM ** -0.5),
        'norm_final': gain((D_MODEL,)),
    }


def reference(x, p, positions, norm_mix, w_in, rw_mu, rw_w0, rw_w2, rw_a0, rw_a2, rw_g2,
              rw_k_k, rw_k_a, rw_r_k, rw_gn_w, rw_gn_b, rw_w_o, mla_q_norm, mla_w_uq,
              mla_kv_norm, mla_w_ukv, mla_w_o, w_out, norm_ffn, peer_w_q, peer_sub_keys,
              peer_u, peer_v, norm_ple, ple_w_gate, ple_w_proj, norm_final):
    for i in range(DEPTH):
        h = rms_norm(x, norm_mix[i])
        z = h @ w_in[i]
        z_rw = z[..., :RW_COLS]
        z_mla = z[..., RW_COLS:RW_COLS + MLA_COLS]
        gates = jax.nn.sigmoid(z[..., RW_COLS + MLA_COLS:])
        gate_a, gate_b = gates[..., :D_MODEL], gates[..., D_MODEL:]
        y_a = rwkv7_branch(z_rw, rw_mu[i], rw_w0[i], rw_w2[i], rw_a0[i], rw_a2[i], rw_g2[i],
                           rw_k_k[i], rw_k_a[i], rw_r_k[i], rw_gn_w[i], rw_gn_b[i], rw_w_o[i])
        y_b = mla_branch(z_mla, positions, mla_q_norm[i], mla_w_uq[i], mla_kv_norm[i],
                         mla_w_ukv[i], mla_w_o[i])
        x = x + (gate_a * y_a + gate_b * y_b) @ w_out[i]
        x = x + peer_ffn(rms_norm(x, norm_ffn[i]), peer_w_q[i], peer_sub_keys[i], peer_u[i], peer_v[i])
        x = x + jax.nn.sigmoid(rms_norm(x, norm_ple[i]) @ ple_w_gate[i]) * (p[i] @ ple_w_proj[i])
    return rms_norm(x, norm_final)
```

```python
import functools
import math

import jax
import jax.numpy as jnp
from jax import lax
from jax.experimental import pallas as pl
from jax.experimental.pallas import tpu as pltpu

F32 = jnp.float32
BF16 = jnp.bfloat16
HIGHEST = lax.Precision.HIGHEST

LANES = 128
SUBLANES = 8
VMEM_LIMIT = 56 * 1024 * 1024

EPS = 1e-6
RW_HEADS = 8
RW_HEAD_DIM = 64
RW_DIM = RW_HEADS * RW_HEAD_DIM
RW_GN_EPS = 64e-5
SCAN_CHUNK = 64

MLA_HEADS = 8
QK_NOPE = 64
QK_ROPE = 32
V_HEAD = 64
Q_LORA = 384
KV_LORA = 256
ROPE_THETA = 10000.0
MASK_CHUNK = 64
NEG_INF = -1e30

PEER_HEADS = 8
N_KEYS = 128
PEER_TOPK = 16
HALF_Q = 128


def _cparams(*sem):
    return pltpu.CompilerParams(dimension_semantics=sem, vmem_limit_bytes=VMEM_LIMIT)


def _dot(a, b, precision=None):
    return jnp.dot(a, b, preferred_element_type=F32, precision=precision)


def _dot_nt(a, b, precision=None):
    return lax.dot_general(a, b, (((1,), (1,)), ((), ())),
                           preferred_element_type=F32, precision=precision)


def _rms(x, gain):
    return x * lax.rsqrt(jnp.mean(x * x, axis=-1, keepdims=True) + EPS) * gain


def _sigmoid(x):
    return 1.0 / (1.0 + jnp.exp(-x))


def _split3(x):
    hi = x.astype(BF16)
    r1 = x - hi.astype(F32)
    mid = r1.astype(BF16)
    lo = (r1 - mid.astype(F32)).astype(BF16)
    return hi, mid, lo


def _seg_sum(x, bd):
    hi, mid, lo = _split3(x)
    return _dot(hi, bd) + _dot(mid, bd) + _dot(lo, bd)


def _norm_matmul_kernel(x_ref, g_ref, w_ref, o_ref, h_ref, *, act):
    @pl.when(pl.program_id(1) == 0)
    def _():
        h_ref[...] = _rms(x_ref[...], g_ref[...]).astype(BF16)

    y = _dot(h_ref[...], w_ref[...])
    if act == "sigmoid":
        y = _sigmoid(y)
    o_ref[...] = y.astype(o_ref.dtype)


def norm_matmul(x, gain, w, *, tm, tn, act=None, out_dtype=F32):
    t, d = x.shape
    n = w.shape[1]
    return pl.pallas_call(
        functools.partial(_norm_matmul_kernel, act=act),
        grid=(t // tm, n // tn),
        in_specs=[pl.BlockSpec((tm, d), lambda i, j: (i, 0)),
                  pl.BlockSpec((1, d), lambda i, j: (0, 0)),
                  pl.BlockSpec((d, tn), lambda i, j: (0, j))],
        out_specs=pl.BlockSpec((tm, tn), lambda i, j: (i, j)),
        out_shape=jax.ShapeDtypeStruct((t, n), out_dtype),
        scratch_shapes=[pltpu.VMEM((tm, d), BF16)],
        compiler_params=_cparams("parallel", "arbitrary"),
        name="norm_matmul",
    )(x, gain.reshape(1, d), w)


RW_ZCOLS = 3 * RW_DIM + 3 * LANES


def _rwkv_prep_kernel(z_ref, zp_ref, mu_ref, w0_ref, w2_ref, a0_ref, a2_ref, g2_ref,
                      kk_ref, ka_ref, bd_ref,
                      r_ref, wl_ref, k_ref, v_ref, a_ref, b_ref, g_ref):
    z = z_ref[0]
    tm = z.shape[0]
    prev_last = zp_ref[0][SUBLANES - 1:SUBLANES, :]
    prev_last = jnp.where(pl.program_id(1) == 0, 0.0, prev_last)
    rolled = pltpu.roll(z, 1, 0)
    row = lax.broadcasted_iota(jnp.int32, (tm, 1), 0)
    z_prev = jnp.where(row == 0, prev_last, rolled)
    z = z + mu_ref[...] * (z_prev - z)

    o1, o2, o3 = RW_DIM, 2 * RW_DIM, 3 * RW_DIM
    r, k, v = z[:, :o1], z[:, o1:o2], z[:, o2:o3]
    zw, za, zg = z[:, o3:o3 + LANES], z[:, o3 + LANES:o3 + 2 * LANES], z[:, o3 + 2 * LANES:]

    wpre = w0_ref[...] + _dot(jnp.tanh(zw), w2_ref[...])
    nx = -wpre
    softplus = jnp.maximum(nx, 0.0) + jnp.log(1.0 + jnp.exp(-jnp.abs(nx)))
    w = -softplus - 0.5
    iclr = _sigmoid(a0_ref[...] + _dot(za, a2_ref[...]))
    g = _dot(_sigmoid(zg), g2_ref[...])

    kk = k * kk_ref[...]
    ss = _seg_sum(kk * kk, bd_ref[...])
    kk = kk / jnp.maximum(jnp.sqrt(ss), 1e-12)

    r_ref[0] = r
    wl_ref[0] = -jnp.exp(w)
    k_ref[0] = k * (1.0 + (iclr - 1.0) * ka_ref[...])
    v_ref[0] = v
    a_ref[0] = -kk
    b_ref[0] = kk * iclr
    g_ref[0] = g


def rwkv_prep(z, mu, w0, w2, a0, a2, g2, k_k, k_a, bd, *, tm):
    bsz, seq, zc = z.shape
    d = RW_DIM
    row = lambda a: a.reshape(1, -1)
    full = lambda a: pl.BlockSpec(a.shape, lambda b, i: (0,) * a.ndim)
    args = (row(mu), row(w0), w2, row(a0), a2, g2, row(k_k), row(k_a), bd)
    out = jax.ShapeDtypeStruct((bsz, seq, d), F32)
    ospec = pl.BlockSpec((1, tm, d), lambda b, i: (b, i, 0))
    return pl.pallas_call(
        _rwkv_prep_kernel,
        grid=(bsz, seq // tm),
        in_specs=[pl.BlockSpec((1, tm, zc), lambda b, i: (b, i, 0)),
                  pl.BlockSpec((1, SUBLANES, zc),
                               lambda b, i: (b, jnp.maximum(i * (tm // SUBLANES) - 1, 0), 0))]
                 + [full(a) for a in args],
        out_specs=[ospec] * 7,
        out_shape=[out] * 7,
        compiler_params=_cparams("parallel", "arbitrary"),
        name="rwkv_prep",
    )(z, z, *args)


def _rwkv_scan_kernel(r_ref, wl_ref, k_ref, v_ref, a_ref, b_ref, y_ref, g_ref):
    c = SCAN_CHUNK
    hd = RW_HEAD_DIM
    prec = HIGHEST

    @pl.when(pl.program_id(2) == 0)
    def _():
        g_ref[...] = jnp.zeros_like(g_ref)

    r, wl, k, v, a, b = (x[0] for x in (r_ref, wl_ref, k_ref, v_ref, a_ref, b_ref))

    ri = lax.broadcasted_iota(jnp.int32, (c, c), 0)
    ci = lax.broadcasted_iota(jnp.int32, (c, c), 1)
    cs = _dot((ri >= ci).astype(F32), wl, prec)
    cs_last = cs[c - 1:c, :]
    p_inc = jnp.exp(cs)
    p_inv = jnp.exp(-cs)
    at = a * jnp.exp(cs - wl)
    rt = r * p_inc
    bt = b * p_inv
    kt = k * p_inv
    p_tail = jnp.exp(cs_last - cs)
    b2 = b * p_tail
    k2 = k * p_tail

    lane = lax.broadcasted_iota(jnp.int32, (c, 2 * hd), 1)
    m0 = lane < hd
    stack = lambda x: jnp.concatenate([jnp.where(m0, x, 0.0), jnp.where(m0, 0.0, x)], axis=0)
    twice = lambda x: jnp.concatenate([x, x], axis=0)
    at_s, rt_s, bt_s, kt_s = stack(at), stack(rt), stack(bt), stack(kt)

    r2 = lax.broadcasted_iota(jnp.int32, (2 * c, 2 * c), 0)
    c2 = lax.broadcasted_iota(jnp.int32, (2 * c, 2 * c), 1)
    same = (r2 >= c) == (c2 >= c)
    strict = jnp.logical_and(same, r2 > c2)
    incl = jnp.logical_and(same, r2 >= c2)

    ab = jnp.where(strict, _dot_nt(at_s, bt_s, prec), 0.0)
    ak = jnp.where(strict, _dot_nt(at_s, kt_s, prec), 0.0)
    rb = jnp.where(incl, _dot_nt(rt_s, bt_s, prec), 0.0)
    rk = jnp.where(incl, _dot_nt(rt_s, kt_s, prec), 0.0)

    tinv = (r2 == c2).astype(F32) + ab
    x = ab
    for _ in range(int(math.log2(c)) - 1):
        x = _dot(x, x, prec)
        tinv = tinv + _dot(tinv, x, prec)

    gt = g_ref[...]
    pick = lambda s: jnp.where(m0, s[:c], s[c:])
    rhs = _dot_nt(at, gt, prec) + pick(_dot(ak, twice(v), prec))
    u = pick(_dot(tinv, twice(rhs), prec))
    y = _dot_nt(rt, gt, prec) + pick(_dot(rb, twice(u), prec) + _dot(rk, twice(v), prec))
    y_ref[0] = y

    upd = _dot(u.T, b2, prec) + _dot(v.T, k2, prec)
    g_ref[...] = gt * jnp.exp(cs_last) + jnp.where(same[:, :], upd, 0.0)


def rwkv_scan(r, wl, k, v, a, b):
    bsz, seq, d = r.shape
    c = SCAN_CHUNK
    spec = pl.BlockSpec((1, c, LANES), lambda bi, hp, ci: (bi, ci, hp))
    return pl.pallas_call(
        _rwkv_scan_kernel,
        grid=(bsz, d // LANES, seq // c),
        in_specs=[spec] * 6,
        out_specs=spec,
        out_shape=jax.ShapeDtypeStruct((bsz, seq, d), F32),
        scratch_shapes=[pltpu.VMEM((LANES, LANES), F32)],
        compiler_params=_cparams("parallel", "parallel", "arbitrary"),
        name="rwkv_scan",
    )(r, wl, k, v, a, b)


MLA_ZCOLS = 1024
MLA_SCALE = 1.0 / math.sqrt(QK_NOPE + QK_ROPE)


def _mla_prep_kernel(z_ref, pos_ref, fl_ref, qn_ref, kvn_ref, wq_ref, wqs_ref, wk_ref, wv_ref,
                     q_ref, k_ref, v_ref):
    z = z_ref[...]
    c_q = _rms(z[:, :Q_LORA], qn_ref[...]).astype(BF16)
    c_kv = _rms(z[:, Q_LORA:Q_LORA + KV_LORA], kvn_ref[...]).astype(BF16)
    kr = z[:, Q_LORA + KV_LORA:Q_LORA + KV_LORA + LANES]
    krs = z[:, Q_LORA + KV_LORA + LANES:Q_LORA + KV_LORA + 2 * LANES]

    ang = pos_ref[...].astype(F32) * fl_ref[...]
    cos, sin = jnp.cos(ang), jnp.sin(ang)
    kr_rot = kr * cos + krs * sin

    q = _dot(c_q, wq_ref[...])
    qs = _dot(c_q, wqs_ref[...])
    kn = _dot(c_kv, wk_ref[...])
    v_ref[...] = _dot(c_kv, wv_ref[...]).astype(BF16)
    for h in range(MLA_HEADS):
        sl = slice(h * LANES, (h + 1) * LANES)
        q_ref[:, sl] = ((q[:, sl] * cos + qs[:, sl] * sin) * MLA_SCALE).astype(BF16)
        k_ref[:, sl] = (kn[:, sl] + kr_rot).astype(BF16)


def mla_prep(z, pos, f_lane, q_norm, kv_norm, wq, wqs, wk, wv, *, tm):
    t = z.shape[0]
    n = MLA_HEADS * LANES
    full = lambda a: pl.BlockSpec(a.shape, lambda i: (0,) * a.ndim)
    args = (f_lane, q_norm.reshape(1, -1), kv_norm.reshape(1, -1), wq, wqs, wk, wv)
    out = jax.ShapeDtypeStruct((t, n), BF16)
    ospec = pl.BlockSpec((tm, n), lambda i: (i, 0))
    return pl.pallas_call(
        _mla_prep_kernel,
        grid=(t // tm,),
        in_specs=[pl.BlockSpec((tm, MLA_ZCOLS), lambda i: (i, 0)),
                  pl.BlockSpec((tm, 1), lambda i: (i, 0))] + [full(a) for a in args],
        out_specs=[ospec] * 3,
        out_shape=[out] * 3,
        compiler_params=_cparams("parallel"),
        name="mla_prep",
    )(z, pos, *args)


def _attn_kernel(q_ref, k_ref, v_ref, o_ref, *, tq):
    iq = pl.program_id(2)
    q = q_ref[0]

    def tile(j, carry, masked):
        m, l, acc = carry
        start = pl.multiple_of(j * tq, tq)
        kj = k_ref[0, pl.ds(start, tq), :]
        vj = v_ref[0, pl.ds(start, tq), :]
        s = _dot_nt(q, kj)
        if masked:
            qc = lax.broadcasted_iota(jnp.int32, (tq, tq), 0) // MASK_CHUNK
            kc = lax.broadcasted_iota(jnp.int32, (tq, tq), 1) // MASK_CHUNK
            s = jnp.where(kc <= qc, s, NEG_INF)
        m_new = jnp.maximum(m, jnp.max(s, axis=-1, keepdims=True))
        alpha = jnp.exp(m - m_new)
        p = jnp.exp(s - m_new)
        l = alpha * l + jnp.sum(p, axis=-1, keepdims=True)
        acc = alpha * acc + _dot(p.astype(BF16), vj)
        return m_new, l, acc

    init = (jnp.full((tq, 1), NEG_INF, F32), jnp.zeros((tq, 1), F32), jnp.zeros((tq, LANES), F32))
    carry = lax.fori_loop(0, iq, lambda j, c: tile(j, c, False), init)
    m, l, acc = tile(iq, carry, True)
    o_ref[0] = (acc / l).astype(BF16)


def attention(q, k, v, *, tq):
    bsz, seq, n = q.shape
    heads = n // LANES
    qspec = pl.BlockSpec((1, tq, LANES), lambda b, h, i: (b, i, h))
    kspec = pl.BlockSpec((1, seq, LANES), lambda b, h, i: (b, 0, h))
    return pl.pallas_call(
        functools.partial(_attn_kernel, tq=tq),
        grid=(bsz, heads, seq // tq),
        in_specs=[qspec, kspec, kspec],
        out_specs=qspec,
        out_shape=jax.ShapeDtypeStruct((bsz, seq, n), BF16),
        compiler_params=_cparams("parallel", "parallel", "arbitrary"),
        name="attention",
    )(q, k, v)


def _combine_kernel(x_ref, y_ref, r_ref, k_ref, v_ref, g_ref, o_ref, gates_ref,
                    bd_ref, rk_ref, gnw_ref, gnb_ref, wa_ref, wb_ref, wo_ref, out_ref):
    bd = bd_ref[...]
    inv_n = 1.0 / RW_HEAD_DIM
    y = y_ref[...]
    v = v_ref[...]
    mean = _seg_sum(y, bd) * inv_n
    yc = y - mean
    var = _seg_sum(yc * yc, bd) * inv_n
    yn = yc * lax.rsqrt(var + RW_GN_EPS) * gnw_ref[...] + gnb_ref[...]
    bonus = _seg_sum(r_ref[...] * k_ref[...] * rk_ref[...], bd) * v
    ya = _dot(((yn + bonus) * g_ref[...]).astype(BF16), wa_ref[...])
    yb = _dot(o_ref[...], wb_ref[...])
    d = ya.shape[1]
    gates = gates_ref[...]
    mix = gates[:, :d] * ya + gates[:, d:] * yb
    out_ref[...] = x_ref[...] + _dot(mix.astype(BF16), wo_ref[...])


def combine(x, y, r, k, v, g, o, gates, bd, r_k, gn_w, gn_b, wa, wb, wo, *, tm):
    t, d = x.shape
    row = lambda a: a.reshape(1, -1)
    tok = lambda a: pl.BlockSpec((tm, a.shape[1]), lambda i: (i, 0))
    full = lambda a: pl.BlockSpec(a.shape, lambda i: (0,) * a.ndim)
    toks = (x, y, r, k, v, g, o, gates)
    consts = (bd, row(r_k), row(gn_w), row(gn_b), wa, wb, wo)
    return pl.pallas_call(
        _combine_kernel,
        grid=(t // tm,),
        in_specs=[tok(a) for a in toks] + [full(a) for a in consts],
        out_specs=pl.BlockSpec((tm, d), lambda i: (i, 0)),
        out_shape=jax.ShapeDtypeStruct((t, d), F32),
        compiler_params=_cparams("parallel"),
        name="combine",
    )(*toks, *consts)


NOT_RANKED = 127.0
STAIR = tuple(PEER_TOPK // (ii + 1) for ii in range(PEER_TOPK))


def _top_ranks(s, k):
    n, t = s.shape
    key = lax.broadcasted_iota(jnp.int32, (n, t), 0)
    slot = lax.broadcasted_iota(jnp.int32, (k, t), 0)

    def body(r, carry):
        s, rank, top = carry
        m = jnp.max(s, axis=0, keepdims=True)
        idx = jnp.min(jnp.where(s == m, key, n), axis=0, keepdims=True)
        hit = key == idx
        rank = jnp.where(hit, r.astype(F32), rank)
        s = jnp.where(hit, -jnp.inf, s)
        top = jnp.where(slot == r, m, top)
        return s, rank, top

    init = (s, jnp.full((n, t), NOT_RANKED, F32), jnp.zeros((k, t), F32))
    _, rank, top = lax.fori_loop(0, k, body, init)
    return top, rank


def _peer_route_kernel(q_ref, keys_ref, lam_ref, cc_ref, rho_ref, e1_ref):
    tm = q_ref.shape[0]
    k = PEER_TOPK
    neg = -jnp.inf

    def head(h, _):
        col = pl.multiple_of(h * 2 * HALF_Q, 2 * HALF_Q)
        s0 = _dot_nt(keys_ref[h, 0], q_ref[:, pl.ds(col, HALF_Q)])
        s1 = _dot_nt(keys_ref[h, 1], q_ref[:, pl.ds(col + HALF_Q, HALF_Q)])
        top0, rank0 = _top_ranks(s0, k)
        top1, rank1 = _top_ranks(s1, k)

        row8 = lax.broadcasted_iota(jnp.int32, (SUBLANES, tm), 0)
        groups = [top0[0:1] + top1[0:8], top0[0:1] + top1[8:16], top0[1:2] + top1[0:8]]
        for ii in range(2, 8):
            groups.append(jnp.where(row8 < STAIR[ii], top0[ii:ii + 1] + top1[0:8], neg))
        groups.append(top0[8:16] + top1[0:1])
        cand = jnp.concatenate(groups, axis=0)
        _, crank = _top_ranks(cand, k)
        sel = crank < NOT_RANKED
        ex = jnp.where(sel, jnp.exp(cand - cand[0:1]), 0.0)
        z = jnp.sum(ex, axis=0, keepdims=True)
        self = sel.astype(F32)
        counts = [jnp.sum(self[0:16], axis=0, keepdims=True)]
        for g in range(2, 9):
            counts.append(jnp.sum(self[8 * g:8 * g + 8], axis=0, keepdims=True))
        lvec = jnp.concatenate(counts + [self[72:80]], axis=0)

        lam = jnp.zeros((N_KEYS, tm), F32)
        for ii in range(k):
            lam = jnp.where(rank0 == float(ii), lvec[ii:ii + 1], lam)
        lam_ref[h] = lam
        cc_ref[h] = jnp.exp(s0 - top0[0:1]) / z
        rho_ref[h] = rank1
        e1_ref[h] = jnp.exp(s1 - top1[0:1])
        return 0

    lax.fori_loop(0, PEER_HEADS, head, 0)


def peer_route(qp, keys, *, tm):
    t = qp.shape[0]
    out = jax.ShapeDtypeStruct((PEER_HEADS, N_KEYS, t), F32)
    ospec = pl.BlockSpec((PEER_HEADS, N_KEYS, tm), lambda i: (0, 0, i))
    return pl.pallas_call(
        _peer_route_kernel,
        grid=(t // tm,),
        in_specs=[pl.BlockSpec((tm, qp.shape[1]), lambda i: (i, 0)),
                  pl.BlockSpec(keys.shape, lambda i: (0, 0, 0, 0))],
        out_specs=[ospec] * 4,
        out_shape=[out] * 4,
        compiler_params=_cparams("parallel"),
        name="peer_route",
    )(qp, keys)


PEER_ROWS = 8


def _erf(x):
    return lax.erf(x)


def _gelu(x):
    return 0.5 * x * (1.0 + _erf(x * (1.0 / math.sqrt(2.0))))


def _peer_expert_kernel(x_ref, gn_ref, u_ref, vt_ref, lam_ref, cc_ref, rho_ref, e1_ref, out_ref,
                        xt_ref, acc_ref, pre_ref, g_ref):
    j = pl.program_id(1)

    @pl.when(j == 0)
    def _():
        xt_ref[...] = _rms(x_ref[...], gn_ref[...]).T.astype(BF16)
        acc_ref[...] = jnp.zeros_like(acc_ref)

    pre_ref[...] = _dot(u_ref[...], xt_ref[...])

    def row(ii, _):
        rows = pl.ds(pl.multiple_of(ii * N_KEYS, N_KEYS), N_KEYS)
        gate = jnp.zeros((N_KEYS, pre_ref.shape[1]), F32)
        for h in range(PEER_HEADS):
            lam = lam_ref[h, pl.ds(ii, 1), :]
            cc = cc_ref[h, pl.ds(ii, 1), :]
            gate = gate + jnp.where(rho_ref[h] < lam, e1_ref[h] * cc, 0.0)
        g_ref[rows, :] = (_gelu(pre_ref[rows, :]) * gate).astype(BF16)
        return 0

    lax.fori_loop(0, PEER_ROWS, row, 0)
    acc_ref[...] += _dot(vt_ref[...], g_ref[...])

    @pl.when(j == pl.num_programs(1) - 1)
    def _():
        out_ref[...] = acc_ref[...].T


def peer_expert(x, gain, u, vt, lam, cc, rho, e1, *, tm):
    t, d = x.shape
    nrow = PEER_ROWS * N_KEYS
    rspec = pl.BlockSpec((PEER_HEADS, PEER_ROWS, tm), lambda i, j: (0, j, i))
    cspec = pl.BlockSpec((PEER_HEADS, N_KEYS, tm), lambda i, j: (0, 0, i))
    return pl.pallas_call(
        _peer_expert_kernel,
        grid=(t // tm, u.shape[0] // nrow),
        in_specs=[pl.BlockSpec((tm, d), lambda i, j: (i, 0)),
                  pl.BlockSpec((1, d), lambda i, j: (0, 0)),
                  pl.BlockSpec((nrow, d), lambda i, j: (j, 0)),
                  pl.BlockSpec((d, nrow), lambda i, j: (0, j)),
                  rspec, rspec, cspec, cspec],
        out_specs=pl.BlockSpec((tm, d), lambda i, j: (i, 0)),
        out_shape=jax.ShapeDtypeStruct((t, d), F32),
        scratch_shapes=[pltpu.VMEM((d, tm), BF16), pltpu.VMEM((d, tm), F32),
                        pltpu.VMEM((nrow, tm), F32), pltpu.VMEM((nrow, tm), BF16)],
        compiler_params=_cparams("parallel", "arbitrary"),
        name="peer_expert",
    )(x, gain.reshape(1, d), u, vt, lam, cc, rho, e1)


def _ple_final_kernel(x_ref, f_ref, p_ref, gp_ref, gf_ref, wg_ref, wp_ref, out_ref):
    x = x_ref[...] + f_ref[...]
    gate = _sigmoid(_dot(_rms(x, gp_ref[...]).astype(BF16), wg_ref[...]))
    x = x + gate * _dot(p_ref[...].astype(BF16), wp_ref[...])
    out_ref[...] = _rms(x, gf_ref[...])


def ple_final(x, f, p, g_ple, g_final, wg, wp, *, tm):
    t, d = x.shape
    tok = lambda a: pl.BlockSpec((tm, a.shape[1]), lambda i: (i, 0))
    full = lambda a: pl.BlockSpec(a.shape, lambda i: (0,) * a.ndim)
    consts = (g_ple.reshape(1, d), g_final.reshape(1, d), wg, wp)
    return pl.pallas_call(
        _ple_final_kernel,
        grid=(t // tm,),
        in_specs=[tok(x), tok(f), tok(p)] + [full(a) for a in consts],
        out_specs=tok(x),
        out_shape=jax.ShapeDtypeStruct((t, d), F32),
        compiler_params=_cparams("parallel"),
        name="ple_final",
    )(x, f, p, *consts)


def _place(cols, width, offset):
    return jnp.pad(cols, ((0, 0), (offset, width - offset - cols.shape[1])))


def _rw_in_weights(w_rw, mu):
    o3 = 3 * RW_DIM
    lw, la = 64, 64
    segs = [w_rw[:, :o3], _place(w_rw[:, o3:o3 + lw], LANES, 0),
            _place(w_rw[:, o3 + lw:o3 + lw + la], LANES, 0), w_rw[:, o3 + lw + la:]]
    mus = [mu[None, :o3], _place(mu[None, o3:o3 + lw], LANES, 0),
           _place(mu[None, o3 + lw:o3 + lw + la], LANES, 0), mu[None, o3 + lw + la:]]
    return jnp.concatenate(segs, axis=1), jnp.concatenate(mus, axis=1)[0]


def _mla_in_weights(w_mla):
    half = QK_ROPE // 2
    lat = Q_LORA + KV_LORA
    kr = w_mla[:, lat:]
    kr_sw = jnp.concatenate([-kr[:, half:], kr[:, :half]], axis=1)
    return jnp.concatenate([w_mla[:, :lat], _place(kr, LANES, QK_NOPE), _place(kr_sw, LANES, QK_NOPE),
                            jnp.zeros((w_mla.shape[0], LANES), w_mla.dtype)], axis=1)


def _mla_up_weights(w_uq, w_ukv):
    half = QK_ROPE // 2
    qd = QK_NOPE + QK_ROPE
    wq = w_uq.reshape(Q_LORA, MLA_HEADS, qd)
    rope = wq[:, :, QK_NOPE:]
    rope_sw = jnp.concatenate([-rope[:, :, half:], rope[:, :, :half]], axis=2)
    pad = lambda t, off: jnp.pad(t, ((0, 0), (0, 0), (off, LANES - off - t.shape[2])))
    wq_pad = pad(wq, 0).reshape(Q_LORA, MLA_HEADS * LANES)
    wq_sw = pad(rope_sw, QK_NOPE).reshape(Q_LORA, MLA_HEADS * LANES)
    lane = jnp.arange(MLA_HEADS * LANES) % LANES
    wk = jnp.where(lane < QK_NOPE, w_ukv, 0.0)
    wv = jnp.where(lane < QK_NOPE, 0.0, w_ukv)
    return wq_pad, wq_sw, wk, wv


def kernel(x, p, positions, norm_mix, w_in, rw_mu, rw_w0, rw_w2, rw_a0, rw_a2, rw_g2, rw_k_k, rw_k_a, rw_r_k, rw_gn_w, rw_gn_b, rw_w_o, mla_q_norm, mla_w_uq, mla_kv_norm, mla_w_ukv, mla_w_o, w_out, norm_ffn, peer_w_q, peer_sub_keys, peer_u, peer_v, norm_ple, ple_w_gate, ple_w_proj, norm_final):
    bsz, seq, d = x.shape
    t = bsz * seq
    depth = p.shape[0]
    bf = lambda a: a.astype(BF16)
    rw_cols = 3 * RW_DIM + 64 + 64 + 128
    mla_cols = Q_LORA + KV_LORA + QK_ROPE

    head_of = jnp.arange(RW_DIM) // RW_HEAD_DIM
    bd = bf(head_of[:, None] == head_of[None, :])
    inv_freq = ROPE_THETA ** (-jnp.arange(0, QK_ROPE, 2, dtype=F32) / QK_ROPE)
    f_lane = _place(jnp.concatenate([inv_freq, inv_freq])[None, :], LANES, QK_NOPE)
    pos = positions.reshape(t, 1)

    xf = x.reshape(t, d)
    assert depth == 1, "the final RMSNorm is fused into the layer's last kernel"
    for i in range(depth):
        w_rw, mu = _rw_in_weights(w_in[i][:, :rw_cols], rw_mu[i])
        w_mla = _mla_in_weights(w_in[i][:, rw_cols:rw_cols + mla_cols])
        w_gates = w_in[i][:, rw_cols + mla_cols:]
        z_rw = norm_matmul(xf, norm_mix[i], bf(w_rw), tm=1024, tn=RW_ZCOLS // 3)
        z_mla = norm_matmul(xf, norm_mix[i], bf(w_mla), tm=1024, tn=MLA_ZCOLS)
        gates = norm_matmul(xf, norm_mix[i], bf(w_gates), tm=1024, tn=1024, act="sigmoid")

        pad_rows = lambda w: jnp.pad(w, ((0, LANES - w.shape[0]), (0, 0)))
        r, wl, k, v, a, b, g = rwkv_prep(
            z_rw.reshape(bsz, seq, RW_ZCOLS), mu, rw_w0[i], pad_rows(rw_w2[i]), rw_a0[i],
            pad_rows(rw_a2[i]), rw_g2[i], rw_k_k[i], rw_k_a[i], bd, tm=256)
        y = rwkv_scan(r, wl, k, v, a, b)

        wq, wqs, wk, wv = _mla_up_weights(mla_w_uq[i], mla_w_ukv[i])
        q, kk, vv = mla_prep(z_mla, pos, f_lane, mla_q_norm[i], mla_kv_norm[i],
                             bf(wq), bf(wqs), bf(wk), bf(wv), tm=512)
        n = MLA_HEADS * LANES
        o = attention(q.reshape(bsz, seq, n), kk.reshape(bsz, seq, n), vv.reshape(bsz, seq, n), tq=256)
        wo_pad = jnp.pad(mla_w_o[i].reshape(MLA_HEADS, V_HEAD, d),
                         ((0, 0), (LANES - V_HEAD, 0), (0, 0))).reshape(n, d)

        flat = lambda a: a.reshape(t, -1)
        x1 = combine(xf, flat(y), flat(r), flat(k), flat(v), flat(g), o.reshape(t, n), gates,
                     bd, rw_r_k[i].reshape(-1), rw_gn_w[i], rw_gn_b[i],
                     bf(rw_w_o[i]), bf(wo_pad), bf(w_out[i]), tm=256)

        qp = norm_matmul(x1, norm_ffn[i], bf(peer_w_q[i]), tm=1024, tn=1024, out_dtype=BF16)
        lam, cc, rho, e1 = peer_route(qp, bf(peer_sub_keys[i]), tm=256)
        ffn = peer_expert(x1, norm_ffn[i], bf(peer_u[i]), bf(peer_v[i].T), lam, cc, rho, e1, tm=256)

        xf = ple_final(x1, ffn, p[i].reshape(t, -1), norm_ple[i], norm_final,
                       bf(ple_w_gate[i]), bf(ple_w_proj[i]), tm=256)
    return xf.reshape(bsz, seq, d)
```

```python
import functools
import math

import jax
import jax.numpy as jnp
from jax import lax
from jax.experimental import pallas as pl
from jax.experimental.pallas import tpu as pltpu

F32 = jnp.float32
BF16 = jnp.bfloat16
HIGHEST = lax.Precision.HIGHEST

LANES = 128
SUBLANES = 8
VMEM_LIMIT = 56 * 1024 * 1024

EPS = 1e-6
RW_HEADS = 8
RW_HEAD_DIM = 64
RW_DIM = RW_HEADS * RW_HEAD_DIM
RW_GN_EPS = 64e-5
SCAN_CHUNK = 64

MLA_HEADS = 8
QK_NOPE = 64
QK_ROPE = 32
V_HEAD = 64
Q_LORA = 384
KV_LORA = 256
ROPE_THETA = 10000.0
MASK_CHUNK = 64
NEG_INF = -1e30

PEER_HEADS = 8
N_KEYS = 128
PEER_TOPK = 16
HALF_Q = 128


def _cparams(*sem):
    return pltpu.CompilerParams(dimension_semantics=sem, vmem_limit_bytes=VMEM_LIMIT)


def _dot(a, b, precision=None):
    return jnp.dot(a, b, preferred_element_type=F32, precision=precision)


def _dot_nt(a, b, precision=None):
    return lax.dot_general(a, b, (((1,), (1,)), ((), ())),
                           preferred_element_type=F32, precision=precision)


def _rms(x, gain):
    return x * lax.rsqrt(jnp.mean(x * x, axis=-1, keepdims=True) + EPS) * gain


def _sigmoid(x):
    return 1.0 / (1.0 + jnp.exp(-x))


def _split3(x):
    hi = x.astype(BF16)
    r1 = x - hi.astype(F32)
    mid = r1.astype(BF16)
    lo = (r1 - mid.astype(F32)).astype(BF16)
    return hi, mid, lo


def _seg_sum(x, bd):
    hi, mid, lo = _split3(x)
    return _dot(hi, bd) + _dot(mid, bd) + _dot(lo, bd)


def _norm_matmul_kernel(x_ref, g_ref, w_ref, o_ref, h_ref, *, act):
    @pl.when(pl.program_id(1) == 0)
    def _():
        h_ref[...] = _rms(x_ref[...], g_ref[...]).astype(BF16)

    y = _dot(h_ref[...], w_ref[...])
    if act == "sigmoid":
        y = _sigmoid(y)
    o_ref[...] = y.astype(o_ref.dtype)


def norm_matmul(x, gain, w, *, tm, tn, act=None, out_dtype=F32):
    t, d = x.shape
    n = w.shape[1]
    return pl.pallas_call(
        functools.partial(_norm_matmul_kernel, act=act),
        grid=(t // tm, n // tn),
        in_specs=[pl.BlockSpec((tm, d), lambda i, j: (i, 0)),
                  pl.BlockSpec((1, d), lambda i, j: (0, 0)),
                  pl.BlockSpec((d, tn), lambda i, j: (0, j))],
        out_specs=pl.BlockSpec((tm, tn), lambda i, j: (i, j)),
        out_shape=jax.ShapeDtypeStruct((t, n), out_dtype),
        scratch_shapes=[pltpu.VMEM((tm, d), BF16)],
        compiler_params=_cparams("parallel", "arbitrary"),
        name="norm_matmul",
    )(x, gain.reshape(1, d), w)


RW_ZCOLS = 3 * RW_DIM + 3 * LANES


def _rwkv_prep_kernel(z_ref, zp_ref, mu_ref, w0_ref, w2_ref, a0_ref, a2_ref, g2_ref,
                      kk_ref, ka_ref, bd_ref,
                      r_ref, wl_ref, k_ref, v_ref, a_ref, b_ref, g_ref):
    z = z_ref[0]
    tm = z.shape[0]
    prev_last = zp_ref[0][SUBLANES - 1:SUBLANES, :]
    prev_last = jnp.where(pl.program_id(1) == 0, 0.0, prev_last)
    rolled = pltpu.roll(z, 1, 0)
    row = lax.broadcasted_iota(jnp.int32, (tm, 1), 0)
    z_prev = jnp.where(row == 0, prev_last, rolled)
    z = z + mu_ref[...] * (z_prev - z)

    o1, o2, o3 = RW_DIM, 2 * RW_DIM, 3 * RW_DIM
    r, k, v = z[:, :o1], z[:, o1:o2], z[:, o2:o3]
    zw, za, zg = z[:, o3:o3 + LANES], z[:, o3 + LANES:o3 + 2 * LANES], z[:, o3 + 2 * LANES:]

    wpre = w0_ref[...] + _dot(jnp.tanh(zw), w2_ref[...])
    nx = -wpre
    softplus = jnp.maximum(nx, 0.0) + jnp.log(1.0 + jnp.exp(-jnp.abs(nx)))
    w = -softplus - 0.5
    iclr = _sigmoid(a0_ref[...] + _dot(za, a2_ref[...]))
    g = _dot(_sigmoid(zg), g2_ref[...])

    kk = k * kk_ref[...]
    ss = _seg_sum(kk * kk, bd_ref[...])
    kk = kk / jnp.maximum(jnp.sqrt(ss), 1e-12)

    r_ref[0] = r
    wl_ref[0] = -jnp.exp(w)
    k_ref[0] = k * (1.0 + (iclr - 1.0) * ka_ref[...])
    v_ref[0] = v
    a_ref[0] = -kk
    b_ref[0] = kk * iclr
    g_ref[0] = g


def rwkv_prep(z, mu, w0, w2, a0, a2, g2, k_k, k_a, bd, *, tm):
    bsz, seq, zc = z.shape
    d = RW_DIM
    row = lambda a: a.reshape(1, -1)
    full = lambda a: pl.BlockSpec(a.shape, lambda b, i: (0,) * a.ndim)
    args = (row(mu), row(w0), w2, row(a0), a2, g2, row(k_k), row(k_a), bd)
    out = jax.ShapeDtypeStruct((bsz, seq, d), F32)
    ospec = pl.BlockSpec((1, tm, d), lambda b, i: (b, i, 0))
    return pl.pallas_call(
        _rwkv_prep_kernel,
        grid=(bsz, seq // tm),
        in_specs=[pl.BlockSpec((1, tm, zc), lambda b, i: (b, i, 0)),
                  pl.BlockSpec((1, SUBLANES, zc),
                               lambda b, i: (b, jnp.maximum(i * (tm // SUBLANES) - 1, 0), 0))]
                 + [full(a) for a in args],
        out_specs=[ospec] * 7,
        out_shape=[out] * 7,
        compiler_params=_cparams("parallel", "arbitrary"),
        name="rwkv_prep",
    )(z, z, *args)


def _rwkv_scan_kernel(r_ref, wl_ref, k_ref, v_ref, a_ref, b_ref, y_ref, g_ref):
    c = SCAN_CHUNK
    hd = RW_HEAD_DIM
    npair = g_ref.shape[0]

    @pl.when(pl.program_id(1) == 0)
    def _():
        g_ref[...] = jnp.zeros_like(g_ref)

    ri = lax.broadcasted_iota(jnp.int32, (c, c), 0)
    ci = lax.broadcasted_iota(jnp.int32, (c, c), 1)
    tril = (ri >= ci).astype(F32)
    lane = lax.broadcasted_iota(jnp.int32, (c, 2 * hd), 1)
    m0 = lane < hd
    r2 = lax.broadcasted_iota(jnp.int32, (2 * c, 2 * c), 0)
    c2 = lax.broadcasted_iota(jnp.int32, (2 * c, 2 * c), 1)
    same = (r2 >= c) == (c2 >= c)
    strict = jnp.logical_and(same, r2 > c2)
    incl = jnp.logical_and(same, r2 >= c2)
    eye = (r2 == c2).astype(F32)

    bf = lambda x: x.astype(BF16)
    stack = lambda x: bf(jnp.concatenate([jnp.where(m0, x, 0.0), jnp.where(m0, 0.0, x)], axis=0))
    twice = lambda x: bf(jnp.concatenate([x, x], axis=0))
    pick = lambda s: jnp.where(m0, s[:c], s[c:])

    for hp in range(npair):
        sl = slice(hp * LANES, (hp + 1) * LANES)
        r, wl, k, v, a, b = (x[0, :, sl] for x in (r_ref, wl_ref, k_ref, v_ref, a_ref, b_ref))

        cs = _dot(tril, wl, HIGHEST)
        cs_last = cs[c - 1:c, :]
        p_inv = jnp.exp(-cs)
        p_tail = jnp.exp(cs_last - cs)
        at = a * jnp.exp(cs - wl)
        rt = r * jnp.exp(cs)
        at_s, rt_s, bt_s, kt_s = stack(at), stack(rt), stack(b * p_inv), stack(k * p_inv)

        ab = jnp.where(strict, _dot_nt(at_s, bt_s), 0.0)
        ak = jnp.where(strict, _dot_nt(at_s, kt_s), 0.0)
        rb = jnp.where(incl, _dot_nt(rt_s, bt_s), 0.0)
        rk = jnp.where(incl, _dot_nt(rt_s, kt_s), 0.0)

        tinv = eye + ab
        x = bf(ab)
        for _ in range(int(math.log2(c)) - 1):
            xx = _dot(x, x)
            x = bf(xx)
            tinv = tinv + _dot(bf(tinv), x)

        gt = g_ref[hp]
        gtb = bf(gt)
        vv = twice(v)
        rhs = _dot_nt(bf(at), gtb) + pick(_dot(bf(ak), vv))
        u = pick(_dot(bf(tinv), twice(rhs)))
        y_ref[0, :, sl] = _dot_nt(bf(rt), gtb) + pick(_dot(bf(rb), twice(u)) + _dot(bf(rk), vv))

        upd = _dot(bf(u.T), bf(b * p_tail)) + _dot(bf(v.T), bf(k * p_tail))
        g_ref[hp] = gt * jnp.exp(cs_last) + jnp.where(same, upd, 0.0)


def rwkv_scan(r, wl, k, v, a, b):
    bsz, seq, d = r.shape
    c = SCAN_CHUNK
    spec = pl.BlockSpec((1, c, d), lambda bi, ci: (bi, ci, 0))
    return pl.pallas_call(
        _rwkv_scan_kernel,
        grid=(bsz, seq // c),
        in_specs=[spec] * 6,
        out_specs=spec,
        out_shape=jax.ShapeDtypeStruct((bsz, seq, d), F32),
        scratch_shapes=[pltpu.VMEM((d // LANES, LANES, LANES), F32)],
        compiler_params=_cparams("parallel", "arbitrary"),
        name="rwkv_scan",
    )(r, wl, k, v, a, b)


MLA_ZCOLS = 1024
MLA_SCALE = 1.0 / math.sqrt(QK_NOPE + QK_ROPE)


def _mla_prep_kernel(z_ref, pos_ref, fl_ref, qn_ref, kvn_ref, wq_ref, wqs_ref, wk_ref, wv_ref,
                     q_ref, k_ref, v_ref):
    z = z_ref[...]
    c_q = _rms(z[:, :Q_LORA], qn_ref[...]).astype(BF16)
    c_kv = _rms(z[:, Q_LORA:Q_LORA + KV_LORA], kvn_ref[...]).astype(BF16)
    kr = z[:, Q_LORA + KV_LORA:Q_LORA + KV_LORA + LANES]
    krs = z[:, Q_LORA + KV_LORA + LANES:Q_LORA + KV_LORA + 2 * LANES]

    ang = pos_ref[...].astype(F32) * fl_ref[...]
    cos, sin = jnp.cos(ang), jnp.sin(ang)
    kr_rot = kr * cos + krs * sin

    q = _dot(c_q, wq_ref[...])
    qs = _dot(c_q, wqs_ref[...])
    kn = _dot(c_kv, wk_ref[...])
    v_ref[...] = _dot(c_kv, wv_ref[...]).astype(BF16)
    for h in range(MLA_HEADS):
        sl = slice(h * LANES, (h + 1) * LANES)
        q_ref[:, sl] = ((q[:, sl] * cos + qs[:, sl] * sin) * MLA_SCALE).astype(BF16)
        k_ref[:, sl] = (kn[:, sl] + kr_rot).astype(BF16)


def mla_prep(z, pos, f_lane, q_norm, kv_norm, wq, wqs, wk, wv, *, tm):
    t = z.shape[0]
    n = MLA_HEADS * LANES
    full = lambda a: pl.BlockSpec(a.shape, lambda i: (0,) * a.ndim)
    args = (f_lane, q_norm.reshape(1, -1), kv_norm.reshape(1, -1), wq, wqs, wk, wv)
    out = jax.ShapeDtypeStruct((t, n), BF16)
    ospec = pl.BlockSpec((tm, n), lambda i: (i, 0))
    return pl.pallas_call(
        _mla_prep_kernel,
        grid=(t // tm,),
        in_specs=[pl.BlockSpec((tm, MLA_ZCOLS), lambda i: (i, 0)),
                  pl.BlockSpec((tm, 1), lambda i: (i, 0))] + [full(a) for a in args],
        out_specs=[ospec] * 3,
        out_shape=[out] * 3,
        compiler_params=_cparams("parallel"),
        name="mla_prep",
    )(z, pos, *args)


ATTN_HEADS_PER_STEP = 4


def _attn_kernel(q_ref, k_ref, v_ref, o_ref, *, tq):
    iq = pl.program_id(2)
    nh = q_ref.shape[2] // LANES
    heads = [slice(h * LANES, (h + 1) * LANES) for h in range(nh)]
    qs = [q_ref[0, :, sl] for sl in heads]

    def tile(j, carry, masked):
        start = pl.multiple_of(j * tq, tq)
        if masked:
            qc = lax.broadcasted_iota(jnp.int32, (tq, tq), 0) // MASK_CHUNK
            kc = lax.broadcasted_iota(jnp.int32, (tq, tq), 1) // MASK_CHUNK
            keep = kc <= qc
        out = []
        for h, sl in enumerate(heads):
            m, l, acc = carry[h]
            s = _dot_nt(qs[h], k_ref[0, pl.ds(start, tq), sl])
            if masked:
                s = jnp.where(keep, s, NEG_INF)
            m_new = jnp.maximum(m, jnp.max(s, axis=-1, keepdims=True))
            alpha = jnp.exp(m - m_new)
            p = jnp.exp(s - m_new)
            l = alpha * l + jnp.sum(p, axis=-1, keepdims=True)
            acc = alpha * acc + _dot(p.astype(BF16), v_ref[0, pl.ds(start, tq), sl])
            out.append((m_new, l, acc))
        return tuple(out)

    init = tuple((jnp.full((tq, 1), NEG_INF, F32), jnp.zeros((tq, 1), F32), jnp.zeros((tq, LANES), F32))
                 for _ in heads)
    carry = lax.fori_loop(0, iq, lambda j, c: tile(j, c, False), init)
    carry = tile(iq, carry, True)
    for h, sl in enumerate(heads):
        m, l, acc = carry[h]
        o_ref[0, :, sl] = (acc / l).astype(BF16)


def attention(q, k, v, *, tq):
    bsz, seq, n = q.shape
    w = ATTN_HEADS_PER_STEP * LANES
    qspec = pl.BlockSpec((1, tq, w), lambda b, h, i: (b, i, h))
    kspec = pl.BlockSpec((1, seq, w), lambda b, h, i: (b, 0, h))
    return pl.pallas_call(
        functools.partial(_attn_kernel, tq=tq),
        grid=(bsz, n // w, seq // tq),
        in_specs=[qspec, kspec, kspec],
        out_specs=qspec,
        out_shape=jax.ShapeDtypeStruct((bsz, seq, n), BF16),
        compiler_params=_cparams("parallel", "parallel", "arbitrary"),
        name="attention",
    )(q, k, v)


def _combine_kernel(x_ref, y_ref, r_ref, k_ref, v_ref, g_ref, o_ref, gates_ref,
                    bd_ref, rk_ref, gnw_ref, gnb_ref, wa_ref, wb_ref, wo_ref, out_ref):
    bd = bd_ref[...]
    inv_n = 1.0 / RW_HEAD_DIM
    y = y_ref[...]
    v = v_ref[...]
    mean = _seg_sum(y, bd) * inv_n
    yc = y - mean
    var = _seg_sum(yc * yc, bd) * inv_n
    yn = yc * lax.rsqrt(var + RW_GN_EPS) * gnw_ref[...] + gnb_ref[...]
    bonus = _seg_sum(r_ref[...] * k_ref[...] * rk_ref[...], bd) * v
    ya = _dot(((yn + bonus) * g_ref[...]).astype(BF16), wa_ref[...])
    yb = _dot(o_ref[...], wb_ref[...])
    d = ya.shape[1]
    gates = gates_ref[...]
    mix = gates[:, :d] * ya + gates[:, d:] * yb
    out_ref[...] = x_ref[...] + _dot(mix.astype(BF16), wo_ref[...])


def combine(x, y, r, k, v, g, o, gates, bd, r_k, gn_w, gn_b, wa, wb, wo, *, tm):
    t, d = x.shape
    row = lambda a: a.reshape(1, -1)
    tok = lambda a: pl.BlockSpec((tm, a.shape[1]), lambda i: (i, 0))
    full = lambda a: pl.BlockSpec(a.shape, lambda i: (0,) * a.ndim)
    toks = (x, y, r, k, v, g, o, gates)
    consts = (bd, row(r_k), row(gn_w), row(gn_b), wa, wb, wo)
    return pl.pallas_call(
        _combine_kernel,
        grid=(t // tm,),
        in_specs=[tok(a) for a in toks] + [full(a) for a in consts],
        out_specs=pl.BlockSpec((tm, d), lambda i: (i, 0)),
        out_shape=jax.ShapeDtypeStruct((t, d), F32),
        compiler_params=_cparams("parallel"),
        name="combine",
    )(*toks, *consts)


NOT_RANKED = 127.0
STAIR = tuple(PEER_TOPK // (ii + 1) for ii in range(PEER_TOPK))


def _top_ranks(s, k):
    n, t = s.shape
    key = lax.broadcasted_iota(jnp.int32, (n, LANES), 0)
    slot = lax.broadcasted_iota(jnp.int32, (k, LANES), 0)

    def body(r, carry):
        s, rank, top = carry
        m = jnp.max(s, axis=0, keepdims=True)
        idx = jnp.min(jnp.where(s == m, key, n), axis=0, keepdims=True)
        hit = key == idx
        rank = jnp.where(hit, r.astype(F32), rank)
        s = jnp.where(hit, -jnp.inf, s)
        top = jnp.where(slot == r, m, top)
        return s, rank, top

    tops, ranks = [], []
    for c in range(t // LANES):
        init = (s[:, c * LANES:(c + 1) * LANES], jnp.full((n, LANES), NOT_RANKED, F32),
                jnp.zeros((k, LANES), F32))
        _, rank, top = lax.fori_loop(0, k, body, init)
        tops.append(top)
        ranks.append(rank)
    return jnp.concatenate(tops, axis=1), jnp.concatenate(ranks, axis=1)


def _peer_route_kernel(q_ref, keys_ref, lam_ref, cc_ref, rho_ref, e1_ref):
    tm = q_ref.shape[0]
    k = PEER_TOPK
    neg = -jnp.inf

    def head(h, _):
        col = pl.multiple_of(h * 2 * HALF_Q, 2 * HALF_Q)
        s0 = _dot_nt(keys_ref[h, 0], q_ref[:, pl.ds(col, HALF_Q)])
        s1 = _dot_nt(keys_ref[h, 1], q_ref[:, pl.ds(col + HALF_Q, HALF_Q)])
        top0, rank0 = _top_ranks(s0, k)
        top1, rank1 = _top_ranks(s1, k)

        row8 = lax.broadcasted_iota(jnp.int32, (SUBLANES, tm), 0)
        groups = [top0[0:1] + top1[0:8], top0[0:1] + top1[8:16], top0[1:2] + top1[0:8]]
        for ii in range(2, 8):
            groups.append(jnp.where(row8 < STAIR[ii], top0[ii:ii + 1] + top1[0:8], neg))
        groups.append(top0[8:16] + top1[0:1])
        cand = jnp.concatenate(groups, axis=0)
        _, crank = _top_ranks(cand, k)
        sel = crank < NOT_RANKED
        ex = jnp.where(sel, jnp.exp(cand - cand[0:1]), 0.0)
        z = jnp.sum(ex, axis=0, keepdims=True)
        self = sel.astype(F32)
        counts = [jnp.sum(self[0:16], axis=0, keepdims=True)]
        for g in range(2, 9):
            counts.append(jnp.sum(self[8 * g:8 * g + 8], axis=0, keepdims=True))
        lvec = jnp.concatenate(counts + [self[72:80]], axis=0)

        lam = jnp.zeros((N_KEYS, tm), F32)
        for ii in range(k):
            lam = jnp.where(rank0 == float(ii), lvec[ii:ii + 1], lam)
        lam_ref[h] = lam
        cc_ref[h] = jnp.exp(s0 - top0[0:1]) / z
        rho_ref[h] = rank1.astype(BF16)
        e1_ref[h] = jnp.exp(s1 - top1[0:1]).astype(BF16)
        return 0

    lax.fori_loop(0, PEER_HEADS, head, 0)


def peer_route(qp, keys, *, tm):
    t = qp.shape[0]
    out = jax.ShapeDtypeStruct((PEER_HEADS, N_KEYS, t), F32)
    out_b = jax.ShapeDtypeStruct((PEER_HEADS, N_KEYS, t), BF16)
    ospec = pl.BlockSpec((PEER_HEADS, N_KEYS, tm), lambda i: (0, 0, i))
    return pl.pallas_call(
        _peer_route_kernel,
        grid=(t // tm,),
        in_specs=[pl.BlockSpec((tm, qp.shape[1]), lambda i: (i, 0)),
                  pl.BlockSpec(keys.shape, lambda i: (0, 0, 0, 0))],
        out_specs=[ospec] * 4,
        out_shape=[out, out, out_b, out_b],
        compiler_params=_cparams("parallel"),
        name="peer_route",
    )(qp, keys)


PEER_ROWS = 8


def _erf(x):
    return lax.erf(x)


def _gelu(x):
    return 0.5 * x * (1.0 + _erf(x * (1.0 / math.sqrt(2.0))))


def _peer_expert_kernel(x_ref, gn_ref, u_ref, vt_ref, lam_ref, cc_ref, rho_ref, e1_ref, out_ref,
                        xt_ref, acc_ref, pre0_ref, pre1_ref):
    j = pl.program_id(1)

    @pl.when(j == 0)
    def _():
        xt_ref[...] = _rms(x_ref[...], gn_ref[...]).T.astype(BF16)
        acc_ref[...] = jnp.zeros_like(acc_ref)
        pre1_ref[...] = jnp.zeros_like(pre1_ref)

    def step(fill_ref, drain_ref):
        gs = []
        for ii in range(PEER_ROWS):
            rows = slice(ii * N_KEYS, (ii + 1) * N_KEYS)
            gate = jnp.zeros((N_KEYS, drain_ref.shape[1]), BF16)
            for h in range(PEER_HEADS):
                lam = lam_ref[h, ii:ii + 1, :].astype(BF16)
                cc = cc_ref[h, ii:ii + 1, :].astype(BF16)
                gate = gate + jnp.where(rho_ref[h] < lam, e1_ref[h] * cc, jnp.zeros((), BF16))
            gs.append(_gelu(drain_ref[rows, :]).astype(BF16) * gate)
        acc_ref[...] += _dot(vt_ref[...], jnp.concatenate(gs, axis=0))
        fill_ref[...] = _dot(u_ref[...], xt_ref[...])

    @pl.when(j % 2 == 0)
    def _():
        step(pre0_ref, pre1_ref)

    @pl.when(j % 2 == 1)
    def _():
        step(pre1_ref, pre0_ref)

    @pl.when(j == pl.num_programs(1) - 1)
    def _():
        out_ref[...] = acc_ref[...].T


def peer_expert(x, gain, u, vt, lam, cc, rho, e1, *, tm):
    t, d = x.shape
    nrow = PEER_ROWS * N_KEYS
    nblk = u.shape[0] // nrow
    fill = lambda j: jnp.minimum(j, nblk - 1)
    drain = lambda j: jnp.maximum(j - 1, 0)
    rspec = pl.BlockSpec((PEER_HEADS, PEER_ROWS, tm), lambda i, j: (0, drain(j), i))
    cspec = pl.BlockSpec((PEER_HEADS, N_KEYS, tm), lambda i, j: (0, 0, i))
    return pl.pallas_call(
        _peer_expert_kernel,
        grid=(t // tm, nblk + 1),
        in_specs=[pl.BlockSpec((tm, d), lambda i, j: (i, 0)),
                  pl.BlockSpec((1, d), lambda i, j: (0, 0)),
                  pl.BlockSpec((nrow, d), lambda i, j: (fill(j), 0)),
                  pl.BlockSpec((d, nrow), lambda i, j: (0, drain(j))),
                  rspec, rspec, cspec, cspec],
        out_specs=pl.BlockSpec((tm, d), lambda i, j: (i, 0)),
        out_shape=jax.ShapeDtypeStruct((t, d), F32),
        scratch_shapes=[pltpu.VMEM((d, tm), BF16), pltpu.VMEM((d, tm), F32),
                        pltpu.VMEM((nrow, tm), F32), pltpu.VMEM((nrow, tm), F32)],
        compiler_params=_cparams("parallel", "arbitrary"),
        name="peer_expert",
    )(x, gain.reshape(1, d), u, vt, lam, cc, rho, e1)


def _ple_final_kernel(x_ref, f_ref, p_ref, gp_ref, gf_ref, wg_ref, wp_ref, out_ref):
    x = x_ref[...] + f_ref[...]
    gate = _sigmoid(_dot(_rms(x, gp_ref[...]).astype(BF16), wg_ref[...]))
    x = x + gate * _dot(p_ref[...].astype(BF16), wp_ref[...])
    out_ref[...] = _rms(x, gf_ref[...])


def ple_final(x, f, p, g_ple, g_final, wg, wp, *, tm):
    t, d = x.shape
    tok = lambda a: pl.BlockSpec((tm, a.shape[1]), lambda i: (i, 0))
    full = lambda a: pl.BlockSpec(a.shape, lambda i: (0,) * a.ndim)
    consts = (g_ple.reshape(1, d), g_final.reshape(1, d), wg, wp)
    return pl.pallas_call(
        _ple_final_kernel,
        grid=(t // tm,),
        in_specs=[tok(x), tok(f), tok(p)] + [full(a) for a in consts],
        out_specs=tok(x),
        out_shape=jax.ShapeDtypeStruct((t, d), F32),
        compiler_params=_cparams("parallel"),
        name="ple_final",
    )(x, f, p, *consts)


def _place(cols, width, offset):
    return jnp.pad(cols, ((0, 0), (offset, width - offset - cols.shape[1])))


def _rw_in_weights(w_rw, mu):
    o3 = 3 * RW_DIM
    lw, la = 64, 64
    segs = [w_rw[:, :o3], _place(w_rw[:, o3:o3 + lw], LANES, 0),
            _place(w_rw[:, o3 + lw:o3 + lw + la], LANES, 0), w_rw[:, o3 + lw + la:]]
    mus = [mu[None, :o3], _place(mu[None, o3:o3 + lw], LANES, 0),
           _place(mu[None, o3 + lw:o3 + lw + la], LANES, 0), mu[None, o3 + lw + la:]]
    return jnp.concatenate(segs, axis=1), jnp.concatenate(mus, axis=1)[0]


def _mla_in_weights(w_mla):
    half = QK_ROPE // 2
    lat = Q_LORA + KV_LORA
    kr = w_mla[:, lat:]
    kr_sw = jnp.concatenate([-kr[:, half:], kr[:, :half]], axis=1)
    return jnp.concatenate([w_mla[:, :lat], _place(kr, LANES, QK_NOPE), _place(kr_sw, LANES, QK_NOPE),
                            jnp.zeros((w_mla.shape[0], LANES), w_mla.dtype)], axis=1)


def _mla_up_weights(w_uq, w_ukv):
    half = QK_ROPE // 2
    qd = QK_NOPE + QK_ROPE
    wq = w_uq.reshape(Q_LORA, MLA_HEADS, qd)
    rope = wq[:, :, QK_NOPE:]
    rope_sw = jnp.concatenate([-rope[:, :, half:], rope[:, :, :half]], axis=2)
    pad = lambda t, off: jnp.pad(t, ((0, 0), (0, 0), (off, LANES - off - t.shape[2])))
    wq_pad = pad(wq, 0).reshape(Q_LORA, MLA_HEADS * LANES)
    wq_sw = pad(rope_sw, QK_NOPE).reshape(Q_LORA, MLA_HEADS * LANES)
    lane = jnp.arange(MLA_HEADS * LANES) % LANES
    wk = jnp.where(lane < QK_NOPE, w_ukv, 0.0)
    wv = jnp.where(lane < QK_NOPE, 0.0, w_ukv)
    return wq_pad, wq_sw, wk, wv


def kernel(x, p, positions, norm_mix, w_in, rw_mu, rw_w0, rw_w2, rw_a0, rw_a2, rw_g2, rw_k_k, rw_k_a, rw_r_k, rw_gn_w, rw_gn_b, rw_w_o, mla_q_norm, mla_w_uq, mla_kv_norm, mla_w_ukv, mla_w_o, w_out, norm_ffn, peer_w_q, peer_sub_keys, peer_u, peer_v, norm_ple, ple_w_gate, ple_w_proj, norm_final):
    bsz, seq, d = x.shape
    t = bsz * seq
    depth = p.shape[0]
    bf = lambda a: a.astype(BF16)
    rw_cols = 3 * RW_DIM + 64 + 64 + 128
    mla_cols = Q_LORA + KV_LORA + QK_ROPE

    head_of = jnp.arange(RW_DIM) // RW_HEAD_DIM
    bd = bf(head_of[:, None] == head_of[None, :])
    inv_freq = ROPE_THETA ** (-jnp.arange(0, QK_ROPE, 2, dtype=F32) / QK_ROPE)
    f_lane = _place(jnp.concatenate([inv_freq, inv_freq])[None, :], LANES, QK_NOPE)
    pos = positions.reshape(t, 1)

    xf = x.reshape(t, d)
    assert depth == 1, "the final RMSNorm is fused into the layer's last kernel"
    for i in range(depth):
        w_rw, mu = _rw_in_weights(w_in[i][:, :rw_cols], rw_mu[i])
        w_mla = _mla_in_weights(w_in[i][:, rw_cols:rw_cols + mla_cols])
        w_gates = w_in[i][:, rw_cols + mla_cols:]
        z_rw = norm_matmul(xf, norm_mix[i], bf(w_rw), tm=1024, tn=RW_ZCOLS // 3)
        z_mla = norm_matmul(xf, norm_mix[i], bf(w_mla), tm=1024, tn=MLA_ZCOLS)
        gates = norm_matmul(xf, norm_mix[i], bf(w_gates), tm=1024, tn=1024, act="sigmoid")

        pad_rows = lambda w: jnp.pad(w, ((0, LANES - w.shape[0]), (0, 0)))
        r, wl, k, v, a, b, g = rwkv_prep(
            z_rw.reshape(bsz, seq, RW_ZCOLS), mu, rw_w0[i], pad_rows(rw_w2[i]), rw_a0[i],
            pad_rows(rw_a2[i]), rw_g2[i], rw_k_k[i], rw_k_a[i], bd, tm=256)
        y = rwkv_scan(r, wl, k, v, a, b)

        wq, wqs, wk, wv = _mla_up_weights(mla_w_uq[i], mla_w_ukv[i])
        q, kk, vv = mla_prep(z_mla, pos, f_lane, mla_q_norm[i], mla_kv_norm[i],
                             bf(wq), bf(wqs), bf(wk), bf(wv), tm=512)
        n = MLA_HEADS * LANES
        o = attention(q.reshape(bsz, seq, n), kk.reshape(bsz, seq, n), vv.reshape(bsz, seq, n), tq=256)
        wo_pad = jnp.pad(mla_w_o[i].reshape(MLA_HEADS, V_HEAD, d),
                         ((0, 0), (LANES - V_HEAD, 0), (0, 0))).reshape(n, d)

        flat = lambda a: a.reshape(t, -1)
        x1 = combine(xf, flat(y), flat(r), flat(k), flat(v), flat(g), o.reshape(t, n), gates,
                     bd, rw_r_k[i].reshape(-1), rw_gn_w[i], rw_gn_b[i],
                     bf(rw_w_o[i]), bf(wo_pad), bf(w_out[i]), tm=256)

        qp = norm_matmul(x1, norm_ffn[i], bf(peer_w_q[i]), tm=1024, tn=1024, out_dtype=BF16)
        lam, cc, rho, e1 = peer_route(qp, bf(peer_sub_keys[i]), tm=256)
        ffn = peer_expert(x1, norm_ffn[i], bf(peer_u[i]), bf(peer_v[i].T), lam, cc, rho, e1, tm=512)

        xf = ple_final(x1, ffn, p[i].reshape(t, -1), norm_ple[i], norm_final,
                       bf(ple_w_gate[i]), bf(ple_w_proj[i]), tm=256)
    return xf.reshape(bsz, seq, d)
```

```python
import functools
import math

import jax
import jax.numpy as jnp
from jax import lax
from jax.experimental import pallas as pl
from jax.experimental.pallas import tpu as pltpu

F32 = jnp.float32
BF16 = jnp.bfloat16
HIGHEST = lax.Precision.HIGHEST

LANES = 128
SUBLANES = 8
VMEM_LIMIT = 56 * 1024 * 1024

EPS = 1e-6
RW_HEADS = 8
RW_HEAD_DIM = 64
RW_DIM = RW_HEADS * RW_HEAD_DIM
RW_GN_EPS = 64e-5
SCAN_CHUNK = 64

MLA_HEADS = 8
QK_NOPE = 64
QK_ROPE = 32
V_HEAD = 64
Q_LORA = 384
KV_LORA = 256
ROPE_THETA = 10000.0
MASK_CHUNK = 64
NEG_INF = -1e30

PEER_HEADS = 8
N_KEYS = 128
PEER_TOPK = 16
HALF_Q = 128


def _cparams(*sem):
    return pltpu.CompilerParams(dimension_semantics=sem, vmem_limit_bytes=VMEM_LIMIT)


def _dot(a, b, precision=None):
    return jnp.dot(a, b, preferred_element_type=F32, precision=precision)


def _dot_nt(a, b, precision=None):
    return lax.dot_general(a, b, (((1,), (1,)), ((), ())),
                           preferred_element_type=F32, precision=precision)


def _rms(x, gain):
    return x * lax.rsqrt(jnp.mean(x * x, axis=-1, keepdims=True) + EPS) * gain


def _sigmoid(x):
    return 1.0 / (1.0 + jnp.exp(-x))


def _split3(x):
    hi = x.astype(BF16)
    r1 = x - hi.astype(F32)
    mid = r1.astype(BF16)
    lo = (r1 - mid.astype(F32)).astype(BF16)
    return hi, mid, lo


def _seg_sum(x, bd):
    hi, mid, lo = _split3(x)
    return _dot(hi, bd) + _dot(mid, bd) + _dot(lo, bd)


def _norm_matmul_kernel(x_ref, g_ref, w_ref, o_ref, h_ref, *, act):
    @pl.when(pl.program_id(1) == 0)
    def _():
        h_ref[...] = _rms(x_ref[...], g_ref[...]).astype(BF16)

    y = _dot(h_ref[...], w_ref[...])
    if act == "sigmoid":
        y = _sigmoid(y)
    o_ref[...] = y.astype(o_ref.dtype)


def norm_matmul(x, gain, w, *, tm, tn, act=None, out_dtype=F32):
    t, d = x.shape
    n = w.shape[1]
    return pl.pallas_call(
        functools.partial(_norm_matmul_kernel, act=act),
        grid=(t // tm, n // tn),
        in_specs=[pl.BlockSpec((tm, d), lambda i, j: (i, 0)),
                  pl.BlockSpec((1, d), lambda i, j: (0, 0)),
                  pl.BlockSpec((d, tn), lambda i, j: (0, j))],
        out_specs=pl.BlockSpec((tm, tn), lambda i, j: (i, j)),
        out_shape=jax.ShapeDtypeStruct((t, n), out_dtype),
        scratch_shapes=[pltpu.VMEM((tm, d), BF16)],
        compiler_params=_cparams("parallel", "arbitrary"),
        name="norm_matmul",
    )(x, gain.reshape(1, d), w)


RW_ZCOLS = 3 * RW_DIM + 3 * LANES


def _rwkv_prep_kernel(z_ref, zp_ref, mu_ref, w0_ref, w2_ref, a0_ref, a2_ref, g2_ref,
                      kk_ref, ka_ref, bd_ref,
                      r_ref, wl_ref, k_ref, v_ref, a_ref, b_ref, g_ref):
    z = z_ref[0]
    tm = z.shape[0]
    prev_last = zp_ref[0][SUBLANES - 1:SUBLANES, :]
    prev_last = jnp.where(pl.program_id(1) == 0, 0.0, prev_last)
    rolled = pltpu.roll(z, 1, 0)
    row = lax.broadcasted_iota(jnp.int32, (tm, 1), 0)
    z_prev = jnp.where(row == 0, prev_last, rolled)
    z = z + mu_ref[...] * (z_prev - z)

    o1, o2, o3 = RW_DIM, 2 * RW_DIM, 3 * RW_DIM
    r, k, v = z[:, :o1], z[:, o1:o2], z[:, o2:o3]
    zw, za, zg = z[:, o3:o3 + LANES], z[:, o3 + LANES:o3 + 2 * LANES], z[:, o3 + 2 * LANES:]

    wpre = w0_ref[...] + _dot(jnp.tanh(zw), w2_ref[...])
    nx = -wpre
    softplus = jnp.maximum(nx, 0.0) + jnp.log(1.0 + jnp.exp(-jnp.abs(nx)))
    w = -softplus - 0.5
    iclr = _sigmoid(a0_ref[...] + _dot(za, a2_ref[...]))
    g = _dot(_sigmoid(zg), g2_ref[...])

    kk = k * kk_ref[...]
    ss = _seg_sum(kk * kk, bd_ref[...])
    kk = kk / jnp.maximum(jnp.sqrt(ss), 1e-12)

    r_ref[0] = r
    wl_ref[0] = -jnp.exp(w)
    k_ref[0] = k * (1.0 + (iclr - 1.0) * ka_ref[...])
    v_ref[0] = v
    a_ref[0] = -kk
    b_ref[0] = kk * iclr
    g_ref[0] = g


def rwkv_prep(z, mu, w0, w2, a0, a2, g2, k_k, k_a, bd, *, tm):
    bsz, seq, zc = z.shape
    d = RW_DIM
    row = lambda a: a.reshape(1, -1)
    full = lambda a: pl.BlockSpec(a.shape, lambda b, i: (0,) * a.ndim)
    args = (row(mu), row(w0), w2, row(a0), a2, g2, row(k_k), row(k_a), bd)
    out = jax.ShapeDtypeStruct((bsz, seq, d), F32)
    ospec = pl.BlockSpec((1, tm, d), lambda b, i: (b, i, 0))
    return pl.pallas_call(
        _rwkv_prep_kernel,
        grid=(bsz, seq // tm),
        in_specs=[pl.BlockSpec((1, tm, zc), lambda b, i: (b, i, 0)),
                  pl.BlockSpec((1, SUBLANES, zc),
                               lambda b, i: (b, jnp.maximum(i * (tm // SUBLANES) - 1, 0), 0))]
                 + [full(a) for a in args],
        out_specs=[ospec] * 7,
        out_shape=[out] * 7,
        compiler_params=_cparams("parallel", "arbitrary"),
        name="rwkv_prep",
    )(z, z, *args)


def _rwkv_scan_kernel(r_ref, wl_ref, k_ref, v_ref, a_ref, b_ref, y_ref, g_ref):
    c = SCAN_CHUNK
    hd = RW_HEAD_DIM
    npair = g_ref.shape[0]

    @pl.when(pl.program_id(1) == 0)
    def _():
        g_ref[...] = jnp.zeros_like(g_ref)

    ri = lax.broadcasted_iota(jnp.int32, (c, c), 0)
    ci = lax.broadcasted_iota(jnp.int32, (c, c), 1)
    tril = (ri >= ci).astype(F32)
    lane = lax.broadcasted_iota(jnp.int32, (c, 2 * hd), 1)
    m0 = lane < hd
    r2 = lax.broadcasted_iota(jnp.int32, (2 * c, 2 * c), 0)
    c2 = lax.broadcasted_iota(jnp.int32, (2 * c, 2 * c), 1)
    same = (r2 >= c) == (c2 >= c)
    strict = jnp.logical_and(same, r2 > c2)
    incl = jnp.logical_and(same, r2 >= c2)
    eye = (r2 == c2).astype(F32)

    bf = lambda x: x.astype(BF16)
    stack = lambda x: bf(jnp.concatenate([jnp.where(m0, x, 0.0), jnp.where(m0, 0.0, x)], axis=0))
    twice = lambda x: bf(jnp.concatenate([x, x], axis=0))
    pick = lambda s: jnp.where(m0, s[:c], s[c:])

    pairs = range(npair)
    each = lambda f, *cols: [f(*args) for args in zip(*cols)]
    per_row = r_ref.shape[2] // LANES
    where = [(hp // per_row, slice((hp % per_row) * LANES, (hp % per_row + 1) * LANES)) for hp in pairs]
    load = lambda ref: [ref[bi, :, sl] for bi, sl in where]
    r, wl, k, v, a, b = (load(x) for x in (r_ref, wl_ref, k_ref, v_ref, a_ref, b_ref))

    cs = each(lambda w: _dot(tril, w, HIGHEST), wl)
    cs_last = each(lambda s: s[c - 1:c, :], cs)
    p_inv = each(lambda s: jnp.exp(-s), cs)
    at = each(lambda a, s, w: a * jnp.exp(s - w), a, cs, wl)
    rt = each(lambda r, s: r * jnp.exp(s), r, cs)
    at_s, rt_s = each(stack, at), each(stack, rt)
    bt_s = each(lambda b, p: stack(b * p), b, p_inv)
    kt_s = each(lambda k, p: stack(k * p), k, p_inv)

    ab = each(lambda x, y: jnp.where(strict, _dot_nt(x, y), 0.0), at_s, bt_s)
    ak = each(lambda x, y: jnp.where(strict, _dot_nt(x, y), 0.0), at_s, kt_s)
    rb = each(lambda x, y: jnp.where(incl, _dot_nt(x, y), 0.0), rt_s, bt_s)
    rk = each(lambda x, y: jnp.where(incl, _dot_nt(x, y), 0.0), rt_s, kt_s)

    tinv = each(lambda m: eye + m, ab)
    x = each(bf, ab)
    for _ in range(int(math.log2(c)) - 1):
        x = each(lambda m: bf(_dot(m, m)), x)
        tinv = each(lambda t, m: t + _dot(bf(t), m), tinv, x)

    gt = [g_ref[hp] for hp in pairs]
    gtb = each(bf, gt)
    vv = each(twice, v)
    rhs = each(lambda at, g, ak, vv: _dot_nt(bf(at), g) + pick(_dot(bf(ak), vv)), at, gtb, ak, vv)
    u = each(lambda t, x: pick(_dot(bf(t), twice(x))), tinv, rhs)
    y = each(lambda rt, g, rb, u, rk, vv:
             _dot_nt(bf(rt), g) + pick(_dot(bf(rb), twice(u)) + _dot(bf(rk), vv)),
             rt, gtb, rb, u, rk, vv)
    upd = each(lambda u, b, v, k, s, sl:
               _dot(bf(u.T), bf(b * jnp.exp(sl - s))) + _dot(bf(v.T), bf(k * jnp.exp(sl - s))),
               u, b, v, k, cs, cs_last)
    for hp in pairs:
        bi, sl = where[hp]
        y_ref[bi, :, sl] = y[hp]
        g_ref[hp] = gt[hp] * jnp.exp(cs_last[hp]) + jnp.where(same, upd[hp], 0.0)


def rwkv_scan(r, wl, k, v, a, b, *, rows):
    bsz, seq, d = r.shape
    c = SCAN_CHUNK
    spec = pl.BlockSpec((rows, c, d), lambda bi, ci: (bi, ci, 0))
    return pl.pallas_call(
        _rwkv_scan_kernel,
        grid=(bsz // rows, seq // c),
        in_specs=[spec] * 6,
        out_specs=spec,
        out_shape=jax.ShapeDtypeStruct((bsz, seq, d), F32),
        scratch_shapes=[pltpu.VMEM((rows * d // LANES, LANES, LANES), F32)],
        compiler_params=_cparams("parallel", "arbitrary"),
        name="rwkv_scan",
    )(r, wl, k, v, a, b)


MLA_ZCOLS = 1024
MLA_SCALE = 1.0 / math.sqrt(QK_NOPE + QK_ROPE)


def _mla_prep_kernel(z_ref, pos_ref, fl_ref, qn_ref, kvn_ref, wq_ref, wqs_ref, wk_ref, wv_ref,
                     q_ref, k_ref, v_ref):
    z = z_ref[...]
    c_q = _rms(z[:, :Q_LORA], qn_ref[...]).astype(BF16)
    c_kv = _rms(z[:, Q_LORA:Q_LORA + KV_LORA], kvn_ref[...]).astype(BF16)
    kr = z[:, Q_LORA + KV_LORA:Q_LORA + KV_LORA + LANES]
    krs = z[:, Q_LORA + KV_LORA + LANES:Q_LORA + KV_LORA + 2 * LANES]

    ang = pos_ref[...].astype(F32) * fl_ref[...]
    cos, sin = jnp.cos(ang), jnp.sin(ang)
    kr_rot = kr * cos + krs * sin

    q = _dot(c_q, wq_ref[...])
    qs = _dot(c_q, wqs_ref[...])
    kn = _dot(c_kv, wk_ref[...])
    v_ref[...] = _dot(c_kv, wv_ref[...]).astype(BF16)
    for h in range(MLA_HEADS):
        sl = slice(h * LANES, (h + 1) * LANES)
        q_ref[:, sl] = ((q[:, sl] * cos + qs[:, sl] * sin) * MLA_SCALE).astype(BF16)
        k_ref[:, sl] = (kn[:, sl] + kr_rot).astype(BF16)


def mla_prep(z, pos, f_lane, q_norm, kv_norm, wq, wqs, wk, wv, *, tm):
    t = z.shape[0]
    n = MLA_HEADS * LANES
    full = lambda a: pl.BlockSpec(a.shape, lambda i: (0,) * a.ndim)
    args = (f_lane, q_norm.reshape(1, -1), kv_norm.reshape(1, -1), wq, wqs, wk, wv)
    out = jax.ShapeDtypeStruct((t, n), BF16)
    ospec = pl.BlockSpec((tm, n), lambda i: (i, 0))
    return pl.pallas_call(
        _mla_prep_kernel,
        grid=(t // tm,),
        in_specs=[pl.BlockSpec((tm, MLA_ZCOLS), lambda i: (i, 0)),
                  pl.BlockSpec((tm, 1), lambda i: (i, 0))] + [full(a) for a in args],
        out_specs=[ospec] * 3,
        out_shape=[out] * 3,
        compiler_params=_cparams("parallel"),
        name="mla_prep",
    )(z, pos, *args)


ATTN_HEADS_PER_STEP = 4


def _attn_kernel(q_ref, k_ref, v_ref, o_ref, *, tq):
    iq = pl.program_id(2)
    nh = q_ref.shape[2] // LANES
    heads = [slice(h * LANES, (h + 1) * LANES) for h in range(nh)]
    qs = [q_ref[0, :, sl] for sl in heads]

    def tile(j, carry, masked):
        start = pl.multiple_of(j * tq, tq)
        if masked:
            qc = lax.broadcasted_iota(jnp.int32, (tq, tq), 0) // MASK_CHUNK
            kc = lax.broadcasted_iota(jnp.int32, (tq, tq), 1) // MASK_CHUNK
            keep = kc <= qc
        out = []
        for h, sl in enumerate(heads):
            m, l, acc = carry[h]
            s = _dot_nt(qs[h], k_ref[0, pl.ds(start, tq), sl])
            if masked:
                s = jnp.where(keep, s, NEG_INF)
            m_new = jnp.maximum(m, jnp.max(s, axis=-1, keepdims=True))
            alpha = jnp.exp(m - m_new)
            p = jnp.exp(s - m_new)
            l = alpha * l + jnp.sum(p, axis=-1, keepdims=True)
            acc = alpha * acc + _dot(p.astype(BF16), v_ref[0, pl.ds(start, tq), sl])
            out.append((m_new, l, acc))
        return tuple(out)

    init = tuple((jnp.full((tq, 1), NEG_INF, F32), jnp.zeros((tq, 1), F32), jnp.zeros((tq, LANES), F32))
                 for _ in heads)
    carry = lax.fori_loop(0, iq, lambda j, c: tile(j, c, False), init)
    carry = tile(iq, carry, True)
    for h, sl in enumerate(heads):
        m, l, acc = carry[h]
        o_ref[0, :, sl] = (acc / l).astype(BF16)


def attention(q, k, v, *, tq):
    bsz, seq, n = q.shape
    w = ATTN_HEADS_PER_STEP * LANES
    qspec = pl.BlockSpec((1, tq, w), lambda b, h, i: (b, i, h))
    kspec = pl.BlockSpec((1, seq, w), lambda b, h, i: (b, 0, h))
    return pl.pallas_call(
        functools.partial(_attn_kernel, tq=tq),
        grid=(bsz, n // w, seq // tq),
        in_specs=[qspec, kspec, kspec],
        out_specs=qspec,
        out_shape=jax.ShapeDtypeStruct((bsz, seq, n), BF16),
        compiler_params=_cparams("parallel", "parallel", "arbitrary"),
        name="attention",
    )(q, k, v)


def _combine_kernel(x_ref, y_ref, r_ref, k_ref, v_ref, g_ref, o_ref, gates_ref,
                    bd_ref, rk_ref, gnw_ref, gnb_ref, wa_ref, wb_ref, wo_ref, out_ref):
    bd = bd_ref[...]
    inv_n = 1.0 / RW_HEAD_DIM
    y = y_ref[...]
    v = v_ref[...]
    mean = _seg_sum(y, bd) * inv_n
    yc = y - mean
    var = _seg_sum(yc * yc, bd) * inv_n
    yn = yc * lax.rsqrt(var + RW_GN_EPS) * gnw_ref[...] + gnb_ref[...]
    bonus = _seg_sum(r_ref[...] * k_ref[...] * rk_ref[...], bd) * v
    ya = _dot(((yn + bonus) * g_ref[...]).astype(BF16), wa_ref[...])
    yb = _dot(o_ref[...], wb_ref[...])
    d = ya.shape[1]
    gates = gates_ref[...]
    mix = gates[:, :d] * ya + gates[:, d:] * yb
    out_ref[...] = x_ref[...] + _dot(mix.astype(BF16), wo_ref[...])


def combine(x, y, r, k, v, g, o, gates, bd, r_k, gn_w, gn_b, wa, wb, wo, *, tm):
    t, d = x.shape
    row = lambda a: a.reshape(1, -1)
    tok = lambda a: pl.BlockSpec((tm, a.shape[1]), lambda i: (i, 0))
    full = lambda a: pl.BlockSpec(a.shape, lambda i: (0,) * a.ndim)
    toks = (x, y, r, k, v, g, o, gates)
    consts = (bd, row(r_k), row(gn_w), row(gn_b), wa, wb, wo)
    return pl.pallas_call(
        _combine_kernel,
        grid=(t // tm,),
        in_specs=[tok(a) for a in toks] + [full(a) for a in consts],
        out_specs=pl.BlockSpec((tm, d), lambda i: (i, 0)),
        out_shape=jax.ShapeDtypeStruct((t, d), F32),
        compiler_params=_cparams("parallel"),
        name="combine",
    )(*toks, *consts)


NOT_RANKED = 127.0
STAIR = tuple(PEER_TOPK // (ii + 1) for ii in range(PEER_TOPK))


def _top_ranks(s, k, exact_ties):
    n, t = s.shape
    key = lax.broadcasted_iota(jnp.int32, (n, LANES), 0)
    slot = lax.broadcasted_iota(jnp.int32, (k, LANES), 0)

    def body(r, carry):
        s, rank, top = carry
        m = jnp.max(s, axis=0, keepdims=True)
        hit = s == m
        if exact_ties:
            hit = key == jnp.min(jnp.where(hit, key, n), axis=0, keepdims=True)
        rank = jnp.where(hit, jnp.asarray(r, F32), rank)
        s = jnp.where(hit, -jnp.inf, s)
        top = jnp.where(slot == r, m, top)
        return s, rank, top

    tops, ranks = [], []
    for c in range(t // LANES):
        init = (s[:, c * LANES:(c + 1) * LANES], jnp.full((n, LANES), NOT_RANKED, F32),
                jnp.zeros((k, LANES), F32))
        _, rank, top = lax.fori_loop(0, k, body, init)
        tops.append(top)
        ranks.append(rank)
    return jnp.concatenate(tops, axis=1), jnp.concatenate(ranks, axis=1)


def _ranked_count_ok(rank, k):
    count = jnp.sum((rank < NOT_RANKED).astype(F32), axis=0, keepdims=True)
    return jnp.max(jnp.abs(count - k)) == 0.0


def _peer_route_kernel(q_ref, keys_ref, lam_ref, cc_ref, rho_ref, e1_ref):
    tm = q_ref.shape[0]
    k = PEER_TOPK
    neg = -jnp.inf

    def head(h, _):
        col = pl.multiple_of(h * 2 * HALF_Q, 2 * HALF_Q)
        s0 = _dot_nt(keys_ref[h, 0], q_ref[:, pl.ds(col, HALF_Q)])
        s1 = _dot_nt(keys_ref[h, 1], q_ref[:, pl.ds(col + HALF_Q, HALF_Q)])
        def rank_all(exact_ties):
            top0, rank0 = _top_ranks(s0, k, exact_ties)
            top1, rank1 = _top_ranks(s1, k, exact_ties)
            row8 = lax.broadcasted_iota(jnp.int32, (SUBLANES, tm), 0)
            groups = [top0[0:1] + top1[0:8], top0[0:1] + top1[8:16], top0[1:2] + top1[0:8]]
            for ii in range(2, 8):
                groups.append(jnp.where(row8 < STAIR[ii], top0[ii:ii + 1] + top1[0:8], neg))
            groups.append(top0[8:16] + top1[0:1])
            cand = jnp.concatenate(groups, axis=0)
            _, crank = _top_ranks(cand, k, exact_ties)
            return top0, rank0, top1, rank1, cand, crank

        quick = rank_all(False)
        tie_free = jnp.logical_and(
            jnp.logical_and(_ranked_count_ok(quick[1], k), _ranked_count_ok(quick[3], k)),
            _ranked_count_ok(quick[5], k))
        top0, rank0, top1, rank1, cand, crank = lax.cond(
            tie_free, lambda: quick, lambda: rank_all(True))
        sel = crank < NOT_RANKED
        ex = jnp.where(sel, jnp.exp(cand - cand[0:1]), 0.0)
        z = jnp.sum(ex, axis=0, keepdims=True)
        self = sel.astype(F32)
        counts = [jnp.sum(self[0:16], axis=0, keepdims=True)]
        for g in range(2, 9):
            counts.append(jnp.sum(self[8 * g:8 * g + 8], axis=0, keepdims=True))
        lvec = jnp.concatenate(counts + [self[72:80]], axis=0)

        lam = jnp.zeros((N_KEYS, tm), F32)
        for ii in range(k):
            lam = jnp.where(rank0 == float(ii), lvec[ii:ii + 1], lam)
        lam_ref[h] = lam
        cc_ref[h] = jnp.exp(s0 - top0[0:1]) / z
        rho_ref[h] = rank1.astype(BF16)
        e1_ref[h] = jnp.exp(s1 - top1[0:1]).astype(BF16)
        return 0

    lax.fori_loop(0, PEER_HEADS, head, 0)


def peer_route(qp, keys, *, tm):
    t = qp.shape[0]
    out = jax.ShapeDtypeStruct((PEER_HEADS, N_KEYS, t), F32)
    out_b = jax.ShapeDtypeStruct((PEER_HEADS, N_KEYS, t), BF16)
    ospec = pl.BlockSpec((PEER_HEADS, N_KEYS, tm), lambda i: (0, 0, i))
    return pl.pallas_call(
        _peer_route_kernel,
        grid=(t // tm,),
        in_specs=[pl.BlockSpec((tm, qp.shape[1]), lambda i: (i, 0)),
                  pl.BlockSpec(keys.shape, lambda i: (0, 0, 0, 0))],
        out_specs=[ospec] * 4,
        out_shape=[out, out, out_b, out_b],
        compiler_params=_cparams("parallel"),
        name="peer_route",
    )(qp, keys)


PEER_ROWS = 8
PEER_TOKEN_CHUNK = 256


def _erf(x):
    return lax.erf(x)


def _gelu(x):
    return 0.5 * x * (1.0 + _erf(x * (1.0 / math.sqrt(2.0))))


def _peer_expert_kernel(x_ref, gn_ref, u_ref, vt_ref, lam_ref, cc_ref, rho_ref, e1_ref, out_ref,
                        xt_ref, acc_ref, pre0_ref, pre1_ref):
    j = pl.program_id(1)
    tm = xt_ref.shape[1]
    chunks = [slice(c, c + PEER_TOKEN_CHUNK) for c in range(0, tm, PEER_TOKEN_CHUNK)]

    @pl.when(j == 0)
    def _():
        xt_ref[...] = _rms(x_ref[...], gn_ref[...]).T.astype(BF16)
        acc_ref[...] = jnp.zeros_like(acc_ref)
        pre1_ref[...] = jnp.zeros_like(pre1_ref)

    def step(fill_ref, drain_ref):
        for cols in chunks:
            fill_ref[:, cols] = _dot(u_ref[...], xt_ref[:, cols])
        for cols in chunks:
            gs = []
            for ii in range(PEER_ROWS):
                rows = slice(ii * N_KEYS, (ii + 1) * N_KEYS)
                gate = jnp.zeros((N_KEYS, PEER_TOKEN_CHUNK), BF16)
                for h in range(PEER_HEADS):
                    lam = lam_ref[h, ii:ii + 1, cols].astype(BF16)
                    cc = cc_ref[h, ii:ii + 1, cols].astype(BF16)
                    gate = gate + jnp.where(rho_ref[h, :, cols] < lam, e1_ref[h, :, cols] * cc,
                                            jnp.zeros((), BF16))
                gs.append(_gelu(drain_ref[rows, cols]).astype(BF16) * gate)
            acc_ref[:, cols] += _dot(vt_ref[...], jnp.concatenate(gs, axis=0))

    @pl.when(j % 2 == 0)
    def _():
        step(pre0_ref, pre1_ref)

    @pl.when(j % 2 == 1)
    def _():
        step(pre1_ref, pre0_ref)

    @pl.when(j == pl.num_programs(1) - 1)
    def _():
        out_ref[...] = acc_ref[...].T


def peer_expert(x, gain, u, vt, lam, cc, rho, e1, *, tm):
    t, d = x.shape
    nrow = PEER_ROWS * N_KEYS
    nblk = u.shape[0] // nrow
    stage = lambda lag: (lambda j: jnp.clip(j - lag, 0, nblk - 1))
    rspec = pl.BlockSpec((PEER_HEADS, PEER_ROWS, tm), lambda i, j: (0, stage(1)(j), i))
    cspec = pl.BlockSpec((PEER_HEADS, N_KEYS, tm), lambda i, j: (0, 0, i))
    return pl.pallas_call(
        _peer_expert_kernel,
        grid=(t // tm, nblk + 1),
        in_specs=[pl.BlockSpec((tm, d), lambda i, j: (i, 0)),
                  pl.BlockSpec((1, d), lambda i, j: (0, 0)),
                  pl.BlockSpec((nrow, d), lambda i, j: (stage(0)(j), 0)),
                  pl.BlockSpec((d, nrow), lambda i, j: (0, stage(1)(j))),
                  rspec, rspec, cspec, cspec],
        out_specs=pl.BlockSpec((tm, d), lambda i, j: (i, 0)),
        out_shape=jax.ShapeDtypeStruct((t, d), F32),
        scratch_shapes=[pltpu.VMEM((d, tm), BF16), pltpu.VMEM((d, tm), F32),
                        pltpu.VMEM((nrow, tm), F32), pltpu.VMEM((nrow, tm), F32)],
        compiler_params=_cparams("parallel", "arbitrary"),
        name="peer_expert",
    )(x, gain.reshape(1, d), u, vt, lam, cc, rho, e1)


def _ple_final_kernel(x_ref, f_ref, p_ref, gp_ref, gf_ref, wg_ref, wp_ref, out_ref):
    x = x_ref[...] + f_ref[...]
    gate = _sigmoid(_dot(_rms(x, gp_ref[...]).astype(BF16), wg_ref[...]))
    x = x + gate * _dot(p_ref[...].astype(BF16), wp_ref[...])
    out_ref[...] = _rms(x, gf_ref[...])


def ple_final(x, f, p, g_ple, g_final, wg, wp, *, tm):
    t, d = x.shape
    tok = lambda a: pl.BlockSpec((tm, a.shape[1]), lambda i: (i, 0))
    full = lambda a: pl.BlockSpec(a.shape, lambda i: (0,) * a.ndim)
    consts = (g_ple.reshape(1, d), g_final.reshape(1, d), wg, wp)
    return pl.pallas_call(
        _ple_final_kernel,
        grid=(t // tm,),
        in_specs=[tok(x), tok(f), tok(p)] + [full(a) for a in consts],
        out_specs=tok(x),
        out_shape=jax.ShapeDtypeStruct((t, d), F32),
        compiler_params=_cparams("parallel"),
        name="ple_final",
    )(x, f, p, *consts)


def _place(cols, width, offset):
    return jnp.pad(cols, ((0, 0), (offset, width - offset - cols.shape[1])))


def _rw_in_weights(w_rw, mu):
    o3 = 3 * RW_DIM
    lw, la = 64, 64
    segs = [w_rw[:, :o3], _place(w_rw[:, o3:o3 + lw], LANES, 0),
            _place(w_rw[:, o3 + lw:o3 + lw + la], LANES, 0), w_rw[:, o3 + lw + la:]]
    mus = [mu[None, :o3], _place(mu[None, o3:o3 + lw], LANES, 0),
           _place(mu[None, o3 + lw:o3 + lw + la], LANES, 0), mu[None, o3 + lw + la:]]
    return jnp.concatenate(segs, axis=1), jnp.concatenate(mus, axis=1)[0]


def _mla_in_weights(w_mla):
    half = QK_ROPE // 2
    lat = Q_LORA + KV_LORA
    kr = w_mla[:, lat:]
    kr_sw = jnp.concatenate([-kr[:, half:], kr[:, :half]], axis=1)
    return jnp.concatenate([w_mla[:, :lat], _place(kr, LANES, QK_NOPE), _place(kr_sw, LANES, QK_NOPE),
                            jnp.zeros((w_mla.shape[0], LANES), w_mla.dtype)], axis=1)


def _mla_up_weights(w_uq, w_ukv):
    half = QK_ROPE // 2
    qd = QK_NOPE + QK_ROPE
    wq = w_uq.reshape(Q_LORA, MLA_HEADS, qd)
    rope = wq[:, :, QK_NOPE:]
    rope_sw = jnp.concatenate([-rope[:, :, half:], rope[:, :, :half]], axis=2)
    pad = lambda t, off: jnp.pad(t, ((0, 0), (0, 0), (off, LANES - off - t.shape[2])))
    wq_pad = pad(wq, 0).reshape(Q_LORA, MLA_HEADS * LANES)
    wq_sw = pad(rope_sw, QK_NOPE).reshape(Q_LORA, MLA_HEADS * LANES)
    lane = jnp.arange(MLA_HEADS * LANES) % LANES
    wk = jnp.where(lane < QK_NOPE, w_ukv, 0.0)
    wv = jnp.where(lane < QK_NOPE, 0.0, w_ukv)
    return wq_pad, wq_sw, wk, wv


def kernel(x, p, positions, norm_mix, w_in, rw_mu, rw_w0, rw_w2, rw_a0, rw_a2, rw_g2, rw_k_k, rw_k_a, rw_r_k, rw_gn_w, rw_gn_b, rw_w_o, mla_q_norm, mla_w_uq, mla_kv_norm, mla_w_ukv, mla_w_o, w_out, norm_ffn, peer_w_q, peer_sub_keys, peer_u, peer_v, norm_ple, ple_w_gate, ple_w_proj, norm_final):
    bsz, seq, d = x.shape
    t = bsz * seq
    depth = p.shape[0]
    bf = lambda a: a.astype(BF16)
    rw_cols = 3 * RW_DIM + 64 + 64 + 128
    mla_cols = Q_LORA + KV_LORA + QK_ROPE

    head_of = jnp.arange(RW_DIM) // RW_HEAD_DIM
    bd = bf(head_of[:, None] == head_of[None, :])
    inv_freq = ROPE_THETA ** (-jnp.arange(0, QK_ROPE, 2, dtype=F32) / QK_ROPE)
    f_lane = _place(jnp.concatenate([inv_freq, inv_freq])[None, :], LANES, QK_NOPE)
    pos = positions.reshape(t, 1)

    xf = x.reshape(t, d)
    assert depth == 1, "the final RMSNorm is fused into the layer's last kernel"
    for i in range(depth):
        w_rw, mu = _rw_in_weights(w_in[i][:, :rw_cols], rw_mu[i])
        w_mla = _mla_in_weights(w_in[i][:, rw_cols:rw_cols + mla_cols])
        w_gates = w_in[i][:, rw_cols + mla_cols:]
        z_rw = norm_matmul(xf, norm_mix[i], bf(w_rw), tm=1024, tn=RW_ZCOLS // 3)
        z_mla = norm_matmul(xf, norm_mix[i], bf(w_mla), tm=1024, tn=MLA_ZCOLS)
        gates = norm_matmul(xf, norm_mix[i], bf(w_gates), tm=1024, tn=1024, act="sigmoid")

        pad_rows = lambda w: jnp.pad(w, ((0, LANES - w.shape[0]), (0, 0)))
        r, wl, k, v, a, b, g = rwkv_prep(
            z_rw.reshape(bsz, seq, RW_ZCOLS), mu, rw_w0[i], pad_rows(rw_w2[i]), rw_a0[i],
            pad_rows(rw_a2[i]), rw_g2[i], rw_k_k[i], rw_k_a[i], bd, tm=256)
        y = rwkv_scan(r, wl, k, v, a, b, rows=2)

        wq, wqs, wk, wv = _mla_up_weights(mla_w_uq[i], mla_w_ukv[i])
        q, kk, vv = mla_prep(z_mla, pos, f_lane, mla_q_norm[i], mla_kv_norm[i],
                             bf(wq), bf(wqs), bf(wk), bf(wv), tm=512)
        n = MLA_HEADS * LANES
        o = attention(q.reshape(bsz, seq, n), kk.reshape(bsz, seq, n), vv.reshape(bsz, seq, n), tq=256)
        wo_pad = jnp.pad(mla_w_o[i].reshape(MLA_HEADS, V_HEAD, d),
                         ((0, 0), (LANES - V_HEAD, 0), (0, 0))).reshape(n, d)

        flat = lambda a: a.reshape(t, -1)
        x1 = combine(xf, flat(y), flat(r), flat(k), flat(v), flat(g), o.reshape(t, n), gates,
                     bd, rw_r_k[i].reshape(-1), rw_gn_w[i], rw_gn_b[i],
                     bf(rw_w_o[i]), bf(wo_pad), bf(w_out[i]), tm=256)

        qp = norm_matmul(x1, norm_ffn[i], bf(peer_w_q[i]), tm=1024, tn=1024, out_dtype=BF16)
        lam, cc, rho, e1 = peer_route(qp, bf(peer_sub_keys[i]), tm=256)
        ffn = peer_expert(x1, norm_ffn[i], bf(peer_u[i]), bf(peer_v[i].T), lam, cc, rho, e1, tm=512)

        xf = ple_final(x1, ffn, p[i].reshape(t, -1), norm_ple[i], norm_final,
                       bf(ple_w_gate[i]), bf(ple_w_proj[i]), tm=256)
    return xf.reshape(bsz, seq, d)
```

```python
import functools
import math

import jax
import jax.numpy as jnp
from jax import lax
from jax.experimental import pallas as pl
from jax.experimental.pallas import tpu as pltpu

F32 = jnp.float32
BF16 = jnp.bfloat16
HIGHEST = lax.Precision.HIGHEST

LANES = 128
SUBLANES = 8
VMEM_LIMIT = 56 * 1024 * 1024

EPS = 1e-6
RW_HEADS = 8
RW_HEAD_DIM = 64
RW_DIM = RW_HEADS * RW_HEAD_DIM
RW_GN_EPS = 64e-5
SCAN_CHUNK = 64

MLA_HEADS = 8
QK_NOPE = 64
QK_ROPE = 32
V_HEAD = 64
Q_LORA = 384
KV_LORA = 256
ROPE_THETA = 10000.0
MASK_CHUNK = 64
NEG_INF = -1e30

PEER_HEADS = 8
N_KEYS = 128
PEER_TOPK = 16
HALF_Q = 128


def _cparams(*sem):
    return pltpu.CompilerParams(dimension_semantics=sem, vmem_limit_bytes=VMEM_LIMIT)


def _dot(a, b, precision=None):
    return jnp.dot(a, b, preferred_element_type=F32, precision=precision)


def _dot_nt(a, b, precision=None):
    return lax.dot_general(a, b, (((1,), (1,)), ((), ())),
                           preferred_element_type=F32, precision=precision)


def _rms(x, gain):
    return x * lax.rsqrt(jnp.mean(x * x, axis=-1, keepdims=True) + EPS) * gain


def _sigmoid(x):
    return 1.0 / (1.0 + jnp.exp(-x))


def _split3(x):
    hi = x.astype(BF16)
    r1 = x - hi.astype(F32)
    mid = r1.astype(BF16)
    lo = (r1 - mid.astype(F32)).astype(BF16)
    return hi, mid, lo


def _seg_sum(x, bd):
    hi, mid, lo = _split3(x)
    return _dot(hi, bd) + _dot(mid, bd) + _dot(lo, bd)


def _norm_matmul_kernel(x_ref, g_ref, w_ref, o_ref, h_ref, *, act):
    @pl.when(pl.program_id(1) == 0)
    def _():
        h_ref[...] = _rms(x_ref[...], g_ref[...]).astype(BF16)

    y = _dot(h_ref[...], w_ref[...])
    if act == "sigmoid":
        y = _sigmoid(y)
    o_ref[...] = y.astype(o_ref.dtype)


def norm_matmul(x, gain, w, *, tm, tn, act=None, out_dtype=F32):
    t, d = x.shape
    n = w.shape[1]
    return pl.pallas_call(
        functools.partial(_norm_matmul_kernel, act=act),
        grid=(t // tm, n // tn),
        in_specs=[pl.BlockSpec((tm, d), lambda i, j: (i, 0)),
                  pl.BlockSpec((1, d), lambda i, j: (0, 0)),
                  pl.BlockSpec((d, tn), lambda i, j: (0, j))],
        out_specs=pl.BlockSpec((tm, tn), lambda i, j: (i, j)),
        out_shape=jax.ShapeDtypeStruct((t, n), out_dtype),
        scratch_shapes=[pltpu.VMEM((tm, d), BF16)],
        compiler_params=_cparams("parallel", "arbitrary"),
        name="norm_matmul",
    )(x, gain.reshape(1, d), w)


RW_ZCOLS = 3 * RW_DIM + 3 * LANES


def _rwkv_prep_kernel(z_ref, zp_ref, mu_ref, w0_ref, w2_ref, a0_ref, a2_ref, g2_ref,
                      kk_ref, ka_ref, bd_ref,
                      r_ref, wl_ref, k_ref, v_ref, a_ref, b_ref, g_ref):
    z = z_ref[0]
    tm = z.shape[0]
    prev_last = zp_ref[0][SUBLANES - 1:SUBLANES, :]
    prev_last = jnp.where(pl.program_id(1) == 0, 0.0, prev_last)
    rolled = pltpu.roll(z, 1, 0)
    row = lax.broadcasted_iota(jnp.int32, (tm, 1), 0)
    z_prev = jnp.where(row == 0, prev_last, rolled)
    z = z + mu_ref[...] * (z_prev - z)

    o1, o2, o3 = RW_DIM, 2 * RW_DIM, 3 * RW_DIM
    r, k, v = z[:, :o1], z[:, o1:o2], z[:, o2:o3]
    zw, za, zg = z[:, o3:o3 + LANES], z[:, o3 + LANES:o3 + 2 * LANES], z[:, o3 + 2 * LANES:]

    wpre = w0_ref[...] + _dot(jnp.tanh(zw), w2_ref[...])
    nx = -wpre
    softplus = jnp.maximum(nx, 0.0) + jnp.log(1.0 + jnp.exp(-jnp.abs(nx)))
    w = -softplus - 0.5
    iclr = _sigmoid(a0_ref[...] + _dot(za, a2_ref[...]))
    g = _dot(_sigmoid(zg), g2_ref[...])

    kk = k * kk_ref[...]
    ss = _seg_sum(kk * kk, bd_ref[...])
    kk = kk / jnp.maximum(jnp.sqrt(ss), 1e-12)

    r_ref[0] = r
    wl_ref[0] = -jnp.exp(w)
    k_ref[0] = k * (1.0 + (iclr - 1.0) * ka_ref[...])
    v_ref[0] = v
    a_ref[0] = -kk
    b_ref[0] = kk * iclr
    g_ref[0] = g


def rwkv_prep(z, mu, w0, w2, a0, a2, g2, k_k, k_a, bd, *, tm):
    bsz, seq, zc = z.shape
    d = RW_DIM
    row = lambda a: a.reshape(1, -1)
    full = lambda a: pl.BlockSpec(a.shape, lambda b, i: (0,) * a.ndim)
    args = (row(mu), row(w0), w2, row(a0), a2, g2, row(k_k), row(k_a), bd)
    out = jax.ShapeDtypeStruct((bsz, seq, d), F32)
    ospec = pl.BlockSpec((1, tm, d), lambda b, i: (b, i, 0))
    return pl.pallas_call(
        _rwkv_prep_kernel,
        grid=(bsz, seq // tm),
        in_specs=[pl.BlockSpec((1, tm, zc), lambda b, i: (b, i, 0)),
                  pl.BlockSpec((1, SUBLANES, zc),
                               lambda b, i: (b, jnp.maximum(i * (tm // SUBLANES) - 1, 0), 0))]
                 + [full(a) for a in args],
        out_specs=[ospec] * 7,
        out_shape=[out] * 7,
        compiler_params=_cparams("parallel", "arbitrary"),
        name="rwkv_prep",
    )(z, z, *args)


def _rwkv_scan_kernel(r_ref, wl_ref, k_ref, v_ref, a_ref, b_ref, y_ref, g_ref):
    c = SCAN_CHUNK
    hd = RW_HEAD_DIM
    npair = g_ref.shape[0]

    @pl.when(pl.program_id(1) == 0)
    def _():
        g_ref[...] = jnp.zeros_like(g_ref)

    ri = lax.broadcasted_iota(jnp.int32, (c, c), 0)
    ci = lax.broadcasted_iota(jnp.int32, (c, c), 1)
    tril = (ri >= ci).astype(F32)
    lane = lax.broadcasted_iota(jnp.int32, (c, 2 * hd), 1)
    m0 = lane < hd
    r2 = lax.broadcasted_iota(jnp.int32, (2 * c, 2 * c), 0)
    c2 = lax.broadcasted_iota(jnp.int32, (2 * c, 2 * c), 1)
    same = (r2 >= c) == (c2 >= c)
    strict = jnp.logical_and(same, r2 > c2)
    incl = jnp.logical_and(same, r2 >= c2)
    eye = (r2 == c2).astype(F32)

    bf = lambda x: x.astype(BF16)
    stack = lambda x: bf(jnp.concatenate([jnp.where(m0, x, 0.0), jnp.where(m0, 0.0, x)], axis=0))
    twice = lambda x: bf(jnp.concatenate([x, x], axis=0))
    pick = lambda s: jnp.where(m0, s[:c], s[c:])

    pairs = range(npair)
    each = lambda f, *cols: [f(*args) for args in zip(*cols)]
    per_row = r_ref.shape[2] // LANES
    where = [(hp // per_row, slice((hp % per_row) * LANES, (hp % per_row + 1) * LANES)) for hp in pairs]
    load = lambda ref: [ref[bi, :, sl] for bi, sl in where]
    r, wl, k, v, a, b = (load(x) for x in (r_ref, wl_ref, k_ref, v_ref, a_ref, b_ref))

    cs = each(lambda w: _dot(tril, w, HIGHEST), wl)
    cs_last = each(lambda s: s[c - 1:c, :], cs)
    p_inv = each(lambda s: jnp.exp(-s), cs)
    at = each(lambda a, s, w: a * jnp.exp(s - w), a, cs, wl)
    rt = each(lambda r, s: r * jnp.exp(s), r, cs)
    at_s, rt_s = each(stack, at), each(stack, rt)
    bt_s = each(lambda b, p: stack(b * p), b, p_inv)
    kt_s = each(lambda k, p: stack(k * p), k, p_inv)

    ab = each(lambda x, y: jnp.where(strict, _dot_nt(x, y), 0.0), at_s, bt_s)
    ak = each(lambda x, y: jnp.where(strict, _dot_nt(x, y), 0.0), at_s, kt_s)
    rb = each(lambda x, y: jnp.where(incl, _dot_nt(x, y), 0.0), rt_s, bt_s)
    rk = each(lambda x, y: jnp.where(incl, _dot_nt(x, y), 0.0), rt_s, kt_s)

    tinv = each(lambda m: eye + m, ab)
    x = each(bf, ab)
    for _ in range(int(math.log2(c)) - 1):
        x = each(lambda m: bf(_dot(m, m)), x)
        tinv = each(lambda t, m: t + _dot(bf(t), m), tinv, x)

    gt = [g_ref[hp] for hp in pairs]
    gtb = each(bf, gt)
    vv = each(twice, v)
    rhs = each(lambda at, g, ak, vv: _dot_nt(bf(at), g) + pick(_dot(bf(ak), vv)), at, gtb, ak, vv)
    u = each(lambda t, x: pick(_dot(bf(t), twice(x))), tinv, rhs)
    y = each(lambda rt, g, rb, u, rk, vv:
             _dot_nt(bf(rt), g) + pick(_dot(bf(rb), twice(u)) + _dot(bf(rk), vv)),
             rt, gtb, rb, u, rk, vv)
    upd = each(lambda u, b, v, k, s, sl:
               _dot(bf(u.T), bf(b * jnp.exp(sl - s))) + _dot(bf(v.T), bf(k * jnp.exp(sl - s))),
               u, b, v, k, cs, cs_last)
    for hp in pairs:
        bi, sl = where[hp]
        y_ref[bi, :, sl] = y[hp]
        g_ref[hp] = gt[hp] * jnp.exp(cs_last[hp]) + jnp.where(same, upd[hp], 0.0)


def rwkv_scan(r, wl, k, v, a, b, *, rows):
    bsz, seq, d = r.shape
    c = SCAN_CHUNK
    spec = pl.BlockSpec((rows, c, d), lambda bi, ci: (bi, ci, 0))
    return pl.pallas_call(
        _rwkv_scan_kernel,
        grid=(bsz // rows, seq // c),
        in_specs=[spec] * 6,
        out_specs=spec,
        out_shape=jax.ShapeDtypeStruct((bsz, seq, d), F32),
        scratch_shapes=[pltpu.VMEM((rows * d // LANES, LANES, LANES), F32)],
        compiler_params=_cparams("parallel", "arbitrary"),
        name="rwkv_scan",
    )(r, wl, k, v, a, b)


MLA_ZCOLS = 1024
MLA_SCALE = 1.0 / math.sqrt(QK_NOPE + QK_ROPE)
ATTN_TILE = 256


def _mla_prep_kernel(z_ref, pos_ref, fl_ref, qn_ref, kvn_ref, wq_ref, wqs_ref, wk_ref, wv_ref,
                     q_ref, k_ref, v_ref):
    z = z_ref[...]
    c_q = _rms(z[:, :Q_LORA], qn_ref[...]).astype(BF16)
    c_kv = _rms(z[:, Q_LORA:Q_LORA + KV_LORA], kvn_ref[...]).astype(BF16)
    kr = z[:, Q_LORA + KV_LORA:Q_LORA + KV_LORA + LANES]
    krs = z[:, Q_LORA + KV_LORA + LANES:Q_LORA + KV_LORA + 2 * LANES]

    ang = pos_ref[...].astype(F32) * fl_ref[...]
    cos, sin = jnp.cos(ang), jnp.sin(ang)
    kr_rot = kr * cos + krs * sin

    q = _dot(c_q, wq_ref[...])
    qs = _dot(c_q, wqs_ref[...])
    kn = _dot(c_kv, wk_ref[...])
    v = _dot(c_kv, wv_ref[...])
    for c in range(v_ref.shape[0]):
        v_ref[c] = v[c * ATTN_TILE:(c + 1) * ATTN_TILE, :].T.astype(BF16)
    for h in range(MLA_HEADS):
        sl = slice(h * LANES, (h + 1) * LANES)
        q_ref[:, sl] = ((q[:, sl] * cos + qs[:, sl] * sin) * MLA_SCALE).astype(BF16)
        k_ref[:, sl] = (kn[:, sl] + kr_rot).astype(BF16)


def mla_prep(z, pos, f_lane, q_norm, kv_norm, wq, wqs, wk, wv, *, tm):
    t = z.shape[0]
    n = MLA_HEADS * LANES
    full = lambda a: pl.BlockSpec(a.shape, lambda i: (0,) * a.ndim)
    args = (f_lane, q_norm.reshape(1, -1), kv_norm.reshape(1, -1), wq, wqs, wk, wv)
    out = jax.ShapeDtypeStruct((t, n), BF16)
    out_vt = jax.ShapeDtypeStruct((t // ATTN_TILE, n, ATTN_TILE), BF16)
    ospec = pl.BlockSpec((tm, n), lambda i: (i, 0))
    vspec = pl.BlockSpec((tm // ATTN_TILE, n, ATTN_TILE), lambda i: (i, 0, 0))
    return pl.pallas_call(
        _mla_prep_kernel,
        grid=(t // tm,),
        in_specs=[pl.BlockSpec((tm, MLA_ZCOLS), lambda i: (i, 0)),
                  pl.BlockSpec((tm, 1), lambda i: (i, 0))] + [full(a) for a in args],
        out_specs=[ospec, ospec, vspec],
        out_shape=[out, out, out_vt],
        compiler_params=_cparams("parallel"),
        name="mla_prep",
    )(z, pos, *args)


ATTN_HEADS_PER_STEP = 8


def _attn_kernel(q_ref, k_ref, vt_ref, o_ref):
    tq = ATTN_TILE
    iq = pl.program_id(2)
    nh = q_ref.shape[2] // LANES
    heads = [slice(h * LANES, (h + 1) * LANES) for h in range(nh)]
    qs = [q_ref[0, :, sl] for sl in heads]

    def tile(j, carry, masked):
        start = pl.multiple_of(j * tq, tq)
        if masked:
            kc = lax.broadcasted_iota(jnp.int32, (tq, tq), 0) // MASK_CHUNK
            qc = lax.broadcasted_iota(jnp.int32, (tq, tq), 1) // MASK_CHUNK
            keep = kc <= qc
        hs = range(nh)
        s = [_dot_nt(k_ref[0, pl.ds(start, tq), heads[h]], qs[h]) for h in hs]
        if masked:
            s = [jnp.where(keep, x, NEG_INF) for x in s]
        m_new = [jnp.maximum(carry[h][0], jnp.max(s[h], axis=0, keepdims=True)) for h in hs]
        alpha = [jnp.exp(carry[h][0] - m_new[h]) for h in hs]
        p = [jnp.exp(s[h] - m_new[h]) for h in hs]
        l = [alpha[h] * carry[h][1] + jnp.sum(p[h], axis=0, keepdims=True) for h in hs]
        pv = [_dot(vt_ref[0, j, heads[h], :], p[h].astype(BF16)) for h in hs]
        return tuple((m_new[h], l[h], alpha[h] * carry[h][2] + pv[h]) for h in hs)

    init = tuple((jnp.full((1, tq), NEG_INF, F32), jnp.zeros((1, tq), F32), jnp.zeros((LANES, tq), F32))
                 for _ in heads)
    carry = lax.fori_loop(0, iq, lambda j, c: tile(j, c, False), init)
    carry = tile(iq, carry, True)
    for h, sl in enumerate(heads):
        m, l, acc = carry[h]
        o_ref[0, :, sl] = (acc / l).T.astype(BF16)


def attention(q, k, vt):
    bsz, seq, n = q.shape
    tq = ATTN_TILE
    w = ATTN_HEADS_PER_STEP * LANES
    qspec = pl.BlockSpec((1, tq, w), lambda b, h, i: (b, i, h))
    kspec = pl.BlockSpec((1, seq, w), lambda b, h, i: (b, 0, h))
    vspec = pl.BlockSpec((1, seq // tq, w, tq), lambda b, h, i: (b, 0, h, 0))
    return pl.pallas_call(
        _attn_kernel,
        grid=(bsz, n // w, seq // tq),
        in_specs=[qspec, kspec, vspec],
        out_specs=qspec,
        out_shape=jax.ShapeDtypeStruct((bsz, seq, n), BF16),
        compiler_params=_cparams("parallel", "parallel", "arbitrary"),
        name="attention",
    )(q, k, vt)


def _combine_kernel(x_ref, y_ref, r_ref, k_ref, v_ref, g_ref, o_ref, gates_ref,
                    bd_ref, rk_ref, gnw_ref, gnb_ref, wa_ref, wb_ref, wo_ref, out_ref):
    bd = bd_ref[...]
    inv_n = 1.0 / RW_HEAD_DIM
    y = y_ref[...]
    v = v_ref[...]
    mean = _seg_sum(y, bd) * inv_n
    yc = y - mean
    var = _seg_sum(yc * yc, bd) * inv_n
    yn = yc * lax.rsqrt(var + RW_GN_EPS) * gnw_ref[...] + gnb_ref[...]
    bonus = _seg_sum(r_ref[...] * k_ref[...] * rk_ref[...], bd) * v
    ya = _dot(((yn + bonus) * g_ref[...]).astype(BF16), wa_ref[...])
    yb = _dot(o_ref[...], wb_ref[...])
    d = ya.shape[1]
    gates = gates_ref[...]
    mix = gates[:, :d] * ya + gates[:, d:] * yb
    out_ref[...] = x_ref[...] + _dot(mix.astype(BF16), wo_ref[...])


def combine(x, y, r, k, v, g, o, gates, bd, r_k, gn_w, gn_b, wa, wb, wo, *, tm):
    t, d = x.shape
    row = lambda a: a.reshape(1, -1)
    tok = lambda a: pl.BlockSpec((tm, a.shape[1]), lambda i: (i, 0))
    full = lambda a: pl.BlockSpec(a.shape, lambda i: (0,) * a.ndim)
    toks = (x, y, r, k, v, g, o, gates)
    consts = (bd, row(r_k), row(gn_w), row(gn_b), wa, wb, wo)
    return pl.pallas_call(
        _combine_kernel,
        grid=(t // tm,),
        in_specs=[tok(a) for a in toks] + [full(a) for a in consts],
        out_specs=pl.BlockSpec((tm, d), lambda i: (i, 0)),
        out_shape=jax.ShapeDtypeStruct((t, d), F32),
        compiler_params=_cparams("parallel"),
        name="combine",
    )(*toks, *consts)


NOT_RANKED = 127.0
RANK_BLOCKS_PER_LOOP = 2
STAIR = tuple(PEER_TOPK // (ii + 1) for ii in range(PEER_TOPK))


def _top_ranks(s, k, exact_ties):
    n, t = s.shape
    key = lax.broadcasted_iota(jnp.int32, (n, LANES), 0)
    slot = lax.broadcasted_iota(jnp.int32, (k, LANES), 0)

    def one(r, s, rank, top):
        m = jnp.max(s, axis=0, keepdims=True)
        hit = s == m
        if exact_ties:
            hit = key == jnp.min(jnp.where(hit, key, n), axis=0, keepdims=True)
        rank = jnp.where(hit, jnp.asarray(r, F32), rank)
        s = jnp.where(hit, -jnp.inf, s)
        top = jnp.where(slot == r, m, top)
        return s, rank, top

    def body(r, carry):
        return tuple(one(r, *c) for c in carry)

    blocks = [s[:, c * LANES:(c + 1) * LANES] for c in range(t // LANES)]
    tops, ranks = [], []
    for g in range(0, len(blocks), RANK_BLOCKS_PER_LOOP):
        init = tuple((b, jnp.full((n, LANES), NOT_RANKED, F32), jnp.zeros((k, LANES), F32))
                     for b in blocks[g:g + RANK_BLOCKS_PER_LOOP])
        for _, rank, top in lax.fori_loop(0, k, body, init):
            tops.append(top)
            ranks.append(rank)
    return jnp.concatenate(tops, axis=1), jnp.concatenate(ranks, axis=1)


def _ranked_count_ok(rank, k):
    count = jnp.sum((rank < NOT_RANKED).astype(F32), axis=0, keepdims=True)
    return jnp.max(jnp.abs(count - k)) == 0.0


def _peer_route_kernel(q_ref, keys_ref, lam_ref, cc_ref, rho_ref, e1_ref):
    tm = q_ref.shape[0]
    k = PEER_TOPK
    neg = -jnp.inf

    def head(h, _):
        col = pl.multiple_of(h * 2 * HALF_Q, 2 * HALF_Q)
        s0 = _dot_nt(keys_ref[h, 0], q_ref[:, pl.ds(col, HALF_Q)])
        s1 = _dot_nt(keys_ref[h, 1], q_ref[:, pl.ds(col + HALF_Q, HALF_Q)])
        def rank_all(exact_ties):
            top0, rank0 = _top_ranks(s0, k, exact_ties)
            top1, rank1 = _top_ranks(s1, k, exact_ties)
            row8 = lax.broadcasted_iota(jnp.int32, (SUBLANES, tm), 0)
            groups = [top0[0:1] + top1[0:8], top0[0:1] + top1[8:16], top0[1:2] + top1[0:8]]
            for ii in range(2, 8):
                groups.append(jnp.where(row8 < STAIR[ii], top0[ii:ii + 1] + top1[0:8], neg))
            groups.append(top0[8:16] + top1[0:1])
            cand = jnp.concatenate(groups, axis=0)
            _, crank = _top_ranks(cand, k, exact_ties)
            return top0, rank0, top1, rank1, cand, crank

        quick = rank_all(False)
        tie_free = jnp.logical_and(
            jnp.logical_and(_ranked_count_ok(quick[1], k), _ranked_count_ok(quick[3], k)),
            _ranked_count_ok(quick[5], k))
        top0, rank0, top1, rank1, cand, crank = lax.cond(
            tie_free, lambda: quick, lambda: rank_all(True))
        sel = crank < NOT_RANKED
        ex = jnp.where(sel, jnp.exp(cand - cand[0:1]), 0.0)
        z = jnp.sum(ex, axis=0, keepdims=True)
        self = sel.astype(F32)
        counts = [jnp.sum(self[0:16], axis=0, keepdims=True)]
        for g in range(2, 9):
            counts.append(jnp.sum(self[8 * g:8 * g + 8], axis=0, keepdims=True))
        lvec = jnp.concatenate(counts + [self[72:80]], axis=0)

        lam = jnp.zeros((N_KEYS, tm), F32)
        for ii in range(k):
            lam = jnp.where(rank0 == float(ii), lvec[ii:ii + 1], lam)
        lam_ref[h] = lam
        cc_ref[h] = jnp.exp(s0 - top0[0:1]) / z
        rho_ref[h] = rank1.astype(BF16)
        e1_ref[h] = jnp.exp(s1 - top1[0:1]).astype(BF16)
        return 0

    lax.fori_loop(0, PEER_HEADS, head, 0)


def peer_route(qp, keys, *, tm):
    t = qp.shape[0]
    out = jax.ShapeDtypeStruct((PEER_HEADS, N_KEYS, t), F32)
    out_b = jax.ShapeDtypeStruct((PEER_HEADS, N_KEYS, t), BF16)
    ospec = pl.BlockSpec((PEER_HEADS, N_KEYS, tm), lambda i: (0, 0, i))
    return pl.pallas_call(
        _peer_route_kernel,
        grid=(t // tm,),
        in_specs=[pl.BlockSpec((tm, qp.shape[1]), lambda i: (i, 0)),
                  pl.BlockSpec(keys.shape, lambda i: (0, 0, 0, 0))],
        out_specs=[ospec] * 4,
        out_shape=[out, out, out_b, out_b],
        compiler_params=_cparams("parallel"),
        name="peer_route",
    )(qp, keys)


PEER_ROWS = 8
PEER_TOKEN_CHUNK = 256


def _erf(x):
    return lax.erf(x)


def _gelu(x):
    return 0.5 * x * (1.0 + _erf(x * (1.0 / math.sqrt(2.0))))


def _peer_expert_kernel(x_ref, gn_ref, u_ref, vt_ref, lam_ref, cc_ref, rho_ref, e1_ref, out_ref,
                        xt_ref, acc_ref, pre0_ref, pre1_ref):
    j = pl.program_id(1)
    tm = xt_ref.shape[1]
    chunks = [slice(c, c + PEER_TOKEN_CHUNK) for c in range(0, tm, PEER_TOKEN_CHUNK)]

    @pl.when(j == 0)
    def _():
        xt_ref[...] = _rms(x_ref[...], gn_ref[...]).T.astype(BF16)
        acc_ref[...] = jnp.zeros_like(acc_ref)
        pre1_ref[...] = jnp.zeros_like(pre1_ref)

    def step(fill_ref, drain_ref):
        for cols in chunks:
            fill_ref[:, cols] = _dot(u_ref[...], xt_ref[:, cols])
        for cols in chunks:
            gs = []
            for ii in range(PEER_ROWS):
                rows = slice(ii * N_KEYS, (ii + 1) * N_KEYS)
                gate = jnp.zeros((N_KEYS, PEER_TOKEN_CHUNK), BF16)
                for h in range(PEER_HEADS):
                    lam = lam_ref[h, ii:ii + 1, cols].astype(BF16)
                    cc = cc_ref[h, ii:ii + 1, cols].astype(BF16)
                    gate = gate + jnp.where(rho_ref[h, :, cols] < lam, e1_ref[h, :, cols] * cc,
                                            jnp.zeros((), BF16))
                gs.append(_gelu(drain_ref[rows, cols]).astype(BF16) * gate)
            acc_ref[:, cols] += _dot(vt_ref[...], jnp.concatenate(gs, axis=0))

    @pl.when(j % 2 == 0)
    def _():
        step(pre0_ref, pre1_ref)

    @pl.when(j % 2 == 1)
    def _():
        step(pre1_ref, pre0_ref)

    @pl.when(j == pl.num_programs(1) - 1)
    def _():
        out_ref[...] = acc_ref[...].T


def peer_expert(x, gain, u, vt, lam, cc, rho, e1, *, tm):
    t, d = x.shape
    nrow = PEER_ROWS * N_KEYS
    nblk = u.shape[0] // nrow
    stage = lambda lag: (lambda j: jnp.clip(j - lag, 0, nblk - 1))
    rspec = pl.BlockSpec((PEER_HEADS, PEER_ROWS, tm), lambda i, j: (0, stage(1)(j), i))
    cspec = pl.BlockSpec((PEER_HEADS, N_KEYS, tm), lambda i, j: (0, 0, i))
    return pl.pallas_call(
        _peer_expert_kernel,
        grid=(t // tm, nblk + 1),
        in_specs=[pl.BlockSpec((tm, d), lambda i, j: (i, 0)),
                  pl.BlockSpec((1, d), lambda i, j: (0, 0)),
                  pl.BlockSpec((nrow, d), lambda i, j: (stage(0)(j), 0)),
                  pl.BlockSpec((d, nrow), lambda i, j: (0, stage(1)(j))),
                  rspec, rspec, cspec, cspec],
        out_specs=pl.BlockSpec((tm, d), lambda i, j: (i, 0)),
        out_shape=jax.ShapeDtypeStruct((t, d), F32),
        scratch_shapes=[pltpu.VMEM((d, tm), BF16), pltpu.VMEM((d, tm), F32),
                        pltpu.VMEM((nrow, tm), F32), pltpu.VMEM((nrow, tm), F32)],
        compiler_params=_cparams("parallel", "arbitrary"),
        name="peer_expert",
    )(x, gain.reshape(1, d), u, vt, lam, cc, rho, e1)


def _ple_final_kernel(x_ref, f_ref, p_ref, gp_ref, gf_ref, wg_ref, wp_ref, out_ref):
    x = x_ref[...] + f_ref[...]
    gate = _sigmoid(_dot(_rms(x, gp_ref[...]).astype(BF16), wg_ref[...]))
    x = x + gate * _dot(p_ref[...].astype(BF16), wp_ref[...])
    out_ref[...] = _rms(x, gf_ref[...])


def ple_final(x, f, p, g_ple, g_final, wg, wp, *, tm):
    t, d = x.shape
    tok = lambda a: pl.BlockSpec((tm, a.shape[1]), lambda i: (i, 0))
    full = lambda a: pl.BlockSpec(a.shape, lambda i: (0,) * a.ndim)
    consts = (g_ple.reshape(1, d), g_final.reshape(1, d), wg, wp)
    return pl.pallas_call(
        _ple_final_kernel,
        grid=(t // tm,),
        in_specs=[tok(x), tok(f), tok(p)] + [full(a) for a in consts],
        out_specs=tok(x),
        out_shape=jax.ShapeDtypeStruct((t, d), F32),
        compiler_params=_cparams("parallel"),
        name="ple_final",
    )(x, f, p, *consts)


def _place(cols, width, offset):
    return jnp.pad(cols, ((0, 0), (offset, width - offset - cols.shape[1])))


def _rw_in_weights(w_rw, mu):
    o3 = 3 * RW_DIM
    lw, la = 64, 64
    segs = [w_rw[:, :o3], _place(w_rw[:, o3:o3 + lw], LANES, 0),
            _place(w_rw[:, o3 + lw:o3 + lw + la], LANES, 0), w_rw[:, o3 + lw + la:]]
    mus = [mu[None, :o3], _place(mu[None, o3:o3 + lw], LANES, 0),
           _place(mu[None, o3 + lw:o3 + lw + la], LANES, 0), mu[None, o3 + lw + la:]]
    return jnp.concatenate(segs, axis=1), jnp.concatenate(mus, axis=1)[0]


def _mla_in_weights(w_mla):
    half = QK_ROPE // 2
    lat = Q_LORA + KV_LORA
    kr = w_mla[:, lat:]
    kr_sw = jnp.concatenate([-kr[:, half:], kr[:, :half]], axis=1)
    return jnp.concatenate([w_mla[:, :lat], _place(kr, LANES, QK_NOPE), _place(kr_sw, LANES, QK_NOPE),
                            jnp.zeros((w_mla.shape[0], LANES), w_mla.dtype)], axis=1)


def _mla_up_weights(w_uq, w_ukv):
    half = QK_ROPE // 2
    qd = QK_NOPE + QK_ROPE
    wq = w_uq.reshape(Q_LORA, MLA_HEADS, qd)
    rope = wq[:, :, QK_NOPE:]
    rope_sw = jnp.concatenate([-rope[:, :, half:], rope[:, :, :half]], axis=2)
    pad = lambda t, off: jnp.pad(t, ((0, 0), (0, 0), (off, LANES - off - t.shape[2])))
    wq_pad = pad(wq, 0).reshape(Q_LORA, MLA_HEADS * LANES)
    wq_sw = pad(rope_sw, QK_NOPE).reshape(Q_LORA, MLA_HEADS * LANES)
    lane = jnp.arange(MLA_HEADS * LANES) % LANES
    wk = jnp.where(lane < QK_NOPE, w_ukv, 0.0)
    wv = jnp.where(lane < QK_NOPE, 0.0, w_ukv)
    return wq_pad, wq_sw, wk, wv


def kernel(x, p, positions, norm_mix, w_in, rw_mu, rw_w0, rw_w2, rw_a0, rw_a2, rw_g2, rw_k_k, rw_k_a, rw_r_k, rw_gn_w, rw_gn_b, rw_w_o, mla_q_norm, mla_w_uq, mla_kv_norm, mla_w_ukv, mla_w_o, w_out, norm_ffn, peer_w_q, peer_sub_keys, peer_u, peer_v, norm_ple, ple_w_gate, ple_w_proj, norm_final):
    bsz, seq, d = x.shape
    t = bsz * seq
    depth = p.shape[0]
    bf = lambda a: a.astype(BF16)
    rw_cols = 3 * RW_DIM + 64 + 64 + 128
    mla_cols = Q_LORA + KV_LORA + QK_ROPE

    head_of = jnp.arange(RW_DIM) // RW_HEAD_DIM
    bd = bf(head_of[:, None] == head_of[None, :])
    inv_freq = ROPE_THETA ** (-jnp.arange(0, QK_ROPE, 2, dtype=F32) / QK_ROPE)
    f_lane = _place(jnp.concatenate([inv_freq, inv_freq])[None, :], LANES, QK_NOPE)
    pos = positions.reshape(t, 1)

    xf = x.reshape(t, d)
    assert depth == 1, "the final RMSNorm is fused into the layer's last kernel"
    for i in range(depth):
        w_rw, mu = _rw_in_weights(w_in[i][:, :rw_cols], rw_mu[i])
        w_mla = _mla_in_weights(w_in[i][:, rw_cols:rw_cols + mla_cols])
        w_gates = w_in[i][:, rw_cols + mla_cols:]
        z_rw = norm_matmul(xf, norm_mix[i], bf(w_rw), tm=1024, tn=RW_ZCOLS // 3)
        z_mla = norm_matmul(xf, norm_mix[i], bf(w_mla), tm=1024, tn=MLA_ZCOLS)
        gates = norm_matmul(xf, norm_mix[i], bf(w_gates), tm=1024, tn=1024, act="sigmoid")

        pad_rows = lambda w: jnp.pad(w, ((0, LANES - w.shape[0]), (0, 0)))
        r, wl, k, v, a, b, g = rwkv_prep(
            z_rw.reshape(bsz, seq, RW_ZCOLS), mu, rw_w0[i], pad_rows(rw_w2[i]), rw_a0[i],
            pad_rows(rw_a2[i]), rw_g2[i], rw_k_k[i], rw_k_a[i], bd, tm=256)
        y = rwkv_scan(r, wl, k, v, a, b, rows=2)

        wq, wqs, wk, wv = _mla_up_weights(mla_w_uq[i], mla_w_ukv[i])
        q, kk, vv = mla_prep(z_mla, pos, f_lane, mla_q_norm[i], mla_kv_norm[i],
                             bf(wq), bf(wqs), bf(wk), bf(wv), tm=512)
        n = MLA_HEADS * LANES
        o = attention(q.reshape(bsz, seq, n), kk.reshape(bsz, seq, n),
                      vv.reshape(bsz, seq // ATTN_TILE, n, ATTN_TILE))
        wo_pad = jnp.pad(mla_w_o[i].reshape(MLA_HEADS, V_HEAD, d),
                         ((0, 0), (LANES - V_HEAD, 0), (0, 0))).reshape(n, d)

        flat = lambda a: a.reshape(t, -1)
        x1 = combine(xf, flat(y), flat(r), flat(k), flat(v), flat(g), o.reshape(t, n), gates,
                     bd, rw_r_k[i].reshape(-1), rw_gn_w[i], rw_gn_b[i],
                     bf(rw_w_o[i]), bf(wo_pad), bf(w_out[i]), tm=256)

        qp = norm_matmul(x1, norm_ffn[i], bf(peer_w_q[i]), tm=1024, tn=1024, out_dtype=BF16)
        lam, cc, rho, e1 = peer_route(qp, bf(peer_sub_keys[i]), tm=256)
        ffn = peer_expert(x1, norm_ffn[i], bf(peer_u[i]), bf(peer_v[i].T), lam, cc, rho, e1, tm=512)

        xf = ple_final(x1, ffn, p[i].reshape(t, -1), norm_ple[i], norm_final,
                       bf(ple_w_gate[i]), bf(ple_w_proj[i]), tm=256)
    return xf.reshape(bsz, seq, d)
```

```python
import functools
import math

import jax
import jax.numpy as jnp
from jax import lax
from jax.experimental import pallas as pl
from jax.experimental.pallas import tpu as pltpu

F32 = jnp.float32
BF16 = jnp.bfloat16
HIGHEST = lax.Precision.HIGHEST

LANES = 128
SUBLANES = 8
VMEM_LIMIT = 56 * 1024 * 1024

EPS = 1e-6
RW_HEADS = 8
RW_HEAD_DIM = 64
RW_DIM = RW_HEADS * RW_HEAD_DIM
RW_GN_EPS = 64e-5
SCAN_CHUNK = 64

MLA_HEADS = 8
QK_NOPE = 64
QK_ROPE = 32
V_HEAD = 64
Q_LORA = 384
KV_LORA = 256
ROPE_THETA = 10000.0
MASK_CHUNK = 64
NEG_INF = -1e30

PEER_HEADS = 8
N_KEYS = 128
PEER_TOPK = 16
HALF_Q = 128


def _cparams(*sem):
    return pltpu.CompilerParams(dimension_semantics=sem, vmem_limit_bytes=VMEM_LIMIT)


def _dot(a, b, precision=None):
    return jnp.dot(a, b, preferred_element_type=F32, precision=precision)


def _dot_nt(a, b, precision=None):
    return lax.dot_general(a, b, (((1,), (1,)), ((), ())),
                           preferred_element_type=F32, precision=precision)


def _rms(x, gain):
    return x * lax.rsqrt(jnp.mean(x * x, axis=-1, keepdims=True) + EPS) * gain


def _sigmoid(x):
    return 1.0 / (1.0 + jnp.exp(-x))


def _split3(x):
    hi = x.astype(BF16)
    r1 = x - hi.astype(F32)
    mid = r1.astype(BF16)
    lo = (r1 - mid.astype(F32)).astype(BF16)
    return hi, mid, lo


def _seg_sum(x, bd):
    hi, mid, lo = _split3(x)
    return _dot(hi, bd) + _dot(mid, bd) + _dot(lo, bd)


def _norm_matmul_kernel(x_ref, g_ref, w_ref, o_ref, h_ref, *, act):
    @pl.when(pl.program_id(1) == 0)
    def _():
        h_ref[...] = _rms(x_ref[...], g_ref[...]).astype(BF16)

    y = _dot(h_ref[...], w_ref[...])
    if act == "sigmoid":
        y = _sigmoid(y)
    o_ref[...] = y.astype(o_ref.dtype)


def norm_matmul(x, gain, w, *, tm, tn, act=None, out_dtype=F32):
    t, d = x.shape
    n = w.shape[1]
    return pl.pallas_call(
        functools.partial(_norm_matmul_kernel, act=act),
        grid=(t // tm, n // tn),
        in_specs=[pl.BlockSpec((tm, d), lambda i, j: (i, 0)),
                  pl.BlockSpec((1, d), lambda i, j: (0, 0)),
                  pl.BlockSpec((d, tn), lambda i, j: (0, j))],
        out_specs=pl.BlockSpec((tm, tn), lambda i, j: (i, j)),
        out_shape=jax.ShapeDtypeStruct((t, n), out_dtype),
        scratch_shapes=[pltpu.VMEM((tm, d), BF16)],
        compiler_params=_cparams("parallel", "arbitrary"),
        name="norm_matmul",
    )(x, gain.reshape(1, d), w)


RW_ZCOLS = 3 * RW_DIM + 3 * LANES


def _rwkv_prep_kernel(z_ref, zp_ref, mu_ref, w0_ref, w2_ref, a0_ref, a2_ref, g2_ref,
                      kk_ref, ka_ref, bd_ref,
                      r_ref, wl_ref, k_ref, v_ref, a_ref, b_ref, g_ref):
    z = z_ref[0]
    tm = z.shape[0]
    prev_last = zp_ref[0][SUBLANES - 1:SUBLANES, :]
    prev_last = jnp.where(pl.program_id(1) == 0, 0.0, prev_last)
    rolled = pltpu.roll(z, 1, 0)
    row = lax.broadcasted_iota(jnp.int32, (tm, 1), 0)
    z_prev = jnp.where(row == 0, prev_last, rolled)
    z = z + mu_ref[...] * (z_prev - z)

    o1, o2, o3 = RW_DIM, 2 * RW_DIM, 3 * RW_DIM
    r, k, v = z[:, :o1], z[:, o1:o2], z[:, o2:o3]
    zw, za, zg = z[:, o3:o3 + LANES], z[:, o3 + LANES:o3 + 2 * LANES], z[:, o3 + 2 * LANES:]

    wpre = w0_ref[...] + _dot(jnp.tanh(zw), w2_ref[...])
    nx = -wpre
    softplus = jnp.maximum(nx, 0.0) + jnp.log(1.0 + jnp.exp(-jnp.abs(nx)))
    w = -softplus - 0.5
    iclr = _sigmoid(a0_ref[...] + _dot(za, a2_ref[...]))
    g = _dot(_sigmoid(zg), g2_ref[...])

    kk = k * kk_ref[...]
    ss = _seg_sum(kk * kk, bd_ref[...])
    kk = kk / jnp.maximum(jnp.sqrt(ss), 1e-12)

    r_ref[0] = r
    wl_ref[0] = -jnp.exp(w)
    k_ref[0] = k * (1.0 + (iclr - 1.0) * ka_ref[...])
    v_ref[0] = v
    a_ref[0] = -kk
    b_ref[0] = kk * iclr
    g_ref[0] = g


def rwkv_prep(z, mu, w0, w2, a0, a2, g2, k_k, k_a, bd, *, tm):
    bsz, seq, zc = z.shape
    d = RW_DIM
    row = lambda a: a.reshape(1, -1)
    full = lambda a: pl.BlockSpec(a.shape, lambda b, i: (0,) * a.ndim)
    args = (row(mu), row(w0), w2, row(a0), a2, g2, row(k_k), row(k_a), bd)
    out = jax.ShapeDtypeStruct((bsz, seq, d), F32)
    ospec = pl.BlockSpec((1, tm, d), lambda b, i: (b, i, 0))
    return pl.pallas_call(
        _rwkv_prep_kernel,
        grid=(bsz, seq // tm),
        in_specs=[pl.BlockSpec((1, tm, zc), lambda b, i: (b, i, 0)),
                  pl.BlockSpec((1, SUBLANES, zc),
                               lambda b, i: (b, jnp.maximum(i * (tm // SUBLANES) - 1, 0), 0))]
                 + [full(a) for a in args],
        out_specs=[ospec] * 7,
        out_shape=[out] * 7,
        compiler_params=_cparams("parallel", "arbitrary"),
        name="rwkv_prep",
    )(z, z, *args)


def _rwkv_scan_kernel(r_ref, wl_ref, k_ref, v_ref, a_ref, b_ref, y_ref, g_ref):
    c = SCAN_CHUNK
    hd = RW_HEAD_DIM
    npair = g_ref.shape[0]

    @pl.when(pl.program_id(1) == 0)
    def _():
        g_ref[...] = jnp.zeros_like(g_ref)

    ri = lax.broadcasted_iota(jnp.int32, (c, c), 0)
    ci = lax.broadcasted_iota(jnp.int32, (c, c), 1)
    tril = (ri >= ci).astype(F32)
    lane = lax.broadcasted_iota(jnp.int32, (c, 2 * hd), 1)
    m0 = lane < hd
    r2 = lax.broadcasted_iota(jnp.int32, (2 * c, 2 * c), 0)
    c2 = lax.broadcasted_iota(jnp.int32, (2 * c, 2 * c), 1)
    same = (r2 >= c) == (c2 >= c)
    strict = jnp.logical_and(same, r2 > c2)
    incl = jnp.logical_and(same, r2 >= c2)
    eye = (r2 == c2).astype(F32)

    bf = lambda x: x.astype(BF16)
    stack = lambda x: bf(jnp.concatenate([jnp.where(m0, x, 0.0), jnp.where(m0, 0.0, x)], axis=0))
    twice = lambda x: bf(jnp.concatenate([x, x], axis=0))
    pick = lambda s: jnp.where(m0, s[:c], s[c:])

    pairs = range(npair)
    each = lambda f, *cols: [f(*args) for args in zip(*cols)]
    per_row = r_ref.shape[2] // LANES
    where = [(hp // per_row, slice((hp % per_row) * LANES, (hp % per_row + 1) * LANES)) for hp in pairs]
    load = lambda ref: [ref[bi, :, sl] for bi, sl in where]
    r, wl, k, v, a, b = (load(x) for x in (r_ref, wl_ref, k_ref, v_ref, a_ref, b_ref))

    cs = each(lambda w: _dot(tril, w, HIGHEST), wl)
    cs_last = each(lambda s: s[c - 1:c, :], cs)
    p_inv = each(lambda s: jnp.exp(-s), cs)
    at = each(lambda a, s, w: a * jnp.exp(s - w), a, cs, wl)
    rt = each(lambda r, s: r * jnp.exp(s), r, cs)
    at_s, rt_s = each(stack, at), each(stack, rt)
    bt_s = each(lambda b, p: stack(b * p), b, p_inv)
    kt_s = each(lambda k, p: stack(k * p), k, p_inv)

    ab = each(lambda x, y: jnp.where(strict, _dot_nt(x, y), 0.0), at_s, bt_s)
    ak = each(lambda x, y: jnp.where(strict, _dot_nt(x, y), 0.0), at_s, kt_s)
    rb = each(lambda x, y: jnp.where(incl, _dot_nt(x, y), 0.0), rt_s, bt_s)
    rk = each(lambda x, y: jnp.where(incl, _dot_nt(x, y), 0.0), rt_s, kt_s)

    tinv = each(lambda m: eye + m, ab)
    x = each(bf, ab)
    for _ in range(int(math.log2(c)) - 1):
        x = each(lambda m: bf(_dot(m, m)), x)
        tinv = each(lambda t, m: t + _dot(bf(t), m), tinv, x)

    gt = [g_ref[hp] for hp in pairs]
    gtb = each(bf, gt)
    vv = each(twice, v)
    rhs = each(lambda at, g, ak, vv: _dot_nt(bf(at), g) + pick(_dot(bf(ak), vv)), at, gtb, ak, vv)
    u = each(lambda t, x: pick(_dot(bf(t), twice(x))), tinv, rhs)
    y = each(lambda rt, g, rb, u, rk, vv:
             _dot_nt(bf(rt), g) + pick(_dot(bf(rb), twice(u)) + _dot(bf(rk), vv)),
             rt, gtb, rb, u, rk, vv)
    upd = each(lambda u, b, v, k, s, sl:
               _dot(bf(u.T), bf(b * jnp.exp(sl - s))) + _dot(bf(v.T), bf(k * jnp.exp(sl - s))),
               u, b, v, k, cs, cs_last)
    for hp in pairs:
        bi, sl = where[hp]
        y_ref[bi, :, sl] = y[hp]
        g_ref[hp] = gt[hp] * jnp.exp(cs_last[hp]) + jnp.where(same, upd[hp], 0.0)


def rwkv_scan(r, wl, k, v, a, b, *, rows):
    bsz, seq, d = r.shape
    c = SCAN_CHUNK
    spec = pl.BlockSpec((rows, c, d), lambda bi, ci: (bi, ci, 0))
    return pl.pallas_call(
        _rwkv_scan_kernel,
        grid=(bsz // rows, seq // c),
        in_specs=[spec] * 6,
        out_specs=spec,
        out_shape=jax.ShapeDtypeStruct((bsz, seq, d), F32),
        scratch_shapes=[pltpu.VMEM((rows * d // LANES, LANES, LANES), F32)],
        compiler_params=_cparams("parallel", "arbitrary"),
        name="rwkv_scan",
    )(r, wl, k, v, a, b)


MLA_ZCOLS = 1024
MLA_SCALE = 1.0 / math.sqrt(QK_NOPE + QK_ROPE)
ATTN_TILE = 256


def _mla_prep_kernel(z_ref, pos_ref, fl_ref, qn_ref, kvn_ref, wq_ref, wqs_ref, wk_ref, wv_ref,
                     q_ref, k_ref, v_ref):
    z = z_ref[...]
    c_q = _rms(z[:, :Q_LORA], qn_ref[...]).astype(BF16)
    c_kv = _rms(z[:, Q_LORA:Q_LORA + KV_LORA], kvn_ref[...]).astype(BF16)
    kr = z[:, Q_LORA + KV_LORA:Q_LORA + KV_LORA + LANES]
    krs = z[:, Q_LORA + KV_LORA + LANES:Q_LORA + KV_LORA + 2 * LANES]

    ang = pos_ref[...].astype(F32) * fl_ref[...]
    cos, sin = jnp.cos(ang), jnp.sin(ang)
    kr_rot = kr * cos + krs * sin

    q = _dot(c_q, wq_ref[...])
    qs = _dot(c_q, wqs_ref[...])
    kn = _dot(c_kv, wk_ref[...])
    v = _dot(c_kv, wv_ref[...])
    for c in range(v_ref.shape[0]):
        v_ref[c] = v[c * ATTN_TILE:(c + 1) * ATTN_TILE, :].T.astype(BF16)
    for h in range(MLA_HEADS):
        sl = slice(h * LANES, (h + 1) * LANES)
        q_ref[:, sl] = ((q[:, sl] * cos + qs[:, sl] * sin) * MLA_SCALE).astype(BF16)
        k_ref[:, sl] = (kn[:, sl] + kr_rot).astype(BF16)


def mla_prep(z, pos, f_lane, q_norm, kv_norm, wq, wqs, wk, wv, *, tm):
    t = z.shape[0]
    n = MLA_HEADS * LANES
    full = lambda a: pl.BlockSpec(a.shape, lambda i: (0,) * a.ndim)
    args = (f_lane, q_norm.reshape(1, -1), kv_norm.reshape(1, -1), wq, wqs, wk, wv)
    out = jax.ShapeDtypeStruct((t, n), BF16)
    out_vt = jax.ShapeDtypeStruct((t // ATTN_TILE, n, ATTN_TILE), BF16)
    ospec = pl.BlockSpec((tm, n), lambda i: (i, 0))
    vspec = pl.BlockSpec((tm // ATTN_TILE, n, ATTN_TILE), lambda i: (i, 0, 0))
    return pl.pallas_call(
        _mla_prep_kernel,
        grid=(t // tm,),
        in_specs=[pl.BlockSpec((tm, MLA_ZCOLS), lambda i: (i, 0)),
                  pl.BlockSpec((tm, 1), lambda i: (i, 0))] + [full(a) for a in args],
        out_specs=[ospec, ospec, vspec],
        out_shape=[out, out, out_vt],
        compiler_params=_cparams("parallel"),
        name="mla_prep",
    )(z, pos, *args)


ATTN_HEADS_PER_STEP = 8


def _attn_kernel(q_ref, k_ref, vt_ref, o_ref):
    tq = ATTN_TILE
    iq = pl.program_id(2)
    nh = q_ref.shape[2] // LANES
    heads = [slice(h * LANES, (h + 1) * LANES) for h in range(nh)]
    qs = [q_ref[0, :, sl] for sl in heads]

    def tile(j, carry, masked):
        start = pl.multiple_of(j * tq, tq)
        if masked:
            kc = lax.broadcasted_iota(jnp.int32, (tq, tq), 0) // MASK_CHUNK
            qc = lax.broadcasted_iota(jnp.int32, (tq, tq), 1) // MASK_CHUNK
            keep = kc <= qc
        hs = range(nh)
        s = [_dot_nt(k_ref[0, pl.ds(start, tq), heads[h]], qs[h]) for h in hs]
        if masked:
            s = [jnp.where(keep, x, NEG_INF) for x in s]
        m_new = [jnp.maximum(carry[h][0], jnp.max(s[h], axis=0, keepdims=True)) for h in hs]
        alpha = [jnp.exp(carry[h][0] - m_new[h]) for h in hs]
        p = [jnp.exp(s[h] - m_new[h]) for h in hs]
        l = [alpha[h] * carry[h][1] + jnp.sum(p[h], axis=0, keepdims=True) for h in hs]
        pv = [_dot(vt_ref[0, j, heads[h], :], p[h].astype(BF16)) for h in hs]
        return tuple((m_new[h], l[h], alpha[h] * carry[h][2] + pv[h]) for h in hs)

    init = tuple((jnp.full((1, tq), NEG_INF, F32), jnp.zeros((1, tq), F32), jnp.zeros((LANES, tq), F32))
                 for _ in heads)
    carry = lax.fori_loop(0, iq, lambda j, c: tile(j, c, False), init)
    carry = tile(iq, carry, True)
    for h, sl in enumerate(heads):
        m, l, acc = carry[h]
        o_ref[0, :, sl] = (acc / l).T.astype(BF16)


def attention(q, k, vt):
    bsz, seq, n = q.shape
    tq = ATTN_TILE
    w = ATTN_HEADS_PER_STEP * LANES
    qspec = pl.BlockSpec((1, tq, w), lambda b, h, i: (b, i, h))
    kspec = pl.BlockSpec((1, seq, w), lambda b, h, i: (b, 0, h))
    vspec = pl.BlockSpec((1, seq // tq, w, tq), lambda b, h, i: (b, 0, h, 0))
    return pl.pallas_call(
        _attn_kernel,
        grid=(bsz, n // w, seq // tq),
        in_specs=[qspec, kspec, vspec],
        out_specs=qspec,
        out_shape=jax.ShapeDtypeStruct((bsz, seq, n), BF16),
        compiler_params=_cparams("parallel", "parallel", "arbitrary"),
        name="attention",
    )(q, k, vt)


def _combine_kernel(x_ref, y_ref, r_ref, k_ref, v_ref, g_ref, o_ref, gates_ref,
                    bd_ref, rk_ref, gnw_ref, gnb_ref, wa_ref, wb_ref, wo_ref, out_ref):
    bd = bd_ref[...]
    inv_n = 1.0 / RW_HEAD_DIM
    y = y_ref[...]
    v = v_ref[...]
    mean = _seg_sum(y, bd) * inv_n
    yc = y - mean
    var = _seg_sum(yc * yc, bd) * inv_n
    yn = yc * lax.rsqrt(var + RW_GN_EPS) * gnw_ref[...] + gnb_ref[...]
    bonus = _seg_sum(r_ref[...] * k_ref[...] * rk_ref[...], bd) * v
    ya = _dot(((yn + bonus) * g_ref[...]).astype(BF16), wa_ref[...])
    yb = _dot(o_ref[...], wb_ref[...])
    d = ya.shape[1]
    gates = gates_ref[...]
    mix = gates[:, :d] * ya + gates[:, d:] * yb
    out_ref[...] = x_ref[...] + _dot(mix.astype(BF16), wo_ref[...])


def combine(x, y, r, k, v, g, o, gates, bd, r_k, gn_w, gn_b, wa, wb, wo, *, tm):
    t, d = x.shape
    row = lambda a: a.reshape(1, -1)
    tok = lambda a: pl.BlockSpec((tm, a.shape[1]), lambda i: (i, 0))
    full = lambda a: pl.BlockSpec(a.shape, lambda i: (0,) * a.ndim)
    toks = (x, y, r, k, v, g, o, gates)
    consts = (bd, row(r_k), row(gn_w), row(gn_b), wa, wb, wo)
    return pl.pallas_call(
        _combine_kernel,
        grid=(t // tm,),
        in_specs=[tok(a) for a in toks] + [full(a) for a in consts],
        out_specs=pl.BlockSpec((tm, d), lambda i: (i, 0)),
        out_shape=jax.ShapeDtypeStruct((t, d), F32),
        compiler_params=_cparams("parallel"),
        name="combine",
    )(*toks, *consts)


NOT_RANKED = 127.0
RANK_BLOCKS_PER_LOOP = 2
STAIR = tuple(PEER_TOPK // (ii + 1) for ii in range(PEER_TOPK))


def _top_ranks(s, k, exact_ties):
    n, t = s.shape
    key = lax.broadcasted_iota(jnp.int32, (n, LANES), 0)
    slot = lax.broadcasted_iota(jnp.int32, (k, LANES), 0)

    def one(r, s, rank, top):
        m = jnp.max(s, axis=0, keepdims=True)
        hit = s == m
        if exact_ties:
            hit = key == jnp.min(jnp.where(hit, key, n), axis=0, keepdims=True)
        rank = jnp.where(hit, jnp.asarray(r, F32), rank)
        s = jnp.where(hit, -jnp.inf, s)
        top = jnp.where(slot == r, m, top)
        return s, rank, top

    def body(r, carry):
        return tuple(one(r, *c) for c in carry)

    blocks = [s[:, c * LANES:(c + 1) * LANES] for c in range(t // LANES)]
    tops, ranks = [], []
    for g in range(0, len(blocks), RANK_BLOCKS_PER_LOOP):
        init = tuple((b, jnp.full((n, LANES), NOT_RANKED, F32), jnp.zeros((k, LANES), F32))
                     for b in blocks[g:g + RANK_BLOCKS_PER_LOOP])
        for _, rank, top in lax.fori_loop(0, k, body, init):
            tops.append(top)
            ranks.append(rank)
    return jnp.concatenate(tops, axis=1), jnp.concatenate(ranks, axis=1)


def _ranked_count_ok(rank, k):
    count = jnp.sum((rank < NOT_RANKED).astype(F32), axis=0, keepdims=True)
    return jnp.max(jnp.abs(count - k)) == 0.0


def _peer_route_kernel(q_ref, keys_ref, lam_ref, cc_ref, rho_ref, e1_ref):
    tm = q_ref.shape[0]
    k = PEER_TOPK
    neg = -jnp.inf

    def head(h, _):
        col = pl.multiple_of(h * 2 * HALF_Q, 2 * HALF_Q)
        s0 = _dot_nt(keys_ref[h, 0], q_ref[:, pl.ds(col, HALF_Q)])
        s1 = _dot_nt(keys_ref[h, 1], q_ref[:, pl.ds(col + HALF_Q, HALF_Q)])
        def rank_all(exact_ties):
            top0, rank0 = _top_ranks(s0, k, exact_ties)
            top1, rank1 = _top_ranks(s1, k, exact_ties)
            row8 = lax.broadcasted_iota(jnp.int32, (SUBLANES, tm), 0)
            groups = [top0[0:1] + top1[0:8], top0[0:1] + top1[8:16], top0[1:2] + top1[0:8]]
            for ii in range(2, 8):
                groups.append(jnp.where(row8 < STAIR[ii], top0[ii:ii + 1] + top1[0:8], neg))
            groups.append(top0[8:16] + top1[0:1])
            cand = jnp.concatenate(groups, axis=0)
            _, crank = _top_ranks(cand, k, exact_ties)
            return top0, rank0, top1, rank1, cand, crank

        quick = rank_all(False)
        tie_free = jnp.logical_and(
            jnp.logical_and(_ranked_count_ok(quick[1], k), _ranked_count_ok(quick[3], k)),
            _ranked_count_ok(quick[5], k))
        top0, rank0, top1, rank1, cand, crank = lax.cond(
            tie_free, lambda: quick, lambda: rank_all(True))
        sel = crank < NOT_RANKED
        ex = jnp.where(sel, jnp.exp(cand - cand[0:1]), 0.0)
        z = jnp.sum(ex, axis=0, keepdims=True)
        self = sel.astype(F32)
        counts = [jnp.sum(self[0:16], axis=0, keepdims=True)]
        for g in range(2, 9):
            counts.append(jnp.sum(self[8 * g:8 * g + 8], axis=0, keepdims=True))
        lvec = jnp.concatenate(counts + [self[72:80]], axis=0)

        lam = jnp.zeros((N_KEYS, tm), F32)
        for ii in range(k):
            lam = jnp.where(rank0 == float(ii), lvec[ii:ii + 1], lam)
        lam_ref[h] = lam
        cc_ref[h] = jnp.exp(s0 - top0[0:1]) / z
        rho_ref[h] = rank1.astype(BF16)
        e1_ref[h] = jnp.exp(s1 - top1[0:1]).astype(BF16)
        return 0

    lax.fori_loop(0, PEER_HEADS, head, 0)


def peer_route(qp, keys, *, tm):
    t = qp.shape[0]
    out = jax.ShapeDtypeStruct((PEER_HEADS, N_KEYS, t), F32)
    out_b = jax.ShapeDtypeStruct((PEER_HEADS, N_KEYS, t), BF16)
    ospec = pl.BlockSpec((PEER_HEADS, N_KEYS, tm), lambda i: (0, 0, i))
    return pl.pallas_call(
        _peer_route_kernel,
        grid=(t // tm,),
        in_specs=[pl.BlockSpec((tm, qp.shape[1]), lambda i: (i, 0)),
                  pl.BlockSpec(keys.shape, lambda i: (0, 0, 0, 0))],
        out_specs=[ospec] * 4,
        out_shape=[out, out, out_b, out_b],
        compiler_params=_cparams("parallel"),
        name="peer_route",
    )(qp, keys)


PEER_ROWS = 8
PEER_TOKEN_CHUNK = 256


def _erf(x):
    return lax.erf(x)


def _gelu(x):
    return 0.5 * x * (1.0 + _erf(x * (1.0 / math.sqrt(2.0))))


def _peer_expert_kernel(x_ref, gn_ref, u_ref, vt_ref, lam_ref, cc_ref, rho_ref, e1_ref, out_ref,
                        xt_ref, acc_ref, pre0_ref, pre1_ref):
    j = pl.program_id(1)
    tm = xt_ref.shape[1]
    chunks = [slice(c, c + PEER_TOKEN_CHUNK) for c in range(0, tm, PEER_TOKEN_CHUNK)]

    @pl.when(j == 0)
    def _():
        xt_ref[...] = _rms(x_ref[...], gn_ref[...]).T.astype(BF16)
        acc_ref[...] = jnp.zeros_like(acc_ref)
        pre1_ref[...] = jnp.zeros_like(pre1_ref)

    def step(fill_ref, drain_ref):
        for cols in chunks:
            fill_ref[:, cols] = _dot(u_ref[...], xt_ref[:, cols])
        for cols in chunks:
            gs = []
            for ii in range(PEER_ROWS):
                rows = slice(ii * N_KEYS, (ii + 1) * N_KEYS)
                gate = jnp.zeros((N_KEYS, PEER_TOKEN_CHUNK), BF16)
                for h in range(PEER_HEADS):
                    lam = lam_ref[h, ii:ii + 1, cols].astype(BF16)
                    cc = cc_ref[h, ii:ii + 1, cols].astype(BF16)
                    gate = gate + jnp.where(rho_ref[h, :, cols] < lam, e1_ref[h, :, cols] * cc,
                                            jnp.zeros((), BF16))
                gs.append(_gelu(drain_ref[rows, cols]).astype(BF16) * gate)
            acc_ref[:, cols] += _dot(vt_ref[0], jnp.concatenate(gs, axis=0))

    @pl.when(j % 2 == 0)
    def _():
        step(pre0_ref, pre1_ref)

    @pl.when(j % 2 == 1)
    def _():
        step(pre1_ref, pre0_ref)

    @pl.when(j == pl.num_programs(1) - 1)
    def _():
        out_ref[...] = acc_ref[...].T


def peer_expert(x, gain, u, vt, lam, cc, rho, e1, *, tm):
    t, d = x.shape
    nrow = PEER_ROWS * N_KEYS
    nblk = u.shape[0] // nrow
    stage = lambda lag: (lambda j: jnp.clip(j - lag, 0, nblk - 1))
    rspec = pl.BlockSpec((PEER_HEADS, PEER_ROWS, tm), lambda i, j: (0, stage(1)(j), i))
    cspec = pl.BlockSpec((PEER_HEADS, N_KEYS, tm), lambda i, j: (0, 0, i))
    return pl.pallas_call(
        _peer_expert_kernel,
        grid=(t // tm, nblk + 1),
        in_specs=[pl.BlockSpec((tm, d), lambda i, j: (i, 0)),
                  pl.BlockSpec((1, d), lambda i, j: (0, 0)),
                  pl.BlockSpec((nrow, d), lambda i, j: (stage(0)(j), 0)),
                  pl.BlockSpec((1, d, nrow), lambda i, j: (stage(1)(j), 0, 0)),
                  rspec, rspec, cspec, cspec],
        out_specs=pl.BlockSpec((tm, d), lambda i, j: (i, 0)),
        out_shape=jax.ShapeDtypeStruct((t, d), F32),
        scratch_shapes=[pltpu.VMEM((d, tm), BF16), pltpu.VMEM((d, tm), F32),
                        pltpu.VMEM((nrow, tm), F32), pltpu.VMEM((nrow, tm), F32)],
        compiler_params=_cparams("parallel", "arbitrary"),
        name="peer_expert",
    )(x, gain.reshape(1, d), u, vt, lam, cc, rho, e1)


def _ple_final_kernel(x_ref, f_ref, p_ref, gp_ref, gf_ref, wg_ref, wp_ref, out_ref):
    x = x_ref[...] + f_ref[...]
    gate = _sigmoid(_dot(_rms(x, gp_ref[...]).astype(BF16), wg_ref[...]))
    x = x + gate * _dot(p_ref[...].astype(BF16), wp_ref[...])
    out_ref[...] = _rms(x, gf_ref[...])


def ple_final(x, f, p, g_ple, g_final, wg, wp, *, tm):
    t, d = x.shape
    tok = lambda a: pl.BlockSpec((tm, a.shape[1]), lambda i: (i, 0))
    full = lambda a: pl.BlockSpec(a.shape, lambda i: (0,) * a.ndim)
    consts = (g_ple.reshape(1, d), g_final.reshape(1, d), wg, wp)
    return pl.pallas_call(
        _ple_final_kernel,
        grid=(t // tm,),
        in_specs=[tok(x), tok(f), tok(p)] + [full(a) for a in consts],
        out_specs=tok(x),
        out_shape=jax.ShapeDtypeStruct((t, d), F32),
        compiler_params=_cparams("parallel"),
        name="ple_final",
    )(x, f, p, *consts)


def _place(cols, width, offset):
    return jnp.pad(cols, ((0, 0), (offset, width - offset - cols.shape[1])))


def _rw_in_weights(w_rw, mu):
    o3 = 3 * RW_DIM
    lw, la = 64, 64
    segs = [w_rw[:, :o3], _place(w_rw[:, o3:o3 + lw], LANES, 0),
            _place(w_rw[:, o3 + lw:o3 + lw + la], LANES, 0), w_rw[:, o3 + lw + la:]]
    mus = [mu[None, :o3], _place(mu[None, o3:o3 + lw], LANES, 0),
           _place(mu[None, o3 + lw:o3 + lw + la], LANES, 0), mu[None, o3 + lw + la:]]
    return jnp.concatenate(segs, axis=1), jnp.concatenate(mus, axis=1)[0]


def _mla_in_weights(w_mla):
    half = QK_ROPE // 2
    lat = Q_LORA + KV_LORA
    kr = w_mla[:, lat:]
    kr_sw = jnp.concatenate([-kr[:, half:], kr[:, :half]], axis=1)
    return jnp.concatenate([w_mla[:, :lat], _place(kr, LANES, QK_NOPE), _place(kr_sw, LANES, QK_NOPE),
                            jnp.zeros((w_mla.shape[0], LANES), w_mla.dtype)], axis=1)


def _mla_up_weights(w_uq, w_ukv):
    half = QK_ROPE // 2
    qd = QK_NOPE + QK_ROPE
    wq = w_uq.reshape(Q_LORA, MLA_HEADS, qd)
    rope = wq[:, :, QK_NOPE:]
    rope_sw = jnp.concatenate([-rope[:, :, half:], rope[:, :, :half]], axis=2)
    pad = lambda t, off: jnp.pad(t, ((0, 0), (0, 0), (off, LANES - off - t.shape[2])))
    wq_pad = pad(wq, 0).reshape(Q_LORA, MLA_HEADS * LANES)
    wq_sw = pad(rope_sw, QK_NOPE).reshape(Q_LORA, MLA_HEADS * LANES)
    lane = jnp.arange(MLA_HEADS * LANES) % LANES
    wk = jnp.where(lane < QK_NOPE, w_ukv, 0.0)
    wv = jnp.where(lane < QK_NOPE, 0.0, w_ukv)
    return wq_pad, wq_sw, wk, wv


def kernel(x, p, positions, norm_mix, w_in, rw_mu, rw_w0, rw_w2, rw_a0, rw_a2, rw_g2, rw_k_k, rw_k_a, rw_r_k, rw_gn_w, rw_gn_b, rw_w_o, mla_q_norm, mla_w_uq, mla_kv_norm, mla_w_ukv, mla_w_o, w_out, norm_ffn, peer_w_q, peer_sub_keys, peer_u, peer_v, norm_ple, ple_w_gate, ple_w_proj, norm_final):
    bsz, seq, d = x.shape
    t = bsz * seq
    depth = p.shape[0]
    bf = lambda a: a.astype(BF16)
    rw_cols = 3 * RW_DIM + 64 + 64 + 128
    mla_cols = Q_LORA + KV_LORA + QK_ROPE

    head_of = jnp.arange(RW_DIM) // RW_HEAD_DIM
    bd = bf(head_of[:, None] == head_of[None, :])
    inv_freq = ROPE_THETA ** (-jnp.arange(0, QK_ROPE, 2, dtype=F32) / QK_ROPE)
    f_lane = _place(jnp.concatenate([inv_freq, inv_freq])[None, :], LANES, QK_NOPE)
    pos = positions.reshape(t, 1)

    xf = x.reshape(t, d)
    assert depth == 1, "the final RMSNorm is fused into the layer's last kernel"
    for i in range(depth):
        w_rw, mu = _rw_in_weights(w_in[i][:, :rw_cols], rw_mu[i])
        w_mla = _mla_in_weights(w_in[i][:, rw_cols:rw_cols + mla_cols])
        w_gates = w_in[i][:, rw_cols + mla_cols:]
        z_rw = norm_matmul(xf, norm_mix[i], bf(w_rw), tm=1024, tn=RW_ZCOLS // 3)
        z_mla = norm_matmul(xf, norm_mix[i], bf(w_mla), tm=1024, tn=MLA_ZCOLS)
        gates = norm_matmul(xf, norm_mix[i], bf(w_gates), tm=1024, tn=1024, act="sigmoid")

        pad_rows = lambda w: jnp.pad(w, ((0, LANES - w.shape[0]), (0, 0)))
        r, wl, k, v, a, b, g = rwkv_prep(
            z_rw.reshape(bsz, seq, RW_ZCOLS), mu, rw_w0[i], pad_rows(rw_w2[i]), rw_a0[i],
            pad_rows(rw_a2[i]), rw_g2[i], rw_k_k[i], rw_k_a[i], bd, tm=256)
        y = rwkv_scan(r, wl, k, v, a, b, rows=4)

        wq, wqs, wk, wv = _mla_up_weights(mla_w_uq[i], mla_w_ukv[i])
        q, kk, vv = mla_prep(z_mla, pos, f_lane, mla_q_norm[i], mla_kv_norm[i],
                             bf(wq), bf(wqs), bf(wk), bf(wv), tm=512)
        n = MLA_HEADS * LANES
        o = attention(q.reshape(bsz, seq, n), kk.reshape(bsz, seq, n),
                      vv.reshape(bsz, seq // ATTN_TILE, n, ATTN_TILE))
        wo_pad = jnp.pad(mla_w_o[i].reshape(MLA_HEADS, V_HEAD, d),
                         ((0, 0), (LANES - V_HEAD, 0), (0, 0))).reshape(n, d)

        flat = lambda a: a.reshape(t, -1)
        x1 = combine(xf, flat(y), flat(r), flat(k), flat(v), flat(g), o.reshape(t, n), gates,
                     bd, rw_r_k[i].reshape(-1), rw_gn_w[i], rw_gn_b[i],
                     bf(rw_w_o[i]), bf(wo_pad), bf(w_out[i]), tm=256)

        qp = norm_matmul(x1, norm_ffn[i], bf(peer_w_q[i]), tm=1024, tn=1024, out_dtype=BF16)
        lam, cc, rho, e1 = peer_route(qp, bf(peer_sub_keys[i]), tm=256)
        nrow = PEER_ROWS * N_KEYS
        vt = bf(peer_v[i]).reshape(-1, nrow, d).transpose(0, 2, 1)
        ffn = peer_expert(x1, norm_ffn[i], bf(peer_u[i]), vt, lam, cc, rho, e1, tm=512)

        xf = ple_final(x1, ffn, p[i].reshape(t, -1), norm_ple[i], norm_final,
                       bf(ple_w_gate[i]), bf(ple_w_proj[i]), tm=256)
    return xf.reshape(bsz, seq, d)
```

```python
import functools
import math

import jax
import jax.numpy as jnp
from jax import lax
from jax.experimental import pallas as pl
from jax.experimental.pallas import tpu as pltpu

F32 = jnp.float32
BF16 = jnp.bfloat16
HIGHEST = lax.Precision.HIGHEST

LANES = 128
SUBLANES = 8
VMEM_LIMIT = 56 * 1024 * 1024

EPS = 1e-6
RW_HEADS = 8
RW_HEAD_DIM = 64
RW_DIM = RW_HEADS * RW_HEAD_DIM
RW_GN_EPS = 64e-5
SCAN_CHUNK = 64
SEG_SUM_WIDTH = 256

MLA_HEADS = 8
QK_NOPE = 64
QK_ROPE = 32
V_HEAD = 64
Q_LORA = 384
KV_LORA = 256
ROPE_THETA = 10000.0
MASK_CHUNK = 64
NEG_INF = -1e30

PEER_HEADS = 8
N_KEYS = 128
PEER_TOPK = 16
HALF_Q = 128


def _cparams(*sem):
    return pltpu.CompilerParams(dimension_semantics=sem, vmem_limit_bytes=VMEM_LIMIT)


def _dot(a, b, precision=None):
    return jnp.dot(a, b, preferred_element_type=F32, precision=precision)


def _dot_nt(a, b, precision=None):
    return lax.dot_general(a, b, (((1,), (1,)), ((), ())),
                           preferred_element_type=F32, precision=precision)


def _rms(x, gain):
    return x * lax.rsqrt(jnp.mean(x * x, axis=-1, keepdims=True) + EPS) * gain


def _sigmoid(x):
    return 1.0 / (1.0 + jnp.exp(-x))


def _seg_sum(x, bd):
    hi = x.astype(BF16)
    lo = (x - hi.astype(F32)).astype(BF16)
    w = bd.shape[0]
    parts = [_dot(hi[:, c:c + w], bd) + _dot(lo[:, c:c + w], bd) for c in range(0, x.shape[1], w)]
    return jnp.concatenate(parts, axis=1)


def _norm_matmul_kernel(x_ref, g_ref, w_ref, o_ref, h_ref, *, act):
    @pl.when(pl.program_id(1) == 0)
    def _():
        h_ref[...] = _rms(x_ref[...], g_ref[...]).astype(BF16)

    y = _dot(h_ref[...], w_ref[...])
    if act == "sigmoid":
        y = _sigmoid(y)
    o_ref[...] = y.astype(o_ref.dtype)


def norm_matmul(x, gain, w, *, tm, tn, act=None, out_dtype=F32):
    t, d = x.shape
    n = w.shape[1]
    return pl.pallas_call(
        functools.partial(_norm_matmul_kernel, act=act),
        grid=(t // tm, n // tn),
        in_specs=[pl.BlockSpec((tm, d), lambda i, j: (i, 0)),
                  pl.BlockSpec((1, d), lambda i, j: (0, 0)),
                  pl.BlockSpec((d, tn), lambda i, j: (0, j))],
        out_specs=pl.BlockSpec((tm, tn), lambda i, j: (i, j)),
        out_shape=jax.ShapeDtypeStruct((t, n), out_dtype),
        scratch_shapes=[pltpu.VMEM((tm, d), BF16)],
        compiler_params=_cparams("parallel", "arbitrary"),
        name="norm_matmul",
    )(x, gain.reshape(1, d), w)


RW_ZCOLS = 3 * RW_DIM + 3 * LANES


def _rwkv_prep_kernel(z_ref, zp_ref, mu_ref, w0_ref, w2_ref, a0_ref, a2_ref, g2_ref,
                      kk_ref, ka_ref, bd_ref,
                      r_ref, wl_ref, k_ref, v_ref, a_ref, b_ref, g_ref):
    z = z_ref[0]
    tm = z.shape[0]
    prev_last = zp_ref[0][SUBLANES - 1:SUBLANES, :]
    prev_last = jnp.where(pl.program_id(1) == 0, 0.0, prev_last)
    rolled = pltpu.roll(z, 1, 0)
    row = lax.broadcasted_iota(jnp.int32, (tm, 1), 0)
    z_prev = jnp.where(row == 0, prev_last, rolled)
    z = z + mu_ref[...] * (z_prev - z)

    o1, o2, o3 = RW_DIM, 2 * RW_DIM, 3 * RW_DIM
    r, k, v = z[:, :o1], z[:, o1:o2], z[:, o2:o3]
    zw, za, zg = z[:, o3:o3 + LANES], z[:, o3 + LANES:o3 + 2 * LANES], z[:, o3 + 2 * LANES:]

    wpre = w0_ref[...] + _dot(jnp.tanh(zw), w2_ref[...])
    nx = -wpre
    softplus = jnp.maximum(nx, 0.0) + jnp.log(1.0 + jnp.exp(-jnp.abs(nx)))
    w = -softplus - 0.5
    iclr = _sigmoid(a0_ref[...] + _dot(za, a2_ref[...]))
    g = _dot(_sigmoid(zg), g2_ref[...])

    kk = k * kk_ref[...]
    ss = _seg_sum(kk * kk, bd_ref[...])
    kk = kk / jnp.maximum(jnp.sqrt(ss), 1e-12)

    r_ref[0] = r
    wl_ref[0] = -jnp.exp(w)
    k_ref[0] = k * (1.0 + (iclr - 1.0) * ka_ref[...])
    v_ref[0] = v
    a_ref[0] = -kk
    b_ref[0] = kk * iclr
    g_ref[0] = g


def rwkv_prep(z, mu, w0, w2, a0, a2, g2, k_k, k_a, bd, *, tm):
    bsz, seq, zc = z.shape
    d = RW_DIM
    row = lambda a: a.reshape(1, -1)
    full = lambda a: pl.BlockSpec(a.shape, lambda b, i: (0,) * a.ndim)
    args = (row(mu), row(w0), w2, row(a0), a2, g2, row(k_k), row(k_a), bd)
    out = jax.ShapeDtypeStruct((bsz, seq, d), F32)
    ospec = pl.BlockSpec((1, tm, d), lambda b, i: (b, i, 0))
    return pl.pallas_call(
        _rwkv_prep_kernel,
        grid=(bsz, seq // tm),
        in_specs=[pl.BlockSpec((1, tm, zc), lambda b, i: (b, i, 0)),
                  pl.BlockSpec((1, SUBLANES, zc),
                               lambda b, i: (b, jnp.maximum(i * (tm // SUBLANES) - 1, 0), 0))]
                 + [full(a) for a in args],
        out_specs=[ospec] * 7,
        out_shape=[out] * 7,
        compiler_params=_cparams("parallel", "arbitrary"),
        name="rwkv_prep",
    )(z, z, *args)


def _rwkv_scan_kernel(r_ref, wl_ref, k_ref, v_ref, a_ref, b_ref, y_ref, g_ref):
    c = SCAN_CHUNK
    hd = RW_HEAD_DIM
    npair = g_ref.shape[0]

    @pl.when(pl.program_id(1) == 0)
    def _():
        g_ref[...] = jnp.zeros_like(g_ref)

    ri = lax.broadcasted_iota(jnp.int32, (c, c), 0)
    ci = lax.broadcasted_iota(jnp.int32, (c, c), 1)
    tril = (ri >= ci).astype(F32)
    lane = lax.broadcasted_iota(jnp.int32, (c, 2 * hd), 1)
    m0 = lane < hd
    r2 = lax.broadcasted_iota(jnp.int32, (2 * c, 2 * c), 0)
    c2 = lax.broadcasted_iota(jnp.int32, (2 * c, 2 * c), 1)
    same = (r2 >= c) == (c2 >= c)
    strict = jnp.logical_and(same, r2 > c2)
    incl = jnp.logical_and(same, r2 >= c2)
    eye = (r2 == c2).astype(F32)

    bf = lambda x: x.astype(BF16)
    stack = lambda x: bf(jnp.concatenate([jnp.where(m0, x, 0.0), jnp.where(m0, 0.0, x)], axis=0))
    twice = lambda x: bf(jnp.concatenate([x, x], axis=0))
    pick = lambda s: jnp.where(m0, s[:c], s[c:])

    pairs = range(npair)
    each = lambda f, *cols: [f(*args) for args in zip(*cols)]
    per_row = r_ref.shape[2] // LANES
    where = [(hp // per_row, slice((hp % per_row) * LANES, (hp % per_row + 1) * LANES)) for hp in pairs]
    load = lambda ref: [ref[bi, :, sl] for bi, sl in where]
    r, wl, k, v, a, b = (load(x) for x in (r_ref, wl_ref, k_ref, v_ref, a_ref, b_ref))

    cs = each(lambda w: _dot(tril, w, HIGHEST), wl)
    cs_last = each(lambda s: s[c - 1:c, :], cs)
    p_inv = each(lambda s: jnp.exp(-s), cs)
    at = each(lambda a, s, w: a * jnp.exp(s - w), a, cs, wl)
    rt = each(lambda r, s: r * jnp.exp(s), r, cs)
    at_s, rt_s = each(stack, at), each(stack, rt)
    bt_s = each(lambda b, p: stack(b * p), b, p_inv)
    kt_s = each(lambda k, p: stack(k * p), k, p_inv)

    ab = each(lambda x, y: jnp.where(strict, _dot_nt(x, y), 0.0), at_s, bt_s)
    ak = each(lambda x, y: jnp.where(strict, _dot_nt(x, y), 0.0), at_s, kt_s)
    rb = each(lambda x, y: jnp.where(incl, _dot_nt(x, y), 0.0), rt_s, bt_s)
    rk = each(lambda x, y: jnp.where(incl, _dot_nt(x, y), 0.0), rt_s, kt_s)

    tinv = each(lambda m: eye + m, ab)
    x = each(bf, ab)
    for _ in range(int(math.log2(c)) - 1):
        x = each(lambda m: bf(_dot(m, m)), x)
        tinv = each(lambda t, m: t + _dot(bf(t), m), tinv, x)

    gt = [g_ref[hp] for hp in pairs]
    gtb = each(bf, gt)
    vv = each(twice, v)
    rhs = each(lambda at, g, ak, vv: _dot_nt(bf(at), g) + pick(_dot(bf(ak), vv)), at, gtb, ak, vv)
    u = each(lambda t, x: pick(_dot(bf(t), twice(x))), tinv, rhs)
    y = each(lambda rt, g, rb, u, rk, vv:
             _dot_nt(bf(rt), g) + pick(_dot(bf(rb), twice(u)) + _dot(bf(rk), vv)),
             rt, gtb, rb, u, rk, vv)
    upd = each(lambda u, b, v, k, s, sl:
               _dot(bf(u.T), bf(b * jnp.exp(sl - s))) + _dot(bf(v.T), bf(k * jnp.exp(sl - s))),
               u, b, v, k, cs, cs_last)
    for hp in pairs:
        bi, sl = where[hp]
        y_ref[bi, :, sl] = y[hp]
        g_ref[hp] = gt[hp] * jnp.exp(cs_last[hp]) + jnp.where(same, upd[hp], 0.0)


def rwkv_scan(r, wl, k, v, a, b, *, rows):
    bsz, seq, d = r.shape
    c = SCAN_CHUNK
    spec = pl.BlockSpec((rows, c, d), lambda bi, ci: (bi, ci, 0))
    return pl.pallas_call(
        _rwkv_scan_kernel,
        grid=(bsz // rows, seq // c),
        in_specs=[spec] * 6,
        out_specs=spec,
        out_shape=jax.ShapeDtypeStruct((bsz, seq, d), F32),
        scratch_shapes=[pltpu.VMEM((rows * d // LANES, LANES, LANES), F32)],
        compiler_params=_cparams("parallel", "arbitrary"),
        name="rwkv_scan",
    )(r, wl, k, v, a, b)


MLA_ZCOLS = 1024
MLA_SCALE = 1.0 / math.sqrt(QK_NOPE + QK_ROPE)
ATTN_TILE = 256


def _mla_prep_kernel(z_ref, pos_ref, fl_ref, qn_ref, kvn_ref, wq_ref, wqs_ref, wk_ref, wv_ref,
                     q_ref, k_ref, v_ref):
    z = z_ref[...]
    c_q = _rms(z[:, :Q_LORA], qn_ref[...]).astype(BF16)
    c_kv = _rms(z[:, Q_LORA:Q_LORA + KV_LORA], kvn_ref[...]).astype(BF16)
    kr = z[:, Q_LORA + KV_LORA:Q_LORA + KV_LORA + LANES]
    krs = z[:, Q_LORA + KV_LORA + LANES:Q_LORA + KV_LORA + 2 * LANES]

    ang = pos_ref[...].astype(F32) * fl_ref[...]
    cos, sin = jnp.cos(ang), jnp.sin(ang)
    kr_rot = kr * cos + krs * sin

    q = _dot(c_q, wq_ref[...])
    qs = _dot(c_q, wqs_ref[...])
    kn = _dot(c_kv, wk_ref[...])
    v = _dot(c_kv, wv_ref[...])
    for c in range(v_ref.shape[0]):
        v_ref[c] = v[c * ATTN_TILE:(c + 1) * ATTN_TILE, :].T.astype(BF16)
    for h in range(MLA_HEADS):
        sl = slice(h * LANES, (h + 1) * LANES)
        q_ref[:, sl] = ((q[:, sl] * cos + qs[:, sl] * sin) * MLA_SCALE).astype(BF16)
        k_ref[:, sl] = (kn[:, sl] + kr_rot).astype(BF16)


def mla_prep(z, pos, f_lane, q_norm, kv_norm, wq, wqs, wk, wv, *, tm):
    t = z.shape[0]
    n = MLA_HEADS * LANES
    full = lambda a: pl.BlockSpec(a.shape, lambda i: (0,) * a.ndim)
    args = (f_lane, q_norm.reshape(1, -1), kv_norm.reshape(1, -1), wq, wqs, wk, wv)
    out = jax.ShapeDtypeStruct((t, n), BF16)
    nv = wv.shape[1]
    out_vt = jax.ShapeDtypeStruct((t // ATTN_TILE, nv, ATTN_TILE), BF16)
    ospec = pl.BlockSpec((tm, n), lambda i: (i, 0))
    vspec = pl.BlockSpec((tm // ATTN_TILE, nv, ATTN_TILE), lambda i: (i, 0, 0))
    return pl.pallas_call(
        _mla_prep_kernel,
        grid=(t // tm,),
        in_specs=[pl.BlockSpec((tm, MLA_ZCOLS), lambda i: (i, 0)),
                  pl.BlockSpec((tm, 1), lambda i: (i, 0))] + [full(a) for a in args],
        out_specs=[ospec, ospec, vspec],
        out_shape=[out, out, out_vt],
        compiler_params=_cparams("parallel"),
        name="mla_prep",
    )(z, pos, *args)


ATTN_HEADS_PER_STEP = 8


def _attn_kernel(q_ref, k_ref, vt_ref, o_ref):
    tq = ATTN_TILE
    iq = pl.program_id(2)
    nh = q_ref.shape[2] // LANES
    heads = [slice(h * LANES, (h + 1) * LANES) for h in range(nh)]
    qs = [q_ref[0, :, sl] for sl in heads]

    def tile(j, carry, masked):
        start = pl.multiple_of(j * tq, tq)
        if masked:
            kc = lax.broadcasted_iota(jnp.int32, (tq, tq), 0) // MASK_CHUNK
            qc = lax.broadcasted_iota(jnp.int32, (tq, tq), 1) // MASK_CHUNK
            keep = kc <= qc
        hs = range(nh)
        s = [_dot_nt(k_ref[0, pl.ds(start, tq), heads[h]], qs[h]) for h in hs]
        if masked:
            s = [jnp.where(keep, x, NEG_INF) for x in s]
        m_new = [jnp.maximum(carry[h][0], jnp.max(s[h], axis=0, keepdims=True)) for h in hs]
        alpha = [jnp.exp(carry[h][0] - m_new[h]) for h in hs]
        p = [jnp.exp(s[h] - m_new[h]) for h in hs]
        l = [alpha[h] * carry[h][1] + jnp.sum(p[h], axis=0, keepdims=True) for h in hs]
        pv = [_dot(vt_ref[0, j, h * V_HEAD:(h + 1) * V_HEAD, :], p[h].astype(BF16))
              for h in hs]
        return tuple((m_new[h], l[h], alpha[h] * carry[h][2] + pv[h]) for h in hs)

    init = tuple((jnp.full((1, tq), NEG_INF, F32), jnp.zeros((1, tq), F32), jnp.zeros((V_HEAD, tq), F32))
                 for _ in heads)
    carry = lax.fori_loop(0, iq, lambda j, c: tile(j, c, False), init)
    carry = tile(iq, carry, True)
    out = [acc / l for _, l, acc in carry]
    for pair in range(nh // 2):
        both = jnp.concatenate(out[2 * pair:2 * pair + 2], axis=0)
        o_ref[0, :, pair * LANES:(pair + 1) * LANES] = both.T.astype(BF16)


def attention(q, k, vt):
    bsz, seq, n = q.shape
    tq = ATTN_TILE
    w = ATTN_HEADS_PER_STEP * LANES
    wv = ATTN_HEADS_PER_STEP * V_HEAD
    qspec = pl.BlockSpec((1, tq, w), lambda b, h, i: (b, i, h))
    kspec = pl.BlockSpec((1, seq, w), lambda b, h, i: (b, 0, h))
    vspec = pl.BlockSpec((1, seq // tq, wv, tq), lambda b, h, i: (b, 0, h, 0))
    return pl.pallas_call(
        _attn_kernel,
        grid=(bsz, n // w, seq // tq),
        in_specs=[qspec, kspec, vspec],
        out_specs=pl.BlockSpec((1, tq, wv), lambda b, h, i: (b, i, h)),
        out_shape=jax.ShapeDtypeStruct((bsz, seq, vt.shape[2]), BF16),
        compiler_params=_cparams("parallel", "parallel", "arbitrary"),
        name="attention",
    )(q, k, vt)


def _combine_kernel(x_ref, y_ref, r_ref, k_ref, v_ref, g_ref, o_ref, gates_ref,
                    bd_ref, rk_ref, gnw_ref, gnb_ref, wa_ref, wb_ref, wo_ref, out_ref):
    bd = bd_ref[...]
    inv_n = 1.0 / RW_HEAD_DIM
    y = y_ref[...]
    v = v_ref[...]
    mean = _seg_sum(y, bd) * inv_n
    yc = y - mean
    var = _seg_sum(yc * yc, bd) * inv_n
    yn = yc * lax.rsqrt(var + RW_GN_EPS) * gnw_ref[...] + gnb_ref[...]
    bonus = _seg_sum(r_ref[...] * k_ref[...] * rk_ref[...], bd) * v
    ya = _dot(((yn + bonus) * g_ref[...]).astype(BF16), wa_ref[...])
    yb = _dot(o_ref[...], wb_ref[...])
    d = ya.shape[1]
    gates = gates_ref[...]
    mix = gates[:, :d] * ya + gates[:, d:] * yb
    out_ref[...] = x_ref[...] + _dot(mix.astype(BF16), wo_ref[...])


def combine(x, y, r, k, v, g, o, gates, bd, r_k, gn_w, gn_b, wa, wb, wo, *, tm):
    t, d = x.shape
    row = lambda a: a.reshape(1, -1)
    tok = lambda a: pl.BlockSpec((tm, a.shape[1]), lambda i: (i, 0))
    full = lambda a: pl.BlockSpec(a.shape, lambda i: (0,) * a.ndim)
    toks = (x, y, r, k, v, g, o, gates)
    consts = (bd, row(r_k), row(gn_w), row(gn_b), wa, wb, wo)
    return pl.pallas_call(
        _combine_kernel,
        grid=(t // tm,),
        in_specs=[tok(a) for a in toks] + [full(a) for a in consts],
        out_specs=pl.BlockSpec((tm, d), lambda i: (i, 0)),
        out_shape=jax.ShapeDtypeStruct((t, d), F32),
        compiler_params=_cparams("parallel"),
        name="combine",
    )(*toks, *consts)


NOT_RANKED = 127.0
RANK_BLOCKS_PER_LOOP = 2
STAIR = tuple(PEER_TOPK // (ii + 1) for ii in range(PEER_TOPK))


def _top_ranks(s, k, exact_ties):
    n, t = s.shape
    key = lax.broadcasted_iota(jnp.int32, (n, LANES), 0)
    slot = lax.broadcasted_iota(jnp.int32, (k, LANES), 0)

    def one(r, s, rank, top):
        m = jnp.max(s, axis=0, keepdims=True)
        hit = s == m
        if exact_ties:
            hit = key == jnp.min(jnp.where(hit, key, n), axis=0, keepdims=True)
        rank = jnp.where(hit, jnp.asarray(r, F32), rank)
        s = jnp.where(hit, -jnp.inf, s)
        top = jnp.where(slot == r, m, top)
        return s, rank, top

    def body(r, carry):
        return tuple(one(r, *c) for c in carry)

    blocks = [s[:, c * LANES:(c + 1) * LANES] for c in range(t // LANES)]
    tops, ranks = [], []
    for g in range(0, len(blocks), RANK_BLOCKS_PER_LOOP):
        init = tuple((b, jnp.full((n, LANES), NOT_RANKED, F32), jnp.zeros((k, LANES), F32))
                     for b in blocks[g:g + RANK_BLOCKS_PER_LOOP])
        for _, rank, top in lax.fori_loop(0, k, body, init):
            tops.append(top)
            ranks.append(rank)
    return jnp.concatenate(tops, axis=1), jnp.concatenate(ranks, axis=1)


def _ranked_excess(rank, k):
    count = jnp.sum((rank < NOT_RANKED).astype(F32), axis=0, keepdims=True)
    return jnp.abs(count - k)


def _peer_route_kernel(q_ref, keys_ref, lam_ref, cc_ref, rho_ref, e1_ref):
    tm = q_ref.shape[0]
    k = PEER_TOPK
    neg = -jnp.inf

    def head(h, ties, exact_ties):
        col = pl.multiple_of(h * 2 * HALF_Q, 2 * HALF_Q)
        s0 = _dot_nt(keys_ref[h, 0], q_ref[:, pl.ds(col, HALF_Q)])
        s1 = _dot_nt(keys_ref[h, 1], q_ref[:, pl.ds(col + HALF_Q, HALF_Q)])
        top0, rank0 = _top_ranks(s0, k, exact_ties)
        top1, rank1 = _top_ranks(s1, k, exact_ties)
        row8 = lax.broadcasted_iota(jnp.int32, (SUBLANES, tm), 0)
        groups = [top0[0:1] + top1[0:8], top0[0:1] + top1[8:16], top0[1:2] + top1[0:8]]
        for ii in range(2, 8):
            groups.append(jnp.where(row8 < STAIR[ii], top0[ii:ii + 1] + top1[0:8], neg))
        groups.append(top0[8:16] + top1[0:1])
        cand = jnp.concatenate(groups, axis=0)
        _, crank = _top_ranks(cand, k, exact_ties)
        ties = ties + _ranked_excess(rank0, k) + _ranked_excess(rank1, k) + _ranked_excess(crank, k)
        sel = crank < NOT_RANKED
        ex = jnp.where(sel, jnp.exp(cand - cand[0:1]), 0.0)
        z = jnp.sum(ex, axis=0, keepdims=True)
        self = sel.astype(F32)
        counts = [jnp.sum(self[0:16], axis=0, keepdims=True)]
        for g in range(2, 9):
            counts.append(jnp.sum(self[8 * g:8 * g + 8], axis=0, keepdims=True))
        lvec = jnp.concatenate(counts + [self[72:80]], axis=0)

        lam = jnp.zeros((N_KEYS, tm), F32)
        for ii in range(k):
            lam = jnp.where(rank0 == float(ii), lvec[ii:ii + 1], lam)
        lam_ref[h] = lam
        cc_ref[h] = jnp.exp(s0 - top0[0:1]) / z
        rho_ref[h] = rank1.astype(BF16)
        e1_ref[h] = jnp.exp(s1 - top1[0:1]).astype(BF16)
        return ties

    no_ties = jnp.zeros((1, tm), F32)
    ties = lax.fori_loop(0, PEER_HEADS, lambda h, t: head(h, t, False), no_ties)

    @pl.when(jnp.max(ties) > 0.0)
    def _():
        lax.fori_loop(0, PEER_HEADS, lambda h, t: head(h, t, True), no_ties)


def peer_route(qp, keys, *, tm):
    t = qp.shape[0]
    out = jax.ShapeDtypeStruct((PEER_HEADS, N_KEYS, t), F32)
    out_b = jax.ShapeDtypeStruct((PEER_HEADS, N_KEYS, t), BF16)
    ospec = pl.BlockSpec((PEER_HEADS, N_KEYS, tm), lambda i: (0, 0, i))
    return pl.pallas_call(
        _peer_route_kernel,
        grid=(t // tm,),
        in_specs=[pl.BlockSpec((tm, qp.shape[1]), lambda i: (i, 0)),
                  pl.BlockSpec(keys.shape, lambda i: (0, 0, 0, 0))],
        out_specs=[ospec] * 4,
        out_shape=[out, out, out_b, out_b],
        compiler_params=_cparams("parallel"),
        name="peer_route",
    )(qp, keys)


PEER_ROWS = 8
PEER_TOKEN_CHUNK = 256


def _erf(x):
    return lax.erf(x)


def _gelu(x):
    return 0.5 * x * (1.0 + _erf(x * (1.0 / math.sqrt(2.0))))


def _peer_expert_kernel(x_ref, gn_ref, u_ref, vt_ref, lam_ref, cc_ref, rho_ref, e1_ref, out_ref,
                        xt_ref, acc_ref, pre0_ref, pre1_ref):
    j = pl.program_id(1)
    tm = xt_ref.shape[1]
    chunks = [slice(c, c + PEER_TOKEN_CHUNK) for c in range(0, tm, PEER_TOKEN_CHUNK)]

    @pl.when(j == 0)
    def _():
        xt_ref[...] = _rms(x_ref[...], gn_ref[...]).T.astype(BF16)
        acc_ref[...] = jnp.zeros_like(acc_ref)
        pre1_ref[...] = jnp.zeros_like(pre1_ref)

    def step(fill_ref, drain_ref):
        for cols in chunks:
            fill_ref[:, cols] = _dot(u_ref[...], xt_ref[:, cols])
        for cols in chunks:
            gs = []
            for ii in range(PEER_ROWS):
                rows = slice(ii * N_KEYS, (ii + 1) * N_KEYS)
                gate = jnp.zeros((N_KEYS, PEER_TOKEN_CHUNK), BF16)
                for h in range(PEER_HEADS):
                    lam = lam_ref[h, ii:ii + 1, cols].astype(BF16)
                    cc = cc_ref[h, ii:ii + 1, cols].astype(BF16)
                    gate = gate + jnp.where(rho_ref[h, :, cols] < lam, e1_ref[h, :, cols] * cc,
                                            jnp.zeros((), BF16))
                gs.append(_gelu(drain_ref[rows, cols]).astype(BF16) * gate)
            acc_ref[:, cols] += _dot(vt_ref[0], jnp.concatenate(gs, axis=0))

    @pl.when(j % 2 == 0)
    def _():
        step(pre0_ref, pre1_ref)

    @pl.when(j % 2 == 1)
    def _():
        step(pre1_ref, pre0_ref)

    @pl.when(j == pl.num_programs(1) - 1)
    def _():
        out_ref[...] = acc_ref[...].T


def peer_expert(x, gain, u, vt, lam, cc, rho, e1, *, tm):
    t, d = x.shape
    nrow = PEER_ROWS * N_KEYS
    nblk = u.shape[0] // nrow
    stage = lambda lag: (lambda j: jnp.clip(j - lag, 0, nblk - 1))
    rspec = pl.BlockSpec((PEER_HEADS, PEER_ROWS, tm), lambda i, j: (0, stage(1)(j), i))
    cspec = pl.BlockSpec((PEER_HEADS, N_KEYS, tm), lambda i, j: (0, 0, i))
    return pl.pallas_call(
        _peer_expert_kernel,
        grid=(t // tm, nblk + 1),
        in_specs=[pl.BlockSpec((tm, d), lambda i, j: (i, 0)),
                  pl.BlockSpec((1, d), lambda i, j: (0, 0)),
                  pl.BlockSpec((nrow, d), lambda i, j: (stage(0)(j), 0)),
                  pl.BlockSpec((1, d, nrow), lambda i, j: (stage(1)(j), 0, 0)),
                  rspec, rspec, cspec, cspec],
        out_specs=pl.BlockSpec((tm, d), lambda i, j: (i, 0)),
        out_shape=jax.ShapeDtypeStruct((t, d), F32),
        scratch_shapes=[pltpu.VMEM((d, tm), BF16), pltpu.VMEM((d, tm), F32),
                        pltpu.VMEM((nrow, tm), F32), pltpu.VMEM((nrow, tm), F32)],
        compiler_params=_cparams("parallel", "arbitrary"),
        name="peer_expert",
    )(x, gain.reshape(1, d), u, vt, lam, cc, rho, e1)


def _ple_final_kernel(x_ref, f_ref, p_ref, gp_ref, gf_ref, wg_ref, wp_ref, out_ref):
    x = x_ref[...] + f_ref[...]
    gate = _sigmoid(_dot(_rms(x, gp_ref[...]).astype(BF16), wg_ref[...]))
    x = x + gate * _dot(p_ref[...].astype(BF16), wp_ref[...])
    out_ref[...] = _rms(x, gf_ref[...])


def ple_final(x, f, p, g_ple, g_final, wg, wp, *, tm):
    t, d = x.shape
    tok = lambda a: pl.BlockSpec((tm, a.shape[1]), lambda i: (i, 0))
    full = lambda a: pl.BlockSpec(a.shape, lambda i: (0,) * a.ndim)
    consts = (g_ple.reshape(1, d), g_final.reshape(1, d), wg, wp)
    return pl.pallas_call(
        _ple_final_kernel,
        grid=(t // tm,),
        in_specs=[tok(x), tok(f), tok(p)] + [full(a) for a in consts],
        out_specs=tok(x),
        out_shape=jax.ShapeDtypeStruct((t, d), F32),
        compiler_params=_cparams("parallel"),
        name="ple_final",
    )(x, f, p, *consts)


def _place(cols, width, offset):
    return jnp.pad(cols, ((0, 0), (offset, width - offset - cols.shape[1])))


def _rw_in_weights(w_rw, mu):
    o3 = 3 * RW_DIM
    lw, la = 64, 64
    segs = [w_rw[:, :o3], _place(w_rw[:, o3:o3 + lw], LANES, 0),
            _place(w_rw[:, o3 + lw:o3 + lw + la], LANES, 0), w_rw[:, o3 + lw + la:]]
    mus = [mu[None, :o3], _place(mu[None, o3:o3 + lw], LANES, 0),
           _place(mu[None, o3 + lw:o3 + lw + la], LANES, 0), mu[None, o3 + lw + la:]]
    return jnp.concatenate(segs, axis=1), jnp.concatenate(mus, axis=1)[0]


def _mla_in_weights(w_mla):
    half = QK_ROPE // 2
    lat = Q_LORA + KV_LORA
    kr = w_mla[:, lat:]
    kr_sw = jnp.concatenate([-kr[:, half:], kr[:, :half]], axis=1)
    return jnp.concatenate([w_mla[:, :lat], _place(kr, LANES, QK_NOPE), _place(kr_sw, LANES, QK_NOPE),
                            jnp.zeros((w_mla.shape[0], LANES), w_mla.dtype)], axis=1)


def _mla_up_weights(w_uq, w_ukv):
    half = QK_ROPE // 2
    qd = QK_NOPE + QK_ROPE
    wq = w_uq.reshape(Q_LORA, MLA_HEADS, qd)
    rope = wq[:, :, QK_NOPE:]
    rope_sw = jnp.concatenate([-rope[:, :, half:], rope[:, :, :half]], axis=2)
    pad = lambda t, off: jnp.pad(t, ((0, 0), (0, 0), (off, LANES - off - t.shape[2])))
    wq_pad = pad(wq, 0).reshape(Q_LORA, MLA_HEADS * LANES)
    wq_sw = pad(rope_sw, QK_NOPE).reshape(Q_LORA, MLA_HEADS * LANES)
    lane = jnp.arange(MLA_HEADS * LANES) % LANES
    wk = jnp.where(lane < QK_NOPE, w_ukv, 0.0)
    wv = w_ukv.reshape(KV_LORA, MLA_HEADS, LANES)[:, :, QK_NOPE:].reshape(KV_LORA, MLA_HEADS * V_HEAD)
    return wq_pad, wq_sw, wk, wv


def kernel(x, p, positions, norm_mix, w_in, rw_mu, rw_w0, rw_w2, rw_a0, rw_a2, rw_g2, rw_k_k, rw_k_a, rw_r_k, rw_gn_w, rw_gn_b, rw_w_o, mla_q_norm, mla_w_uq, mla_kv_norm, mla_w_ukv, mla_w_o, w_out, norm_ffn, peer_w_q, peer_sub_keys, peer_u, peer_v, norm_ple, ple_w_gate, ple_w_proj, norm_final):
    bsz, seq, d = x.shape
    t = bsz * seq
    depth = p.shape[0]
    bf = lambda a: a.astype(BF16)
    rw_cols = 3 * RW_DIM + 64 + 64 + 128
    mla_cols = Q_LORA + KV_LORA + QK_ROPE

    head_of = jnp.arange(SEG_SUM_WIDTH) // RW_HEAD_DIM
    bd = bf(head_of[:, None] == head_of[None, :])
    inv_freq = ROPE_THETA ** (-jnp.arange(0, QK_ROPE, 2, dtype=F32) / QK_ROPE)
    f_lane = _place(jnp.concatenate([inv_freq, inv_freq])[None, :], LANES, QK_NOPE)
    pos = positions.reshape(t, 1)

    xf = x.reshape(t, d)
    assert depth == 1, "the final RMSNorm is fused into the layer's last kernel"
    for i in range(depth):
        w_rw, mu = _rw_in_weights(w_in[i][:, :rw_cols], rw_mu[i])
        w_mla = _mla_in_weights(w_in[i][:, rw_cols:rw_cols + mla_cols])
        w_gates = w_in[i][:, rw_cols + mla_cols:]
        z_rw = norm_matmul(xf, norm_mix[i], bf(w_rw), tm=1024, tn=RW_ZCOLS // 3)
        z_mla = norm_matmul(xf, norm_mix[i], bf(w_mla), tm=1024, tn=MLA_ZCOLS)
        gates = norm_matmul(xf, norm_mix[i], bf(w_gates), tm=1024, tn=1024, act="sigmoid")

        pad_rows = lambda w: jnp.pad(w, ((0, LANES - w.shape[0]), (0, 0)))
        r, wl, k, v, a, b, g = rwkv_prep(
            z_rw.reshape(bsz, seq, RW_ZCOLS), mu, rw_w0[i], pad_rows(rw_w2[i]), rw_a0[i],
            pad_rows(rw_a2[i]), rw_g2[i], rw_k_k[i], rw_k_a[i], bd, tm=256)
        y = rwkv_scan(r, wl, k, v, a, b, rows=math.gcd(bsz, 4))

        wq, wqs, wk, wv = _mla_up_weights(mla_w_uq[i], mla_w_ukv[i])
        q, kk, vv = mla_prep(z_mla, pos, f_lane, mla_q_norm[i], mla_kv_norm[i],
                             bf(wq), bf(wqs), bf(wk), bf(wv), tm=512)
        n = MLA_HEADS * LANES
        nv = MLA_HEADS * V_HEAD
        o = attention(q.reshape(bsz, seq, n), kk.reshape(bsz, seq, n),
                      vv.reshape(bsz, seq // ATTN_TILE, nv, ATTN_TILE))

        flat = lambda a: a.reshape(t, -1)
        x1 = combine(xf, flat(y), flat(r), flat(k), flat(v), flat(g), flat(o), gates,
                     bd, rw_r_k[i].reshape(-1), rw_gn_w[i], rw_gn_b[i],
                     bf(rw_w_o[i]), bf(mla_w_o[i]), bf(w_out[i]), tm=256)

        qp = norm_matmul(x1, norm_ffn[i], bf(peer_w_q[i]), tm=1024, tn=1024, out_dtype=BF16)
        lam, cc, rho, e1 = peer_route(qp, bf(peer_sub_keys[i]), tm=256)
        nrow = PEER_ROWS * N_KEYS
        vt = bf(peer_v[i]).reshape(-1, nrow, d).transpose(0, 2, 1)
        ffn = peer_expert(x1, norm_ffn[i], bf(peer_u[i]), vt, lam, cc, rho, e1, tm=512)

        xf = ple_final(x1, ffn, p[i].reshape(t, -1), norm_ple[i], norm_final,
                       bf(ple_w_gate[i]), bf(ple_w_proj[i]), tm=256)
    return xf.reshape(bsz, seq, d)
```

```python
import functools
import math

import jax
import jax.numpy as jnp
from jax import lax
from jax.experimental import pallas as pl
from jax.experimental.pallas import tpu as pltpu

F32 = jnp.float32
BF16 = jnp.bfloat16
HIGHEST = lax.Precision.HIGHEST

LANES = 128
SUBLANES = 8
VMEM_LIMIT = 56 * 1024 * 1024

EPS = 1e-6
RW_HEADS = 8
RW_HEAD_DIM = 64
RW_DIM = RW_HEADS * RW_HEAD_DIM
RW_GN_EPS = 64e-5
SCAN_CHUNK = 64
SEG_SUM_WIDTH = 256

MLA_HEADS = 8
QK_NOPE = 64
QK_ROPE = 32
V_HEAD = 64
Q_LORA = 384
KV_LORA = 256
ROPE_THETA = 10000.0
MASK_CHUNK = 64
NEG_INF = -1e30

PEER_HEADS = 8
N_KEYS = 128
PEER_TOPK = 16
HALF_Q = 128


def _cparams(*sem):
    return pltpu.CompilerParams(dimension_semantics=sem, vmem_limit_bytes=VMEM_LIMIT)


def _dot(a, b, precision=None):
    return jnp.dot(a, b, preferred_element_type=F32, precision=precision)


def _dot_nt(a, b, precision=None):
    return lax.dot_general(a, b, (((1,), (1,)), ((), ())),
                           preferred_element_type=F32, precision=precision)


def _rms(x, gain):
    return x * lax.rsqrt(jnp.mean(x * x, axis=-1, keepdims=True) + EPS) * gain


def _sigmoid(x):
    return 1.0 / (1.0 + jnp.exp(-x))


def _seg_sum(x, bd):
    hi = x.astype(BF16)
    lo = (x - hi.astype(F32)).astype(BF16)
    w = bd.shape[0]
    parts = [_dot(hi[:, c:c + w], bd) + _dot(lo[:, c:c + w], bd) for c in range(0, x.shape[1], w)]
    return jnp.concatenate(parts, axis=1)


def _norm_matmul_kernel(x_ref, g_ref, w_ref, o_ref, h_ref, *, act):
    @pl.when(pl.program_id(1) == 0)
    def _():
        h_ref[...] = _rms(x_ref[...], g_ref[...]).astype(BF16)

    y = _dot(h_ref[...], w_ref[...])
    if act == "sigmoid":
        y = _sigmoid(y)
    o_ref[...] = y.astype(o_ref.dtype)


def norm_matmul(x, gain, w, *, tm, tn, act=None, out_dtype=F32):
    t, d = x.shape
    n = w.shape[1]
    return pl.pallas_call(
        functools.partial(_norm_matmul_kernel, act=act),
        grid=(t // tm, n // tn),
        in_specs=[pl.BlockSpec((tm, d), lambda i, j: (i, 0)),
                  pl.BlockSpec((1, d), lambda i, j: (0, 0)),
                  pl.BlockSpec((d, tn), lambda i, j: (0, j))],
        out_specs=pl.BlockSpec((tm, tn), lambda i, j: (i, j)),
        out_shape=jax.ShapeDtypeStruct((t, n), out_dtype),
        scratch_shapes=[pltpu.VMEM((tm, d), BF16)],
        compiler_params=_cparams("parallel", "arbitrary"),
        name="norm_matmul",
    )(x, gain.reshape(1, d), w)


RW_ZCOLS = 3 * RW_DIM + 3 * LANES


def _rwkv_prep_kernel(z_ref, zp_ref, mu_ref, w0_ref, w2_ref, a0_ref, a2_ref, g2_ref,
                      kk_ref, ka_ref, bd_ref,
                      r_ref, wl_ref, k_ref, v_ref, a_ref, b_ref, g_ref):
    z = z_ref[0]
    tm = z.shape[0]
    prev_last = zp_ref[0][SUBLANES - 1:SUBLANES, :]
    prev_last = jnp.where(pl.program_id(1) == 0, 0.0, prev_last)
    rolled = pltpu.roll(z, 1, 0)
    row = lax.broadcasted_iota(jnp.int32, (tm, 1), 0)
    z_prev = jnp.where(row == 0, prev_last, rolled)
    z = z + mu_ref[...] * (z_prev - z)

    o1, o2, o3 = RW_DIM, 2 * RW_DIM, 3 * RW_DIM
    r, k, v = z[:, :o1], z[:, o1:o2], z[:, o2:o3]
    zw, za, zg = z[:, o3:o3 + LANES], z[:, o3 + LANES:o3 + 2 * LANES], z[:, o3 + 2 * LANES:]

    wpre = w0_ref[...] + _dot(jnp.tanh(zw), w2_ref[...])
    nx = -wpre
    softplus = jnp.maximum(nx, 0.0) + jnp.log(1.0 + jnp.exp(-jnp.abs(nx)))
    w = -softplus - 0.5
    iclr = _sigmoid(a0_ref[...] + _dot(za, a2_ref[...]))
    g = _dot(_sigmoid(zg), g2_ref[...])

    kk = k * kk_ref[...]
    ss = _seg_sum(kk * kk, bd_ref[...])
    kk = kk / jnp.maximum(jnp.sqrt(ss), 1e-12)

    r_ref[0] = r
    wl_ref[0] = -jnp.exp(w)
    k_ref[0] = k * (1.0 + (iclr - 1.0) * ka_ref[...])
    v_ref[0] = v
    a_ref[0] = -kk
    b_ref[0] = kk * iclr
    g_ref[0] = g


def rwkv_prep(z, mu, w0, w2, a0, a2, g2, k_k, k_a, bd, *, tm):
    bsz, seq, zc = z.shape
    d = RW_DIM
    row = lambda a: a.reshape(1, -1)
    full = lambda a: pl.BlockSpec(a.shape, lambda b, i: (0,) * a.ndim)
    args = (row(mu), row(w0), w2, row(a0), a2, g2, row(k_k), row(k_a), bd)
    out = jax.ShapeDtypeStruct((bsz, seq, d), F32)
    ospec = pl.BlockSpec((1, tm, d), lambda b, i: (b, i, 0))
    return pl.pallas_call(
        _rwkv_prep_kernel,
        grid=(bsz, seq // tm),
        in_specs=[pl.BlockSpec((1, tm, zc), lambda b, i: (b, i, 0)),
                  pl.BlockSpec((1, SUBLANES, zc),
                               lambda b, i: (b, jnp.maximum(i * (tm // SUBLANES) - 1, 0), 0))]
                 + [full(a) for a in args],
        out_specs=[ospec] * 7,
        out_shape=[out] * 7,
        compiler_params=_cparams("parallel", "arbitrary"),
        name="rwkv_prep",
    )(z, z, *args)


def _rwkv_scan_kernel(r_ref, wl_ref, k_ref, v_ref, a_ref, b_ref, y_ref, g_ref):
    c = SCAN_CHUNK
    hd = RW_HEAD_DIM
    npair = g_ref.shape[0]

    @pl.when(pl.program_id(1) == 0)
    def _():
        g_ref[...] = jnp.zeros_like(g_ref)

    ri = lax.broadcasted_iota(jnp.int32, (c, c), 0)
    ci = lax.broadcasted_iota(jnp.int32, (c, c), 1)
    tril = (ri >= ci).astype(F32)
    lane = lax.broadcasted_iota(jnp.int32, (c, 2 * hd), 1)
    m0 = lane < hd
    r2 = lax.broadcasted_iota(jnp.int32, (2 * c, 2 * c), 0)
    c2 = lax.broadcasted_iota(jnp.int32, (2 * c, 2 * c), 1)
    same = (r2 >= c) == (c2 >= c)
    strict = jnp.logical_and(same, r2 > c2)
    incl = jnp.logical_and(same, r2 >= c2)
    eye = (r2 == c2).astype(F32)

    bf = lambda x: x.astype(BF16)
    stack = lambda x: bf(jnp.concatenate([jnp.where(m0, x, 0.0), jnp.where(m0, 0.0, x)], axis=0))
    twice = lambda x: bf(jnp.concatenate([x, x], axis=0))
    pick = lambda s: jnp.where(m0, s[:c], s[c:])

    pairs = range(npair)
    each = lambda f, *cols: [f(*args) for args in zip(*cols)]
    per_row = r_ref.shape[2] // LANES
    where = [(hp // per_row, slice((hp % per_row) * LANES, (hp % per_row + 1) * LANES)) for hp in pairs]
    load = lambda ref: [ref[bi, :, sl] for bi, sl in where]
    r, wl, k, v, a, b = (load(x) for x in (r_ref, wl_ref, k_ref, v_ref, a_ref, b_ref))

    cs = each(lambda w: _dot(tril, w, HIGHEST), wl)
    cs_last = each(lambda s: s[c - 1:c, :], cs)
    p_inv = each(lambda s: jnp.exp(-s), cs)
    at = each(lambda a, s, w: a * jnp.exp(s - w), a, cs, wl)
    rt = each(lambda r, s: r * jnp.exp(s), r, cs)
    at_s, rt_s = each(stack, at), each(stack, rt)
    bt_s = each(lambda b, p: stack(b * p), b, p_inv)
    kt_s = each(lambda k, p: stack(k * p), k, p_inv)

    ab = each(lambda x, y: jnp.where(strict, _dot_nt(x, y), 0.0), at_s, bt_s)
    ak = each(lambda x, y: jnp.where(strict, _dot_nt(x, y), 0.0), at_s, kt_s)
    rb = each(lambda x, y: jnp.where(incl, _dot_nt(x, y), 0.0), rt_s, bt_s)
    rk = each(lambda x, y: jnp.where(incl, _dot_nt(x, y), 0.0), rt_s, kt_s)

    tinv = each(lambda m: eye + m, ab)
    x = each(bf, ab)
    for _ in range(int(math.log2(c)) - 1):
        x = each(lambda m: bf(_dot(m, m)), x)
        tinv = each(lambda t, m: t + _dot(bf(t), m), tinv, x)

    gt = [g_ref[hp] for hp in pairs]
    gtb = each(bf, gt)
    vv = each(twice, v)
    rhs = each(lambda at, g, ak, vv: _dot_nt(bf(at), g) + pick(_dot(bf(ak), vv)), at, gtb, ak, vv)
    u = each(lambda t, x: pick(_dot(bf(t), twice(x))), tinv, rhs)
    y = each(lambda rt, g, rb, u, rk, vv:
             _dot_nt(bf(rt), g) + pick(_dot(bf(rb), twice(u)) + _dot(bf(rk), vv)),
             rt, gtb, rb, u, rk, vv)
    upd = each(lambda u, b, v, k, s, sl:
               _dot(bf(u.T), bf(b * jnp.exp(sl - s))) + _dot(bf(v.T), bf(k * jnp.exp(sl - s))),
               u, b, v, k, cs, cs_last)
    for hp in pairs:
        bi, sl = where[hp]
        y_ref[bi, :, sl] = y[hp]
        g_ref[hp] = gt[hp] * jnp.exp(cs_last[hp]) + jnp.where(same, upd[hp], 0.0)


def rwkv_scan(r, wl, k, v, a, b, *, rows):
    bsz, seq, d = r.shape
    c = SCAN_CHUNK
    spec = pl.BlockSpec((rows, c, d), lambda bi, ci: (bi, ci, 0))
    return pl.pallas_call(
        _rwkv_scan_kernel,
        grid=(bsz // rows, seq // c),
        in_specs=[spec] * 6,
        out_specs=spec,
        out_shape=jax.ShapeDtypeStruct((bsz, seq, d), F32),
        scratch_shapes=[pltpu.VMEM((rows * d // LANES, LANES, LANES), F32)],
        compiler_params=_cparams("parallel", "arbitrary"),
        name="rwkv_scan",
    )(r, wl, k, v, a, b)


MLA_ZCOLS = 1024
MLA_SCALE = 1.0 / math.sqrt(QK_NOPE + QK_ROPE)
ATTN_TILE = 256


def _mla_prep_kernel(z_ref, pos_ref, fl_ref, qn_ref, kvn_ref, wq_ref, wqs_ref, wk_ref, wv_ref,
                     q_ref, k_ref, v_ref):
    z = z_ref[...]
    c_q = _rms(z[:, :Q_LORA], qn_ref[...]).astype(BF16)
    c_kv = _rms(z[:, Q_LORA:Q_LORA + KV_LORA], kvn_ref[...]).astype(BF16)
    kr = z[:, Q_LORA + KV_LORA:Q_LORA + KV_LORA + LANES]
    krs = z[:, Q_LORA + KV_LORA + LANES:Q_LORA + KV_LORA + 2 * LANES]

    ang = pos_ref[...].astype(F32) * fl_ref[...]
    cos, sin = jnp.cos(ang), jnp.sin(ang)
    kr_rot = kr * cos + krs * sin

    q = _dot(c_q, wq_ref[...])
    qs = _dot(c_q, wqs_ref[...])
    kn = _dot(c_kv, wk_ref[...])
    v = _dot(c_kv, wv_ref[...])
    for c in range(v_ref.shape[0]):
        v_ref[c] = v[c * ATTN_TILE:(c + 1) * ATTN_TILE, :].T.astype(BF16)
    for h in range(MLA_HEADS):
        sl = slice(h * LANES, (h + 1) * LANES)
        q_ref[:, sl] = ((q[:, sl] * cos + qs[:, sl] * sin) * MLA_SCALE).astype(BF16)
        k_ref[:, sl] = (kn[:, sl] + kr_rot).astype(BF16)


def mla_prep(z, pos, f_lane, q_norm, kv_norm, wq, wqs, wk, wv, *, tm):
    t = z.shape[0]
    n = MLA_HEADS * LANES
    full = lambda a: pl.BlockSpec(a.shape, lambda i: (0,) * a.ndim)
    args = (f_lane, q_norm.reshape(1, -1), kv_norm.reshape(1, -1), wq, wqs, wk, wv)
    out = jax.ShapeDtypeStruct((t, n), BF16)
    nv = wv.shape[1]
    out_vt = jax.ShapeDtypeStruct((t // ATTN_TILE, nv, ATTN_TILE), BF16)
    ospec = pl.BlockSpec((tm, n), lambda i: (i, 0))
    vspec = pl.BlockSpec((tm // ATTN_TILE, nv, ATTN_TILE), lambda i: (i, 0, 0))
    return pl.pallas_call(
        _mla_prep_kernel,
        grid=(t // tm,),
        in_specs=[pl.BlockSpec((tm, MLA_ZCOLS), lambda i: (i, 0)),
                  pl.BlockSpec((tm, 1), lambda i: (i, 0))] + [full(a) for a in args],
        out_specs=[ospec, ospec, vspec],
        out_shape=[out, out, out_vt],
        compiler_params=_cparams("parallel"),
        name="mla_prep",
    )(z, pos, *args)


ATTN_HEADS_PER_STEP = 8


def _attn_kernel(q_ref, k_ref, vt_ref, o_ref):
    tq = ATTN_TILE
    iq = pl.program_id(2)
    nh = q_ref.shape[2] // LANES
    heads = [slice(h * LANES, (h + 1) * LANES) for h in range(nh)]
    qs = [q_ref[0, :, sl] for sl in heads]

    def tile(j, carry, masked):
        start = pl.multiple_of(j * tq, tq)
        if masked:
            kc = lax.broadcasted_iota(jnp.int32, (tq, tq), 0) // MASK_CHUNK
            qc = lax.broadcasted_iota(jnp.int32, (tq, tq), 1) // MASK_CHUNK
            keep = kc <= qc
        hs = range(nh)
        s = [_dot_nt(k_ref[0, pl.ds(start, tq), heads[h]], qs[h]) for h in hs]
        if masked:
            s = [jnp.where(keep, x, NEG_INF) for x in s]
        m_new = [jnp.maximum(carry[h][0], jnp.max(s[h], axis=0, keepdims=True)) for h in hs]
        alpha = [jnp.exp(carry[h][0] - m_new[h]) for h in hs]
        p = [jnp.exp(s[h] - m_new[h]) for h in hs]
        l = [alpha[h] * carry[h][1] + jnp.sum(p[h], axis=0, keepdims=True) for h in hs]
        pv = [_dot(vt_ref[0, j, h * V_HEAD:(h + 1) * V_HEAD, :], p[h].astype(BF16))
              for h in hs]
        return tuple((m_new[h], l[h], alpha[h] * carry[h][2] + pv[h]) for h in hs)

    init = tuple((jnp.full((1, tq), NEG_INF, F32), jnp.zeros((1, tq), F32), jnp.zeros((V_HEAD, tq), F32))
                 for _ in heads)
    carry = lax.fori_loop(0, iq, lambda j, c: tile(j, c, False), init)
    carry = tile(iq, carry, True)
    out = [acc / l for _, l, acc in carry]
    for pair in range(nh // 2):
        both = jnp.concatenate(out[2 * pair:2 * pair + 2], axis=0)
        o_ref[0, :, pair * LANES:(pair + 1) * LANES] = both.T.astype(BF16)


def attention(q, k, vt):
    bsz, seq, n = q.shape
    tq = ATTN_TILE
    w = ATTN_HEADS_PER_STEP * LANES
    wv = ATTN_HEADS_PER_STEP * V_HEAD
    qspec = pl.BlockSpec((1, tq, w), lambda b, h, i: (b, i, h))
    kspec = pl.BlockSpec((1, seq, w), lambda b, h, i: (b, 0, h))
    vspec = pl.BlockSpec((1, seq // tq, wv, tq), lambda b, h, i: (b, 0, h, 0))
    return pl.pallas_call(
        _attn_kernel,
        grid=(bsz, n // w, seq // tq),
        in_specs=[qspec, kspec, vspec],
        out_specs=pl.BlockSpec((1, tq, wv), lambda b, h, i: (b, i, h)),
        out_shape=jax.ShapeDtypeStruct((bsz, seq, vt.shape[2]), BF16),
        compiler_params=_cparams("parallel", "parallel", "arbitrary"),
        name="attention",
    )(q, k, vt)


def _combine_kernel(x_ref, y_ref, r_ref, k_ref, v_ref, g_ref, o_ref, gates_ref,
                    bd_ref, rk_ref, gnw_ref, gnb_ref, wa_ref, wb_ref, wo_ref, out_ref):
    bd = bd_ref[...]
    inv_n = 1.0 / RW_HEAD_DIM
    y = y_ref[...]
    v = v_ref[...]
    mean = _seg_sum(y, bd) * inv_n
    yc = y - mean
    var = _seg_sum(yc * yc, bd) * inv_n
    yn = yc * lax.rsqrt(var + RW_GN_EPS) * gnw_ref[...] + gnb_ref[...]
    bonus = _seg_sum(r_ref[...] * k_ref[...] * rk_ref[...], bd) * v
    ya = _dot(((yn + bonus) * g_ref[...]).astype(BF16), wa_ref[...])
    yb = _dot(o_ref[...], wb_ref[...])
    d = ya.shape[1]
    gates = gates_ref[...]
    mix = gates[:, :d] * ya + gates[:, d:] * yb
    out_ref[...] = x_ref[...] + _dot(mix.astype(BF16), wo_ref[...])


def combine(x, y, r, k, v, g, o, gates, bd, r_k, gn_w, gn_b, wa, wb, wo, *, tm):
    t, d = x.shape
    row = lambda a: a.reshape(1, -1)
    tok = lambda a: pl.BlockSpec((tm, a.shape[1]), lambda i: (i, 0))
    full = lambda a: pl.BlockSpec(a.shape, lambda i: (0,) * a.ndim)
    toks = (x, y, r, k, v, g, o, gates)
    consts = (bd, row(r_k), row(gn_w), row(gn_b), wa, wb, wo)
    return pl.pallas_call(
        _combine_kernel,
        grid=(t // tm,),
        in_specs=[tok(a) for a in toks] + [full(a) for a in consts],
        out_specs=pl.BlockSpec((tm, d), lambda i: (i, 0)),
        out_shape=jax.ShapeDtypeStruct((t, d), F32),
        compiler_params=_cparams("parallel"),
        name="combine",
    )(*toks, *consts)


NOT_RANKED = 127.0
RANK_BLOCKS_PER_LOOP = 2
STAIR = tuple(PEER_TOPK // (ii + 1) for ii in range(PEER_TOPK))


def _top_ranks(s, k):
    n, t = s.shape
    key = lax.broadcasted_iota(jnp.int32, (n, LANES), 0)
    slot = lax.broadcasted_iota(jnp.int32, (k, LANES), 0)

    def one(exact_ties, r, s, rank, top):
        m = jnp.max(s, axis=0, keepdims=True)
        hit = s == m
        if exact_ties:
            hit = key == jnp.min(jnp.where(hit, key, n), axis=0, keepdims=True)
        rank = jnp.where(hit, jnp.asarray(r, F32), rank)
        s = jnp.where(hit, -jnp.inf, s)
        top = jnp.where(slot == r, m, top)
        return s, rank, top

    def rank_group(blocks, exact_ties):
        init = tuple((b, jnp.full((n, LANES), NOT_RANKED, F32), jnp.zeros((k, LANES), F32))
                     for b in blocks)
        body = lambda r, carry: tuple(one(exact_ties, r, *c) for c in carry)
        res = lax.fori_loop(0, k, body, init)
        return tuple(top for _, _, top in res), tuple(rank for _, rank, _ in res)

    blocks = [s[:, c * LANES:(c + 1) * LANES] for c in range(t // LANES)]
    tops, ranks = [], []
    for g in range(0, len(blocks), RANK_BLOCKS_PER_LOOP):
        group = blocks[g:g + RANK_BLOCKS_PER_LOOP]
        quick = rank_group(group, False)
        excess = sum(_ranked_excess(rank, k) for rank in quick[1])
        top, rank = lax.cond(jnp.max(excess) == 0.0, lambda: quick, lambda: rank_group(group, True))
        tops.extend(top)
        ranks.extend(rank)
    return jnp.concatenate(tops, axis=1), jnp.concatenate(ranks, axis=1)


def _ranked_excess(rank, k):
    count = jnp.sum((rank < NOT_RANKED).astype(F32), axis=0, keepdims=True)
    return jnp.abs(count - k)


def _peer_route_kernel(q_ref, keys_ref, lam_ref, cc_ref, rho_ref, e1_ref):
    tm = q_ref.shape[0]
    k = PEER_TOPK
    neg = -jnp.inf

    def head(h, _):
        col = pl.multiple_of(h * 2 * HALF_Q, 2 * HALF_Q)
        s0 = _dot_nt(keys_ref[h, 0], q_ref[:, pl.ds(col, HALF_Q)])
        s1 = _dot_nt(keys_ref[h, 1], q_ref[:, pl.ds(col + HALF_Q, HALF_Q)])
        top0, rank0 = _top_ranks(s0, k)
        top1, rank1 = _top_ranks(s1, k)
        row8 = lax.broadcasted_iota(jnp.int32, (SUBLANES, tm), 0)
        groups = [top0[0:1] + top1[0:8], top0[0:1] + top1[8:16], top0[1:2] + top1[0:8]]
        for ii in range(2, 8):
            groups.append(jnp.where(row8 < STAIR[ii], top0[ii:ii + 1] + top1[0:8], neg))
        groups.append(top0[8:16] + top1[0:1])
        cand = jnp.concatenate(groups, axis=0)
        _, crank = _top_ranks(cand, k)
        sel = crank < NOT_RANKED
        ex = jnp.where(sel, jnp.exp(cand - cand[0:1]), 0.0)
        z = jnp.sum(ex, axis=0, keepdims=True)
        self = sel.astype(F32)
        counts = [jnp.sum(self[0:16], axis=0, keepdims=True)]
        for g in range(2, 9):
            counts.append(jnp.sum(self[8 * g:8 * g + 8], axis=0, keepdims=True))
        lvec = jnp.concatenate(counts + [self[72:80]], axis=0)

        lam = jnp.zeros((N_KEYS, tm), F32)
        for ii in range(k):
            lam = jnp.where(rank0 == float(ii), lvec[ii:ii + 1], lam)
        lam_ref[h] = lam
        cc_ref[h] = jnp.exp(s0 - top0[0:1]) / z
        rho_ref[h] = rank1.astype(BF16)
        e1_ref[h] = jnp.exp(s1 - top1[0:1]).astype(BF16)
        return 0

    lax.fori_loop(0, PEER_HEADS, head, 0)


def peer_route(qp, keys, *, tm):
    t = qp.shape[0]
    out = jax.ShapeDtypeStruct((PEER_HEADS, N_KEYS, t), F32)
    out_b = jax.ShapeDtypeStruct((PEER_HEADS, N_KEYS, t), BF16)
    ospec = pl.BlockSpec((PEER_HEADS, N_KEYS, tm), lambda i: (0, 0, i))
    return pl.pallas_call(
        _peer_route_kernel,
        grid=(t // tm,),
        in_specs=[pl.BlockSpec((tm, qp.shape[1]), lambda i: (i, 0)),
                  pl.BlockSpec(keys.shape, lambda i: (0, 0, 0, 0))],
        out_specs=[ospec] * 4,
        out_shape=[out, out, out_b, out_b],
        compiler_params=_cparams("parallel"),
        name="peer_route",
    )(qp, keys)


PEER_ROWS = 8
PEER_TOKEN_CHUNK = 256


def _erf(x):
    return lax.erf(x)


def _gelu(x):
    return 0.5 * x * (1.0 + _erf(x * (1.0 / math.sqrt(2.0))))


def _peer_expert_kernel(x_ref, gn_ref, u_ref, vt_ref, lam_ref, cc_ref, rho_ref, e1_ref, out_ref,
                        xt_ref, acc_ref, pre0_ref, pre1_ref):
    j = pl.program_id(1)
    tm = xt_ref.shape[1]
    chunks = [slice(c, c + PEER_TOKEN_CHUNK) for c in range(0, tm, PEER_TOKEN_CHUNK)]

    @pl.when(j == 0)
    def _():
        xt_ref[...] = _rms(x_ref[...], gn_ref[...]).T.astype(BF16)
        acc_ref[...] = jnp.zeros_like(acc_ref)
        pre1_ref[...] = jnp.zeros_like(pre1_ref)

    def step(fill_ref, drain_ref):
        for cols in chunks:
            fill_ref[:, cols] = _dot(u_ref[...], xt_ref[:, cols])
        for cols in chunks:
            gs = []
            for ii in range(PEER_ROWS):
                rows = slice(ii * N_KEYS, (ii + 1) * N_KEYS)
                gate = jnp.zeros((N_KEYS, PEER_TOKEN_CHUNK), BF16)
                for h in range(PEER_HEADS):
                    lam = lam_ref[h, ii:ii + 1, cols].astype(BF16)
                    cc = cc_ref[h, ii:ii + 1, cols].astype(BF16)
                    gate = gate + jnp.where(rho_ref[h, :, cols] < lam, e1_ref[h, :, cols] * cc,
                                            jnp.zeros((), BF16))
                gs.append(_gelu(drain_ref[rows, cols]).astype(BF16) * gate)
            acc_ref[:, cols] += _dot(vt_ref[0], jnp.concatenate(gs, axis=0))

    @pl.when(j % 2 == 0)
    def _():
        step(pre0_ref, pre1_ref)

    @pl.when(j % 2 == 1)
    def _():
        step(pre1_ref, pre0_ref)

    @pl.when(j == pl.num_programs(1) - 1)
    def _():
        out_ref[...] = acc_ref[...].T


def peer_expert(x, gain, u, vt, lam, cc, rho, e1, *, tm):
    t, d = x.shape
    nrow = PEER_ROWS * N_KEYS
    nblk = u.shape[0] // nrow
    stage = lambda lag: (lambda j: jnp.clip(j - lag, 0, nblk - 1))
    rspec = pl.BlockSpec((PEER_HEADS, PEER_ROWS, tm), lambda i, j: (0, stage(1)(j), i))
    cspec = pl.BlockSpec((PEER_HEADS, N_KEYS, tm), lambda i, j: (0, 0, i))
    return pl.pallas_call(
        _peer_expert_kernel,
        grid=(t // tm, nblk + 1),
        in_specs=[pl.BlockSpec((tm, d), lambda i, j: (i, 0)),
                  pl.BlockSpec((1, d), lambda i, j: (0, 0)),
                  pl.BlockSpec((nrow, d), lambda i, j: (stage(0)(j), 0)),
                  pl.BlockSpec((1, d, nrow), lambda i, j: (stage(1)(j), 0, 0)),
                  rspec, rspec, cspec, cspec],
        out_specs=pl.BlockSpec((tm, d), lambda i, j: (i, 0)),
        out_shape=jax.ShapeDtypeStruct((t, d), F32),
        scratch_shapes=[pltpu.VMEM((d, tm), BF16), pltpu.VMEM((d, tm), F32),
                        pltpu.VMEM((nrow, tm), F32), pltpu.VMEM((nrow, tm), F32)],
        compiler_params=_cparams("parallel", "arbitrary"),
        name="peer_expert",
    )(x, gain.reshape(1, d), u, vt, lam, cc, rho, e1)


def _ple_final_kernel(x_ref, f_ref, p_ref, gp_ref, gf_ref, wg_ref, wp_ref, out_ref):
    x = x_ref[...] + f_ref[...]
    gate = _sigmoid(_dot(_rms(x, gp_ref[...]).astype(BF16), wg_ref[...]))
    x = x + gate * _dot(p_ref[...].astype(BF16), wp_ref[...])
    out_ref[...] = _rms(x, gf_ref[...])


def ple_final(x, f, p, g_ple, g_final, wg, wp, *, tm):
    t, d = x.shape
    tok = lambda a: pl.BlockSpec((tm, a.shape[1]), lambda i: (i, 0))
    full = lambda a: pl.BlockSpec(a.shape, lambda i: (0,) * a.ndim)
    consts = (g_ple.reshape(1, d), g_final.reshape(1, d), wg, wp)
    return pl.pallas_call(
        _ple_final_kernel,
        grid=(t // tm,),
        in_specs=[tok(x), tok(f), tok(p)] + [full(a) for a in consts],
        out_specs=tok(x),
        out_shape=jax.ShapeDtypeStruct((t, d), F32),
        compiler_params=_cparams("parallel"),
        name="ple_final",
    )(x, f, p, *consts)


def _place(cols, width, offset):
    return jnp.pad(cols, ((0, 0), (offset, width - offset - cols.shape[1])))


def _rw_in_weights(w_rw, mu):
    o3 = 3 * RW_DIM
    lw, la = 64, 64
    segs = [w_rw[:, :o3], _place(w_rw[:, o3:o3 + lw], LANES, 0),
            _place(w_rw[:, o3 + lw:o3 + lw + la], LANES, 0), w_rw[:, o3 + lw + la:]]
    mus = [mu[None, :o3], _place(mu[None, o3:o3 + lw], LANES, 0),
           _place(mu[None, o3 + lw:o3 + lw + la], LANES, 0), mu[None, o3 + lw + la:]]
    return jnp.concatenate(segs, axis=1), jnp.concatenate(mus, axis=1)[0]


def _mla_in_weights(w_mla):
    half = QK_ROPE // 2
    lat = Q_LORA + KV_LORA
    kr = w_mla[:, lat:]
    kr_sw = jnp.concatenate([-kr[:, half:], kr[:, :half]], axis=1)
    return jnp.concatenate([w_mla[:, :lat], _place(kr, LANES, QK_NOPE), _place(kr_sw, LANES, QK_NOPE),
                            jnp.zeros((w_mla.shape[0], LANES), w_mla.dtype)], axis=1)


def _mla_up_weights(w_uq, w_ukv):
    half = QK_ROPE // 2
    qd = QK_NOPE + QK_ROPE
    wq = w_uq.reshape(Q_LORA, MLA_HEADS, qd)
    rope = wq[:, :, QK_NOPE:]
    rope_sw = jnp.concatenate([-rope[:, :, half:], rope[:, :, :half]], axis=2)
    pad = lambda t, off: jnp.pad(t, ((0, 0), (0, 0), (off, LANES - off - t.shape[2])))
    wq_pad = pad(wq, 0).reshape(Q_LORA, MLA_HEADS * LANES)
    wq_sw = pad(rope_sw, QK_NOPE).reshape(Q_LORA, MLA_HEADS * LANES)
    lane = jnp.arange(MLA_HEADS * LANES) % LANES
    wk = jnp.where(lane < QK_NOPE, w_ukv, 0.0)
    wv = w_ukv.reshape(KV_LORA, MLA_HEADS, LANES)[:, :, QK_NOPE:].reshape(KV_LORA, MLA_HEADS * V_HEAD)
    return wq_pad, wq_sw, wk, wv


def kernel(x, p, positions, norm_mix, w_in, rw_mu, rw_w0, rw_w2, rw_a0, rw_a2, rw_g2, rw_k_k, rw_k_a, rw_r_k, rw_gn_w, rw_gn_b, rw_w_o, mla_q_norm, mla_w_uq, mla_kv_norm, mla_w_ukv, mla_w_o, w_out, norm_ffn, peer_w_q, peer_sub_keys, peer_u, peer_v, norm_ple, ple_w_gate, ple_w_proj, norm_final):
    bsz, seq, d = x.shape
    t = bsz * seq
    depth = p.shape[0]
    bf = lambda a: a.astype(BF16)
    rw_cols = 3 * RW_DIM + 64 + 64 + 128
    mla_cols = Q_LORA + KV_LORA + QK_ROPE

    head_of = jnp.arange(SEG_SUM_WIDTH) // RW_HEAD_DIM
    bd = bf(head_of[:, None] == head_of[None, :])
    inv_freq = ROPE_THETA ** (-jnp.arange(0, QK_ROPE, 2, dtype=F32) / QK_ROPE)
    f_lane = _place(jnp.concatenate([inv_freq, inv_freq])[None, :], LANES, QK_NOPE)
    pos = positions.reshape(t, 1)

    xf = x.reshape(t, d)
    assert depth == 1, "the final RMSNorm is fused into the layer's last kernel"
    for i in range(depth):
        w_rw, mu = _rw_in_weights(w_in[i][:, :rw_cols], rw_mu[i])
        w_mla = _mla_in_weights(w_in[i][:, rw_cols:rw_cols + mla_cols])
        w_gates = w_in[i][:, rw_cols + mla_cols:]
        z_rw = norm_matmul(xf, norm_mix[i], bf(w_rw), tm=1024, tn=RW_ZCOLS // 3)
        z_mla = norm_matmul(xf, norm_mix[i], bf(w_mla), tm=1024, tn=MLA_ZCOLS)
        gates = norm_matmul(xf, norm_mix[i], bf(w_gates), tm=1024, tn=1024, act="sigmoid")

        pad_rows = lambda w: jnp.pad(w, ((0, LANES - w.shape[0]), (0, 0)))
        r, wl, k, v, a, b, g = rwkv_prep(
            z_rw.reshape(bsz, seq, RW_ZCOLS), mu, rw_w0[i], pad_rows(rw_w2[i]), rw_a0[i],
            pad_rows(rw_a2[i]), rw_g2[i], rw_k_k[i], rw_k_a[i], bd, tm=256)
        y = rwkv_scan(r, wl, k, v, a, b, rows=math.gcd(bsz, 4))

        wq, wqs, wk, wv = _mla_up_weights(mla_w_uq[i], mla_w_ukv[i])
        q, kk, vv = mla_prep(z_mla, pos, f_lane, mla_q_norm[i], mla_kv_norm[i],
                             bf(wq), bf(wqs), bf(wk), bf(wv), tm=512)
        n = MLA_HEADS * LANES
        nv = MLA_HEADS * V_HEAD
        o = attention(q.reshape(bsz, seq, n), kk.reshape(bsz, seq, n),
                      vv.reshape(bsz, seq // ATTN_TILE, nv, ATTN_TILE))

        flat = lambda a: a.reshape(t, -1)
        x1 = combine(xf, flat(y), flat(r), flat(k), flat(v), flat(g), flat(o), gates,
                     bd, rw_r_k[i].reshape(-1), rw_gn_w[i], rw_gn_b[i],
                     bf(rw_w_o[i]), bf(mla_w_o[i]), bf(w_out[i]), tm=256)

        qp = norm_matmul(x1, norm_ffn[i], bf(peer_w_q[i]), tm=1024, tn=1024, out_dtype=BF16)
        lam, cc, rho, e1 = peer_route(qp, bf(peer_sub_keys[i]), tm=256)
        nrow = PEER_ROWS * N_KEYS
        vt = bf(peer_v[i]).reshape(-1, nrow, d).transpose(0, 2, 1)
        ffn = peer_expert(x1, norm_ffn[i], bf(peer_u[i]), vt, lam, cc, rho, e1, tm=512)

        xf = ple_final(x1, ffn, p[i].reshape(t, -1), norm_ple[i], norm_final,
                       bf(ple_w_gate[i]), bf(ple_w_proj[i]), tm=256)
    return xf.reshape(bsz, seq, d)
```

```python
import functools
import math

import jax
import jax.numpy as jnp
from jax import lax
from jax.experimental import pallas as pl
from jax.experimental.pallas import tpu as pltpu

F32 = jnp.float32
BF16 = jnp.bfloat16
HIGHEST = lax.Precision.HIGHEST

LANES = 128
SUBLANES = 8
VMEM_LIMIT = 56 * 1024 * 1024

EPS = 1e-6
RW_HEADS = 8
RW_HEAD_DIM = 64
RW_DIM = RW_HEADS * RW_HEAD_DIM
RW_GN_EPS = 64e-5
SCAN_CHUNK = 64
SEG_SUM_WIDTH = 256

MLA_HEADS = 8
QK_NOPE = 64
QK_ROPE = 32
V_HEAD = 64
Q_LORA = 384
KV_LORA = 256
ROPE_THETA = 10000.0
MASK_CHUNK = 64
NEG_INF = -1e30

PEER_HEADS = 8
N_KEYS = 128
PEER_TOPK = 16
HALF_Q = 128


def _cparams(*sem):
    return pltpu.CompilerParams(dimension_semantics=sem, vmem_limit_bytes=VMEM_LIMIT)


def _dot(a, b, precision=None):
    return jnp.dot(a, b, preferred_element_type=F32, precision=precision)


def _dot_nt(a, b, precision=None):
    return lax.dot_general(a, b, (((1,), (1,)), ((), ())),
                           preferred_element_type=F32, precision=precision)


def _rms(x, gain):
    return x * lax.rsqrt(jnp.mean(x * x, axis=-1, keepdims=True) + EPS) * gain


def _sigmoid(x):
    return 1.0 / (1.0 + jnp.exp(-x))


def _seg_sum(x, bd):
    hi = x.astype(BF16)
    lo = (x - hi.astype(F32)).astype(BF16)
    w = bd.shape[0]
    parts = [_dot(hi[:, c:c + w], bd) + _dot(lo[:, c:c + w], bd) for c in range(0, x.shape[1], w)]
    return jnp.concatenate(parts, axis=1)


def _norm_matmul_kernel(x_ref, g_ref, w_ref, o_ref, h_ref, *, act):
    @pl.when(pl.program_id(1) == 0)
    def _():
        h_ref[...] = _rms(x_ref[...], g_ref[...]).astype(BF16)

    y = _dot(h_ref[...], w_ref[...])
    if act == "sigmoid":
        y = _sigmoid(y)
    o_ref[...] = y.astype(o_ref.dtype)


def norm_matmul(x, gain, w, *, tm, tn, act=None, out_dtype=F32):
    t, d = x.shape
    n = w.shape[1]
    return pl.pallas_call(
        functools.partial(_norm_matmul_kernel, act=act),
        grid=(t // tm, n // tn),
        in_specs=[pl.BlockSpec((tm, d), lambda i, j: (i, 0)),
                  pl.BlockSpec((1, d), lambda i, j: (0, 0)),
                  pl.BlockSpec((d, tn), lambda i, j: (0, j))],
        out_specs=pl.BlockSpec((tm, tn), lambda i, j: (i, j)),
        out_shape=jax.ShapeDtypeStruct((t, n), out_dtype),
        scratch_shapes=[pltpu.VMEM((tm, d), BF16)],
        compiler_params=_cparams("parallel", "arbitrary"),
        name="norm_matmul",
    )(x, gain.reshape(1, d), w)


RW_ZCOLS = 3 * RW_DIM + 3 * LANES


def _rwkv_prep_kernel(z_ref, zp_ref, mu_ref, w0_ref, w2_ref, a0_ref, a2_ref, g2_ref,
                      kk_ref, ka_ref, bd_ref,
                      r_ref, wl_ref, k_ref, v_ref, a_ref, b_ref, g_ref):
    z = z_ref[0]
    tm = z.shape[0]
    prev_last = zp_ref[0][SUBLANES - 1:SUBLANES, :]
    prev_last = jnp.where(pl.program_id(1) == 0, 0.0, prev_last)
    rolled = pltpu.roll(z, 1, 0)
    row = lax.broadcasted_iota(jnp.int32, (tm, 1), 0)
    z_prev = jnp.where(row == 0, prev_last, rolled)
    z = z + mu_ref[...] * (z_prev - z)

    o1, o2, o3 = RW_DIM, 2 * RW_DIM, 3 * RW_DIM
    r, k, v = z[:, :o1], z[:, o1:o2], z[:, o2:o3]
    zw, za, zg = z[:, o3:o3 + LANES], z[:, o3 + LANES:o3 + 2 * LANES], z[:, o3 + 2 * LANES:]

    wpre = w0_ref[...] + _dot(jnp.tanh(zw), w2_ref[...])
    nx = -wpre
    softplus = jnp.maximum(nx, 0.0) + jnp.log(1.0 + jnp.exp(-jnp.abs(nx)))
    w = -softplus - 0.5
    iclr = _sigmoid(a0_ref[...] + _dot(za, a2_ref[...]))
    g = _dot(_sigmoid(zg), g2_ref[...])

    kk = k * kk_ref[...]
    ss = _seg_sum(kk * kk, bd_ref[...])
    kk = kk / jnp.maximum(jnp.sqrt(ss), 1e-12)

    r_ref[0] = r
    wl_ref[0] = -jnp.exp(w)
    k_ref[0] = k * (1.0 + (iclr - 1.0) * ka_ref[...])
    v_ref[0] = v
    a_ref[0] = -kk
    b_ref[0] = kk * iclr
    g_ref[0] = g


def rwkv_prep(z, mu, w0, w2, a0, a2, g2, k_k, k_a, bd, *, tm):
    bsz, seq, zc = z.shape
    d = RW_DIM
    row = lambda a: a.reshape(1, -1)
    full = lambda a: pl.BlockSpec(a.shape, lambda b, i: (0,) * a.ndim)
    args = (row(mu), row(w0), w2, row(a0), a2, g2, row(k_k), row(k_a), bd)
    out = jax.ShapeDtypeStruct((bsz, seq, d), F32)
    ospec = pl.BlockSpec((1, tm, d), lambda b, i: (b, i, 0))
    return pl.pallas_call(
        _rwkv_prep_kernel,
        grid=(bsz, seq // tm),
        in_specs=[pl.BlockSpec((1, tm, zc), lambda b, i: (b, i, 0)),
                  pl.BlockSpec((1, SUBLANES, zc),
                               lambda b, i: (b, jnp.maximum(i * (tm // SUBLANES) - 1, 0), 0))]
                 + [full(a) for a in args],
        out_specs=[ospec] * 7,
        out_shape=[out] * 7,
        compiler_params=_cparams("parallel", "arbitrary"),
        name="rwkv_prep",
    )(z, z, *args)


def _rwkv_scan_kernel(r_ref, wl_ref, k_ref, v_ref, a_ref, b_ref, y_ref, g_ref):
    c = SCAN_CHUNK
    hd = RW_HEAD_DIM
    npair = g_ref.shape[0]

    @pl.when(pl.program_id(1) == 0)
    def _():
        g_ref[...] = jnp.zeros_like(g_ref)

    ri = lax.broadcasted_iota(jnp.int32, (c, c), 0)
    ci = lax.broadcasted_iota(jnp.int32, (c, c), 1)
    tril = (ri >= ci).astype(F32)
    lane = lax.broadcasted_iota(jnp.int32, (c, 2 * hd), 1)
    m0 = lane < hd
    r2 = lax.broadcasted_iota(jnp.int32, (2 * c, 2 * c), 0)
    c2 = lax.broadcasted_iota(jnp.int32, (2 * c, 2 * c), 1)
    same = (r2 >= c) == (c2 >= c)
    strict = jnp.logical_and(same, r2 > c2)
    incl = jnp.logical_and(same, r2 >= c2)
    eye = (r2 == c2).astype(F32)

    bf = lambda x: x.astype(BF16)
    stack = lambda x: bf(jnp.concatenate([jnp.where(m0, x, 0.0), jnp.where(m0, 0.0, x)], axis=0))
    twice = lambda x: bf(jnp.concatenate([x, x], axis=0))
    pick = lambda s: jnp.where(m0, s[:c], s[c:])

    pairs = range(npair)
    each = lambda f, *cols: [f(*args) for args in zip(*cols)]
    per_row = r_ref.shape[2] // LANES
    where = [(hp // per_row, slice((hp % per_row) * LANES, (hp % per_row + 1) * LANES)) for hp in pairs]
    load = lambda ref: [ref[bi, :, sl] for bi, sl in where]
    r, wl, k, v, a, b = (load(x) for x in (r_ref, wl_ref, k_ref, v_ref, a_ref, b_ref))

    cs = each(lambda w: _dot(tril, w, HIGHEST), wl)
    cs_last = each(lambda s: s[c - 1:c, :], cs)
    p_inv = each(lambda s: jnp.exp(-s), cs)
    at = each(lambda a, s, w: a * jnp.exp(s - w), a, cs, wl)
    rt = each(lambda r, s: r * jnp.exp(s), r, cs)
    at_s, rt_s = each(stack, at), each(stack, rt)
    bt_s = each(lambda b, p: stack(b * p), b, p_inv)
    kt_s = each(lambda k, p: stack(k * p), k, p_inv)

    ab = each(lambda x, y: jnp.where(strict, _dot_nt(x, y), 0.0), at_s, bt_s)
    ak = each(lambda x, y: jnp.where(strict, _dot_nt(x, y), 0.0), at_s, kt_s)
    rb = each(lambda x, y: jnp.where(incl, _dot_nt(x, y), 0.0), rt_s, bt_s)
    rk = each(lambda x, y: jnp.where(incl, _dot_nt(x, y), 0.0), rt_s, kt_s)

    tinv = each(lambda m: eye + m, ab)
    x = each(bf, ab)
    for _ in range(int(math.log2(c)) - 1):
        x = each(lambda m: bf(_dot(m, m)), x)
        tinv = each(lambda t, m: t + _dot(bf(t), m), tinv, x)

    gt = [g_ref[hp] for hp in pairs]
    gtb = each(bf, gt)
    vv = each(twice, v)
    rhs = each(lambda at, g, ak, vv: _dot_nt(bf(at), g) + pick(_dot(bf(ak), vv)), at, gtb, ak, vv)
    u = each(lambda t, x: pick(_dot(bf(t), twice(x))), tinv, rhs)
    y = each(lambda rt, g, rb, u, rk, vv:
             _dot_nt(bf(rt), g) + pick(_dot(bf(rb), twice(u)) + _dot(bf(rk), vv)),
             rt, gtb, rb, u, rk, vv)
    upd = each(lambda u, b, v, k, s, sl:
               _dot(bf(u.T), bf(b * jnp.exp(sl - s))) + _dot(bf(v.T), bf(k * jnp.exp(sl - s))),
               u, b, v, k, cs, cs_last)
    for hp in pairs:
        bi, sl = where[hp]
        y_ref[bi, :, sl] = y[hp]
        g_ref[hp] = gt[hp] * jnp.exp(cs_last[hp]) + jnp.where(same, upd[hp], 0.0)


def rwkv_scan(r, wl, k, v, a, b, *, rows):
    bsz, seq, d = r.shape
    c = SCAN_CHUNK
    spec = pl.BlockSpec((rows, c, d), lambda bi, ci: (bi, ci, 0))
    return pl.pallas_call(
        _rwkv_scan_kernel,
        grid=(bsz // rows, seq // c),
        in_specs=[spec] * 6,
        out_specs=spec,
        out_shape=jax.ShapeDtypeStruct((bsz, seq, d), F32),
        scratch_shapes=[pltpu.VMEM((rows * d // LANES, LANES, LANES), F32)],
        compiler_params=_cparams("parallel", "arbitrary"),
        name="rwkv_scan",
    )(r, wl, k, v, a, b)


MLA_ZCOLS = 1024
MLA_SCALE = 1.0 / math.sqrt(QK_NOPE + QK_ROPE)
ATTN_TILE = 256


def _mla_prep_kernel(z_ref, pos_ref, fl_ref, qn_ref, kvn_ref, wq_ref, wqs_ref, wk_ref, wv_ref,
                     q_ref, k_ref, v_ref):
    z = z_ref[...]
    c_q = _rms(z[:, :Q_LORA], qn_ref[...]).astype(BF16)
    c_kv = _rms(z[:, Q_LORA:Q_LORA + KV_LORA], kvn_ref[...]).astype(BF16)
    kr = z[:, Q_LORA + KV_LORA:Q_LORA + KV_LORA + LANES]
    krs = z[:, Q_LORA + KV_LORA + LANES:Q_LORA + KV_LORA + 2 * LANES]

    ang = pos_ref[...].astype(F32) * fl_ref[...]
    cos, sin = jnp.cos(ang), jnp.sin(ang)
    kr_rot = kr * cos + krs * sin

    q = _dot(c_q, wq_ref[...])
    qs = _dot(c_q, wqs_ref[...])
    kn = _dot(c_kv, wk_ref[...])
    v = _dot(c_kv, wv_ref[...])
    for c in range(v_ref.shape[0]):
        v_ref[c] = v[c * ATTN_TILE:(c + 1) * ATTN_TILE, :].T.astype(BF16)
    for h in range(MLA_HEADS):
        sl = slice(h * LANES, (h + 1) * LANES)
        q_ref[:, sl] = ((q[:, sl] * cos + qs[:, sl] * sin) * MLA_SCALE).astype(BF16)
        k_ref[:, sl] = (kn[:, sl] + kr_rot).astype(BF16)


def mla_prep(z, pos, f_lane, q_norm, kv_norm, wq, wqs, wk, wv, *, tm):
    t = z.shape[0]
    n = MLA_HEADS * LANES
    full = lambda a: pl.BlockSpec(a.shape, lambda i: (0,) * a.ndim)
    args = (f_lane, q_norm.reshape(1, -1), kv_norm.reshape(1, -1), wq, wqs, wk, wv)
    out = jax.ShapeDtypeStruct((t, n), BF16)
    nv = wv.shape[1]
    out_vt = jax.ShapeDtypeStruct((t // ATTN_TILE, nv, ATTN_TILE), BF16)
    ospec = pl.BlockSpec((tm, n), lambda i: (i, 0))
    vspec = pl.BlockSpec((tm // ATTN_TILE, nv, ATTN_TILE), lambda i: (i, 0, 0))
    return pl.pallas_call(
        _mla_prep_kernel,
        grid=(t // tm,),
        in_specs=[pl.BlockSpec((tm, MLA_ZCOLS), lambda i: (i, 0)),
                  pl.BlockSpec((tm, 1), lambda i: (i, 0))] + [full(a) for a in args],
        out_specs=[ospec, ospec, vspec],
        out_shape=[out, out, out_vt],
        compiler_params=_cparams("parallel"),
        name="mla_prep",
    )(z, pos, *args)


ATTN_HEADS_PER_STEP = 8


def _attn_kernel(q_ref, k_ref, vt_ref, o_ref):
    tq = ATTN_TILE
    iq = pl.program_id(2)
    nh = q_ref.shape[2] // LANES
    heads = [slice(h * LANES, (h + 1) * LANES) for h in range(nh)]
    qs = [q_ref[0, :, sl] for sl in heads]

    def tile(j, carry, masked):
        start = pl.multiple_of(j * tq, tq)
        if masked:
            kc = lax.broadcasted_iota(jnp.int32, (tq, tq), 0) // MASK_CHUNK
            qc = lax.broadcasted_iota(jnp.int32, (tq, tq), 1) // MASK_CHUNK
            keep = kc <= qc
        hs = range(nh)
        s = [_dot_nt(k_ref[0, pl.ds(start, tq), heads[h]], qs[h]) for h in hs]
        if masked:
            s = [jnp.where(keep, x, NEG_INF) for x in s]
        m_new = [jnp.maximum(carry[h][0], jnp.max(s[h], axis=0, keepdims=True)) for h in hs]
        alpha = [jnp.exp(carry[h][0] - m_new[h]) for h in hs]
        p = [jnp.exp(s[h] - m_new[h]) for h in hs]
        l = [alpha[h] * carry[h][1] + jnp.sum(p[h], axis=0, keepdims=True) for h in hs]
        pv = [_dot(vt_ref[0, j, h * V_HEAD:(h + 1) * V_HEAD, :], p[h].astype(BF16))
              for h in hs]
        return tuple((m_new[h], l[h], alpha[h] * carry[h][2] + pv[h]) for h in hs)

    init = tuple((jnp.full((1, tq), NEG_INF, F32), jnp.zeros((1, tq), F32), jnp.zeros((V_HEAD, tq), F32))
                 for _ in heads)
    carry = lax.fori_loop(0, iq, lambda j, c: tile(j, c, False), init)
    carry = tile(iq, carry, True)
    out = [acc / l for _, l, acc in carry]
    for pair in range(nh // 2):
        both = jnp.concatenate(out[2 * pair:2 * pair + 2], axis=0)
        o_ref[0, :, pair * LANES:(pair + 1) * LANES] = both.T.astype(BF16)


def attention(q, k, vt):
    bsz, seq, n = q.shape
    tq = ATTN_TILE
    w = ATTN_HEADS_PER_STEP * LANES
    wv = ATTN_HEADS_PER_STEP * V_HEAD
    qspec = pl.BlockSpec((1, tq, w), lambda b, h, i: (b, i, h))
    kspec = pl.BlockSpec((1, seq, w), lambda b, h, i: (b, 0, h))
    vspec = pl.BlockSpec((1, seq // tq, wv, tq), lambda b, h, i: (b, 0, h, 0))
    return pl.pallas_call(
        _attn_kernel,
        grid=(bsz, n // w, seq // tq),
        in_specs=[qspec, kspec, vspec],
        out_specs=pl.BlockSpec((1, tq, wv), lambda b, h, i: (b, i, h)),
        out_shape=jax.ShapeDtypeStruct((bsz, seq, vt.shape[2]), BF16),
        compiler_params=_cparams("parallel", "parallel", "arbitrary"),
        name="attention",
    )(q, k, vt)


def _combine_kernel(x_ref, y_ref, r_ref, k_ref, v_ref, g_ref, o_ref, gates_ref,
                    bd_ref, rk_ref, gnw_ref, gnb_ref, wa_ref, wb_ref, wo_ref, out_ref):
    bd = bd_ref[...]
    inv_n = 1.0 / RW_HEAD_DIM
    y = y_ref[...]
    v = v_ref[...]
    mean = _seg_sum(y, bd) * inv_n
    yc = y - mean
    var = _seg_sum(yc * yc, bd) * inv_n
    yn = yc * lax.rsqrt(var + RW_GN_EPS) * gnw_ref[...] + gnb_ref[...]
    bonus = _seg_sum(r_ref[...] * k_ref[...] * rk_ref[...], bd) * v
    ya = _dot(((yn + bonus) * g_ref[...]).astype(BF16), wa_ref[...])
    yb = _dot(o_ref[...], wb_ref[...])
    d = ya.shape[1]
    gates = gates_ref[...]
    mix = gates[:, :d] * ya + gates[:, d:] * yb
    out_ref[...] = x_ref[...] + _dot(mix.astype(BF16), wo_ref[...])


def combine(x, y, r, k, v, g, o, gates, bd, r_k, gn_w, gn_b, wa, wb, wo, *, tm):
    t, d = x.shape
    row = lambda a: a.reshape(1, -1)
    tok = lambda a: pl.BlockSpec((tm, a.shape[1]), lambda i: (i, 0))
    full = lambda a: pl.BlockSpec(a.shape, lambda i: (0,) * a.ndim)
    toks = (x, y, r, k, v, g, o, gates)
    consts = (bd, row(r_k), row(gn_w), row(gn_b), wa, wb, wo)
    return pl.pallas_call(
        _combine_kernel,
        grid=(t // tm,),
        in_specs=[tok(a) for a in toks] + [full(a) for a in consts],
        out_specs=pl.BlockSpec((tm, d), lambda i: (i, 0)),
        out_shape=jax.ShapeDtypeStruct((t, d), F32),
        compiler_params=_cparams("parallel"),
        name="combine",
    )(*toks, *consts)


NOT_RANKED = 127.0
RANK_BLOCKS_PER_LOOP = 2
STAIR = tuple(PEER_TOPK // (ii + 1) for ii in range(PEER_TOPK))


def _top_ranks(s, k, exact_ties):
    n, t = s.shape
    key = lax.broadcasted_iota(jnp.int32, (n, LANES), 0)
    slot = lax.broadcasted_iota(jnp.int32, (k, LANES), 0)

    def one(r, s, rank, top):
        m = jnp.max(s, axis=0, keepdims=True)
        hit = s == m
        if exact_ties:
            hit = key == jnp.min(jnp.where(hit, key, n), axis=0, keepdims=True)
        rank = jnp.where(hit, jnp.asarray(r, F32), rank)
        s = jnp.where(hit, -jnp.inf, s)
        top = jnp.where(slot == r, m, top)
        return s, rank, top

    def body(r, carry):
        return tuple(one(r, *c) for c in carry)

    blocks = [s[:, c * LANES:(c + 1) * LANES] for c in range(t // LANES)]
    tops, ranks = [], []
    for g in range(0, len(blocks), RANK_BLOCKS_PER_LOOP):
        init = tuple((b, jnp.full((n, LANES), NOT_RANKED, F32), jnp.zeros((k, LANES), F32))
                     for b in blocks[g:g + RANK_BLOCKS_PER_LOOP])
        for _, rank, top in lax.fori_loop(0, k, body, init):
            tops.append(top)
            ranks.append(rank)
    return jnp.concatenate(tops, axis=1), jnp.concatenate(ranks, axis=1)


def _ranked_excess(rank, k):
    count = jnp.sum((rank < NOT_RANKED).astype(F32), axis=0, keepdims=True)
    return jnp.abs(count - k)


def _peer_route_kernel(q_ref, keys_ref, lam_ref, cc_ref, rho_ref, e1_ref):
    tm = q_ref.shape[0]
    k = PEER_TOPK
    neg = -jnp.inf

    def head(h, _):
        col = pl.multiple_of(h * 2 * HALF_Q, 2 * HALF_Q)
        s0 = _dot_nt(keys_ref[h, 0], q_ref[:, pl.ds(col, HALF_Q)])
        s1 = _dot_nt(keys_ref[h, 1], q_ref[:, pl.ds(col + HALF_Q, HALF_Q)])
        def rank_all(exact_ties):
            top0, rank0 = _top_ranks(s0, k, exact_ties)
            top1, rank1 = _top_ranks(s1, k, exact_ties)
            row8 = lax.broadcasted_iota(jnp.int32, (SUBLANES, tm), 0)
            groups = [top0[0:1] + top1[0:8], top0[0:1] + top1[8:16], top0[1:2] + top1[0:8]]
            for ii in range(2, 8):
                groups.append(jnp.where(row8 < STAIR[ii], top0[ii:ii + 1] + top1[0:8], neg))
            groups.append(top0[8:16] + top1[0:1])
            cand = jnp.concatenate(groups, axis=0)
            _, crank = _top_ranks(cand, k, exact_ties)
            return top0, rank0, top1, rank1, cand, crank

        quick = rank_all(False)
        excess = _ranked_excess(quick[1], k) + _ranked_excess(quick[3], k) + _ranked_excess(quick[5], k)
        top0, rank0, top1, rank1, cand, crank = lax.cond(
            jnp.max(excess) == 0.0, lambda: quick, lambda: rank_all(True))
        sel = crank < NOT_RANKED
        ex = jnp.where(sel, jnp.exp(cand - cand[0:1]), 0.0)
        z = jnp.sum(ex, axis=0, keepdims=True)
        self = sel.astype(F32)
        counts = [jnp.sum(self[0:16], axis=0, keepdims=True)]
        for g in range(2, 9):
            counts.append(jnp.sum(self[8 * g:8 * g + 8], axis=0, keepdims=True))
        lvec = jnp.concatenate(counts + [self[72:80]], axis=0)

        lam = jnp.zeros((N_KEYS, tm), F32)
        for ii in range(k):
            lam = jnp.where(rank0 == float(ii), lvec[ii:ii + 1], lam)
        lam_ref[h] = lam
        cc_ref[h] = jnp.exp(s0 - top0[0:1]) / z
        rho_ref[h] = rank1.astype(BF16)
        e1_ref[h] = jnp.exp(s1 - top1[0:1]).astype(BF16)
        return 0

    lax.fori_loop(0, PEER_HEADS, head, 0)


def peer_route(qp, keys, *, tm):
    t = qp.shape[0]
    out = jax.ShapeDtypeStruct((PEER_HEADS, N_KEYS, t), F32)
    out_b = jax.ShapeDtypeStruct((PEER_HEADS, N_KEYS, t), BF16)
    ospec = pl.BlockSpec((PEER_HEADS, N_KEYS, tm), lambda i: (0, 0, i))
    return pl.pallas_call(
        _peer_route_kernel,
        grid=(t // tm,),
        in_specs=[pl.BlockSpec((tm, qp.shape[1]), lambda i: (i, 0)),
                  pl.BlockSpec(keys.shape, lambda i: (0, 0, 0, 0))],
        out_specs=[ospec] * 4,
        out_shape=[out, out, out_b, out_b],
        compiler_params=_cparams("parallel"),
        name="peer_route",
    )(qp, keys)


PEER_ROWS = 8
PEER_TOKEN_CHUNK = 256


def _erf(x):
    return lax.erf(x)


def _gelu(x):
    return 0.5 * x * (1.0 + _erf(x * (1.0 / math.sqrt(2.0))))


def _peer_expert_kernel(x_ref, gn_ref, u_ref, vt_ref, lam_ref, cc_ref, rho_ref, e1_ref, out_ref,
                        xt_ref, acc_ref, pre0_ref, pre1_ref):
    j = pl.program_id(1)
    tm = xt_ref.shape[1]
    chunks = [slice(c, c + PEER_TOKEN_CHUNK) for c in range(0, tm, PEER_TOKEN_CHUNK)]

    @pl.when(j == 0)
    def _():
        xt_ref[...] = _rms(x_ref[...], gn_ref[...]).T.astype(BF16)
        acc_ref[...] = jnp.zeros_like(acc_ref)
        pre1_ref[...] = jnp.zeros_like(pre1_ref)

    def step(fill_ref, drain_ref):
        for cols in chunks:
            fill_ref[:, cols] = _dot(u_ref[...], xt_ref[:, cols])
        for cols in chunks:
            gs = []
            for ii in range(PEER_ROWS):
                rows = slice(ii * N_KEYS, (ii + 1) * N_KEYS)
                gate = jnp.zeros((N_KEYS, PEER_TOKEN_CHUNK), BF16)
                for h in range(PEER_HEADS):
                    lam = lam_ref[h, ii:ii + 1, cols].astype(BF16)
                    cc = cc_ref[h, ii:ii + 1, cols].astype(BF16)
                    gate = gate + jnp.where(rho_ref[h, :, cols] < lam, e1_ref[h, :, cols] * cc,
                                            jnp.zeros((), BF16))
                gs.append(_gelu(drain_ref[rows, cols]).astype(BF16) * gate)
            acc_ref[:, cols] += _dot(vt_ref[0], jnp.concatenate(gs, axis=0))

    @pl.when(j % 2 == 0)
    def _():
        step(pre0_ref, pre1_ref)

    @pl.when(j % 2 == 1)
    def _():
        step(pre1_ref, pre0_ref)

    @pl.when(j == pl.num_programs(1) - 1)
    def _():
        out_ref[...] = acc_ref[...].T


def peer_expert(x, gain, u, vt, lam, cc, rho, e1, *, tm):
    t, d = x.shape
    nrow = PEER_ROWS * N_KEYS
    nblk = u.shape[0] // nrow
    stage = lambda lag: (lambda j: jnp.clip(j - lag, 0, nblk - 1))
    rspec = pl.BlockSpec((PEER_HEADS, PEER_ROWS, tm), lambda i, j: (0, stage(1)(j), i))
    cspec = pl.BlockSpec((PEER_HEADS, N_KEYS, tm), lambda i, j: (0, 0, i))
    return pl.pallas_call(
        _peer_expert_kernel,
        grid=(t // tm, nblk + 1),
        in_specs=[pl.BlockSpec((tm, d), lambda i, j: (i, 0)),
                  pl.BlockSpec((1, d), lambda i, j: (0, 0)),
                  pl.BlockSpec((nrow, d), lambda i, j: (stage(0)(j), 0)),
                  pl.BlockSpec((1, d, nrow), lambda i, j: (stage(1)(j), 0, 0)),
                  rspec, rspec, cspec, cspec],
        out_specs=pl.BlockSpec((tm, d), lambda i, j: (i, 0)),
        out_shape=jax.ShapeDtypeStruct((t, d), F32),
        scratch_shapes=[pltpu.VMEM((d, tm), BF16), pltpu.VMEM((d, tm), F32),
                        pltpu.VMEM((nrow, tm), F32), pltpu.VMEM((nrow, tm), F32)],
        compiler_params=_cparams("parallel", "arbitrary"),
        name="peer_expert",
    )(x, gain.reshape(1, d), u, vt, lam, cc, rho, e1)


def _ple_final_kernel(x_ref, f_ref, p_ref, gp_ref, gf_ref, wg_ref, wp_ref, out_ref):
    x = x_ref[...] + f_ref[...]
    gate = _sigmoid(_dot(_rms(x, gp_ref[...]).astype(BF16), wg_ref[...]))
    x = x + gate * _dot(p_ref[...].astype(BF16), wp_ref[...])
    out_ref[...] = _rms(x, gf_ref[...])


def ple_final(x, f, p, g_ple, g_final, wg, wp, *, tm):
    t, d = x.shape
    tok = lambda a: pl.BlockSpec((tm, a.shape[1]), lambda i: (i, 0))
    full = lambda a: pl.BlockSpec(a.shape, lambda i: (0,) * a.ndim)
    consts = (g_ple.reshape(1, d), g_final.reshape(1, d), wg, wp)
    return pl.pallas_call(
        _ple_final_kernel,
        grid=(t // tm,),
        in_specs=[tok(x), tok(f), tok(p)] + [full(a) for a in consts],
        out_specs=tok(x),
        out_shape=jax.ShapeDtypeStruct((t, d), F32),
        compiler_params=_cparams("parallel"),
        name="ple_final",
    )(x, f, p, *consts)


def _place(cols, width, offset):
    return jnp.pad(cols, ((0, 0), (offset, width - offset - cols.shape[1])))


def _rw_in_weights(w_rw, mu):
    o3 = 3 * RW_DIM
    lw, la = 64, 64
    segs = [w_rw[:, :o3], _place(w_rw[:, o3:o3 + lw], LANES, 0),
            _place(w_rw[:, o3 + lw:o3 + lw + la], LANES, 0), w_rw[:, o3 + lw + la:]]
    mus = [mu[None, :o3], _place(mu[None, o3:o3 + lw], LANES, 0),
           _place(mu[None, o3 + lw:o3 + lw + la], LANES, 0), mu[None, o3 + lw + la:]]
    return jnp.concatenate(segs, axis=1), jnp.concatenate(mus, axis=1)[0]


def _mla_in_weights(w_mla):
    half = QK_ROPE // 2
    lat = Q_LORA + KV_LORA
    kr = w_mla[:, lat:]
    kr_sw = jnp.concatenate([-kr[:, half:], kr[:, :half]], axis=1)
    return jnp.concatenate([w_mla[:, :lat], _place(kr, LANES, QK_NOPE), _place(kr_sw, LANES, QK_NOPE),
                            jnp.zeros((w_mla.shape[0], LANES), w_mla.dtype)], axis=1)


def _mla_up_weights(w_uq, w_ukv):
    half = QK_ROPE // 2
    qd = QK_NOPE + QK_ROPE
    wq = w_uq.reshape(Q_LORA, MLA_HEADS, qd)
    rope = wq[:, :, QK_NOPE:]
    rope_sw = jnp.concatenate([-rope[:, :, half:], rope[:, :, :half]], axis=2)
    pad = lambda t, off: jnp.pad(t, ((0, 0), (0, 0), (off, LANES - off - t.shape[2])))
    wq_pad = pad(wq, 0).reshape(Q_LORA, MLA_HEADS * LANES)
    wq_sw = pad(rope_sw, QK_NOPE).reshape(Q_LORA, MLA_HEADS * LANES)
    lane = jnp.arange(MLA_HEADS * LANES) % LANES
    wk = jnp.where(lane < QK_NOPE, w_ukv, 0.0)
    wv = w_ukv.reshape(KV_LORA, MLA_HEADS, LANES)[:, :, QK_NOPE:].reshape(KV_LORA, MLA_HEADS * V_HEAD)
    return wq_pad, wq_sw, wk, wv


def kernel(x, p, positions, norm_mix, w_in, rw_mu, rw_w0, rw_w2, rw_a0, rw_a2, rw_g2, rw_k_k, rw_k_a, rw_r_k, rw_gn_w, rw_gn_b, rw_w_o, mla_q_norm, mla_w_uq, mla_kv_norm, mla_w_ukv, mla_w_o, w_out, norm_ffn, peer_w_q, peer_sub_keys, peer_u, peer_v, norm_ple, ple_w_gate, ple_w_proj, norm_final):
    bsz, seq, d = x.shape
    t = bsz * seq
    depth = p.shape[0]
    bf = lambda a: a.astype(BF16)
    rw_cols = 3 * RW_DIM + 64 + 64 + 128
    mla_cols = Q_LORA + KV_LORA + QK_ROPE

    head_of = jnp.arange(SEG_SUM_WIDTH) // RW_HEAD_DIM
    bd = bf(head_of[:, None] == head_of[None, :])
    inv_freq = ROPE_THETA ** (-jnp.arange(0, QK_ROPE, 2, dtype=F32) / QK_ROPE)
    f_lane = _place(jnp.concatenate([inv_freq, inv_freq])[None, :], LANES, QK_NOPE)
    pos = positions.reshape(t, 1)

    xf = x.reshape(t, d)
    assert depth == 1, "the final RMSNorm is fused into the layer's last kernel"
    for i in range(depth):
        w_rw, mu = _rw_in_weights(w_in[i][:, :rw_cols], rw_mu[i])
        w_mla = _mla_in_weights(w_in[i][:, rw_cols:rw_cols + mla_cols])
        w_gates = w_in[i][:, rw_cols + mla_cols:]
        z_rw = norm_matmul(xf, norm_mix[i], bf(w_rw), tm=1024, tn=RW_ZCOLS // 3)
        z_mla = norm_matmul(xf, norm_mix[i], bf(w_mla), tm=1024, tn=MLA_ZCOLS)
        gates = norm_matmul(xf, norm_mix[i], bf(w_gates), tm=1024, tn=1024, act="sigmoid")

        pad_rows = lambda w: jnp.pad(w, ((0, LANES - w.shape[0]), (0, 0)))
        r, wl, k, v, a, b, g = rwkv_prep(
            z_rw.reshape(bsz, seq, RW_ZCOLS), mu, rw_w0[i], pad_rows(rw_w2[i]), rw_a0[i],
            pad_rows(rw_a2[i]), rw_g2[i], rw_k_k[i], rw_k_a[i], bd, tm=256)
        y = rwkv_scan(r, wl, k, v, a, b, rows=math.gcd(bsz, 4))

        wq, wqs, wk, wv = _mla_up_weights(mla_w_uq[i], mla_w_ukv[i])
        q, kk, vv = mla_prep(z_mla, pos, f_lane, mla_q_norm[i], mla_kv_norm[i],
                             bf(wq), bf(wqs), bf(wk), bf(wv), tm=512)
        n = MLA_HEADS * LANES
        nv = MLA_HEADS * V_HEAD
        o = attention(q.reshape(bsz, seq, n), kk.reshape(bsz, seq, n),
                      vv.reshape(bsz, seq // ATTN_TILE, nv, ATTN_TILE))

        flat = lambda a: a.reshape(t, -1)
        x1 = combine(xf, flat(y), flat(r), flat(k), flat(v), flat(g), flat(o), gates,
                     bd, rw_r_k[i].reshape(-1), rw_gn_w[i], rw_gn_b[i],
                     bf(rw_w_o[i]), bf(mla_w_o[i]), bf(w_out[i]), tm=256)

        qp = norm_matmul(x1, norm_ffn[i], bf(peer_w_q[i]), tm=1024, tn=1024, out_dtype=BF16)
        lam, cc, rho, e1 = peer_route(qp, bf(peer_sub_keys[i]), tm=256)
        nrow = PEER_ROWS * N_KEYS
        vt = bf(peer_v[i]).reshape(-1, nrow, d).transpose(0, 2, 1)
        ffn = peer_expert(x1, norm_ffn[i], bf(peer_u[i]), vt, lam, cc, rho, e1, tm=512)

        xf = ple_final(x1, ffn, p[i].reshape(t, -1), norm_ple[i], norm_final,
                       bf(ple_w_gate[i]), bf(ple_w_proj[i]), tm=256)
    return xf.reshape(bsz, seq, d)
```

```python
import functools
import math

import jax
import jax.numpy as jnp
from jax import lax
from jax.experimental import pallas as pl
from jax.experimental.pallas import tpu as pltpu

F32 = jnp.float32
BF16 = jnp.bfloat16
HIGHEST = lax.Precision.HIGHEST

LANES = 128
SUBLANES = 8
VMEM_LIMIT = 56 * 1024 * 1024

EPS = 1e-6
RW_HEADS = 8
RW_HEAD_DIM = 64
RW_DIM = RW_HEADS * RW_HEAD_DIM
RW_GN_EPS = 64e-5
SCAN_CHUNK = 64
SEG_SUM_WIDTH = 256

MLA_HEADS = 8
QK_NOPE = 64
QK_ROPE = 32
V_HEAD = 64
Q_LORA = 384
KV_LORA = 256
ROPE_THETA = 10000.0
MASK_CHUNK = 64
NEG_INF = -1e30

PEER_HEADS = 8
N_KEYS = 128
PEER_TOPK = 16
HALF_Q = 128


def _cparams(*sem):
    return pltpu.CompilerParams(dimension_semantics=sem, vmem_limit_bytes=VMEM_LIMIT)


def _dot(a, b, precision=None):
    return jnp.dot(a, b, preferred_element_type=F32, precision=precision)


def _dot_nt(a, b, precision=None):
    return lax.dot_general(a, b, (((1,), (1,)), ((), ())),
                           preferred_element_type=F32, precision=precision)


def _rms(x, gain):
    return x * lax.rsqrt(jnp.mean(x * x, axis=-1, keepdims=True) + EPS) * gain


def _sigmoid(x):
    return 1.0 / (1.0 + jnp.exp(-x))


def _seg_sum(x, bd):
    hi = x.astype(BF16)
    lo = (x - hi.astype(F32)).astype(BF16)
    w = bd.shape[0]
    parts = [_dot(hi[:, c:c + w], bd) + _dot(lo[:, c:c + w], bd) for c in range(0, x.shape[1], w)]
    return jnp.concatenate(parts, axis=1)


def _norm_matmul_kernel(x_ref, g_ref, w_ref, o_ref, h_ref, *, sigmoid_blocks):
    j = pl.program_id(1)

    @pl.when(j == 0)
    def _():
        h_ref[...] = _rms(x_ref[...], g_ref[...]).astype(BF16)

    def project(act):
        o_ref[...] = act(_dot(h_ref[...], w_ref[...])).astype(o_ref.dtype)

    if sigmoid_blocks is None:
        project(lambda y: y)
    else:
        gated = jnp.logical_and(j >= sigmoid_blocks[0], j < sigmoid_blocks[1])
        pl.when(gated)(lambda: project(_sigmoid))
        pl.when(jnp.logical_not(gated))(lambda: project(lambda y: y))


def norm_matmul(x, gain, w, *, tm, tn, sigmoid_blocks=None, out_dtype=F32):
    t, d = x.shape
    n = w.shape[1]
    return pl.pallas_call(
        functools.partial(_norm_matmul_kernel, sigmoid_blocks=sigmoid_blocks),
        grid=(t // tm, n // tn),
        in_specs=[pl.BlockSpec((tm, d), lambda i, j: (i, 0)),
                  pl.BlockSpec((1, d), lambda i, j: (0, 0)),
                  pl.BlockSpec((d, tn), lambda i, j: (0, j))],
        out_specs=pl.BlockSpec((tm, tn), lambda i, j: (i, j)),
        out_shape=jax.ShapeDtypeStruct((t, n), out_dtype),
        scratch_shapes=[pltpu.VMEM((tm, d), BF16)],
        compiler_params=_cparams("parallel", "arbitrary"),
        name="norm_matmul",
    )(x, gain.reshape(1, d), w)


RW_ZCOLS = 3 * RW_DIM + 4 * LANES


def _rwkv_prep_kernel(z_ref, zp_ref, mu_ref, w0_ref, w2_ref, a0_ref, a2_ref, g2_ref,
                      kk_ref, ka_ref, bd_ref,
                      r_ref, wl_ref, k_ref, v_ref, a_ref, b_ref, g_ref):
    z = z_ref[0]
    tm = z.shape[0]
    prev_last = zp_ref[0][SUBLANES - 1:SUBLANES, :]
    prev_last = jnp.where(pl.program_id(1) == 0, 0.0, prev_last)
    rolled = pltpu.roll(z, 1, 0)
    row = lax.broadcasted_iota(jnp.int32, (tm, 1), 0)
    z_prev = jnp.where(row == 0, prev_last, rolled)
    z = z + mu_ref[...] * (z_prev - z)

    o1, o2, o3 = RW_DIM, 2 * RW_DIM, 3 * RW_DIM
    r, k, v = z[:, :o1], z[:, o1:o2], z[:, o2:o3]
    zw, za, zg = (z[:, o3 + c * LANES:o3 + (c + 1) * LANES] for c in range(3))

    wpre = w0_ref[...] + _dot(jnp.tanh(zw), w2_ref[...])
    nx = -wpre
    softplus = jnp.maximum(nx, 0.0) + jnp.log(1.0 + jnp.exp(-jnp.abs(nx)))
    w = -softplus - 0.5
    iclr = _sigmoid(a0_ref[...] + _dot(za, a2_ref[...]))
    g = _dot(_sigmoid(zg), g2_ref[...])

    kk = k * kk_ref[...]
    ss = _seg_sum(kk * kk, bd_ref[...])
    kk = kk / jnp.maximum(jnp.sqrt(ss), 1e-12)

    r_ref[0] = r
    wl_ref[0] = -jnp.exp(w)
    k_ref[0] = k * (1.0 + (iclr - 1.0) * ka_ref[...])
    v_ref[0] = v
    a_ref[0] = -kk
    b_ref[0] = kk * iclr
    g_ref[0] = g


def rwkv_prep(z, mu, w0, w2, a0, a2, g2, k_k, k_a, bd, *, tm):
    bsz, seq, _ = z.shape
    zc = RW_ZCOLS
    d = RW_DIM
    row = lambda a: a.reshape(1, -1)
    full = lambda a: pl.BlockSpec(a.shape, lambda b, i: (0,) * a.ndim)
    args = (row(mu), row(w0), w2, row(a0), a2, g2, row(k_k), row(k_a), bd)
    out = jax.ShapeDtypeStruct((bsz, seq, d), F32)
    ospec = pl.BlockSpec((1, tm, d), lambda b, i: (b, i, 0))
    return pl.pallas_call(
        _rwkv_prep_kernel,
        grid=(bsz, seq // tm),
        in_specs=[pl.BlockSpec((1, tm, zc), lambda b, i: (b, i, 0)),
                  pl.BlockSpec((1, SUBLANES, zc),
                               lambda b, i: (b, jnp.maximum(i * (tm // SUBLANES) - 1, 0), 0))]
                 + [full(a) for a in args],
        out_specs=[ospec] * 7,
        out_shape=[out] * 7,
        compiler_params=_cparams("parallel", "arbitrary"),
        name="rwkv_prep",
    )(z, z, *args)


def _rwkv_scan_kernel(r_ref, wl_ref, k_ref, v_ref, a_ref, b_ref, y_ref, g_ref):
    c = SCAN_CHUNK
    hd = RW_HEAD_DIM
    npair = g_ref.shape[0]

    @pl.when(pl.program_id(1) == 0)
    def _():
        g_ref[...] = jnp.zeros_like(g_ref)

    ri = lax.broadcasted_iota(jnp.int32, (c, c), 0)
    ci = lax.broadcasted_iota(jnp.int32, (c, c), 1)
    tril = (ri >= ci).astype(F32)
    lane = lax.broadcasted_iota(jnp.int32, (c, 2 * hd), 1)
    m0 = lane < hd
    r2 = lax.broadcasted_iota(jnp.int32, (2 * c, 2 * c), 0)
    c2 = lax.broadcasted_iota(jnp.int32, (2 * c, 2 * c), 1)
    same = (r2 >= c) == (c2 >= c)
    strict = jnp.logical_and(same, r2 > c2)
    incl = jnp.logical_and(same, r2 >= c2)
    eye = (r2 == c2).astype(F32)

    bf = lambda x: x.astype(BF16)
    stack = lambda x: bf(jnp.concatenate([jnp.where(m0, x, 0.0), jnp.where(m0, 0.0, x)], axis=0))
    twice = lambda x: bf(jnp.concatenate([x, x], axis=0))
    pick = lambda s: jnp.where(m0, s[:c], s[c:])

    pairs = range(npair)
    each = lambda f, *cols: [f(*args) for args in zip(*cols)]
    per_row = r_ref.shape[2] // LANES
    where = [(hp // per_row, slice((hp % per_row) * LANES, (hp % per_row + 1) * LANES)) for hp in pairs]
    load = lambda ref: [ref[bi, :, sl] for bi, sl in where]
    r, wl, k, v, a, b = (load(x) for x in (r_ref, wl_ref, k_ref, v_ref, a_ref, b_ref))

    cs = each(lambda w: _dot(tril, w, HIGHEST), wl)
    cs_last = each(lambda s: s[c - 1:c, :], cs)
    p_inv = each(lambda s: jnp.exp(-s), cs)
    at = each(lambda a, s, w: a * jnp.exp(s - w), a, cs, wl)
    rt = each(lambda r, s: r * jnp.exp(s), r, cs)
    at_s, rt_s = each(stack, at), each(stack, rt)
    bt_s = each(lambda b, p: stack(b * p), b, p_inv)
    kt_s = each(lambda k, p: stack(k * p), k, p_inv)

    ab = each(lambda x, y: jnp.where(strict, _dot_nt(x, y), 0.0), at_s, bt_s)
    ak = each(lambda x, y: jnp.where(strict, _dot_nt(x, y), 0.0), at_s, kt_s)
    rb = each(lambda x, y: jnp.where(incl, _dot_nt(x, y), 0.0), rt_s, bt_s)
    rk = each(lambda x, y: jnp.where(incl, _dot_nt(x, y), 0.0), rt_s, kt_s)

    tinv = each(lambda m: eye + m, ab)
    x = each(bf, ab)
    for _ in range(int(math.log2(c)) - 1):
        x = each(lambda m: bf(_dot(m, m)), x)
        tinv = each(lambda t, m: t + _dot(bf(t), m), tinv, x)

    gt = [g_ref[hp] for hp in pairs]
    gtb = each(bf, gt)
    vv = each(twice, v)
    rhs = each(lambda at, g, ak, vv: _dot_nt(bf(at), g) + pick(_dot(bf(ak), vv)), at, gtb, ak, vv)
    u = each(lambda t, x: pick(_dot(bf(t), twice(x))), tinv, rhs)
    y = each(lambda rt, g, rb, u, rk, vv:
             _dot_nt(bf(rt), g) + pick(_dot(bf(rb), twice(u)) + _dot(bf(rk), vv)),
             rt, gtb, rb, u, rk, vv)
    upd = each(lambda u, b, v, k, s, sl:
               _dot(bf(u.T), bf(b * jnp.exp(sl - s))) + _dot(bf(v.T), bf(k * jnp.exp(sl - s))),
               u, b, v, k, cs, cs_last)
    for hp in pairs:
        bi, sl = where[hp]
        y_ref[bi, :, sl] = y[hp]
        g_ref[hp] = gt[hp] * jnp.exp(cs_last[hp]) + jnp.where(same, upd[hp], 0.0)


def rwkv_scan(r, wl, k, v, a, b, *, rows):
    bsz, seq, d = r.shape
    c = SCAN_CHUNK
    spec = pl.BlockSpec((rows, c, d), lambda bi, ci: (bi, ci, 0))
    return pl.pallas_call(
        _rwkv_scan_kernel,
        grid=(bsz // rows, seq // c),
        in_specs=[spec] * 6,
        out_specs=spec,
        out_shape=jax.ShapeDtypeStruct((bsz, seq, d), F32),
        scratch_shapes=[pltpu.VMEM((rows * d // LANES, LANES, LANES), F32)],
        compiler_params=_cparams("parallel", "arbitrary"),
        name="rwkv_scan",
    )(r, wl, k, v, a, b)


MLA_ZCOLS = 1024
MLA_SCALE = 1.0 / math.sqrt(QK_NOPE + QK_ROPE)
ATTN_TILE = 256


def _mla_prep_kernel(z_ref, pos_ref, fl_ref, qn_ref, kvn_ref, wq_ref, wqs_ref, wk_ref, wv_ref,
                     q_ref, k_ref, v_ref):
    z = z_ref[...]
    c_q = _rms(z[:, :Q_LORA], qn_ref[...]).astype(BF16)
    c_kv = _rms(z[:, Q_LORA:Q_LORA + KV_LORA], kvn_ref[...]).astype(BF16)
    kr = z[:, Q_LORA + KV_LORA:Q_LORA + KV_LORA + LANES]
    krs = z[:, Q_LORA + KV_LORA + LANES:Q_LORA + KV_LORA + 2 * LANES]

    ang = pos_ref[...].astype(F32) * fl_ref[...]
    cos, sin = jnp.cos(ang), jnp.sin(ang)
    kr_rot = kr * cos + krs * sin

    q = _dot(c_q, wq_ref[...])
    qs = _dot(c_q, wqs_ref[...])
    kn = _dot(c_kv, wk_ref[...])
    v = _dot(c_kv, wv_ref[...])
    for c in range(v_ref.shape[0]):
        v_ref[c] = v[c * ATTN_TILE:(c + 1) * ATTN_TILE, :].T.astype(BF16)
    for h in range(MLA_HEADS):
        sl = slice(h * LANES, (h + 1) * LANES)
        q_ref[:, sl] = ((q[:, sl] * cos + qs[:, sl] * sin) * MLA_SCALE).astype(BF16)
        k_ref[:, sl] = (kn[:, sl] + kr_rot).astype(BF16)


def mla_prep(z, zblock, pos, f_lane, q_norm, kv_norm, wq, wqs, wk, wv, *, tm):
    t = z.shape[0]
    n = MLA_HEADS * LANES
    full = lambda a: pl.BlockSpec(a.shape, lambda i: (0,) * a.ndim)
    args = (f_lane, q_norm.reshape(1, -1), kv_norm.reshape(1, -1), wq, wqs, wk, wv)
    out = jax.ShapeDtypeStruct((t, n), BF16)
    nv = wv.shape[1]
    out_vt = jax.ShapeDtypeStruct((t // ATTN_TILE, nv, ATTN_TILE), BF16)
    ospec = pl.BlockSpec((tm, n), lambda i: (i, 0))
    vspec = pl.BlockSpec((tm // ATTN_TILE, nv, ATTN_TILE), lambda i: (i, 0, 0))
    return pl.pallas_call(
        _mla_prep_kernel,
        grid=(t // tm,),
        in_specs=[pl.BlockSpec((tm, MLA_ZCOLS), lambda i: (i, zblock)),
                  pl.BlockSpec((tm, 1), lambda i: (i, 0))] + [full(a) for a in args],
        out_specs=[ospec, ospec, vspec],
        out_shape=[out, out, out_vt],
        compiler_params=_cparams("parallel"),
        name="mla_prep",
    )(z, pos, *args)


ATTN_HEADS_PER_STEP = 8


def _attn_kernel(q_ref, k_ref, vt_ref, o_ref):
    tq = ATTN_TILE
    iq = pl.program_id(2)
    nh = q_ref.shape[2] // LANES
    heads = [slice(h * LANES, (h + 1) * LANES) for h in range(nh)]
    qs = [q_ref[0, :, sl] for sl in heads]

    def tile(j, carry, masked):
        start = pl.multiple_of(j * tq, tq)
        if masked:
            kc = lax.broadcasted_iota(jnp.int32, (tq, tq), 0) // MASK_CHUNK
            qc = lax.broadcasted_iota(jnp.int32, (tq, tq), 1) // MASK_CHUNK
            keep = kc <= qc
        hs = range(nh)
        s = [_dot_nt(k_ref[0, pl.ds(start, tq), heads[h]], qs[h]) for h in hs]
        if masked:
            s = [jnp.where(keep, x, NEG_INF) for x in s]
        m_new = [jnp.maximum(carry[h][0], jnp.max(s[h], axis=0, keepdims=True)) for h in hs]
        alpha = [jnp.exp(carry[h][0] - m_new[h]) for h in hs]
        p = [jnp.exp(s[h] - m_new[h]) for h in hs]
        l = [alpha[h] * carry[h][1] + jnp.sum(p[h], axis=0, keepdims=True) for h in hs]
        pv = [_dot(vt_ref[0, j, h * V_HEAD:(h + 1) * V_HEAD, :], p[h].astype(BF16))
              for h in hs]
        return tuple((m_new[h], l[h], alpha[h] * carry[h][2] + pv[h]) for h in hs)

    init = tuple((jnp.full((1, tq), NEG_INF, F32), jnp.zeros((1, tq), F32), jnp.zeros((V_HEAD, tq), F32))
                 for _ in heads)
    carry = lax.fori_loop(0, iq, lambda j, c: tile(j, c, False), init)
    carry = tile(iq, carry, True)
    out = [acc / l for _, l, acc in carry]
    for pair in range(nh // 2):
        both = jnp.concatenate(out[2 * pair:2 * pair + 2], axis=0)
        o_ref[0, :, pair * LANES:(pair + 1) * LANES] = both.T.astype(BF16)


def attention(q, k, vt):
    bsz, seq, n = q.shape
    tq = ATTN_TILE
    w = ATTN_HEADS_PER_STEP * LANES
    wv = ATTN_HEADS_PER_STEP * V_HEAD
    qspec = pl.BlockSpec((1, tq, w), lambda b, h, i: (b, i, h))
    kspec = pl.BlockSpec((1, seq, w), lambda b, h, i: (b, 0, h))
    vspec = pl.BlockSpec((1, seq // tq, wv, tq), lambda b, h, i: (b, 0, h, 0))
    return pl.pallas_call(
        _attn_kernel,
        grid=(bsz, n // w, seq // tq),
        in_specs=[qspec, kspec, vspec],
        out_specs=pl.BlockSpec((1, tq, wv), lambda b, h, i: (b, i, h)),
        out_shape=jax.ShapeDtypeStruct((bsz, seq, vt.shape[2]), BF16),
        compiler_params=_cparams("parallel", "parallel", "arbitrary"),
        name="attention",
    )(q, k, vt)


def _combine_kernel(x_ref, y_ref, r_ref, k_ref, v_ref, g_ref, o_ref, gates_ref,
                    bd_ref, rk_ref, gnw_ref, gnb_ref, wa_ref, wb_ref, wo_ref, out_ref):
    bd = bd_ref[...]
    inv_n = 1.0 / RW_HEAD_DIM
    y = y_ref[...]
    v = v_ref[...]
    mean = _seg_sum(y, bd) * inv_n
    yc = y - mean
    var = _seg_sum(yc * yc, bd) * inv_n
    yn = yc * lax.rsqrt(var + RW_GN_EPS) * gnw_ref[...] + gnb_ref[...]
    bonus = _seg_sum(r_ref[...] * k_ref[...] * rk_ref[...], bd) * v
    ya = _dot(((yn + bonus) * g_ref[...]).astype(BF16), wa_ref[...])
    yb = _dot(o_ref[...], wb_ref[...])
    d = ya.shape[1]
    gates = gates_ref[...]
    mix = gates[:, :d] * ya + gates[:, d:] * yb
    out_ref[...] = x_ref[...] + _dot(mix.astype(BF16), wo_ref[...])


def combine(x, y, r, k, v, g, o, z, gate_block, bd, r_k, gn_w, gn_b, wa, wb, wo, *, tm):
    t, d = x.shape
    row = lambda a: a.reshape(1, -1)
    tok = lambda a: pl.BlockSpec((tm, a.shape[1]), lambda i: (i, 0))
    full = lambda a: pl.BlockSpec(a.shape, lambda i: (0,) * a.ndim)
    toks = (x, y, r, k, v, g, o, z)
    consts = (bd, row(r_k), row(gn_w), row(gn_b), wa, wb, wo)
    gate_spec = pl.BlockSpec((tm, 2 * d), lambda i: (i, gate_block))
    return pl.pallas_call(
        _combine_kernel,
        grid=(t // tm,),
        in_specs=[tok(a) for a in toks[:-1]] + [gate_spec] + [full(a) for a in consts],
        out_specs=pl.BlockSpec((tm, d), lambda i: (i, 0)),
        out_shape=jax.ShapeDtypeStruct((t, d), F32),
        compiler_params=_cparams("parallel"),
        name="combine",
    )(*toks, *consts)


NOT_RANKED = 127.0
RANK_BLOCKS_PER_LOOP = 2
STAIR = tuple(PEER_TOPK // (ii + 1) for ii in range(PEER_TOPK))


def _top_ranks(s, k, exact_ties):
    n, t = s.shape
    key = lax.broadcasted_iota(jnp.int32, (n, LANES), 0)
    slot = lax.broadcasted_iota(jnp.int32, (k, LANES), 0)

    def one(r, s, rank, top):
        m = jnp.max(s, axis=0, keepdims=True)
        hit = s == m
        if exact_ties:
            hit = key == jnp.min(jnp.where(hit, key, n), axis=0, keepdims=True)
        rank = jnp.where(hit, jnp.asarray(r, F32), rank)
        s = jnp.where(hit, -jnp.inf, s)
        top = jnp.where(slot == r, m, top)
        return s, rank, top

    def body(r, carry):
        return tuple(one(r, *c) for c in carry)

    blocks = [s[:, c * LANES:(c + 1) * LANES] for c in range(t // LANES)]
    tops, ranks = [], []
    for g in range(0, len(blocks), RANK_BLOCKS_PER_LOOP):
        init = tuple((b, jnp.full((n, LANES), NOT_RANKED, F32), jnp.zeros((k, LANES), F32))
                     for b in blocks[g:g + RANK_BLOCKS_PER_LOOP])
        for _, rank, top in lax.fori_loop(0, k, body, init):
            tops.append(top)
            ranks.append(rank)
    return jnp.concatenate(tops, axis=1), jnp.concatenate(ranks, axis=1)


def _ranked_excess(rank, k):
    count = jnp.sum((rank < NOT_RANKED).astype(F32), axis=0, keepdims=True)
    return jnp.abs(count - k)


def _peer_route_kernel(q_ref, keys_ref, lam_ref, cc_ref, rho_ref, e1_ref):
    tm = q_ref.shape[0]
    k = PEER_TOPK
    neg = -jnp.inf

    def head(h, _):
        col = pl.multiple_of(h * 2 * HALF_Q, 2 * HALF_Q)
        s0 = _dot_nt(keys_ref[h, 0], q_ref[:, pl.ds(col, HALF_Q)])
        s1 = _dot_nt(keys_ref[h, 1], q_ref[:, pl.ds(col + HALF_Q, HALF_Q)])
        def rank_all(exact_ties):
            top0, rank0 = _top_ranks(s0, k, exact_ties)
            top1, rank1 = _top_ranks(s1, k, exact_ties)
            row8 = lax.broadcasted_iota(jnp.int32, (SUBLANES, tm), 0)
            groups = [top0[0:1] + top1[0:8], top0[0:1] + top1[8:16], top0[1:2] + top1[0:8]]
            for ii in range(2, 8):
                groups.append(jnp.where(row8 < STAIR[ii], top0[ii:ii + 1] + top1[0:8], neg))
            groups.append(top0[8:16] + top1[0:1])
            cand = jnp.concatenate(groups, axis=0)
            _, crank = _top_ranks(cand, k, exact_ties)
            return top0, rank0, top1, rank1, cand, crank

        quick = rank_all(False)
        excess = _ranked_excess(quick[1], k) + _ranked_excess(quick[3], k) + _ranked_excess(quick[5], k)
        top0, rank0, top1, rank1, cand, crank = lax.cond(
            jnp.max(excess) == 0.0, lambda: quick, lambda: rank_all(True))
        sel = crank < NOT_RANKED
        ex = jnp.where(sel, jnp.exp(cand - cand[0:1]), 0.0)
        z = jnp.sum(ex, axis=0, keepdims=True)
        self = sel.astype(F32)
        counts = [jnp.sum(self[0:16], axis=0, keepdims=True)]
        for g in range(2, 9):
            counts.append(jnp.sum(self[8 * g:8 * g + 8], axis=0, keepdims=True))
        lvec = jnp.concatenate(counts + [self[72:80]], axis=0)

        lam = jnp.zeros((N_KEYS, tm), F32)
        for ii in range(k):
            lam = jnp.where(rank0 == float(ii), lvec[ii:ii + 1], lam)
        lam_ref[h] = lam
        cc_ref[h] = jnp.exp(s0 - top0[0:1]) / z
        rho_ref[h] = rank1.astype(BF16)
        e1_ref[h] = jnp.exp(s1 - top1[0:1]).astype(BF16)
        return 0

    lax.fori_loop(0, PEER_HEADS, head, 0)


def peer_route(qp, keys, *, tm):
    t = qp.shape[0]
    out = jax.ShapeDtypeStruct((PEER_HEADS, N_KEYS, t), F32)
    out_b = jax.ShapeDtypeStruct((PEER_HEADS, N_KEYS, t), BF16)
    ospec = pl.BlockSpec((PEER_HEADS, N_KEYS, tm), lambda i: (0, 0, i))
    return pl.pallas_call(
        _peer_route_kernel,
        grid=(t // tm,),
        in_specs=[pl.BlockSpec((tm, qp.shape[1]), lambda i: (i, 0)),
                  pl.BlockSpec(keys.shape, lambda i: (0, 0, 0, 0))],
        out_specs=[ospec] * 4,
        out_shape=[out, out, out_b, out_b],
        compiler_params=_cparams("parallel"),
        name="peer_route",
    )(qp, keys)


PEER_ROWS = 8
PEER_TOKEN_CHUNK = 256


def _erf(x):
    return lax.erf(x)


def _gelu(x):
    return 0.5 * x * (1.0 + _erf(x * (1.0 / math.sqrt(2.0))))


def _peer_expert_kernel(x_ref, gn_ref, u_ref, vt_ref, lam_ref, cc_ref, rho_ref, e1_ref, out_ref,
                        xt_ref, acc_ref, pre0_ref, pre1_ref):
    j = pl.program_id(1)
    tm = xt_ref.shape[1]
    chunks = [slice(c, c + PEER_TOKEN_CHUNK) for c in range(0, tm, PEER_TOKEN_CHUNK)]

    @pl.when(j == 0)
    def _():
        xt_ref[...] = _rms(x_ref[...], gn_ref[...]).T.astype(BF16)
        acc_ref[...] = jnp.zeros_like(acc_ref)
        pre1_ref[...] = jnp.zeros_like(pre1_ref)

    def step(fill_ref, drain_ref):
        for cols in chunks:
            fill_ref[:, cols] = _dot(u_ref[...], xt_ref[:, cols])
        for cols in chunks:
            gs = []
            for ii in range(PEER_ROWS):
                rows = slice(ii * N_KEYS, (ii + 1) * N_KEYS)
                gate = jnp.zeros((N_KEYS, PEER_TOKEN_CHUNK), BF16)
                for h in range(PEER_HEADS):
                    lam = lam_ref[h, ii:ii + 1, cols].astype(BF16)
                    cc = cc_ref[h, ii:ii + 1, cols].astype(BF16)
                    gate = gate + jnp.where(rho_ref[h, :, cols] < lam, e1_ref[h, :, cols] * cc,
                                            jnp.zeros((), BF16))
                gs.append(_gelu(drain_ref[rows, cols]).astype(BF16) * gate)
            acc_ref[:, cols] += _dot(vt_ref[0], jnp.concatenate(gs, axis=0))

    @pl.when(j % 2 == 0)
    def _():
        step(pre0_ref, pre1_ref)

    @pl.when(j % 2 == 1)
    def _():
        step(pre1_ref, pre0_ref)

    @pl.when(j == pl.num_programs(1) - 1)
    def _():
        out_ref[...] = acc_ref[...].T


def peer_expert(x, gain, u, vt, lam, cc, rho, e1, *, tm):
    t, d = x.shape
    nrow = PEER_ROWS * N_KEYS
    nblk = u.shape[0] // nrow
    stage = lambda lag: (lambda j: jnp.clip(j - lag, 0, nblk - 1))
    rspec = pl.BlockSpec((PEER_HEADS, PEER_ROWS, tm), lambda i, j: (0, stage(1)(j), i))
    cspec = pl.BlockSpec((PEER_HEADS, N_KEYS, tm), lambda i, j: (0, 0, i))
    return pl.pallas_call(
        _peer_expert_kernel,
        grid=(t // tm, nblk + 1),
        in_specs=[pl.BlockSpec((tm, d), lambda i, j: (i, 0)),
                  pl.BlockSpec((1, d), lambda i, j: (0, 0)),
                  pl.BlockSpec((nrow, d), lambda i, j: (stage(0)(j), 0)),
                  pl.BlockSpec((1, d, nrow), lambda i, j: (stage(1)(j), 0, 0)),
                  rspec, rspec, cspec, cspec],
        out_specs=pl.BlockSpec((tm, d), lambda i, j: (i, 0)),
        out_shape=jax.ShapeDtypeStruct((t, d), F32),
        scratch_shapes=[pltpu.VMEM((d, tm), BF16), pltpu.VMEM((d, tm), F32),
                        pltpu.VMEM((nrow, tm), F32), pltpu.VMEM((nrow, tm), F32)],
        compiler_params=_cparams("parallel", "arbitrary"),
        name="peer_expert",
    )(x, gain.reshape(1, d), u, vt, lam, cc, rho, e1)


def _ple_final_kernel(x_ref, f_ref, p_ref, gp_ref, gf_ref, wg_ref, wp_ref, out_ref):
    x = x_ref[...] + f_ref[...]
    gate = _sigmoid(_dot(_rms(x, gp_ref[...]).astype(BF16), wg_ref[...]))
    x = x + gate * _dot(p_ref[...].astype(BF16), wp_ref[...])
    out_ref[...] = _rms(x, gf_ref[...])


def ple_final(x, f, p, g_ple, g_final, wg, wp, *, tm):
    t, d = x.shape
    tok = lambda a: pl.BlockSpec((tm, a.shape[1]), lambda i: (i, 0))
    full = lambda a: pl.BlockSpec(a.shape, lambda i: (0,) * a.ndim)
    consts = (g_ple.reshape(1, d), g_final.reshape(1, d), wg, wp)
    return pl.pallas_call(
        _ple_final_kernel,
        grid=(t // tm,),
        in_specs=[tok(x), tok(f), tok(p)] + [full(a) for a in consts],
        out_specs=tok(x),
        out_shape=jax.ShapeDtypeStruct((t, d), F32),
        compiler_params=_cparams("parallel"),
        name="ple_final",
    )(x, f, p, *consts)


def _place(cols, width, offset):
    return jnp.pad(cols, ((0, 0), (offset, width - offset - cols.shape[1])))


def _rw_in_weights(w_rw, mu):
    o3 = 3 * RW_DIM
    lw, la = 64, 64
    segs = [w_rw[:, :o3], _place(w_rw[:, o3:o3 + lw], LANES, 0),
            _place(w_rw[:, o3 + lw:o3 + lw + la], LANES, 0), _place(w_rw[:, o3 + lw + la:], 2 * LANES, 0)]
    mus = [mu[None, :o3], _place(mu[None, o3:o3 + lw], LANES, 0),
           _place(mu[None, o3 + lw:o3 + lw + la], LANES, 0), _place(mu[None, o3 + lw + la:], 2 * LANES, 0)]
    return jnp.concatenate(segs, axis=1), jnp.concatenate(mus, axis=1)[0]


def _mla_in_weights(w_mla):
    half = QK_ROPE // 2
    lat = Q_LORA + KV_LORA
    kr = w_mla[:, lat:]
    kr_sw = jnp.concatenate([-kr[:, half:], kr[:, :half]], axis=1)
    return jnp.concatenate([w_mla[:, :lat], _place(kr, LANES, QK_NOPE), _place(kr_sw, LANES, QK_NOPE),
                            jnp.zeros((w_mla.shape[0], LANES), w_mla.dtype)], axis=1)


def _mla_up_weights(w_uq, w_ukv):
    half = QK_ROPE // 2
    qd = QK_NOPE + QK_ROPE
    wq = w_uq.reshape(Q_LORA, MLA_HEADS, qd)
    rope = wq[:, :, QK_NOPE:]
    rope_sw = jnp.concatenate([-rope[:, :, half:], rope[:, :, :half]], axis=2)
    pad = lambda t, off: jnp.pad(t, ((0, 0), (0, 0), (off, LANES - off - t.shape[2])))
    wq_pad = pad(wq, 0).reshape(Q_LORA, MLA_HEADS * LANES)
    wq_sw = pad(rope_sw, QK_NOPE).reshape(Q_LORA, MLA_HEADS * LANES)
    lane = jnp.arange(MLA_HEADS * LANES) % LANES
    wk = jnp.where(lane < QK_NOPE, w_ukv, 0.0)
    wv = w_ukv.reshape(KV_LORA, MLA_HEADS, LANES)[:, :, QK_NOPE:].reshape(KV_LORA, MLA_HEADS * V_HEAD)
    return wq_pad, wq_sw, wk, wv


def kernel(x, p, positions, norm_mix, w_in, rw_mu, rw_w0, rw_w2, rw_a0, rw_a2, rw_g2, rw_k_k, rw_k_a, rw_r_k, rw_gn_w, rw_gn_b, rw_w_o, mla_q_norm, mla_w_uq, mla_kv_norm, mla_w_ukv, mla_w_o, w_out, norm_ffn, peer_w_q, peer_sub_keys, peer_u, peer_v, norm_ple, ple_w_gate, ple_w_proj, norm_final):
    bsz, seq, d = x.shape
    t = bsz * seq
    depth = p.shape[0]
    bf = lambda a: a.astype(BF16)
    rw_cols = 3 * RW_DIM + 64 + 64 + 128
    mla_cols = Q_LORA + KV_LORA + QK_ROPE

    head_of = jnp.arange(SEG_SUM_WIDTH) // RW_HEAD_DIM
    bd = bf(head_of[:, None] == head_of[None, :])
    inv_freq = ROPE_THETA ** (-jnp.arange(0, QK_ROPE, 2, dtype=F32) / QK_ROPE)
    f_lane = _place(jnp.concatenate([inv_freq, inv_freq])[None, :], LANES, QK_NOPE)
    pos = positions.reshape(t, 1)

    xf = x.reshape(t, d)
    assert depth == 1, "the final RMSNorm is fused into the layer's last kernel"
    for i in range(depth):
        w_rw, mu = _rw_in_weights(w_in[i][:, :rw_cols], rw_mu[i])
        w_mla = _mla_in_weights(w_in[i][:, rw_cols:rw_cols + mla_cols])
        w_gates = w_in[i][:, rw_cols + mla_cols:]
        tn = MLA_ZCOLS
        w_all = jnp.concatenate([w_rw, w_gates, w_mla], axis=1)
        gate_lo = RW_ZCOLS // tn
        gate_hi = gate_lo + w_gates.shape[1] // tn
        z = norm_matmul(xf, norm_mix[i], bf(w_all), tm=1024, tn=tn, sigmoid_blocks=(gate_lo, gate_hi))

        pad_rows = lambda w: jnp.pad(w, ((0, LANES - w.shape[0]), (0, 0)))
        r, wl, k, v, a, b, g = rwkv_prep(
            z.reshape(bsz, seq, -1), mu, rw_w0[i], pad_rows(rw_w2[i]), rw_a0[i],
            pad_rows(rw_a2[i]), rw_g2[i], rw_k_k[i], rw_k_a[i], bd, tm=256)
        y = rwkv_scan(r, wl, k, v, a, b, rows=math.gcd(bsz, 4))

        wq, wqs, wk, wv = _mla_up_weights(mla_w_uq[i], mla_w_ukv[i])
        q, kk, vv = mla_prep(z, gate_hi, pos, f_lane, mla_q_norm[i], mla_kv_norm[i],
                             bf(wq), bf(wqs), bf(wk), bf(wv), tm=512)
        n = MLA_HEADS * LANES
        nv = MLA_HEADS * V_HEAD
        o = attention(q.reshape(bsz, seq, n), kk.reshape(bsz, seq, n),
                      vv.reshape(bsz, seq // ATTN_TILE, nv, ATTN_TILE))

        flat = lambda a: a.reshape(t, -1)
        x1 = combine(xf, flat(y), flat(r), flat(k), flat(v), flat(g), flat(o), z, gate_lo * tn // (2 * d),
                     bd, rw_r_k[i].reshape(-1), rw_gn_w[i], rw_gn_b[i],
                     bf(rw_w_o[i]), bf(mla_w_o[i]), bf(w_out[i]), tm=256)

        qp = norm_matmul(x1, norm_ffn[i], bf(peer_w_q[i]), tm=1024, tn=1024, out_dtype=BF16)
        lam, cc, rho, e1 = peer_route(qp, bf(peer_sub_keys[i]), tm=256)
        nrow = PEER_ROWS * N_KEYS
        vt = bf(peer_v[i]).reshape(-1, nrow, d).transpose(0, 2, 1)
        ffn = peer_expert(x1, norm_ffn[i], bf(peer_u[i]), vt, lam, cc, rho, e1, tm=512)

        xf = ple_final(x1, ffn, p[i].reshape(t, -1), norm_ple[i], norm_final,
                       bf(ple_w_gate[i]), bf(ple_w_proj[i]), tm=256)
    return xf.reshape(bsz, seq, d)
```

```python
import functools
import math

import jax
import jax.numpy as jnp
from jax import lax
from jax.experimental import pallas as pl
from jax.experimental.pallas import tpu as pltpu

F32 = jnp.float32
BF16 = jnp.bfloat16
HIGHEST = lax.Precision.HIGHEST

LANES = 128
SUBLANES = 8
VMEM_LIMIT = 56 * 1024 * 1024

EPS = 1e-6
RW_HEADS = 8
RW_HEAD_DIM = 64
RW_DIM = RW_HEADS * RW_HEAD_DIM
RW_GN_EPS = 64e-5
SCAN_CHUNK = 64
SEG_SUM_WIDTH = 256

MLA_HEADS = 8
QK_NOPE = 64
QK_ROPE = 32
V_HEAD = 64
Q_LORA = 384
KV_LORA = 256
ROPE_THETA = 10000.0
MASK_CHUNK = 64
NEG_INF = -1e30

PEER_HEADS = 8
N_KEYS = 128
PEER_TOPK = 16
HALF_Q = 128


def _cparams(*sem):
    return pltpu.CompilerParams(dimension_semantics=sem, vmem_limit_bytes=VMEM_LIMIT)


def _dot(a, b, precision=None):
    return jnp.dot(a, b, preferred_element_type=F32, precision=precision)


def _dot_nt(a, b, precision=None):
    return lax.dot_general(a, b, (((1,), (1,)), ((), ())),
                           preferred_element_type=F32, precision=precision)


def _rms(x, gain):
    return x * lax.rsqrt(jnp.mean(x * x, axis=-1, keepdims=True) + EPS) * gain


def _sigmoid(x):
    return 1.0 / (1.0 + jnp.exp(-x))


def _seg_sum(x, bd):
    hi = x.astype(BF16)
    lo = (x - hi.astype(F32)).astype(BF16)
    w = bd.shape[0]
    parts = [_dot(hi[:, c:c + w], bd) + _dot(lo[:, c:c + w], bd) for c in range(0, x.shape[1], w)]
    return jnp.concatenate(parts, axis=1)


def _norm_matmul_kernel(x_ref, g_ref, w_ref, o_ref, h_ref, *, sigmoid_blocks):
    j = pl.program_id(1)

    @pl.when(j == 0)
    def _():
        h_ref[...] = _rms(x_ref[...], g_ref[...]).astype(BF16)

    def project(act):
        o_ref[...] = act(_dot(h_ref[...], w_ref[...])).astype(o_ref.dtype)

    if sigmoid_blocks is None:
        project(lambda y: y)
    else:
        gated = jnp.logical_and(j >= sigmoid_blocks[0], j < sigmoid_blocks[1])
        pl.when(gated)(lambda: project(_sigmoid))
        pl.when(jnp.logical_not(gated))(lambda: project(lambda y: y))


def norm_matmul(x, gain, w, *, tm, tn, sigmoid_blocks=None, out_dtype=F32):
    t, d = x.shape
    n = w.shape[1]
    return pl.pallas_call(
        functools.partial(_norm_matmul_kernel, sigmoid_blocks=sigmoid_blocks),
        grid=(t // tm, n // tn),
        in_specs=[pl.BlockSpec((tm, d), lambda i, j: (i, 0)),
                  pl.BlockSpec((1, d), lambda i, j: (0, 0)),
                  pl.BlockSpec((d, tn), lambda i, j: (0, j))],
        out_specs=pl.BlockSpec((tm, tn), lambda i, j: (i, j)),
        out_shape=jax.ShapeDtypeStruct((t, n), out_dtype),
        scratch_shapes=[pltpu.VMEM((tm, d), BF16)],
        compiler_params=_cparams("parallel", "arbitrary"),
        name="norm_matmul",
    )(x, gain.reshape(1, d), w)


RW_ZCOLS = 3 * RW_DIM + 4 * LANES


def _rwkv_prep_kernel(z_ref, zp_ref, mu_ref, w0_ref, w2_ref, a0_ref, a2_ref, g2_ref,
                      kk_ref, ka_ref, bd_ref,
                      r_ref, wl_ref, k_ref, v_ref, a_ref, b_ref, g_ref):
    z = z_ref[0]
    tm = z.shape[0]
    prev_last = zp_ref[0][SUBLANES - 1:SUBLANES, :]
    prev_last = jnp.where(pl.program_id(1) == 0, 0.0, prev_last)
    rolled = pltpu.roll(z, 1, 0)
    row = lax.broadcasted_iota(jnp.int32, (tm, 1), 0)
    z_prev = jnp.where(row == 0, prev_last, rolled)
    z = z + mu_ref[...] * (z_prev - z)

    o1, o2, o3 = RW_DIM, 2 * RW_DIM, 3 * RW_DIM
    r, k, v = z[:, :o1], z[:, o1:o2], z[:, o2:o3]
    zw, za, zg = (z[:, o3 + c * LANES:o3 + (c + 1) * LANES] for c in range(3))

    wpre = w0_ref[...] + _dot(jnp.tanh(zw), w2_ref[...])
    nx = -wpre
    softplus = jnp.maximum(nx, 0.0) + jnp.log(1.0 + jnp.exp(-jnp.abs(nx)))
    w = -softplus - 0.5
    iclr = _sigmoid(a0_ref[...] + _dot(za, a2_ref[...]))
    g = _dot(_sigmoid(zg), g2_ref[...])

    kk = k * kk_ref[...]
    ss = _seg_sum(kk * kk, bd_ref[...])
    kk = kk / jnp.maximum(jnp.sqrt(ss), 1e-12)

    r_ref[0] = r
    wl_ref[0] = -jnp.exp(w)
    k_ref[0] = k * (1.0 + (iclr - 1.0) * ka_ref[...])
    v_ref[0] = v
    a_ref[0] = -kk
    b_ref[0] = kk * iclr
    g_ref[0] = g


def rwkv_prep(z, mu, w0, w2, a0, a2, g2, k_k, k_a, bd, *, tm):
    bsz, seq, _ = z.shape
    zc = RW_ZCOLS
    d = RW_DIM
    row = lambda a: a.reshape(1, -1)
    full = lambda a: pl.BlockSpec(a.shape, lambda b, i: (0,) * a.ndim)
    args = (row(mu), row(w0), w2, row(a0), a2, g2, row(k_k), row(k_a), bd)
    out = jax.ShapeDtypeStruct((bsz, seq, d), F32)
    ospec = pl.BlockSpec((1, tm, d), lambda b, i: (b, i, 0))
    return pl.pallas_call(
        _rwkv_prep_kernel,
        grid=(bsz, seq // tm),
        in_specs=[pl.BlockSpec((1, tm, zc), lambda b, i: (b, i, 0)),
                  pl.BlockSpec((1, SUBLANES, zc),
                               lambda b, i: (b, jnp.maximum(i * (tm // SUBLANES) - 1, 0), 0))]
                 + [full(a) for a in args],
        out_specs=[ospec] * 7,
        out_shape=[out] * 7,
        compiler_params=_cparams("parallel", "arbitrary"),
        name="rwkv_prep",
    )(z, z, *args)


def _rwkv_scan_kernel(r_ref, wl_ref, k_ref, v_ref, a_ref, b_ref, y_ref, g_ref):
    c = SCAN_CHUNK
    hd = RW_HEAD_DIM
    npair = g_ref.shape[0]

    @pl.when(pl.program_id(1) == 0)
    def _():
        g_ref[...] = jnp.zeros_like(g_ref)

    ri = lax.broadcasted_iota(jnp.int32, (c, c), 0)
    ci = lax.broadcasted_iota(jnp.int32, (c, c), 1)
    tril = (ri >= ci).astype(F32)
    lane = lax.broadcasted_iota(jnp.int32, (c, 2 * hd), 1)
    m0 = lane < hd
    r2 = lax.broadcasted_iota(jnp.int32, (2 * c, 2 * c), 0)
    c2 = lax.broadcasted_iota(jnp.int32, (2 * c, 2 * c), 1)
    same = (r2 >= c) == (c2 >= c)
    strict = jnp.logical_and(same, r2 > c2)
    incl = jnp.logical_and(same, r2 >= c2)
    eye = (r2 == c2).astype(F32)

    bf = lambda x: x.astype(BF16)
    stack = lambda x: bf(jnp.concatenate([jnp.where(m0, x, 0.0), jnp.where(m0, 0.0, x)], axis=0))
    twice = lambda x: bf(jnp.concatenate([x, x], axis=0))
    pick = lambda s: jnp.where(m0, s[:c], s[c:])

    pairs = range(npair)
    each = lambda f, *cols: [f(*args) for args in zip(*cols)]
    per_row = r_ref.shape[2] // LANES
    where = [(hp // per_row, slice((hp % per_row) * LANES, (hp % per_row + 1) * LANES)) for hp in pairs]
    load = lambda ref: [ref[bi, :, sl] for bi, sl in where]
    r, wl, k, v, a, b = (load(x) for x in (r_ref, wl_ref, k_ref, v_ref, a_ref, b_ref))

    cs = each(lambda w: _dot(tril, w, HIGHEST), wl)
    cs_last = each(lambda s: s[c - 1:c, :], cs)
    p_inv = each(lambda s: jnp.exp(-s), cs)
    at = each(lambda a, s, w: a * jnp.exp(s - w), a, cs, wl)
    rt = each(lambda r, s: r * jnp.exp(s), r, cs)
    at_s, rt_s = each(stack, at), each(stack, rt)
    bt_s = each(lambda b, p: stack(b * p), b, p_inv)
    kt_s = each(lambda k, p: stack(k * p), k, p_inv)

    ab = each(lambda x, y: jnp.where(strict, _dot_nt(x, y), 0.0), at_s, bt_s)
    ak = each(lambda x, y: jnp.where(strict, _dot_nt(x, y), 0.0), at_s, kt_s)
    rb = each(lambda x, y: jnp.where(incl, _dot_nt(x, y), 0.0), rt_s, bt_s)
    rk = each(lambda x, y: jnp.where(incl, _dot_nt(x, y), 0.0), rt_s, kt_s)

    tinv = each(lambda m: eye + m, ab)
    x = each(bf, ab)
    for _ in range(int(math.log2(c)) - 1):
        x = each(lambda m: bf(_dot(m, m)), x)
        tinv = each(lambda t, m: t + _dot(bf(t), m), tinv, x)

    gt = [g_ref[hp] for hp in pairs]
    gtb = each(bf, gt)
    vv = each(twice, v)
    rhs = each(lambda at, g, ak, vv: _dot_nt(bf(at), g) + pick(_dot(bf(ak), vv)), at, gtb, ak, vv)
    u = each(lambda t, x: pick(_dot(bf(t), twice(x))), tinv, rhs)
    y = each(lambda rt, g, rb, u, rk, vv:
             _dot_nt(bf(rt), g) + pick(_dot(bf(rb), twice(u)) + _dot(bf(rk), vv)),
             rt, gtb, rb, u, rk, vv)
    upd = each(lambda u, b, v, k, s, sl:
               _dot(bf(u.T), bf(b * jnp.exp(sl - s))) + _dot(bf(v.T), bf(k * jnp.exp(sl - s))),
               u, b, v, k, cs, cs_last)
    for hp in pairs:
        bi, sl = where[hp]
        y_ref[bi, :, sl] = y[hp]
        g_ref[hp] = gt[hp] * jnp.exp(cs_last[hp]) + jnp.where(same, upd[hp], 0.0)


def rwkv_scan(r, wl, k, v, a, b, *, rows):
    bsz, seq, d = r.shape
    c = SCAN_CHUNK
    spec = pl.BlockSpec((rows, c, d), lambda bi, ci: (bi, ci, 0))
    return pl.pallas_call(
        _rwkv_scan_kernel,
        grid=(bsz // rows, seq // c),
        in_specs=[spec] * 6,
        out_specs=spec,
        out_shape=jax.ShapeDtypeStruct((bsz, seq, d), F32),
        scratch_shapes=[pltpu.VMEM((rows * d // LANES, LANES, LANES), F32)],
        compiler_params=_cparams("parallel", "arbitrary"),
        name="rwkv_scan",
    )(r, wl, k, v, a, b)


MLA_ZCOLS = 1024
MLA_SCALE = math.log2(math.e) / math.sqrt(QK_NOPE + QK_ROPE)
ATTN_TILE = 256


def _mla_prep_kernel(z_ref, pos_ref, fl_ref, qn_ref, kvn_ref, wq_ref, wqs_ref, wk_ref, wv_ref,
                     q_ref, k_ref, v_ref):
    z = z_ref[...]
    c_q = _rms(z[:, :Q_LORA], qn_ref[...]).astype(BF16)
    c_kv = _rms(z[:, Q_LORA:Q_LORA + KV_LORA], kvn_ref[...]).astype(BF16)
    kr = z[:, Q_LORA + KV_LORA:Q_LORA + KV_LORA + LANES]
    krs = z[:, Q_LORA + KV_LORA + LANES:Q_LORA + KV_LORA + 2 * LANES]

    ang = pos_ref[...].astype(F32) * fl_ref[...]
    cos, sin = jnp.cos(ang), jnp.sin(ang)
    kr_rot = kr * cos + krs * sin

    q = _dot(c_q, wq_ref[...])
    qs = _dot(c_q, wqs_ref[...])
    kn = _dot(c_kv, wk_ref[...])
    v = _dot(c_kv, wv_ref[...])
    for c in range(v_ref.shape[0]):
        v_ref[c] = v[c * ATTN_TILE:(c + 1) * ATTN_TILE, :].T.astype(BF16)
    for h in range(MLA_HEADS):
        sl = slice(h * LANES, (h + 1) * LANES)
        q_ref[:, sl] = ((q[:, sl] * cos + qs[:, sl] * sin) * MLA_SCALE).astype(BF16)
        k_ref[:, sl] = (kn[:, sl] + kr_rot).astype(BF16)


def mla_prep(z, zblock, pos, f_lane, q_norm, kv_norm, wq, wqs, wk, wv, *, tm):
    t = z.shape[0]
    n = MLA_HEADS * LANES
    full = lambda a: pl.BlockSpec(a.shape, lambda i: (0,) * a.ndim)
    args = (f_lane, q_norm.reshape(1, -1), kv_norm.reshape(1, -1), wq, wqs, wk, wv)
    out = jax.ShapeDtypeStruct((t, n), BF16)
    nv = wv.shape[1]
    out_vt = jax.ShapeDtypeStruct((t // ATTN_TILE, nv, ATTN_TILE), BF16)
    ospec = pl.BlockSpec((tm, n), lambda i: (i, 0))
    vspec = pl.BlockSpec((tm // ATTN_TILE, nv, ATTN_TILE), lambda i: (i, 0, 0))
    return pl.pallas_call(
        _mla_prep_kernel,
        grid=(t // tm,),
        in_specs=[pl.BlockSpec((tm, MLA_ZCOLS), lambda i: (i, zblock)),
                  pl.BlockSpec((tm, 1), lambda i: (i, 0))] + [full(a) for a in args],
        out_specs=[ospec, ospec, vspec],
        out_shape=[out, out, out_vt],
        compiler_params=_cparams("parallel"),
        name="mla_prep",
    )(z, pos, *args)


ATTN_HEADS_PER_STEP = 8


def _attn_kernel(q_ref, k_ref, vt_ref, o_ref):
    tq = ATTN_TILE
    iq = pl.program_id(2)
    nh = q_ref.shape[2] // LANES
    heads = [slice(h * LANES, (h + 1) * LANES) for h in range(nh)]
    qs = [q_ref[0, :, sl] for sl in heads]

    def tile(j, carry, masked):
        start = pl.multiple_of(j * tq, tq)
        if masked:
            kc = lax.broadcasted_iota(jnp.int32, (tq, tq), 0) // MASK_CHUNK
            qc = lax.broadcasted_iota(jnp.int32, (tq, tq), 1) // MASK_CHUNK
            keep = kc <= qc
        hs = range(nh)
        s = [_dot_nt(k_ref[0, pl.ds(start, tq), heads[h]], qs[h]) for h in hs]
        if masked:
            s = [jnp.where(keep, x, NEG_INF) for x in s]
        m_new = [jnp.maximum(carry[h][0], jnp.max(s[h], axis=0, keepdims=True)) for h in hs]
        alpha = [jnp.exp2(carry[h][0] - m_new[h]) for h in hs]
        p = [jnp.exp2(s[h] - m_new[h]) for h in hs]
        l = [alpha[h] * carry[h][1] + jnp.sum(p[h], axis=0, keepdims=True) for h in hs]
        pv = [_dot(vt_ref[0, j, h * V_HEAD:(h + 1) * V_HEAD, :], p[h].astype(BF16))
              for h in hs]
        return tuple((m_new[h], l[h], alpha[h] * carry[h][2] + pv[h]) for h in hs)

    init = tuple((jnp.full((1, tq), NEG_INF, F32), jnp.zeros((1, tq), F32), jnp.zeros((V_HEAD, tq), F32))
                 for _ in heads)
    carry = lax.fori_loop(0, iq, lambda j, c: tile(j, c, False), init)
    carry = tile(iq, carry, True)
    out = [acc / l for _, l, acc in carry]
    for pair in range(nh // 2):
        both = jnp.concatenate(out[2 * pair:2 * pair + 2], axis=0)
        o_ref[0, :, pair * LANES:(pair + 1) * LANES] = both.T.astype(BF16)


def attention(q, k, vt):
    bsz, seq, n = q.shape
    tq = ATTN_TILE
    w = ATTN_HEADS_PER_STEP * LANES
    wv = ATTN_HEADS_PER_STEP * V_HEAD
    qspec = pl.BlockSpec((1, tq, w), lambda b, h, i: (b, i, h))
    kspec = pl.BlockSpec((1, seq, w), lambda b, h, i: (b, 0, h))
    vspec = pl.BlockSpec((1, seq // tq, wv, tq), lambda b, h, i: (b, 0, h, 0))
    return pl.pallas_call(
        _attn_kernel,
        grid=(bsz, n // w, seq // tq),
        in_specs=[qspec, kspec, vspec],
        out_specs=pl.BlockSpec((1, tq, wv), lambda b, h, i: (b, i, h)),
        out_shape=jax.ShapeDtypeStruct((bsz, seq, vt.shape[2]), BF16),
        compiler_params=_cparams("parallel", "parallel", "arbitrary"),
        name="attention",
    )(q, k, vt)


def _combine_kernel(x_ref, y_ref, r_ref, k_ref, v_ref, g_ref, o_ref, gates_ref,
                    bd_ref, rk_ref, gnw_ref, gnb_ref, wa_ref, wb_ref, wo_ref, gq_ref, wq_ref,
                    out_ref, qp_ref):
    bd = bd_ref[...]
    inv_n = 1.0 / RW_HEAD_DIM
    y = y_ref[...]
    v = v_ref[...]
    mean = _seg_sum(y, bd) * inv_n
    yc = y - mean
    var = _seg_sum(yc * yc, bd) * inv_n
    yn = yc * lax.rsqrt(var + RW_GN_EPS) * gnw_ref[...] + gnb_ref[...]
    bonus = _seg_sum(r_ref[...] * k_ref[...] * rk_ref[...], bd) * v
    ya = _dot(((yn + bonus) * g_ref[...]).astype(BF16), wa_ref[...])
    yb = _dot(o_ref[...], wb_ref[...])
    d = ya.shape[1]
    gates = gates_ref[...]
    mix = gates[:, :d] * ya + gates[:, d:] * yb
    x1 = x_ref[...] + _dot(mix.astype(BF16), wo_ref[...])
    out_ref[...] = x1
    qp_ref[...] = _dot(_rms(x1, gq_ref[...]).astype(BF16), wq_ref[...]).astype(BF16)


def combine(x, y, r, k, v, g, o, z, gate_block, bd, r_k, gn_w, gn_b, wa, wb, wo, g_ffn, w_q, *, tm):
    t, d = x.shape
    row = lambda a: a.reshape(1, -1)
    tok = lambda a: pl.BlockSpec((tm, a.shape[1]), lambda i: (i, 0))
    full = lambda a: pl.BlockSpec(a.shape, lambda i: (0,) * a.ndim)
    toks = (x, y, r, k, v, g, o, z)
    consts = (bd, row(r_k), row(gn_w), row(gn_b), wa, wb, wo, row(g_ffn), w_q)
    gate_spec = pl.BlockSpec((tm, 2 * d), lambda i: (i, gate_block))
    nq = w_q.shape[1]
    return pl.pallas_call(
        _combine_kernel,
        grid=(t // tm,),
        in_specs=[tok(a) for a in toks[:-1]] + [gate_spec] + [full(a) for a in consts],
        out_specs=[pl.BlockSpec((tm, d), lambda i: (i, 0)), pl.BlockSpec((tm, nq), lambda i: (i, 0))],
        out_shape=[jax.ShapeDtypeStruct((t, d), F32), jax.ShapeDtypeStruct((t, nq), BF16)],
        compiler_params=_cparams("parallel"),
        name="combine",
    )(*toks, *consts)


NOT_RANKED = 127.0
RANK_BLOCKS_PER_LOOP = 2
STAIR = tuple(PEER_TOPK // (ii + 1) for ii in range(PEER_TOPK))


def _top_ranks(s, k, exact_ties):
    n, t = s.shape
    key = lax.broadcasted_iota(jnp.int32, (n, LANES), 0)
    slot = lax.broadcasted_iota(jnp.int32, (k, LANES), 0)

    def one(r, s, rank, top):
        m = jnp.max(s, axis=0, keepdims=True)
        hit = s == m
        if exact_ties:
            hit = key == jnp.min(jnp.where(hit, key, n), axis=0, keepdims=True)
        rank = jnp.where(hit, jnp.asarray(r, F32), rank)
        s = jnp.where(hit, -jnp.inf, s)
        top = jnp.where(slot == r, m, top)
        return s, rank, top

    def body(r, carry):
        return tuple(one(r, *c) for c in carry)

    blocks = [s[:, c * LANES:(c + 1) * LANES] for c in range(t // LANES)]
    tops, ranks = [], []
    for g in range(0, len(blocks), RANK_BLOCKS_PER_LOOP):
        init = tuple((b, jnp.full((n, LANES), NOT_RANKED, F32), jnp.zeros((k, LANES), F32))
                     for b in blocks[g:g + RANK_BLOCKS_PER_LOOP])
        for _, rank, top in lax.fori_loop(0, k, body, init):
            tops.append(top)
            ranks.append(rank)
    return jnp.concatenate(tops, axis=1), jnp.concatenate(ranks, axis=1)


def _ranked_excess(rank, k):
    count = jnp.sum((rank < NOT_RANKED).astype(F32), axis=0, keepdims=True)
    return jnp.abs(count - k)


def _peer_route_kernel(q_ref, keys_ref, lam_ref, cc_ref, rho_ref, e1_ref):
    tm = q_ref.shape[0]
    k = PEER_TOPK
    neg = -jnp.inf

    def head(h, _):
        col = pl.multiple_of(h * 2 * HALF_Q, 2 * HALF_Q)
        s0 = _dot_nt(keys_ref[h, 0], q_ref[:, pl.ds(col, HALF_Q)])
        s1 = _dot_nt(keys_ref[h, 1], q_ref[:, pl.ds(col + HALF_Q, HALF_Q)])
        def rank_all(exact_ties):
            top0, rank0 = _top_ranks(s0, k, exact_ties)
            top1, rank1 = _top_ranks(s1, k, exact_ties)
            row8 = lax.broadcasted_iota(jnp.int32, (SUBLANES, tm), 0)
            groups = [top0[0:1] + top1[0:8], top0[0:1] + top1[8:16], top0[1:2] + top1[0:8]]
            for ii in range(2, 8):
                groups.append(jnp.where(row8 < STAIR[ii], top0[ii:ii + 1] + top1[0:8], neg))
            groups.append(top0[8:16] + top1[0:1])
            cand = jnp.concatenate(groups, axis=0)
            _, crank = _top_ranks(cand, k, exact_ties)
            return top0, rank0, top1, rank1, cand, crank

        quick = rank_all(False)
        excess = _ranked_excess(quick[1], k) + _ranked_excess(quick[3], k) + _ranked_excess(quick[5], k)
        top0, rank0, top1, rank1, cand, crank = lax.cond(
            jnp.max(excess) == 0.0, lambda: quick, lambda: rank_all(True))
        sel = crank < NOT_RANKED
        ex = jnp.where(sel, jnp.exp(cand - cand[0:1]), 0.0)
        z = jnp.sum(ex, axis=0, keepdims=True)
        self = sel.astype(F32)
        counts = [jnp.sum(self[0:16], axis=0, keepdims=True)]
        for g in range(2, 9):
            counts.append(jnp.sum(self[8 * g:8 * g + 8], axis=0, keepdims=True))
        lvec = jnp.concatenate(counts + [self[72:80]], axis=0)

        lam = jnp.zeros((N_KEYS, tm), F32)
        for ii in range(k):
            lam = jnp.where(rank0 == float(ii), lvec[ii:ii + 1], lam)
        lam_ref[h] = lam
        cc_ref[h] = jnp.exp(s0 - top0[0:1]) / z
        rho_ref[h] = rank1.astype(BF16)
        e1_ref[h] = jnp.exp(s1 - top1[0:1]).astype(BF16)
        return 0

    lax.fori_loop(0, PEER_HEADS, head, 0)


def peer_route(qp, keys, *, tm):
    t = qp.shape[0]
    out = jax.ShapeDtypeStruct((PEER_HEADS, N_KEYS, t), F32)
    out_b = jax.ShapeDtypeStruct((PEER_HEADS, N_KEYS, t), BF16)
    ospec = pl.BlockSpec((PEER_HEADS, N_KEYS, tm), lambda i: (0, 0, i))
    return pl.pallas_call(
        _peer_route_kernel,
        grid=(t // tm,),
        in_specs=[pl.BlockSpec((tm, qp.shape[1]), lambda i: (i, 0)),
                  pl.BlockSpec(keys.shape, lambda i: (0, 0, 0, 0))],
        out_specs=[ospec] * 4,
        out_shape=[out, out, out_b, out_b],
        compiler_params=_cparams("parallel"),
        name="peer_route",
    )(qp, keys)


PEER_ROWS = 8
PEER_TOKEN_CHUNK = 256


def _erf(x):
    return lax.erf(x)


def _gelu(x):
    return 0.5 * x * (1.0 + _erf(x * (1.0 / math.sqrt(2.0))))


def _peer_expert_kernel(x_ref, gn_ref, u_ref, vt_ref, lam_ref, cc_ref, rho_ref, e1_ref, out_ref,
                        xt_ref, acc_ref, pre0_ref, pre1_ref):
    j = pl.program_id(1)
    tm = xt_ref.shape[1]
    chunks = [slice(c, c + PEER_TOKEN_CHUNK) for c in range(0, tm, PEER_TOKEN_CHUNK)]

    @pl.when(j == 0)
    def _():
        xt_ref[...] = _rms(x_ref[...], gn_ref[...]).T.astype(BF16)
        acc_ref[...] = jnp.zeros_like(acc_ref)
        pre1_ref[...] = jnp.zeros_like(pre1_ref)

    def step(fill_ref, drain_ref):
        for cols in chunks:
            fill_ref[:, cols] = _dot(u_ref[...], xt_ref[:, cols])
        for cols in chunks:
            gs = []
            for ii in range(PEER_ROWS):
                rows = slice(ii * N_KEYS, (ii + 1) * N_KEYS)
                gate = jnp.zeros((N_KEYS, PEER_TOKEN_CHUNK), BF16)
                for h in range(PEER_HEADS):
                    lam = lam_ref[h, ii:ii + 1, cols].astype(BF16)
                    cc = cc_ref[h, ii:ii + 1, cols].astype(BF16)
                    gate = gate + jnp.where(rho_ref[h, :, cols] < lam, e1_ref[h, :, cols] * cc,
                                            jnp.zeros((), BF16))
                gs.append(_gelu(drain_ref[rows, cols]).astype(BF16) * gate)
            acc_ref[:, cols] += _dot(vt_ref[0], jnp.concatenate(gs, axis=0))

    @pl.when(j % 2 == 0)
    def _():
        step(pre0_ref, pre1_ref)

    @pl.when(j % 2 == 1)
    def _():
        step(pre1_ref, pre0_ref)

    @pl.when(j == pl.num_programs(1) - 1)
    def _():
        out_ref[...] = acc_ref[...].T


def peer_expert(x, gain, u, vt, lam, cc, rho, e1, *, tm):
    t, d = x.shape
    nrow = PEER_ROWS * N_KEYS
    nblk = u.shape[0] // nrow
    stage = lambda lag: (lambda j: jnp.clip(j - lag, 0, nblk - 1))
    rspec = pl.BlockSpec((PEER_HEADS, PEER_ROWS, tm), lambda i, j: (0, stage(1)(j), i))
    cspec = pl.BlockSpec((PEER_HEADS, N_KEYS, tm), lambda i, j: (0, 0, i))
    return pl.pallas_call(
        _peer_expert_kernel,
        grid=(t // tm, nblk + 1),
        in_specs=[pl.BlockSpec((tm, d), lambda i, j: (i, 0)),
                  pl.BlockSpec((1, d), lambda i, j: (0, 0)),
                  pl.BlockSpec((nrow, d), lambda i, j: (stage(0)(j), 0)),
                  pl.BlockSpec((1, d, nrow), lambda i, j: (stage(1)(j), 0, 0)),
                  rspec, rspec, cspec, cspec],
        out_specs=pl.BlockSpec((tm, d), lambda i, j: (i, 0)),
        out_shape=jax.ShapeDtypeStruct((t, d), F32),
        scratch_shapes=[pltpu.VMEM((d, tm), BF16), pltpu.VMEM((d, tm), F32),
                        pltpu.VMEM((nrow, tm), F32), pltpu.VMEM((nrow, tm), F32)],
        compiler_params=_cparams("parallel", "arbitrary"),
        name="peer_expert",
    )(x, gain.reshape(1, d), u, vt, lam, cc, rho, e1)


def _ple_final_kernel(x_ref, f_ref, p_ref, gp_ref, gf_ref, wg_ref, wp_ref, out_ref):
    x = x_ref[...] + f_ref[...]
    gate = _sigmoid(_dot(_rms(x, gp_ref[...]).astype(BF16), wg_ref[...]))
    x = x + gate * _dot(p_ref[...].astype(BF16), wp_ref[...])
    out_ref[...] = _rms(x, gf_ref[...])


def ple_final(x, f, p, g_ple, g_final, wg, wp, *, tm):
    t, d = x.shape
    tok = lambda a: pl.BlockSpec((tm, a.shape[1]), lambda i: (i, 0))
    full = lambda a: pl.BlockSpec(a.shape, lambda i: (0,) * a.ndim)
    consts = (g_ple.reshape(1, d), g_final.reshape(1, d), wg, wp)
    return pl.pallas_call(
        _ple_final_kernel,
        grid=(t // tm,),
        in_specs=[tok(x), tok(f), tok(p)] + [full(a) for a in consts],
        out_specs=tok(x),
        out_shape=jax.ShapeDtypeStruct((t, d), F32),
        compiler_params=_cparams("parallel"),
        name="ple_final",
    )(x, f, p, *consts)


def _place(cols, width, offset):
    return jnp.pad(cols, ((0, 0), (offset, width - offset - cols.shape[1])))


def _rw_in_weights(w_rw, mu):
    o3 = 3 * RW_DIM
    lw, la = 64, 64
    segs = [w_rw[:, :o3], _place(w_rw[:, o3:o3 + lw], LANES, 0),
            _place(w_rw[:, o3 + lw:o3 + lw + la], LANES, 0), _place(w_rw[:, o3 + lw + la:], 2 * LANES, 0)]
    mus = [mu[None, :o3], _place(mu[None, o3:o3 + lw], LANES, 0),
           _place(mu[None, o3 + lw:o3 + lw + la], LANES, 0), _place(mu[None, o3 + lw + la:], 2 * LANES, 0)]
    return jnp.concatenate(segs, axis=1), jnp.concatenate(mus, axis=1)[0]


def _mla_in_weights(w_mla):
    half = QK_ROPE // 2
    lat = Q_LORA + KV_LORA
    kr = w_mla[:, lat:]
    kr_sw = jnp.concatenate([-kr[:, half:], kr[:, :half]], axis=1)
    return jnp.concatenate([w_mla[:, :lat], _place(kr, LANES, QK_NOPE), _place(kr_sw, LANES, QK_NOPE),
                            jnp.zeros((w_mla.shape[0], LANES), w_mla.dtype)], axis=1)


def _mla_up_weights(w_uq, w_ukv):
    half = QK_ROPE // 2
    qd = QK_NOPE + QK_ROPE
    wq = w_uq.reshape(Q_LORA, MLA_HEADS, qd)
    rope = wq[:, :, QK_NOPE:]
    rope_sw = jnp.concatenate([-rope[:, :, half:], rope[:, :, :half]], axis=2)
    pad = lambda t, off: jnp.pad(t, ((0, 0), (0, 0), (off, LANES - off - t.shape[2])))
    wq_pad = pad(wq, 0).reshape(Q_LORA, MLA_HEADS * LANES)
    wq_sw = pad(rope_sw, QK_NOPE).reshape(Q_LORA, MLA_HEADS * LANES)
    lane = jnp.arange(MLA_HEADS * LANES) % LANES
    wk = jnp.where(lane < QK_NOPE, w_ukv, 0.0)
    wv = w_ukv.reshape(KV_LORA, MLA_HEADS, LANES)[:, :, QK_NOPE:].reshape(KV_LORA, MLA_HEADS * V_HEAD)
    return wq_pad, wq_sw, wk, wv


def kernel(x, p, positions, norm_mix, w_in, rw_mu, rw_w0, rw_w2, rw_a0, rw_a2, rw_g2, rw_k_k, rw_k_a, rw_r_k, rw_gn_w, rw_gn_b, rw_w_o, mla_q_norm, mla_w_uq, mla_kv_norm, mla_w_ukv, mla_w_o, w_out, norm_ffn, peer_w_q, peer_sub_keys, peer_u, peer_v, norm_ple, ple_w_gate, ple_w_proj, norm_final):
    bsz, seq, d = x.shape
    t = bsz * seq
    depth = p.shape[0]
    bf = lambda a: a.astype(BF16)
    rw_cols = 3 * RW_DIM + 64 + 64 + 128
    mla_cols = Q_LORA + KV_LORA + QK_ROPE

    head_of = jnp.arange(SEG_SUM_WIDTH) // RW_HEAD_DIM
    bd = bf(head_of[:, None] == head_of[None, :])
    inv_freq = ROPE_THETA ** (-jnp.arange(0, QK_ROPE, 2, dtype=F32) / QK_ROPE)
    f_lane = _place(jnp.concatenate([inv_freq, inv_freq])[None, :], LANES, QK_NOPE)
    pos = positions.reshape(t, 1)

    xf = x.reshape(t, d)
    assert depth == 1, "the final RMSNorm is fused into the layer's last kernel"
    for i in range(depth):
        w_rw, mu = _rw_in_weights(w_in[i][:, :rw_cols], rw_mu[i])
        w_mla = _mla_in_weights(w_in[i][:, rw_cols:rw_cols + mla_cols])
        w_gates = w_in[i][:, rw_cols + mla_cols:]
        tn = MLA_ZCOLS
        w_all = jnp.concatenate([w_rw, w_gates, w_mla], axis=1)
        gate_lo = RW_ZCOLS // tn
        gate_hi = gate_lo + w_gates.shape[1] // tn
        z = norm_matmul(xf, norm_mix[i], bf(w_all), tm=1024, tn=tn, sigmoid_blocks=(gate_lo, gate_hi))

        pad_rows = lambda w: jnp.pad(w, ((0, LANES - w.shape[0]), (0, 0)))
        r, wl, k, v, a, b, g = rwkv_prep(
            z.reshape(bsz, seq, -1), mu, rw_w0[i], pad_rows(rw_w2[i]), rw_a0[i],
            pad_rows(rw_a2[i]), rw_g2[i], rw_k_k[i], rw_k_a[i], bd, tm=256)
        y = rwkv_scan(r, wl, k, v, a, b, rows=math.gcd(bsz, 4))

        wq, wqs, wk, wv = _mla_up_weights(mla_w_uq[i], mla_w_ukv[i])
        q, kk, vv = mla_prep(z, gate_hi, pos, f_lane, mla_q_norm[i], mla_kv_norm[i],
                             bf(wq), bf(wqs), bf(wk), bf(wv), tm=512)
        n = MLA_HEADS * LANES
        nv = MLA_HEADS * V_HEAD
        o = attention(q.reshape(bsz, seq, n), kk.reshape(bsz, seq, n),
                      vv.reshape(bsz, seq // ATTN_TILE, nv, ATTN_TILE))

        flat = lambda a: a.reshape(t, -1)
        x1, qp = combine(xf, flat(y), flat(r), flat(k), flat(v), flat(g), flat(o), z, gate_lo * tn // (2 * d),
                         bd, rw_r_k[i].reshape(-1), rw_gn_w[i], rw_gn_b[i],
                         bf(rw_w_o[i]), bf(mla_w_o[i]), bf(w_out[i]), norm_ffn[i], bf(peer_w_q[i]), tm=256)
        lam, cc, rho, e1 = peer_route(qp, bf(peer_sub_keys[i]), tm=256)
        nrow = PEER_ROWS * N_KEYS
        vt = bf(peer_v[i]).reshape(-1, nrow, d).transpose(0, 2, 1)
        ffn = peer_expert(x1, norm_ffn[i], bf(peer_u[i]), vt, lam, cc, rho, e1, tm=512)

        xf = ple_final(x1, ffn, p[i].reshape(t, -1), norm_ple[i], norm_final,
                       bf(ple_w_gate[i]), bf(ple_w_proj[i]), tm=256)
    return xf.reshape(bsz, seq, d)
```

```python
import functools
import math

import jax
import jax.numpy as jnp
from jax import lax
from jax.experimental import pallas as pl
from jax.experimental.pallas import tpu as pltpu

F32 = jnp.float32
BF16 = jnp.bfloat16
HIGHEST = lax.Precision.HIGHEST

LANES = 128
SUBLANES = 8
VMEM_LIMIT = 56 * 1024 * 1024

EPS = 1e-6
RW_HEADS = 8
RW_HEAD_DIM = 64
RW_DIM = RW_HEADS * RW_HEAD_DIM
RW_GN_EPS = 64e-5
SCAN_CHUNK = 64
SEG_SUM_WIDTH = 256

MLA_HEADS = 8
QK_NOPE = 64
QK_ROPE = 32
V_HEAD = 64
Q_LORA = 384
KV_LORA = 256
ROPE_THETA = 10000.0
MASK_CHUNK = 64
NEG_INF = -1e30

PEER_HEADS = 8
N_KEYS = 128
PEER_TOPK = 16
HALF_Q = 128


def _cparams(*sem):
    return pltpu.CompilerParams(dimension_semantics=sem, vmem_limit_bytes=VMEM_LIMIT)


def _dot(a, b, precision=None):
    return jnp.dot(a, b, preferred_element_type=F32, precision=precision)


def _dot_nt(a, b, precision=None):
    return lax.dot_general(a, b, (((1,), (1,)), ((), ())),
                           preferred_element_type=F32, precision=precision)


def _rms(x, gain):
    return x * lax.rsqrt(jnp.mean(x * x, axis=-1, keepdims=True) + EPS) * gain


def _sigmoid(x):
    return 1.0 / (1.0 + jnp.exp(-x))


def _seg_sum(x, bd):
    hi = x.astype(BF16)
    lo = (x - hi.astype(F32)).astype(BF16)
    w = bd.shape[0]
    parts = [_dot(hi[:, c:c + w], bd) + _dot(lo[:, c:c + w], bd) for c in range(0, x.shape[1], w)]
    return jnp.concatenate(parts, axis=1)


def _norm_matmul_kernel(x_ref, g_ref, w_ref, o_ref, h_ref, *, sigmoid_blocks):
    j = pl.program_id(1)

    @pl.when(j == 0)
    def _():
        h_ref[...] = _rms(x_ref[...], g_ref[...]).astype(BF16)

    def project(act):
        o_ref[...] = act(_dot(h_ref[...], w_ref[...])).astype(o_ref.dtype)

    if sigmoid_blocks is None:
        project(lambda y: y)
    else:
        gated = jnp.logical_and(j >= sigmoid_blocks[0], j < sigmoid_blocks[1])
        pl.when(gated)(lambda: project(_sigmoid))
        pl.when(jnp.logical_not(gated))(lambda: project(lambda y: y))


def norm_matmul(x, gain, w, *, tm, tn, sigmoid_blocks=None, out_dtype=F32):
    t, d = x.shape
    n = w.shape[1]
    return pl.pallas_call(
        functools.partial(_norm_matmul_kernel, sigmoid_blocks=sigmoid_blocks),
        grid=(t // tm, n // tn),
        in_specs=[pl.BlockSpec((tm, d), lambda i, j: (i, 0)),
                  pl.BlockSpec((1, d), lambda i, j: (0, 0)),
                  pl.BlockSpec((d, tn), lambda i, j: (0, j))],
        out_specs=pl.BlockSpec((tm, tn), lambda i, j: (i, j)),
        out_shape=jax.ShapeDtypeStruct((t, n), out_dtype),
        scratch_shapes=[pltpu.VMEM((tm, d), BF16)],
        compiler_params=_cparams("parallel", "arbitrary"),
        name="norm_matmul",
    )(x, gain.reshape(1, d), w)


RW_ZCOLS = 3 * RW_DIM + 4 * LANES


def _rwkv_prep_kernel(z_ref, zp_ref, mu_ref, w0_ref, w2_ref, a0_ref, a2_ref, g2_ref,
                      kk_ref, ka_ref, bd_ref,
                      r_ref, wl_ref, k_ref, v_ref, a_ref, b_ref, g_ref):
    z = z_ref[0]
    tm = z.shape[0]
    prev_last = zp_ref[0][SUBLANES - 1:SUBLANES, :]
    prev_last = jnp.where(pl.program_id(1) == 0, 0.0, prev_last)
    rolled = pltpu.roll(z, 1, 0)
    row = lax.broadcasted_iota(jnp.int32, (tm, 1), 0)
    z_prev = jnp.where(row == 0, prev_last, rolled)
    z = z + mu_ref[...] * (z_prev - z)

    o1, o2, o3 = RW_DIM, 2 * RW_DIM, 3 * RW_DIM
    r, k, v = z[:, :o1], z[:, o1:o2], z[:, o2:o3]
    zw, za, zg = (z[:, o3 + c * LANES:o3 + (c + 1) * LANES] for c in range(3))

    wpre = w0_ref[...] + _dot(jnp.tanh(zw), w2_ref[...])
    nx = -wpre
    softplus = jnp.maximum(nx, 0.0) + jnp.log(1.0 + jnp.exp(-jnp.abs(nx)))
    w = -softplus - 0.5
    iclr = _sigmoid(a0_ref[...] + _dot(za, a2_ref[...]))
    g = _dot(_sigmoid(zg), g2_ref[...])

    kk = k * kk_ref[...]
    ss = _seg_sum(kk * kk, bd_ref[...])
    kk = kk / jnp.maximum(jnp.sqrt(ss), 1e-12)

    r_ref[0] = r
    wl_ref[0] = -jnp.exp(w)
    k_ref[0] = k * (1.0 + (iclr - 1.0) * ka_ref[...])
    v_ref[0] = v
    a_ref[0] = -kk
    b_ref[0] = kk * iclr
    g_ref[0] = g


def rwkv_prep(z, mu, w0, w2, a0, a2, g2, k_k, k_a, bd, *, tm):
    bsz, seq, _ = z.shape
    zc = RW_ZCOLS
    d = RW_DIM
    row = lambda a: a.reshape(1, -1)
    full = lambda a: pl.BlockSpec(a.shape, lambda b, i: (0,) * a.ndim)
    args = (row(mu), row(w0), w2, row(a0), a2, g2, row(k_k), row(k_a), bd)
    out = jax.ShapeDtypeStruct((bsz, seq, d), F32)
    ospec = pl.BlockSpec((1, tm, d), lambda b, i: (b, i, 0))
    return pl.pallas_call(
        _rwkv_prep_kernel,
        grid=(bsz, seq // tm),
        in_specs=[pl.BlockSpec((1, tm, zc), lambda b, i: (b, i, 0)),
                  pl.BlockSpec((1, SUBLANES, zc),
                               lambda b, i: (b, jnp.maximum(i * (tm // SUBLANES) - 1, 0), 0))]
                 + [full(a) for a in args],
        out_specs=[ospec] * 7,
        out_shape=[out] * 7,
        compiler_params=_cparams("parallel", "arbitrary"),
        name="rwkv_prep",
    )(z, z, *args)


def _rwkv_scan_kernel(r_ref, wl_ref, k_ref, v_ref, a_ref, b_ref, y_ref, g_ref):
    c = SCAN_CHUNK
    hd = RW_HEAD_DIM
    npair = g_ref.shape[0]

    @pl.when(pl.program_id(1) == 0)
    def _():
        g_ref[...] = jnp.zeros_like(g_ref)

    ri = lax.broadcasted_iota(jnp.int32, (c, c), 0)
    ci = lax.broadcasted_iota(jnp.int32, (c, c), 1)
    tril = (ri >= ci).astype(F32)
    lane = lax.broadcasted_iota(jnp.int32, (c, 2 * hd), 1)
    m0 = lane < hd
    r2 = lax.broadcasted_iota(jnp.int32, (2 * c, 2 * c), 0)
    c2 = lax.broadcasted_iota(jnp.int32, (2 * c, 2 * c), 1)
    same = (r2 >= c) == (c2 >= c)
    strict = jnp.logical_and(same, r2 > c2)
    incl = jnp.logical_and(same, r2 >= c2)
    eye = (r2 == c2).astype(F32)

    bf = lambda x: x.astype(BF16)
    stack = lambda x: bf(jnp.concatenate([jnp.where(m0, x, 0.0), jnp.where(m0, 0.0, x)], axis=0))
    twice = lambda x: bf(jnp.concatenate([x, x], axis=0))
    pick = lambda s: jnp.where(m0, s[:c], s[c:])

    pairs = range(npair)
    each = lambda f, *cols: [f(*args) for args in zip(*cols)]
    per_row = r_ref.shape[2] // LANES
    where = [(hp // per_row, slice((hp % per_row) * LANES, (hp % per_row + 1) * LANES)) for hp in pairs]
    load = lambda ref: [ref[bi, :, sl] for bi, sl in where]
    r, wl, k, v, a, b = (load(x) for x in (r_ref, wl_ref, k_ref, v_ref, a_ref, b_ref))

    cs = each(lambda w: _dot(tril, w, HIGHEST), wl)
    cs_last = each(lambda s: s[c - 1:c, :], cs)
    p_inv = each(lambda s: jnp.exp(-s), cs)
    at = each(lambda a, s, w: a * jnp.exp(s - w), a, cs, wl)
    rt = each(lambda r, s: r * jnp.exp(s), r, cs)
    at_s, rt_s = each(stack, at), each(stack, rt)
    bt_s = each(lambda b, p: stack(b * p), b, p_inv)
    kt_s = each(lambda k, p: stack(k * p), k, p_inv)

    ab = each(lambda x, y: jnp.where(strict, _dot_nt(x, y), 0.0), at_s, bt_s)
    ak = each(lambda x, y: jnp.where(strict, _dot_nt(x, y), 0.0), at_s, kt_s)
    rb = each(lambda x, y: jnp.where(incl, _dot_nt(x, y), 0.0), rt_s, bt_s)
    rk = each(lambda x, y: jnp.where(incl, _dot_nt(x, y), 0.0), rt_s, kt_s)

    tinv = each(lambda m: eye + m, ab)
    x = each(bf, ab)
    for _ in range(int(math.log2(c)) - 1):
        x = each(lambda m: bf(_dot(m, m)), x)
        tinv = each(lambda t, m: t + _dot(bf(t), m), tinv, x)

    gt = [g_ref[hp] for hp in pairs]
    gtb = each(bf, gt)
    vv = each(twice, v)
    rhs = each(lambda at, g, ak, vv: _dot_nt(bf(at), g) + pick(_dot(bf(ak), vv)), at, gtb, ak, vv)
    u = each(lambda t, x: pick(_dot(bf(t), twice(x))), tinv, rhs)
    y = each(lambda rt, g, rb, u, rk, vv:
             _dot_nt(bf(rt), g) + pick(_dot(bf(rb), twice(u)) + _dot(bf(rk), vv)),
             rt, gtb, rb, u, rk, vv)
    upd = each(lambda u, b, v, k, s, sl:
               _dot(bf(u.T), bf(b * jnp.exp(sl - s))) + _dot(bf(v.T), bf(k * jnp.exp(sl - s))),
               u, b, v, k, cs, cs_last)
    for hp in pairs:
        bi, sl = where[hp]
        y_ref[bi, :, sl] = y[hp]
        g_ref[hp] = gt[hp] * jnp.exp(cs_last[hp]) + jnp.where(same, upd[hp], 0.0)


def rwkv_scan(r, wl, k, v, a, b, *, rows):
    bsz, seq, d = r.shape
    c = SCAN_CHUNK
    spec = pl.BlockSpec((rows, c, d), lambda bi, ci: (bi, ci, 0))
    return pl.pallas_call(
        _rwkv_scan_kernel,
        grid=(bsz // rows, seq // c),
        in_specs=[spec] * 6,
        out_specs=spec,
        out_shape=jax.ShapeDtypeStruct((bsz, seq, d), F32),
        scratch_shapes=[pltpu.VMEM((rows * d // LANES, LANES, LANES), F32)],
        compiler_params=_cparams("parallel", "arbitrary"),
        name="rwkv_scan",
    )(r, wl, k, v, a, b)


MLA_ZCOLS = 1024
MLA_SCALE = math.log2(math.e) / math.sqrt(QK_NOPE + QK_ROPE)
ATTN_TILE = 256


def _mla_prep_kernel(z_ref, pos_ref, fl_ref, qn_ref, kvn_ref, wq_ref, wqs_ref, wk_ref, wv_ref,
                     q_ref, k_ref, v_ref):
    z = z_ref[...]
    c_q = _rms(z[:, :Q_LORA], qn_ref[...]).astype(BF16)
    c_kv = _rms(z[:, Q_LORA:Q_LORA + KV_LORA], kvn_ref[...]).astype(BF16)
    kr = z[:, Q_LORA + KV_LORA:Q_LORA + KV_LORA + LANES]
    krs = z[:, Q_LORA + KV_LORA + LANES:Q_LORA + KV_LORA + 2 * LANES]

    ang = pos_ref[...].astype(F32) * fl_ref[...]
    cos, sin = jnp.cos(ang), jnp.sin(ang)
    kr_rot = kr * cos + krs * sin

    q = _dot(c_q, wq_ref[...])
    qs = _dot(c_q, wqs_ref[...])
    kn = _dot(c_kv, wk_ref[...])
    v = _dot(c_kv, wv_ref[...])
    for c in range(v_ref.shape[0]):
        v_ref[c] = v[c * ATTN_TILE:(c + 1) * ATTN_TILE, :].T.astype(BF16)
    for h in range(MLA_HEADS):
        sl = slice(h * LANES, (h + 1) * LANES)
        q_ref[:, sl] = ((q[:, sl] * cos + qs[:, sl] * sin) * MLA_SCALE).astype(BF16)
        k_ref[:, sl] = (kn[:, sl] + kr_rot).astype(BF16)


def mla_prep(z, zblock, pos, f_lane, q_norm, kv_norm, wq, wqs, wk, wv, *, tm):
    t = z.shape[0]
    n = MLA_HEADS * LANES
    full = lambda a: pl.BlockSpec(a.shape, lambda i: (0,) * a.ndim)
    args = (f_lane, q_norm.reshape(1, -1), kv_norm.reshape(1, -1), wq, wqs, wk, wv)
    out = jax.ShapeDtypeStruct((t, n), BF16)
    nv = wv.shape[1]
    out_vt = jax.ShapeDtypeStruct((t // ATTN_TILE, nv, ATTN_TILE), BF16)
    ospec = pl.BlockSpec((tm, n), lambda i: (i, 0))
    vspec = pl.BlockSpec((tm // ATTN_TILE, nv, ATTN_TILE), lambda i: (i, 0, 0))
    return pl.pallas_call(
        _mla_prep_kernel,
        grid=(t // tm,),
        in_specs=[pl.BlockSpec((tm, MLA_ZCOLS), lambda i: (i, zblock)),
                  pl.BlockSpec((tm, 1), lambda i: (i, 0))] + [full(a) for a in args],
        out_specs=[ospec, ospec, vspec],
        out_shape=[out, out, out_vt],
        compiler_params=_cparams("parallel"),
        name="mla_prep",
    )(z, pos, *args)


ATTN_HEADS_PER_STEP = 8


def _attn_kernel(q_ref, k_ref, vt_ref, o_ref):
    tq = ATTN_TILE
    iq = pl.program_id(2)
    nh = q_ref.shape[2] // LANES
    heads = [slice(h * LANES, (h + 1) * LANES) for h in range(nh)]
    qs = [q_ref[0, :, sl] for sl in heads]

    def tile(j, carry, masked):
        start = pl.multiple_of(j * tq, tq)
        if masked:
            kc = lax.broadcasted_iota(jnp.int32, (tq, tq), 0) // MASK_CHUNK
            qc = lax.broadcasted_iota(jnp.int32, (tq, tq), 1) // MASK_CHUNK
            keep = kc <= qc
        hs = range(nh)
        s = [_dot_nt(k_ref[0, pl.ds(start, tq), heads[h]], qs[h]) for h in hs]
        if masked:
            s = [jnp.where(keep, x, NEG_INF) for x in s]
        m_new = [jnp.maximum(carry[h][0], jnp.max(s[h], axis=0, keepdims=True)) for h in hs]
        alpha = [jnp.exp2(carry[h][0] - m_new[h]) for h in hs]
        p = [jnp.exp2(s[h] - m_new[h]) for h in hs]
        l = [alpha[h] * carry[h][1] + jnp.sum(p[h], axis=0, keepdims=True) for h in hs]
        pv = [_dot(vt_ref[0, j, h * V_HEAD:(h + 1) * V_HEAD, :], p[h].astype(BF16))
              for h in hs]
        return tuple((m_new[h], l[h], alpha[h] * carry[h][2] + pv[h]) for h in hs)

    init = tuple((jnp.full((1, tq), NEG_INF, F32), jnp.zeros((1, tq), F32), jnp.zeros((V_HEAD, tq), F32))
                 for _ in heads)
    carry = lax.fori_loop(0, iq, lambda j, c: tile(j, c, False), init)
    carry = tile(iq, carry, True)
    out = [acc / l for _, l, acc in carry]
    for pair in range(nh // 2):
        both = jnp.concatenate(out[2 * pair:2 * pair + 2], axis=0)
        o_ref[0, :, pair * LANES:(pair + 1) * LANES] = both.T.astype(BF16)


def attention(q, k, vt):
    bsz, seq, n = q.shape
    tq = ATTN_TILE
    w = ATTN_HEADS_PER_STEP * LANES
    wv = ATTN_HEADS_PER_STEP * V_HEAD
    qspec = pl.BlockSpec((1, tq, w), lambda b, h, i: (b, i, h))
    kspec = pl.BlockSpec((1, seq, w), lambda b, h, i: (b, 0, h))
    vspec = pl.BlockSpec((1, seq // tq, wv, tq), lambda b, h, i: (b, 0, h, 0))
    return pl.pallas_call(
        _attn_kernel,
        grid=(bsz, n // w, seq // tq),
        in_specs=[qspec, kspec, vspec],
        out_specs=pl.BlockSpec((1, tq, wv), lambda b, h, i: (b, i, h)),
        out_shape=jax.ShapeDtypeStruct((bsz, seq, vt.shape[2]), BF16),
        compiler_params=_cparams("parallel", "parallel", "arbitrary"),
        name="attention",
    )(q, k, vt)


def _combine_kernel(x_ref, y_ref, r_ref, k_ref, v_ref, g_ref, o_ref, gates_ref,
                    bd_ref, rk_ref, gnw_ref, gnb_ref, wa_ref, wb_ref, wo_ref, gq_ref, wq_ref,
                    out_ref, qp_ref):
    bd = bd_ref[...]
    inv_n = 1.0 / RW_HEAD_DIM
    y = y_ref[...]
    v = v_ref[...]
    mean = _seg_sum(y, bd) * inv_n
    yc = y - mean
    var = _seg_sum(yc * yc, bd) * inv_n
    yn = yc * lax.rsqrt(var + RW_GN_EPS) * gnw_ref[...] + gnb_ref[...]
    bonus = _seg_sum(r_ref[...] * k_ref[...] * rk_ref[...], bd) * v
    ya = _dot(((yn + bonus) * g_ref[...]).astype(BF16), wa_ref[...])
    yb = _dot(o_ref[...], wb_ref[...])
    d = ya.shape[1]
    gates = gates_ref[...]
    mix = gates[:, :d] * ya + gates[:, d:] * yb
    x1 = x_ref[...] + _dot(mix.astype(BF16), wo_ref[...])
    out_ref[...] = x1
    qp_ref[...] = _dot(_rms(x1, gq_ref[...]).astype(BF16), wq_ref[...]).astype(BF16)


def combine(x, y, r, k, v, g, o, z, gate_block, bd, r_k, gn_w, gn_b, wa, wb, wo, g_ffn, w_q, *, tm):
    t, d = x.shape
    row = lambda a: a.reshape(1, -1)
    tok = lambda a: pl.BlockSpec((tm, a.shape[1]), lambda i: (i, 0))
    full = lambda a: pl.BlockSpec(a.shape, lambda i: (0,) * a.ndim)
    toks = (x, y, r, k, v, g, o, z)
    consts = (bd, row(r_k), row(gn_w), row(gn_b), wa, wb, wo, row(g_ffn), w_q)
    gate_spec = pl.BlockSpec((tm, 2 * d), lambda i: (i, gate_block))
    nq = w_q.shape[1]
    return pl.pallas_call(
        _combine_kernel,
        grid=(t // tm,),
        in_specs=[tok(a) for a in toks[:-1]] + [gate_spec] + [full(a) for a in consts],
        out_specs=[pl.BlockSpec((tm, d), lambda i: (i, 0)), pl.BlockSpec((tm, nq), lambda i: (i, 0))],
        out_shape=[jax.ShapeDtypeStruct((t, d), F32), jax.ShapeDtypeStruct((t, nq), BF16)],
        compiler_params=_cparams("parallel"),
        name="combine",
    )(*toks, *consts)


NOT_RANKED = 127.0
RANK_BLOCKS_PER_LOOP = 2
STAIR = tuple(PEER_TOPK // (ii + 1) for ii in range(PEER_TOPK))


def _top_ranks(s, k, exact_ties):
    n, t = s.shape
    key = lax.broadcasted_iota(jnp.int32, (n, LANES), 0)
    slot = lax.broadcasted_iota(jnp.int32, (k, LANES), 0)

    def one(r, s, rank, top):
        m = jnp.max(s, axis=0, keepdims=True)
        hit = s == m
        if exact_ties:
            hit = key == jnp.min(jnp.where(hit, key, n), axis=0, keepdims=True)
        rank = jnp.where(hit, jnp.asarray(r, F32), rank)
        s = jnp.where(hit, -jnp.inf, s)
        top = jnp.where(slot == r, m, top)
        return s, rank, top

    def body(r, carry):
        return tuple(one(r, *c) for c in carry)

    blocks = [s[:, c * LANES:(c + 1) * LANES] for c in range(t // LANES)]
    tops, ranks = [], []
    for g in range(0, len(blocks), RANK_BLOCKS_PER_LOOP):
        init = tuple((b, jnp.full((n, LANES), NOT_RANKED, F32), jnp.zeros((k, LANES), F32))
                     for b in blocks[g:g + RANK_BLOCKS_PER_LOOP])
        for _, rank, top in lax.fori_loop(0, k, body, init):
            tops.append(top)
            ranks.append(rank)
    return jnp.concatenate(tops, axis=1), jnp.concatenate(ranks, axis=1)


def _ranked_excess(rank, k):
    count = jnp.sum((rank < NOT_RANKED).astype(F32), axis=0, keepdims=True)
    return jnp.abs(count - k)


def _peer_route_kernel(q_ref, keys_ref, lam_ref, cc_ref, rho_ref, e1_ref):
    tm = q_ref.shape[0]
    k = PEER_TOPK
    neg = -jnp.inf

    def head(h, _):
        col = pl.multiple_of(h * 2 * HALF_Q, 2 * HALF_Q)
        s0 = _dot_nt(keys_ref[h, 0], q_ref[:, pl.ds(col, HALF_Q)])
        s1 = _dot_nt(keys_ref[h, 1], q_ref[:, pl.ds(col + HALF_Q, HALF_Q)])
        def rank_all(exact_ties):
            top0, rank0 = _top_ranks(s0, k, exact_ties)
            top1, rank1 = _top_ranks(s1, k, exact_ties)
            row8 = lax.broadcasted_iota(jnp.int32, (SUBLANES, tm), 0)
            groups = [top0[0:1] + top1[0:8], top0[0:1] + top1[8:16], top0[1:2] + top1[0:8]]
            for ii in range(2, 8):
                groups.append(jnp.where(row8 < STAIR[ii], top0[ii:ii + 1] + top1[0:8], neg))
            groups.append(top0[8:16] + top1[0:1])
            cand = jnp.concatenate(groups, axis=0)
            _, crank = _top_ranks(cand, k, exact_ties)
            return top0, rank0, top1, rank1, cand, crank

        quick = rank_all(False)
        excess = _ranked_excess(quick[1], k) + _ranked_excess(quick[3], k) + _ranked_excess(quick[5], k)
        top0, rank0, top1, rank1, cand, crank = lax.cond(
            jnp.max(excess) == 0.0, lambda: quick, lambda: rank_all(True))
        sel = crank < NOT_RANKED
        ex = jnp.where(sel, jnp.exp(cand - cand[0:1]), 0.0)
        z = jnp.sum(ex, axis=0, keepdims=True)
        self = sel.astype(F32)
        counts = [jnp.sum(self[0:16], axis=0, keepdims=True)]
        for g in range(2, 9):
            counts.append(jnp.sum(self[8 * g:8 * g + 8], axis=0, keepdims=True))
        lvec = jnp.concatenate(counts + [self[72:80]], axis=0)

        lam = jnp.zeros((N_KEYS, tm), F32)
        for ii in range(k):
            lam = jnp.where(rank0 == float(ii), lvec[ii:ii + 1], lam)
        lam_ref[h] = lam
        cc_ref[h] = jnp.exp(s0 - top0[0:1]) * (0.5 / z)
        rho_ref[h] = rank1.astype(BF16)
        e1_ref[h] = jnp.exp(s1 - top1[0:1]).astype(BF16)
        return 0

    lax.fori_loop(0, PEER_HEADS, head, 0)


def peer_route(qp, keys, *, tm):
    t = qp.shape[0]
    out = jax.ShapeDtypeStruct((PEER_HEADS, N_KEYS, t), F32)
    out_b = jax.ShapeDtypeStruct((PEER_HEADS, N_KEYS, t), BF16)
    ospec = pl.BlockSpec((PEER_HEADS, N_KEYS, tm), lambda i: (0, 0, i))
    return pl.pallas_call(
        _peer_route_kernel,
        grid=(t // tm,),
        in_specs=[pl.BlockSpec((tm, qp.shape[1]), lambda i: (i, 0)),
                  pl.BlockSpec(keys.shape, lambda i: (0, 0, 0, 0))],
        out_specs=[ospec] * 4,
        out_shape=[out, out, out_b, out_b],
        compiler_params=_cparams("parallel"),
        name="peer_route",
    )(qp, keys)


PEER_ROWS = 8
PEER_TOKEN_CHUNK = 256


def _erf(x):
    return lax.erf(x)


def _gelu_twice(x):
    return x * (1.0 + _erf(x * (1.0 / math.sqrt(2.0))))


def _peer_expert_kernel(x_ref, gn_ref, u_ref, vt_ref, lam_ref, cc_ref, rho_ref, e1_ref, out_ref,
                        xt_ref, acc_ref, pre0_ref, pre1_ref):
    j = pl.program_id(1)
    tm = xt_ref.shape[1]
    chunks = [slice(c, c + PEER_TOKEN_CHUNK) for c in range(0, tm, PEER_TOKEN_CHUNK)]

    @pl.when(j == 0)
    def _():
        xt_ref[...] = _rms(x_ref[...], gn_ref[...]).T.astype(BF16)
        acc_ref[...] = jnp.zeros_like(acc_ref)
        pre1_ref[...] = jnp.zeros_like(pre1_ref)

    def step(fill_ref, drain_ref):
        for cols in chunks:
            fill_ref[:, cols] = _dot(u_ref[...], xt_ref[:, cols])
        for cols in chunks:
            gs = []
            for ii in range(PEER_ROWS):
                rows = slice(ii * N_KEYS, (ii + 1) * N_KEYS)
                terms = []
                for h in range(PEER_HEADS):
                    lam = lam_ref[h, ii:ii + 1, cols].astype(BF16)
                    cc = cc_ref[h, ii:ii + 1, cols].astype(BF16)
                    terms.append(jnp.where(rho_ref[h, :, cols] < lam, e1_ref[h, :, cols] * cc,
                                           jnp.zeros((), BF16)))
                gs.append(_gelu_twice(drain_ref[rows, cols]).astype(BF16) * sum(terms[1:], terms[0]))
            acc_ref[:, cols] += _dot(vt_ref[0], jnp.concatenate(gs, axis=0))

    @pl.when(j % 2 == 0)
    def _():
        step(pre0_ref, pre1_ref)

    @pl.when(j % 2 == 1)
    def _():
        step(pre1_ref, pre0_ref)

    @pl.when(j == pl.num_programs(1) - 1)
    def _():
        out_ref[...] = acc_ref[...].T


def peer_expert(x, gain, u, vt, lam, cc, rho, e1, *, tm):
    t, d = x.shape
    nrow = PEER_ROWS * N_KEYS
    nblk = u.shape[0] // nrow
    stage = lambda lag: (lambda j: jnp.clip(j - lag, 0, nblk - 1))
    rspec = pl.BlockSpec((PEER_HEADS, PEER_ROWS, tm), lambda i, j: (0, stage(1)(j), i))
    cspec = pl.BlockSpec((PEER_HEADS, N_KEYS, tm), lambda i, j: (0, 0, i))
    return pl.pallas_call(
        _peer_expert_kernel,
        grid=(t // tm, nblk + 1),
        in_specs=[pl.BlockSpec((tm, d), lambda i, j: (i, 0)),
                  pl.BlockSpec((1, d), lambda i, j: (0, 0)),
                  pl.BlockSpec((nrow, d), lambda i, j: (stage(0)(j), 0)),
                  pl.BlockSpec((1, d, nrow), lambda i, j: (stage(1)(j), 0, 0)),
                  rspec, rspec, cspec, cspec],
        out_specs=pl.BlockSpec((tm, d), lambda i, j: (i, 0)),
        out_shape=jax.ShapeDtypeStruct((t, d), F32),
        scratch_shapes=[pltpu.VMEM((d, tm), BF16), pltpu.VMEM((d, tm), F32),
                        pltpu.VMEM((nrow, tm), F32), pltpu.VMEM((nrow, tm), F32)],
        compiler_params=_cparams("parallel", "arbitrary"),
        name="peer_expert",
    )(x, gain.reshape(1, d), u, vt, lam, cc, rho, e1)


def _ple_final_kernel(x_ref, f_ref, p_ref, gp_ref, gf_ref, wg_ref, wp_ref, out_ref):
    x = x_ref[...] + f_ref[...]
    gate = _sigmoid(_dot(_rms(x, gp_ref[...]).astype(BF16), wg_ref[...]))
    x = x + gate * _dot(p_ref[...].astype(BF16), wp_ref[...])
    out_ref[...] = _rms(x, gf_ref[...])


def ple_final(x, f, p, g_ple, g_final, wg, wp, *, tm):
    t, d = x.shape
    tok = lambda a: pl.BlockSpec((tm, a.shape[1]), lambda i: (i, 0))
    full = lambda a: pl.BlockSpec(a.shape, lambda i: (0,) * a.ndim)
    consts = (g_ple.reshape(1, d), g_final.reshape(1, d), wg, wp)
    return pl.pallas_call(
        _ple_final_kernel,
        grid=(t // tm,),
        in_specs=[tok(x), tok(f), tok(p)] + [full(a) for a in consts],
        out_specs=tok(x),
        out_shape=jax.ShapeDtypeStruct((t, d), F32),
        compiler_params=_cparams("parallel"),
        name="ple_final",
    )(x, f, p, *consts)


def _place(cols, width, offset):
    return jnp.pad(cols, ((0, 0), (offset, width - offset - cols.shape[1])))


def _rw_in_weights(w_rw, mu):
    o3 = 3 * RW_DIM
    lw, la = 64, 64
    segs = [w_rw[:, :o3], _place(w_rw[:, o3:o3 + lw], LANES, 0),
            _place(w_rw[:, o3 + lw:o3 + lw + la], LANES, 0), _place(w_rw[:, o3 + lw + la:], 2 * LANES, 0)]
    mus = [mu[None, :o3], _place(mu[None, o3:o3 + lw], LANES, 0),
           _place(mu[None, o3 + lw:o3 + lw + la], LANES, 0), _place(mu[None, o3 + lw + la:], 2 * LANES, 0)]
    return jnp.concatenate(segs, axis=1), jnp.concatenate(mus, axis=1)[0]


def _mla_in_weights(w_mla):
    half = QK_ROPE // 2
    lat = Q_LORA + KV_LORA
    kr = w_mla[:, lat:]
    kr_sw = jnp.concatenate([-kr[:, half:], kr[:, :half]], axis=1)
    return jnp.concatenate([w_mla[:, :lat], _place(kr, LANES, QK_NOPE), _place(kr_sw, LANES, QK_NOPE),
                            jnp.zeros((w_mla.shape[0], LANES), w_mla.dtype)], axis=1)


def _mla_up_weights(w_uq, w_ukv):
    half = QK_ROPE // 2
    qd = QK_NOPE + QK_ROPE
    wq = w_uq.reshape(Q_LORA, MLA_HEADS, qd)
    rope = wq[:, :, QK_NOPE:]
    rope_sw = jnp.concatenate([-rope[:, :, half:], rope[:, :, :half]], axis=2)
    pad = lambda t, off: jnp.pad(t, ((0, 0), (0, 0), (off, LANES - off - t.shape[2])))
    wq_pad = pad(wq, 0).reshape(Q_LORA, MLA_HEADS * LANES)
    wq_sw = pad(rope_sw, QK_NOPE).reshape(Q_LORA, MLA_HEADS * LANES)
    lane = jnp.arange(MLA_HEADS * LANES) % LANES
    wk = jnp.where(lane < QK_NOPE, w_ukv, 0.0)
    wv = w_ukv.reshape(KV_LORA, MLA_HEADS, LANES)[:, :, QK_NOPE:].reshape(KV_LORA, MLA_HEADS * V_HEAD)
    return wq_pad, wq_sw, wk, wv


def kernel(x, p, positions, norm_mix, w_in, rw_mu, rw_w0, rw_w2, rw_a0, rw_a2, rw_g2, rw_k_k, rw_k_a, rw_r_k, rw_gn_w, rw_gn_b, rw_w_o, mla_q_norm, mla_w_uq, mla_kv_norm, mla_w_ukv, mla_w_o, w_out, norm_ffn, peer_w_q, peer_sub_keys, peer_u, peer_v, norm_ple, ple_w_gate, ple_w_proj, norm_final):
    bsz, seq, d = x.shape
    t = bsz * seq
    depth = p.shape[0]
    bf = lambda a: a.astype(BF16)
    rw_cols = 3 * RW_DIM + 64 + 64 + 128
    mla_cols = Q_LORA + KV_LORA + QK_ROPE

    head_of = jnp.arange(SEG_SUM_WIDTH) // RW_HEAD_DIM
    bd = bf(head_of[:, None] == head_of[None, :])
    inv_freq = ROPE_THETA ** (-jnp.arange(0, QK_ROPE, 2, dtype=F32) / QK_ROPE)
    f_lane = _place(jnp.concatenate([inv_freq, inv_freq])[None, :], LANES, QK_NOPE)
    pos = positions.reshape(t, 1)

    xf = x.reshape(t, d)
    assert depth == 1, "the final RMSNorm is fused into the layer's last kernel"
    for i in range(depth):
        w_rw, mu = _rw_in_weights(w_in[i][:, :rw_cols], rw_mu[i])
        w_mla = _mla_in_weights(w_in[i][:, rw_cols:rw_cols + mla_cols])
        w_gates = w_in[i][:, rw_cols + mla_cols:]
        tn = MLA_ZCOLS
        w_all = jnp.concatenate([w_rw, w_gates, w_mla], axis=1)
        gate_lo = RW_ZCOLS // tn
        gate_hi = gate_lo + w_gates.shape[1] // tn
        z = norm_matmul(xf, norm_mix[i], bf(w_all), tm=1024, tn=tn, sigmoid_blocks=(gate_lo, gate_hi))

        pad_rows = lambda w: jnp.pad(w, ((0, LANES - w.shape[0]), (0, 0)))
        r, wl, k, v, a, b, g = rwkv_prep(
            z.reshape(bsz, seq, -1), mu, rw_w0[i], pad_rows(rw_w2[i]), rw_a0[i],
            pad_rows(rw_a2[i]), rw_g2[i], rw_k_k[i], rw_k_a[i], bd, tm=256)
        y = rwkv_scan(r, wl, k, v, a, b, rows=math.gcd(bsz, 4))

        wq, wqs, wk, wv = _mla_up_weights(mla_w_uq[i], mla_w_ukv[i])
        q, kk, vv = mla_prep(z, gate_hi, pos, f_lane, mla_q_norm[i], mla_kv_norm[i],
                             bf(wq), bf(wqs), bf(wk), bf(wv), tm=512)
        n = MLA_HEADS * LANES
        nv = MLA_HEADS * V_HEAD
        o = attention(q.reshape(bsz, seq, n), kk.reshape(bsz, seq, n),
                      vv.reshape(bsz, seq // ATTN_TILE, nv, ATTN_TILE))

        flat = lambda a: a.reshape(t, -1)
        x1, qp = combine(xf, flat(y), flat(r), flat(k), flat(v), flat(g), flat(o), z, gate_lo * tn // (2 * d),
                         bd, rw_r_k[i].reshape(-1), rw_gn_w[i], rw_gn_b[i],
                         bf(rw_w_o[i]), bf(mla_w_o[i]), bf(w_out[i]), norm_ffn[i], bf(peer_w_q[i]), tm=256)
        lam, cc, rho, e1 = peer_route(qp, bf(peer_sub_keys[i]), tm=256)
        nrow = PEER_ROWS * N_KEYS
        vt = bf(peer_v[i]).reshape(-1, nrow, d).transpose(0, 2, 1)
        ffn = peer_expert(x1, norm_ffn[i], bf(peer_u[i]), vt, lam, cc, rho, e1, tm=512)

        xf = ple_final(x1, ffn, p[i].reshape(t, -1), norm_ple[i], norm_final,
                       bf(ple_w_gate[i]), bf(ple_w_proj[i]), tm=256)
    return xf.reshape(bsz, seq, d)
```

```python
import functools
import math

import jax
import jax.numpy as jnp
from jax import lax
from jax.experimental import pallas as pl
from jax.experimental.pallas import tpu as pltpu

F32 = jnp.float32
BF16 = jnp.bfloat16
HIGHEST = lax.Precision.HIGHEST

LANES = 128
SUBLANES = 8
VMEM_LIMIT = 56 * 1024 * 1024

EPS = 1e-6
RW_HEADS = 8
RW_HEAD_DIM = 64
RW_DIM = RW_HEADS * RW_HEAD_DIM
RW_GN_EPS = 64e-5
SCAN_CHUNK = 64
SEG_SUM_WIDTH = 256

MLA_HEADS = 8
QK_NOPE = 64
QK_ROPE = 32
V_HEAD = 64
Q_LORA = 384
KV_LORA = 256
ROPE_THETA = 10000.0
MASK_CHUNK = 64
NEG_INF = -1e30

PEER_HEADS = 8
N_KEYS = 128
PEER_TOPK = 16
HALF_Q = 128


def _cparams(*sem):
    return pltpu.CompilerParams(dimension_semantics=sem, vmem_limit_bytes=VMEM_LIMIT)


def _dot(a, b, precision=None):
    return jnp.dot(a, b, preferred_element_type=F32, precision=precision)


def _dot_nt(a, b, precision=None):
    return lax.dot_general(a, b, (((1,), (1,)), ((), ())),
                           preferred_element_type=F32, precision=precision)


def _rms(x, gain):
    return x * lax.rsqrt(jnp.mean(x * x, axis=-1, keepdims=True) + EPS) * gain


def _sigmoid(x):
    return 1.0 / (1.0 + jnp.exp(-x))


def _seg_sum(x, bd):
    hi = x.astype(BF16)
    lo = (x - hi.astype(F32)).astype(BF16)
    w = bd.shape[0]
    parts = [_dot(hi[:, c:c + w], bd) + _dot(lo[:, c:c + w], bd) for c in range(0, x.shape[1], w)]
    return jnp.concatenate(parts, axis=1)


def _norm_matmul_kernel(x_ref, g_ref, w_ref, o_ref, h_ref, *, sigmoid_blocks):
    j = pl.program_id(1)

    @pl.when(j == 0)
    def _():
        h_ref[...] = _rms(x_ref[...], g_ref[...]).astype(BF16)

    def project(act):
        o_ref[...] = act(_dot(h_ref[...], w_ref[...])).astype(o_ref.dtype)

    if sigmoid_blocks is None:
        project(lambda y: y)
    else:
        gated = jnp.logical_and(j >= sigmoid_blocks[0], j < sigmoid_blocks[1])
        pl.when(gated)(lambda: project(_sigmoid))
        pl.when(jnp.logical_not(gated))(lambda: project(lambda y: y))


def norm_matmul(x, gain, w, *, tm, tn, sigmoid_blocks=None, out_dtype=F32):
    t, d = x.shape
    n = w.shape[1]
    return pl.pallas_call(
        functools.partial(_norm_matmul_kernel, sigmoid_blocks=sigmoid_blocks),
        grid=(t // tm, n // tn),
        in_specs=[pl.BlockSpec((tm, d), lambda i, j: (i, 0)),
                  pl.BlockSpec((1, d), lambda i, j: (0, 0)),
                  pl.BlockSpec((d, tn), lambda i, j: (0, j))],
        out_specs=pl.BlockSpec((tm, tn), lambda i, j: (i, j)),
        out_shape=jax.ShapeDtypeStruct((t, n), out_dtype),
        scratch_shapes=[pltpu.VMEM((tm, d), BF16)],
        compiler_params=_cparams("parallel", "arbitrary"),
        name="norm_matmul",
    )(x, gain.reshape(1, d), w)


RW_ZCOLS = 3 * RW_DIM + 4 * LANES


def _rwkv_prep_kernel(z_ref, zp_ref, mu_ref, w0_ref, w2_ref, a0_ref, a2_ref, g2_ref,
                      kk_ref, ka_ref, bd_ref,
                      r_ref, wl_ref, k_ref, v_ref, a_ref, b_ref, g_ref):
    z = z_ref[0]
    tm = z.shape[0]
    prev_last = zp_ref[0][SUBLANES - 1:SUBLANES, :]
    prev_last = jnp.where(pl.program_id(1) == 0, 0.0, prev_last)
    rolled = pltpu.roll(z, 1, 0)
    row = lax.broadcasted_iota(jnp.int32, (tm, 1), 0)
    z_prev = jnp.where(row == 0, prev_last, rolled)
    z = z + mu_ref[...] * (z_prev - z)

    o1, o2, o3 = RW_DIM, 2 * RW_DIM, 3 * RW_DIM
    r, k, v = z[:, :o1], z[:, o1:o2], z[:, o2:o3]
    zw, za, zg = (z[:, o3 + c * LANES:o3 + (c + 1) * LANES] for c in range(3))

    wpre = w0_ref[...] + _dot(jnp.tanh(zw), w2_ref[...])
    nx = -wpre
    softplus = jnp.maximum(nx, 0.0) + jnp.log(1.0 + jnp.exp(-jnp.abs(nx)))
    w = -softplus - 0.5
    iclr = _sigmoid(a0_ref[...] + _dot(za, a2_ref[...]))
    g = _dot(_sigmoid(zg), g2_ref[...])

    kk = k * kk_ref[...]
    ss = _seg_sum(kk * kk, bd_ref[...])
    kk = kk / jnp.maximum(jnp.sqrt(ss), 1e-12)

    r_ref[0] = r
    wl_ref[0] = -jnp.exp(w)
    k_ref[0] = k * (1.0 + (iclr - 1.0) * ka_ref[...])
    v_ref[0] = v
    a_ref[0] = -kk
    b_ref[0] = kk * iclr
    g_ref[0] = g


def rwkv_prep(z, mu, w0, w2, a0, a2, g2, k_k, k_a, bd, *, tm):
    bsz, seq, _ = z.shape
    zc = RW_ZCOLS
    d = RW_DIM
    row = lambda a: a.reshape(1, -1)
    full = lambda a: pl.BlockSpec(a.shape, lambda b, i: (0,) * a.ndim)
    args = (row(mu), row(w0), w2, row(a0), a2, g2, row(k_k), row(k_a), bd)
    out = jax.ShapeDtypeStruct((bsz, seq, d), F32)
    ospec = pl.BlockSpec((1, tm, d), lambda b, i: (b, i, 0))
    return pl.pallas_call(
        _rwkv_prep_kernel,
        grid=(bsz, seq // tm),
        in_specs=[pl.BlockSpec((1, tm, zc), lambda b, i: (b, i, 0)),
                  pl.BlockSpec((1, SUBLANES, zc),
                               lambda b, i: (b, jnp.maximum(i * (tm // SUBLANES) - 1, 0), 0))]
                 + [full(a) for a in args],
        out_specs=[ospec] * 7,
        out_shape=[out] * 7,
        compiler_params=_cparams("parallel", "arbitrary"),
        name="rwkv_prep",
    )(z, z, *args)


def _rwkv_scan_kernel(r_ref, wl_ref, k_ref, v_ref, a_ref, b_ref, y_ref, g_ref):
    c = SCAN_CHUNK
    hd = RW_HEAD_DIM
    npair = g_ref.shape[0]

    @pl.when(pl.program_id(1) == 0)
    def _():
        g_ref[...] = jnp.zeros_like(g_ref)

    ri = lax.broadcasted_iota(jnp.int32, (c, c), 0)
    ci = lax.broadcasted_iota(jnp.int32, (c, c), 1)
    tril = (ri >= ci).astype(F32)
    lane = lax.broadcasted_iota(jnp.int32, (c, 2 * hd), 1)
    m0 = lane < hd
    r2 = lax.broadcasted_iota(jnp.int32, (2 * c, 2 * c), 0)
    c2 = lax.broadcasted_iota(jnp.int32, (2 * c, 2 * c), 1)
    same = (r2 >= c) == (c2 >= c)
    strict = jnp.logical_and(same, r2 > c2)
    incl = jnp.logical_and(same, r2 >= c2)
    eye = (r2 == c2).astype(F32)

    bf = lambda x: x.astype(BF16)
    stack = lambda x: bf(jnp.concatenate([jnp.where(m0, x, 0.0), jnp.where(m0, 0.0, x)], axis=0))
    twice = lambda x: bf(jnp.concatenate([x, x], axis=0))
    pick = lambda s: jnp.where(m0, s[:c], s[c:])

    pairs = range(npair)
    each = lambda f, *cols: [f(*args) for args in zip(*cols)]
    per_row = r_ref.shape[2] // LANES
    where = [(hp // per_row, slice((hp % per_row) * LANES, (hp % per_row + 1) * LANES)) for hp in pairs]
    load = lambda ref: [ref[bi, :, sl] for bi, sl in where]
    r, wl, k, v, a, b = (load(x) for x in (r_ref, wl_ref, k_ref, v_ref, a_ref, b_ref))

    cs = each(lambda w: _dot(tril, w, HIGHEST), wl)
    cs_last = each(lambda s: s[c - 1:c, :], cs)
    p_inv = each(lambda s: jnp.exp(-s), cs)
    at = each(lambda a, s, w: a * jnp.exp(s - w), a, cs, wl)
    rt = each(lambda r, s: r * jnp.exp(s), r, cs)
    at_s, rt_s = each(stack, at), each(stack, rt)
    bt_s = each(lambda b, p: stack(b * p), b, p_inv)
    kt_s = each(lambda k, p: stack(k * p), k, p_inv)

    ab = each(lambda x, y: jnp.where(strict, _dot_nt(x, y), 0.0), at_s, bt_s)
    ak = each(lambda x, y: jnp.where(strict, _dot_nt(x, y), 0.0), at_s, kt_s)
    rb = each(lambda x, y: jnp.where(incl, _dot_nt(x, y), 0.0), rt_s, bt_s)
    rk = each(lambda x, y: jnp.where(incl, _dot_nt(x, y), 0.0), rt_s, kt_s)

    tinv = each(lambda m: eye + m, ab)
    x = each(bf, ab)
    for _ in range(int(math.log2(c)) - 1):
        x = each(lambda m: bf(_dot(m, m)), x)
        tinv = each(lambda t, m: t + _dot(bf(t), m), tinv, x)

    gt = [g_ref[hp] for hp in pairs]
    gtb = each(bf, gt)
    vv = each(twice, v)
    rhs = each(lambda at, g, ak, vv: _dot_nt(bf(at), g) + pick(_dot(bf(ak), vv)), at, gtb, ak, vv)
    u = each(lambda t, x: pick(_dot(bf(t), twice(x))), tinv, rhs)
    y = each(lambda rt, g, rb, u, rk, vv:
             _dot_nt(bf(rt), g) + pick(_dot(bf(rb), twice(u)) + _dot(bf(rk), vv)),
             rt, gtb, rb, u, rk, vv)
    upd = each(lambda u, b, v, k, s, sl:
               _dot(bf(u.T), bf(b * jnp.exp(sl - s))) + _dot(bf(v.T), bf(k * jnp.exp(sl - s))),
               u, b, v, k, cs, cs_last)
    for hp in pairs:
        bi, sl = where[hp]
        y_ref[bi, :, sl] = y[hp]
        g_ref[hp] = gt[hp] * jnp.exp(cs_last[hp]) + jnp.where(same, upd[hp], 0.0)


def rwkv_scan(r, wl, k, v, a, b, *, rows):
    bsz, seq, d = r.shape
    c = SCAN_CHUNK
    spec = pl.BlockSpec((rows, c, d), lambda bi, ci: (bi, ci, 0))
    return pl.pallas_call(
        _rwkv_scan_kernel,
        grid=(bsz // rows, seq // c),
        in_specs=[spec] * 6,
        out_specs=spec,
        out_shape=jax.ShapeDtypeStruct((bsz, seq, d), F32),
        scratch_shapes=[pltpu.VMEM((rows * d // LANES, LANES, LANES), F32)],
        compiler_params=_cparams("parallel", "arbitrary"),
        name="rwkv_scan",
    )(r, wl, k, v, a, b)


MLA_ZCOLS = 1024
MLA_SCALE = math.log2(math.e) / math.sqrt(QK_NOPE + QK_ROPE)
ATTN_TILE = 256


def _mla_prep_kernel(z_ref, pos_ref, fl_ref, qn_ref, kvn_ref, wq_ref, wqs_ref, wk_ref, wv_ref,
                     q_ref, k_ref, v_ref):
    z = z_ref[...]
    c_q = _rms(z[:, :Q_LORA], qn_ref[...]).astype(BF16)
    c_kv = _rms(z[:, Q_LORA:Q_LORA + KV_LORA], kvn_ref[...]).astype(BF16)
    kr = z[:, Q_LORA + KV_LORA:Q_LORA + KV_LORA + LANES]
    krs = z[:, Q_LORA + KV_LORA + LANES:Q_LORA + KV_LORA + 2 * LANES]

    ang = pos_ref[...].astype(F32) * fl_ref[...]
    cos, sin = jnp.cos(ang), jnp.sin(ang)
    kr_rot = kr * cos + krs * sin

    q = _dot(c_q, wq_ref[...])
    qs = _dot(c_q, wqs_ref[...])
    kn = _dot(c_kv, wk_ref[...])
    v = _dot(c_kv, wv_ref[...])
    for c in range(v_ref.shape[0]):
        v_ref[c] = v[c * ATTN_TILE:(c + 1) * ATTN_TILE, :].T.astype(BF16)
    for h in range(MLA_HEADS):
        sl = slice(h * LANES, (h + 1) * LANES)
        q_ref[:, sl] = ((q[:, sl] * cos + qs[:, sl] * sin) * MLA_SCALE).astype(BF16)
        k_ref[:, sl] = (kn[:, sl] + kr_rot).astype(BF16)


def mla_prep(z, zblock, pos, f_lane, q_norm, kv_norm, wq, wqs, wk, wv, *, tm):
    t = z.shape[0]
    n = MLA_HEADS * LANES
    full = lambda a: pl.BlockSpec(a.shape, lambda i: (0,) * a.ndim)
    args = (f_lane, q_norm.reshape(1, -1), kv_norm.reshape(1, -1), wq, wqs, wk, wv)
    out = jax.ShapeDtypeStruct((t, n), BF16)
    nv = wv.shape[1]
    out_vt = jax.ShapeDtypeStruct((t // ATTN_TILE, nv, ATTN_TILE), BF16)
    ospec = pl.BlockSpec((tm, n), lambda i: (i, 0))
    vspec = pl.BlockSpec((tm // ATTN_TILE, nv, ATTN_TILE), lambda i: (i, 0, 0))
    return pl.pallas_call(
        _mla_prep_kernel,
        grid=(t // tm,),
        in_specs=[pl.BlockSpec((tm, MLA_ZCOLS), lambda i: (i, zblock)),
                  pl.BlockSpec((tm, 1), lambda i: (i, 0))] + [full(a) for a in args],
        out_specs=[ospec, ospec, vspec],
        out_shape=[out, out, out_vt],
        compiler_params=_cparams("parallel"),
        name="mla_prep",
    )(z, pos, *args)


ATTN_HEADS_PER_STEP = 8


def _attn_kernel(q_ref, k_ref, vt_ref, o_ref):
    tq = ATTN_TILE
    iq = pl.program_id(2)
    nh = q_ref.shape[2] // LANES
    heads = [slice(h * LANES, (h + 1) * LANES) for h in range(nh)]
    qs = [q_ref[0, :, sl] for sl in heads]

    def tile(j, carry, masked):
        start = pl.multiple_of(j * tq, tq)
        if masked:
            kc = lax.broadcasted_iota(jnp.int32, (tq, tq), 0) // MASK_CHUNK
            qc = lax.broadcasted_iota(jnp.int32, (tq, tq), 1) // MASK_CHUNK
            keep = kc <= qc
        hs = range(nh)
        s = [_dot_nt(k_ref[0, pl.ds(start, tq), heads[h]], qs[h]) for h in hs]
        if masked:
            s = [jnp.where(keep, x, NEG_INF) for x in s]
        m_new = [jnp.maximum(carry[h][0], jnp.max(s[h], axis=0, keepdims=True)) for h in hs]
        alpha = [jnp.exp2(carry[h][0] - m_new[h]) for h in hs]
        p = [jnp.exp2(s[h] - m_new[h]) for h in hs]
        l = [alpha[h] * carry[h][1] + jnp.sum(p[h], axis=0, keepdims=True) for h in hs]
        pv = [_dot(vt_ref[0, j, h * V_HEAD:(h + 1) * V_HEAD, :], p[h].astype(BF16))
              for h in hs]
        return tuple((m_new[h], l[h], alpha[h] * carry[h][2] + pv[h]) for h in hs)

    init = tuple((jnp.full((1, tq), NEG_INF, F32), jnp.zeros((1, tq), F32), jnp.zeros((V_HEAD, tq), F32))
                 for _ in heads)
    carry = lax.fori_loop(0, iq, lambda j, c: tile(j, c, False), init)
    carry = tile(iq, carry, True)
    out = [acc / l for _, l, acc in carry]
    for pair in range(nh // 2):
        both = jnp.concatenate(out[2 * pair:2 * pair + 2], axis=0)
        o_ref[0, :, pair * LANES:(pair + 1) * LANES] = both.T.astype(BF16)


def attention(q, k, vt):
    bsz, seq, n = q.shape
    tq = ATTN_TILE
    w = ATTN_HEADS_PER_STEP * LANES
    wv = ATTN_HEADS_PER_STEP * V_HEAD
    qspec = pl.BlockSpec((1, tq, w), lambda b, h, i: (b, i, h))
    kspec = pl.BlockSpec((1, seq, w), lambda b, h, i: (b, 0, h))
    vspec = pl.BlockSpec((1, seq // tq, wv, tq), lambda b, h, i: (b, 0, h, 0))
    return pl.pallas_call(
        _attn_kernel,
        grid=(bsz, n // w, seq // tq),
        in_specs=[qspec, kspec, vspec],
        out_specs=pl.BlockSpec((1, tq, wv), lambda b, h, i: (b, i, h)),
        out_shape=jax.ShapeDtypeStruct((bsz, seq, vt.shape[2]), BF16),
        compiler_params=_cparams("parallel", "parallel", "arbitrary"),
        name="attention",
    )(q, k, vt)


def _combine_kernel(x_ref, y_ref, r_ref, k_ref, v_ref, g_ref, o_ref, gates_ref,
                    bd_ref, rk_ref, gnw_ref, gnb_ref, wa_ref, wb_ref, wo_ref, gq_ref, wq_ref,
                    out_ref, qp_ref):
    bd = bd_ref[...]
    inv_n = 1.0 / RW_HEAD_DIM
    y = y_ref[...]
    v = v_ref[...]
    mean = _seg_sum(y, bd) * inv_n
    yc = y - mean
    var = _seg_sum(yc * yc, bd) * inv_n
    yn = yc * lax.rsqrt(var + RW_GN_EPS) * gnw_ref[...] + gnb_ref[...]
    bonus = _seg_sum(r_ref[...] * k_ref[...] * rk_ref[...], bd) * v
    ya = _dot(((yn + bonus) * g_ref[...]).astype(BF16), wa_ref[...])
    yb = _dot(o_ref[...], wb_ref[...])
    d = ya.shape[1]
    gates = gates_ref[...]
    mix = gates[:, :d] * ya + gates[:, d:] * yb
    x1 = x_ref[...] + _dot(mix.astype(BF16), wo_ref[...])
    out_ref[...] = x1
    qp_ref[...] = _dot(_rms(x1, gq_ref[...]).astype(BF16), wq_ref[...]).astype(BF16)


def combine(x, y, r, k, v, g, o, z, gate_block, bd, r_k, gn_w, gn_b, wa, wb, wo, g_ffn, w_q, *, tm):
    t, d = x.shape
    row = lambda a: a.reshape(1, -1)
    tok = lambda a: pl.BlockSpec((tm, a.shape[1]), lambda i: (i, 0))
    full = lambda a: pl.BlockSpec(a.shape, lambda i: (0,) * a.ndim)
    toks = (x, y, r, k, v, g, o, z)
    consts = (bd, row(r_k), row(gn_w), row(gn_b), wa, wb, wo, row(g_ffn), w_q)
    gate_spec = pl.BlockSpec((tm, 2 * d), lambda i: (i, gate_block))
    nq = w_q.shape[1]
    return pl.pallas_call(
        _combine_kernel,
        grid=(t // tm,),
        in_specs=[tok(a) for a in toks[:-1]] + [gate_spec] + [full(a) for a in consts],
        out_specs=[pl.BlockSpec((tm, d), lambda i: (i, 0)), pl.BlockSpec((tm, nq), lambda i: (i, 0))],
        out_shape=[jax.ShapeDtypeStruct((t, d), F32), jax.ShapeDtypeStruct((t, nq), BF16)],
        compiler_params=_cparams("parallel"),
        name="combine",
    )(*toks, *consts)


NOT_RANKED = 127.0
RANK_BLOCKS_PER_LOOP = 2
STAIR = tuple(PEER_TOPK // (ii + 1) for ii in range(PEER_TOPK))


def _top_ranks(s, k, exact_ties):
    n, t = s.shape
    key = lax.broadcasted_iota(jnp.int32, (n, LANES), 0)
    slot = lax.broadcasted_iota(jnp.int32, (k, LANES), 0)

    def one(r, s, rank, top):
        m = jnp.max(s, axis=0, keepdims=True)
        hit = s == m
        if exact_ties:
            hit = key == jnp.min(jnp.where(hit, key, n), axis=0, keepdims=True)
        rank = jnp.where(hit, jnp.asarray(r, F32), rank)
        s = jnp.where(hit, -jnp.inf, s)
        top = jnp.where(slot == r, m, top)
        return s, rank, top

    def body(r, carry):
        return tuple(one(r, *c) for c in carry)

    blocks = [s[:, c * LANES:(c + 1) * LANES] for c in range(t // LANES)]
    tops, ranks = [], []
    for g in range(0, len(blocks), RANK_BLOCKS_PER_LOOP):
        init = tuple((b, jnp.full((n, LANES), NOT_RANKED, F32), jnp.zeros((k, LANES), F32))
                     for b in blocks[g:g + RANK_BLOCKS_PER_LOOP])
        for _, rank, top in lax.fori_loop(0, k, body, init):
            tops.append(top)
            ranks.append(rank)
    return jnp.concatenate(tops, axis=1), jnp.concatenate(ranks, axis=1)


def _ranked_excess(rank, k):
    count = jnp.sum((rank < NOT_RANKED).astype(F32), axis=0, keepdims=True)
    return jnp.abs(count - k)


def _peer_route_kernel(q_ref, keys_ref, lam_ref, cc_ref, rho_ref, e1_ref):
    tm = q_ref.shape[0]
    k = PEER_TOPK
    neg = -jnp.inf

    def head(h, _):
        col = pl.multiple_of(h * 2 * HALF_Q, 2 * HALF_Q)
        s0 = _dot_nt(keys_ref[h, 0], q_ref[:, pl.ds(col, HALF_Q)])
        s1 = _dot_nt(keys_ref[h, 1], q_ref[:, pl.ds(col + HALF_Q, HALF_Q)])
        def rank_all(exact_ties):
            top0, rank0 = _top_ranks(s0, k, exact_ties)
            top1, rank1 = _top_ranks(s1, k, exact_ties)
            row8 = lax.broadcasted_iota(jnp.int32, (SUBLANES, tm), 0)
            groups = [top0[0:1] + top1[0:8], top0[0:1] + top1[8:16], top0[1:2] + top1[0:8]]
            for ii in range(2, 8):
                groups.append(jnp.where(row8 < STAIR[ii], top0[ii:ii + 1] + top1[0:8], neg))
            groups.append(top0[8:16] + top1[0:1])
            cand = jnp.concatenate(groups, axis=0)
            _, crank = _top_ranks(cand, k, exact_ties)
            return top0, rank0, top1, rank1, cand, crank

        quick = rank_all(False)
        excess = _ranked_excess(quick[1], k) + _ranked_excess(quick[3], k) + _ranked_excess(quick[5], k)
        top0, rank0, top1, rank1, cand, crank = lax.cond(
            jnp.max(excess) == 0.0, lambda: quick, lambda: rank_all(True))
        sel = crank < NOT_RANKED
        ex = jnp.where(sel, jnp.exp(cand - cand[0:1]), 0.0)
        z = jnp.sum(ex, axis=0, keepdims=True)
        self = sel.astype(F32)
        counts = [jnp.sum(self[0:16], axis=0, keepdims=True)]
        for g in range(2, 9):
            counts.append(jnp.sum(self[8 * g:8 * g + 8], axis=0, keepdims=True))
        lvec = jnp.concatenate(counts + [self[72:80]], axis=0)

        lam = jnp.zeros((N_KEYS, tm), F32)
        for ii in range(k):
            lam = jnp.where(rank0 == float(ii), lvec[ii:ii + 1], lam)
        lam_ref[h] = lam
        cc_ref[h] = jnp.exp(s0 - top0[0:1]) * (0.5 / z)
        rho_ref[h] = rank1.astype(BF16)
        e1_ref[h] = jnp.exp(s1 - top1[0:1]).astype(BF16)
        return 0

    lax.fori_loop(0, PEER_HEADS, head, 0)


def peer_route(qp, keys, *, tm):
    t = qp.shape[0]
    out = jax.ShapeDtypeStruct((PEER_HEADS, N_KEYS, t), F32)
    out_b = jax.ShapeDtypeStruct((PEER_HEADS, N_KEYS, t), BF16)
    ospec = pl.BlockSpec((PEER_HEADS, N_KEYS, tm), lambda i: (0, 0, i))
    return pl.pallas_call(
        _peer_route_kernel,
        grid=(t // tm,),
        in_specs=[pl.BlockSpec((tm, qp.shape[1]), lambda i: (i, 0)),
                  pl.BlockSpec(keys.shape, lambda i: (0, 0, 0, 0))],
        out_specs=[ospec] * 4,
        out_shape=[out, out, out_b, out_b],
        compiler_params=_cparams("parallel"),
        name="peer_route",
    )(qp, keys)


PEER_ROWS = 8
PEER_TOKEN_CHUNK = 256


def _erf(x):
    return lax.erf(x)


def _gelu_twice(x):
    return x * (1.0 + _erf(x * (1.0 / math.sqrt(2.0))))


def _peer_expert_kernel(x_ref, gn_ref, u_ref, vt_ref, lam_ref, cc_ref, rho_ref, e1_ref, out_ref,
                        xt_ref, acc_ref, pre0_ref, pre1_ref):
    j = pl.program_id(1)
    last = pl.num_programs(1) - 1
    tm = xt_ref.shape[1]
    chunks = [slice(c, c + PEER_TOKEN_CHUNK) for c in range(0, tm, PEER_TOKEN_CHUNK)]

    @pl.when(j == 0)
    def _():
        xt_ref[...] = _rms(x_ref[...], gn_ref[...]).T.astype(BF16)
        acc_ref[...] = jnp.zeros_like(acc_ref)

    def step(fill_ref, drain_ref):
        for cols in chunks if fill_ref is not None else ():
            fill_ref[:, cols] = _dot(u_ref[...], xt_ref[:, cols])
        for cols in chunks if drain_ref is not None else ():
            gs = []
            for ii in range(PEER_ROWS):
                rows = slice(ii * N_KEYS, (ii + 1) * N_KEYS)
                terms = []
                for h in range(PEER_HEADS):
                    lam = lam_ref[h, ii:ii + 1, cols].astype(BF16)
                    cc = cc_ref[h, ii:ii + 1, cols].astype(BF16)
                    terms.append(jnp.where(rho_ref[h, :, cols] < lam, e1_ref[h, :, cols] * cc,
                                           jnp.zeros((), BF16)))
                gs.append(_gelu_twice(drain_ref[rows, cols]).astype(BF16) * sum(terms[1:], terms[0]))
            acc_ref[:, cols] += _dot(vt_ref[0], jnp.concatenate(gs, axis=0))

    even = j % 2 == 0
    inner = jnp.logical_and(j > 0, j < last)
    pl.when(j == 0)(lambda: step(pre0_ref, None))
    pl.when(jnp.logical_and(inner, even))(lambda: step(pre0_ref, pre1_ref))
    pl.when(jnp.logical_and(inner, jnp.logical_not(even)))(lambda: step(pre1_ref, pre0_ref))
    pl.when(jnp.logical_and(j == last, even))(lambda: step(None, pre1_ref))
    pl.when(jnp.logical_and(j == last, jnp.logical_not(even)))(lambda: step(None, pre0_ref))

    @pl.when(j == last)
    def _():
        out_ref[...] = acc_ref[...].T


def peer_expert(x, gain, u, vt, lam, cc, rho, e1, *, tm):
    t, d = x.shape
    nrow = PEER_ROWS * N_KEYS
    nblk = u.shape[0] // nrow
    stage = lambda lag: (lambda j: jnp.clip(j - lag, 0, nblk - 1))
    rspec = pl.BlockSpec((PEER_HEADS, PEER_ROWS, tm), lambda i, j: (0, stage(1)(j), i))
    cspec = pl.BlockSpec((PEER_HEADS, N_KEYS, tm), lambda i, j: (0, 0, i))
    return pl.pallas_call(
        _peer_expert_kernel,
        grid=(t // tm, nblk + 1),
        in_specs=[pl.BlockSpec((tm, d), lambda i, j: (i, 0)),
                  pl.BlockSpec((1, d), lambda i, j: (0, 0)),
                  pl.BlockSpec((nrow, d), lambda i, j: (stage(0)(j), 0)),
                  pl.BlockSpec((1, d, nrow), lambda i, j: (stage(1)(j), 0, 0)),
                  rspec, rspec, cspec, cspec],
        out_specs=pl.BlockSpec((tm, d), lambda i, j: (i, 0)),
        out_shape=jax.ShapeDtypeStruct((t, d), F32),
        scratch_shapes=[pltpu.VMEM((d, tm), BF16), pltpu.VMEM((d, tm), F32),
                        pltpu.VMEM((nrow, tm), F32), pltpu.VMEM((nrow, tm), F32)],
        compiler_params=_cparams("parallel", "arbitrary"),
        name="peer_expert",
    )(x, gain.reshape(1, d), u, vt, lam, cc, rho, e1)


def _ple_final_kernel(x_ref, f_ref, p_ref, gp_ref, gf_ref, wg_ref, wp_ref, out_ref):
    x = x_ref[...] + f_ref[...]
    gate = _sigmoid(_dot(_rms(x, gp_ref[...]).astype(BF16), wg_ref[...]))
    x = x + gate * _dot(p_ref[...].astype(BF16), wp_ref[...])
    out_ref[...] = _rms(x, gf_ref[...])


def ple_final(x, f, p, g_ple, g_final, wg, wp, *, tm):
    t, d = x.shape
    tok = lambda a: pl.BlockSpec((tm, a.shape[1]), lambda i: (i, 0))
    full = lambda a: pl.BlockSpec(a.shape, lambda i: (0,) * a.ndim)
    consts = (g_ple.reshape(1, d), g_final.reshape(1, d), wg, wp)
    return pl.pallas_call(
        _ple_final_kernel,
        grid=(t // tm,),
        in_specs=[tok(x), tok(f), tok(p)] + [full(a) for a in consts],
        out_specs=tok(x),
        out_shape=jax.ShapeDtypeStruct((t, d), F32),
        compiler_params=_cparams("parallel"),
        name="ple_final",
    )(x, f, p, *consts)


def _place(cols, width, offset):
    return jnp.pad(cols, ((0, 0), (offset, width - offset - cols.shape[1])))


def _rw_in_weights(w_rw, mu):
    o3 = 3 * RW_DIM
    lw, la = 64, 64
    segs = [w_rw[:, :o3], _place(w_rw[:, o3:o3 + lw], LANES, 0),
            _place(w_rw[:, o3 + lw:o3 + lw + la], LANES, 0), _place(w_rw[:, o3 + lw + la:], 2 * LANES, 0)]
    mus = [mu[None, :o3], _place(mu[None, o3:o3 + lw], LANES, 0),
           _place(mu[None, o3 + lw:o3 + lw + la], LANES, 0), _place(mu[None, o3 + lw + la:], 2 * LANES, 0)]
    return jnp.concatenate(segs, axis=1), jnp.concatenate(mus, axis=1)[0]


def _mla_in_weights(w_mla):
    half = QK_ROPE // 2
    lat = Q_LORA + KV_LORA
    kr = w_mla[:, lat:]
    kr_sw = jnp.concatenate([-kr[:, half:], kr[:, :half]], axis=1)
    return jnp.concatenate([w_mla[:, :lat], _place(kr, LANES, QK_NOPE), _place(kr_sw, LANES, QK_NOPE),
                            jnp.zeros((w_mla.shape[0], LANES), w_mla.dtype)], axis=1)


def _mla_up_weights(w_uq, w_ukv):
    half = QK_ROPE // 2
    qd = QK_NOPE + QK_ROPE
    wq = w_uq.reshape(Q_LORA, MLA_HEADS, qd)
    rope = wq[:, :, QK_NOPE:]
    rope_sw = jnp.concatenate([-rope[:, :, half:], rope[:, :, :half]], axis=2)
    pad = lambda t, off: jnp.pad(t, ((0, 0), (0, 0), (off, LANES - off - t.shape[2])))
    wq_pad = pad(wq, 0).reshape(Q_LORA, MLA_HEADS * LANES)
    wq_sw = pad(rope_sw, QK_NOPE).reshape(Q_LORA, MLA_HEADS * LANES)
    lane = jnp.arange(MLA_HEADS * LANES) % LANES
    wk = jnp.where(lane < QK_NOPE, w_ukv, 0.0)
    wv = w_ukv.reshape(KV_LORA, MLA_HEADS, LANES)[:, :, QK_NOPE:].reshape(KV_LORA, MLA_HEADS * V_HEAD)
    return wq_pad, wq_sw, wk, wv


def kernel(x, p, positions, norm_mix, w_in, rw_mu, rw_w0, rw_w2, rw_a0, rw_a2, rw_g2, rw_k_k, rw_k_a, rw_r_k, rw_gn_w, rw_gn_b, rw_w_o, mla_q_norm, mla_w_uq, mla_kv_norm, mla_w_ukv, mla_w_o, w_out, norm_ffn, peer_w_q, peer_sub_keys, peer_u, peer_v, norm_ple, ple_w_gate, ple_w_proj, norm_final):
    bsz, seq, d = x.shape
    t = bsz * seq
    depth = p.shape[0]
    bf = lambda a: a.astype(BF16)
    rw_cols = 3 * RW_DIM + 64 + 64 + 128
    mla_cols = Q_LORA + KV_LORA + QK_ROPE

    head_of = jnp.arange(SEG_SUM_WIDTH) // RW_HEAD_DIM
    bd = bf(head_of[:, None] == head_of[None, :])
    inv_freq = ROPE_THETA ** (-jnp.arange(0, QK_ROPE, 2, dtype=F32) / QK_ROPE)
    f_lane = _place(jnp.concatenate([inv_freq, inv_freq])[None, :], LANES, QK_NOPE)
    pos = positions.reshape(t, 1)

    xf = x.reshape(t, d)
    assert depth == 1, "the final RMSNorm is fused into the layer's last kernel"
    for i in range(depth):
        w_rw, mu = _rw_in_weights(w_in[i][:, :rw_cols], rw_mu[i])
        w_mla = _mla_in_weights(w_in[i][:, rw_cols:rw_cols + mla_cols])
        w_gates = w_in[i][:, rw_cols + mla_cols:]
        tn = MLA_ZCOLS
        w_all = jnp.concatenate([w_rw, w_gates, w_mla], axis=1)
        gate_lo = RW_ZCOLS // tn
        gate_hi = gate_lo + w_gates.shape[1] // tn
        z = norm_matmul(xf, norm_mix[i], bf(w_all), tm=1024, tn=tn, sigmoid_blocks=(gate_lo, gate_hi))

        pad_rows = lambda w: jnp.pad(w, ((0, LANES - w.shape[0]), (0, 0)))
        r, wl, k, v, a, b, g = rwkv_prep(
            z.reshape(bsz, seq, -1), mu, rw_w0[i], pad_rows(rw_w2[i]), rw_a0[i],
            pad_rows(rw_a2[i]), rw_g2[i], rw_k_k[i], rw_k_a[i], bd, tm=256)
        y = rwkv_scan(r, wl, k, v, a, b, rows=math.gcd(bsz, 4))

        wq, wqs, wk, wv = _mla_up_weights(mla_w_uq[i], mla_w_ukv[i])
        q, kk, vv = mla_prep(z, gate_hi, pos, f_lane, mla_q_norm[i], mla_kv_norm[i],
                             bf(wq), bf(wqs), bf(wk), bf(wv), tm=512)
        n = MLA_HEADS * LANES
        nv = MLA_HEADS * V_HEAD
        o = attention(q.reshape(bsz, seq, n), kk.reshape(bsz, seq, n),
                      vv.reshape(bsz, seq // ATTN_TILE, nv, ATTN_TILE))

        flat = lambda a: a.reshape(t, -1)
        x1, qp = combine(xf, flat(y), flat(r), flat(k), flat(v), flat(g), flat(o), z, gate_lo * tn // (2 * d),
                         bd, rw_r_k[i].reshape(-1), rw_gn_w[i], rw_gn_b[i],
                         bf(rw_w_o[i]), bf(mla_w_o[i]), bf(w_out[i]), norm_ffn[i], bf(peer_w_q[i]), tm=256)
        lam, cc, rho, e1 = peer_route(qp, bf(peer_sub_keys[i]), tm=256)
        nrow = PEER_ROWS * N_KEYS
        vt = bf(peer_v[i]).reshape(-1, nrow, d).transpose(0, 2, 1)
        ffn = peer_expert(x1, norm_ffn[i], bf(peer_u[i]), vt, lam, cc, rho, e1, tm=512)

        xf = ple_final(x1, ffn, p[i].reshape(t, -1), norm_ple[i], norm_final,
                       bf(ple_w_gate[i]), bf(ple_w_proj[i]), tm=256)
    return xf.reshape(bsz, seq, d)
```

```python
import functools
import math

import jax
import jax.numpy as jnp
from jax import lax
from jax.experimental import pallas as pl
from jax.experimental.pallas import tpu as pltpu

F32 = jnp.float32
BF16 = jnp.bfloat16
HIGHEST = lax.Precision.HIGHEST

LANES = 128
SUBLANES = 8
VMEM_LIMIT = 56 * 1024 * 1024

EPS = 1e-6
RW_HEADS = 8
RW_HEAD_DIM = 64
RW_DIM = RW_HEADS * RW_HEAD_DIM
RW_GN_EPS = 64e-5
SCAN_CHUNK = 64
SEG_SUM_WIDTH = 256

MLA_HEADS = 8
QK_NOPE = 64
QK_ROPE = 32
V_HEAD = 64
Q_LORA = 384
KV_LORA = 256
ROPE_THETA = 10000.0
MASK_CHUNK = 64
NEG_INF = -1e30

PEER_HEADS = 8
N_KEYS = 128
PEER_TOPK = 16
HALF_Q = 128


def _cparams(*sem):
    return pltpu.CompilerParams(dimension_semantics=sem, vmem_limit_bytes=VMEM_LIMIT)


def _dot(a, b, precision=None):
    return jnp.dot(a, b, preferred_element_type=F32, precision=precision)


def _dot_nt(a, b, precision=None):
    return lax.dot_general(a, b, (((1,), (1,)), ((), ())),
                           preferred_element_type=F32, precision=precision)


def _rms(x, gain):
    return x * lax.rsqrt(jnp.mean(x * x, axis=-1, keepdims=True) + EPS) * gain


def _sigmoid(x):
    return 1.0 / (1.0 + jnp.exp(-x))


def _seg_sum(x, bd):
    hi = x.astype(BF16)
    lo = (x - hi.astype(F32)).astype(BF16)
    w = bd.shape[0]
    parts = [_dot(hi[:, c:c + w], bd) + _dot(lo[:, c:c + w], bd) for c in range(0, x.shape[1], w)]
    return jnp.concatenate(parts, axis=1)


def _norm_matmul_kernel(x_ref, g_ref, w_ref, o_ref, h_ref, *, sigmoid_blocks):
    j = pl.program_id(1)

    @pl.when(j == 0)
    def _():
        h_ref[...] = _rms(x_ref[...], g_ref[...]).astype(BF16)

    def project(act):
        o_ref[...] = act(_dot(h_ref[...], w_ref[...])).astype(o_ref.dtype)

    if sigmoid_blocks is None:
        project(lambda y: y)
    else:
        gated = jnp.logical_and(j >= sigmoid_blocks[0], j < sigmoid_blocks[1])
        pl.when(gated)(lambda: project(_sigmoid))
        pl.when(jnp.logical_not(gated))(lambda: project(lambda y: y))


def norm_matmul(x, gain, w, *, tm, tn, sigmoid_blocks=None, out_dtype=F32):
    t, d = x.shape
    n = w.shape[1]
    return pl.pallas_call(
        functools.partial(_norm_matmul_kernel, sigmoid_blocks=sigmoid_blocks),
        grid=(t // tm, n // tn),
        in_specs=[pl.BlockSpec((tm, d), lambda i, j: (i, 0)),
                  pl.BlockSpec((1, d), lambda i, j: (0, 0)),
                  pl.BlockSpec((d, tn), lambda i, j: (0, j))],
        out_specs=pl.BlockSpec((tm, tn), lambda i, j: (i, j)),
        out_shape=jax.ShapeDtypeStruct((t, n), out_dtype),
        scratch_shapes=[pltpu.VMEM((tm, d), BF16)],
        compiler_params=_cparams("parallel", "arbitrary"),
        name="norm_matmul",
    )(x, gain.reshape(1, d), w)


RW_ZCOLS = 3 * RW_DIM + 4 * LANES


def _rwkv_prep_kernel(z_ref, zp_ref, mu_ref, w0_ref, w2_ref, a0_ref, a2_ref, g2_ref,
                      kk_ref, ka_ref, bd_ref,
                      r_ref, wl_ref, k_ref, v_ref, a_ref, b_ref, g_ref):
    z = z_ref[0]
    tm = z.shape[0]
    prev_last = zp_ref[0][SUBLANES - 1:SUBLANES, :]
    prev_last = jnp.where(pl.program_id(1) == 0, 0.0, prev_last)
    rolled = pltpu.roll(z, 1, 0)
    row = lax.broadcasted_iota(jnp.int32, (tm, 1), 0)
    z_prev = jnp.where(row == 0, prev_last, rolled)
    z = z + mu_ref[...] * (z_prev - z)

    o1, o2, o3 = RW_DIM, 2 * RW_DIM, 3 * RW_DIM
    r, k, v = z[:, :o1], z[:, o1:o2], z[:, o2:o3]
    zw, za, zg = (z[:, o3 + c * LANES:o3 + (c + 1) * LANES] for c in range(3))

    wpre = w0_ref[...] + _dot(jnp.tanh(zw), w2_ref[...])
    nx = -wpre
    softplus = jnp.maximum(nx, 0.0) + jnp.log(1.0 + jnp.exp(-jnp.abs(nx)))
    w = -softplus - 0.5
    iclr = _sigmoid(a0_ref[...] + _dot(za, a2_ref[...]))
    g = _dot(_sigmoid(zg), g2_ref[...])

    kk = k * kk_ref[...]
    ss = _seg_sum(kk * kk, bd_ref[...])
    kk = kk / jnp.maximum(jnp.sqrt(ss), 1e-12)

    r_ref[0] = r
    wl_ref[0] = -jnp.exp(w)
    k_ref[0] = k * (1.0 + (iclr - 1.0) * ka_ref[...])
    v_ref[0] = v
    a_ref[0] = -kk
    b_ref[0] = kk * iclr
    g_ref[0] = g


def rwkv_prep(z, mu, w0, w2, a0, a2, g2, k_k, k_a, bd, *, tm):
    bsz, seq, _ = z.shape
    zc = RW_ZCOLS
    d = RW_DIM
    row = lambda a: a.reshape(1, -1)
    full = lambda a: pl.BlockSpec(a.shape, lambda b, i: (0,) * a.ndim)
    args = (row(mu), row(w0), w2, row(a0), a2, g2, row(k_k), row(k_a), bd)
    out = jax.ShapeDtypeStruct((bsz, seq, d), F32)
    ospec = pl.BlockSpec((1, tm, d), lambda b, i: (b, i, 0))
    return pl.pallas_call(
        _rwkv_prep_kernel,
        grid=(bsz, seq // tm),
        in_specs=[pl.BlockSpec((1, tm, zc), lambda b, i: (b, i, 0)),
                  pl.BlockSpec((1, SUBLANES, zc),
                               lambda b, i: (b, jnp.maximum(i * (tm // SUBLANES) - 1, 0), 0))]
                 + [full(a) for a in args],
        out_specs=[ospec] * 7,
        out_shape=[out] * 7,
        compiler_params=_cparams("parallel", "arbitrary"),
        name="rwkv_prep",
    )(z, z, *args)


def _rwkv_scan_kernel(r_ref, wl_ref, k_ref, v_ref, a_ref, b_ref, y_ref, g_ref):
    c = SCAN_CHUNK
    hd = RW_HEAD_DIM
    npair = g_ref.shape[0]

    @pl.when(pl.program_id(1) == 0)
    def _():
        g_ref[...] = jnp.zeros_like(g_ref)

    ri = lax.broadcasted_iota(jnp.int32, (c, c), 0)
    ci = lax.broadcasted_iota(jnp.int32, (c, c), 1)
    tril = (ri >= ci).astype(F32)
    lane = lax.broadcasted_iota(jnp.int32, (c, 2 * hd), 1)
    m0 = lane < hd
    r2 = lax.broadcasted_iota(jnp.int32, (2 * c, 2 * c), 0)
    c2 = lax.broadcasted_iota(jnp.int32, (2 * c, 2 * c), 1)
    same = (r2 >= c) == (c2 >= c)
    strict = jnp.logical_and(same, r2 > c2)
    incl = jnp.logical_and(same, r2 >= c2)
    eye = (r2 == c2).astype(F32)

    bf = lambda x: x.astype(BF16)
    stack = lambda x: bf(jnp.concatenate([jnp.where(m0, x, 0.0), jnp.where(m0, 0.0, x)], axis=0))
    twice = lambda x: bf(jnp.concatenate([x, x], axis=0))
    pick = lambda s: jnp.where(m0, s[:c], s[c:])

    pairs = range(npair)
    each = lambda f, *cols: [f(*args) for args in zip(*cols)]
    per_row = r_ref.shape[2] // LANES
    where = [(hp // per_row, slice((hp % per_row) * LANES, (hp % per_row + 1) * LANES)) for hp in pairs]
    load = lambda ref: [ref[bi, :, sl] for bi, sl in where]
    r, wl, k, v, a, b = (load(x) for x in (r_ref, wl_ref, k_ref, v_ref, a_ref, b_ref))

    cs = each(lambda w: _dot(tril, w, HIGHEST), wl)
    cs_last = each(lambda s: s[c - 1:c, :], cs)
    p_inv = each(lambda s: jnp.exp(-s), cs)
    at = each(lambda a, s, w: a * jnp.exp(s - w), a, cs, wl)
    rt = each(lambda r, s: r * jnp.exp(s), r, cs)
    at_s, rt_s = each(stack, at), each(stack, rt)
    bt_s = each(lambda b, p: stack(b * p), b, p_inv)
    kt_s = each(lambda k, p: stack(k * p), k, p_inv)

    ab = each(lambda x, y: jnp.where(strict, _dot_nt(x, y), 0.0), at_s, bt_s)
    ak = each(lambda x, y: jnp.where(strict, _dot_nt(x, y), 0.0), at_s, kt_s)
    rb = each(lambda x, y: jnp.where(incl, _dot_nt(x, y), 0.0), rt_s, bt_s)
    rk = each(lambda x, y: jnp.where(incl, _dot_nt(x, y), 0.0), rt_s, kt_s)

    tinv = each(lambda m: eye + m, ab)
    x = each(bf, ab)
    for _ in range(int(math.log2(c)) - 1):
        x = each(lambda m: bf(_dot(m, m)), x)
        tinv = each(lambda t, m: t + _dot(bf(t), m), tinv, x)

    gt = [g_ref[hp] for hp in pairs]
    gtb = each(bf, gt)
    vv = each(twice, v)
    rhs = each(lambda at, g, ak, vv: _dot_nt(bf(at), g) + pick(_dot(bf(ak), vv)), at, gtb, ak, vv)
    u = each(lambda t, x: pick(_dot(bf(t), twice(x))), tinv, rhs)
    y = each(lambda rt, g, rb, u, rk, vv:
             _dot_nt(bf(rt), g) + pick(_dot(bf(rb), twice(u)) + _dot(bf(rk), vv)),
             rt, gtb, rb, u, rk, vv)
    upd = each(lambda u, b, v, k, s, sl:
               _dot(bf(u.T), bf(b * jnp.exp(sl - s))) + _dot(bf(v.T), bf(k * jnp.exp(sl - s))),
               u, b, v, k, cs, cs_last)
    for hp in pairs:
        bi, sl = where[hp]
        y_ref[bi, :, sl] = y[hp]
        g_ref[hp] = gt[hp] * jnp.exp(cs_last[hp]) + jnp.where(same, upd[hp], 0.0)


def rwkv_scan(r, wl, k, v, a, b, *, rows):
    bsz, seq, d = r.shape
    c = SCAN_CHUNK
    spec = pl.BlockSpec((rows, c, d), lambda bi, ci: (bi, ci, 0))
    return pl.pallas_call(
        _rwkv_scan_kernel,
        grid=(bsz // rows, seq // c),
        in_specs=[spec] * 6,
        out_specs=spec,
        out_shape=jax.ShapeDtypeStruct((bsz, seq, d), F32),
        scratch_shapes=[pltpu.VMEM((rows * d // LANES, LANES, LANES), F32)],
        compiler_params=_cparams("parallel", "arbitrary"),
        name="rwkv_scan",
    )(r, wl, k, v, a, b)


MLA_ZCOLS = 1024
MLA_SCALE = math.log2(math.e) / math.sqrt(QK_NOPE + QK_ROPE)
ATTN_TILE = 256


def _mla_prep_kernel(z_ref, pos_ref, fl_ref, qn_ref, kvn_ref, wq_ref, wqs_ref, wk_ref, wv_ref,
                     q_ref, k_ref, v_ref):
    z = z_ref[...]
    c_q = _rms(z[:, :Q_LORA], qn_ref[...]).astype(BF16)
    c_kv = _rms(z[:, Q_LORA:Q_LORA + KV_LORA], kvn_ref[...]).astype(BF16)
    kr = z[:, Q_LORA + KV_LORA:Q_LORA + KV_LORA + LANES]
    krs = z[:, Q_LORA + KV_LORA + LANES:Q_LORA + KV_LORA + 2 * LANES]

    ang = pos_ref[...].astype(F32) * fl_ref[...]
    cos, sin = jnp.cos(ang), jnp.sin(ang)
    kr_rot = kr * cos + krs * sin

    q = _dot(c_q, wq_ref[...])
    qs = _dot(c_q, wqs_ref[...])
    kn = _dot(c_kv, wk_ref[...])
    v = _dot(c_kv, wv_ref[...])
    for c in range(v_ref.shape[0]):
        v_ref[c] = v[c * ATTN_TILE:(c + 1) * ATTN_TILE, :].T.astype(BF16)
    for h in range(MLA_HEADS):
        sl = slice(h * LANES, (h + 1) * LANES)
        q_ref[:, sl] = ((q[:, sl] * cos + qs[:, sl] * sin) * MLA_SCALE).astype(BF16)
        k_ref[:, sl] = (kn[:, sl] + kr_rot).astype(BF16)


def mla_prep(z, zblock, pos, f_lane, q_norm, kv_norm, wq, wqs, wk, wv, *, tm):
    t = z.shape[0]
    n = MLA_HEADS * LANES
    full = lambda a: pl.BlockSpec(a.shape, lambda i: (0,) * a.ndim)
    args = (f_lane, q_norm.reshape(1, -1), kv_norm.reshape(1, -1), wq, wqs, wk, wv)
    out = jax.ShapeDtypeStruct((t, n), BF16)
    nv = wv.shape[1]
    out_vt = jax.ShapeDtypeStruct((t // ATTN_TILE, nv, ATTN_TILE), BF16)
    ospec = pl.BlockSpec((tm, n), lambda i: (i, 0))
    vspec = pl.BlockSpec((tm // ATTN_TILE, nv, ATTN_TILE), lambda i: (i, 0, 0))
    return pl.pallas_call(
        _mla_prep_kernel,
        grid=(t // tm,),
        in_specs=[pl.BlockSpec((tm, MLA_ZCOLS), lambda i: (i, zblock)),
                  pl.BlockSpec((tm, 1), lambda i: (i, 0))] + [full(a) for a in args],
        out_specs=[ospec, ospec, vspec],
        out_shape=[out, out, out_vt],
        compiler_params=_cparams("parallel"),
        name="mla_prep",
    )(z, pos, *args)


ATTN_HEADS_PER_STEP = 8


def _attn_kernel(q_ref, k_ref, vt_ref, o_ref):
    tq = ATTN_TILE
    iq = pl.program_id(2)
    nh = q_ref.shape[2] // LANES
    heads = [slice(h * LANES, (h + 1) * LANES) for h in range(nh)]
    qs = [q_ref[0, :, sl] for sl in heads]

    def tile(j, carry, masked):
        start = pl.multiple_of(j * tq, tq)
        if masked:
            kc = lax.broadcasted_iota(jnp.int32, (tq, tq), 0) // MASK_CHUNK
            qc = lax.broadcasted_iota(jnp.int32, (tq, tq), 1) // MASK_CHUNK
            keep = kc <= qc
        hs = range(nh)
        s = [_dot_nt(k_ref[0, pl.ds(start, tq), heads[h]], qs[h]) for h in hs]
        if masked:
            s = [jnp.where(keep, x, NEG_INF) for x in s]
        m_new = [jnp.maximum(carry[h][0], jnp.max(s[h], axis=0, keepdims=True)) for h in hs]
        alpha = [jnp.exp2(carry[h][0] - m_new[h]) for h in hs]
        p = [jnp.exp2(s[h] - m_new[h]) for h in hs]
        l = [alpha[h] * carry[h][1] + jnp.sum(p[h], axis=0, keepdims=True) for h in hs]
        pv = [_dot(vt_ref[0, j, h * V_HEAD:(h + 1) * V_HEAD, :], p[h].astype(BF16))
              for h in hs]
        return tuple((m_new[h], l[h], alpha[h] * carry[h][2] + pv[h]) for h in hs)

    init = tuple((jnp.full((1, tq), NEG_INF, F32), jnp.zeros((1, tq), F32), jnp.zeros((V_HEAD, tq), F32))
                 for _ in heads)
    carry = lax.fori_loop(0, iq, lambda j, c: tile(j, c, False), init)
    carry = tile(iq, carry, True)
    out = [acc / l for _, l, acc in carry]
    for pair in range(nh // 2):
        both = jnp.concatenate(out[2 * pair:2 * pair + 2], axis=0)
        o_ref[0, :, pair * LANES:(pair + 1) * LANES] = both.T.astype(BF16)


def attention(q, k, vt):
    bsz, seq, n = q.shape
    tq = ATTN_TILE
    w = ATTN_HEADS_PER_STEP * LANES
    wv = ATTN_HEADS_PER_STEP * V_HEAD
    qspec = pl.BlockSpec((1, tq, w), lambda b, h, i: (b, i, h))
    kspec = pl.BlockSpec((1, seq, w), lambda b, h, i: (b, 0, h))
    vspec = pl.BlockSpec((1, seq // tq, wv, tq), lambda b, h, i: (b, 0, h, 0))
    return pl.pallas_call(
        _attn_kernel,
        grid=(bsz, n // w, seq // tq),
        in_specs=[qspec, kspec, vspec],
        out_specs=pl.BlockSpec((1, tq, wv), lambda b, h, i: (b, i, h)),
        out_shape=jax.ShapeDtypeStruct((bsz, seq, vt.shape[2]), BF16),
        compiler_params=_cparams("parallel", "parallel", "arbitrary"),
        name="attention",
    )(q, k, vt)


def _combine_kernel(x_ref, y_ref, r_ref, k_ref, v_ref, g_ref, o_ref, gates_ref,
                    bd_ref, rk_ref, gnw_ref, gnb_ref, wa_ref, wb_ref, wo_ref, gq_ref, wq_ref,
                    out_ref, qp_ref):
    bd = bd_ref[...]
    inv_n = 1.0 / RW_HEAD_DIM
    y = y_ref[...]
    v = v_ref[...]
    mean = _seg_sum(y, bd) * inv_n
    yc = y - mean
    var = _seg_sum(yc * yc, bd) * inv_n
    yn = yc * lax.rsqrt(var + RW_GN_EPS) * gnw_ref[...] + gnb_ref[...]
    bonus = _seg_sum(r_ref[...] * k_ref[...] * rk_ref[...], bd) * v
    ya = _dot(((yn + bonus) * g_ref[...]).astype(BF16), wa_ref[...])
    yb = _dot(o_ref[...], wb_ref[...])
    d = ya.shape[1]
    gates = gates_ref[...]
    mix = gates[:, :d] * ya + gates[:, d:] * yb
    x1 = x_ref[...] + _dot(mix.astype(BF16), wo_ref[...])
    out_ref[...] = x1
    qp_ref[...] = _dot(_rms(x1, gq_ref[...]).astype(BF16), wq_ref[...]).astype(BF16)


def combine(x, y, r, k, v, g, o, z, gate_block, bd, r_k, gn_w, gn_b, wa, wb, wo, g_ffn, w_q, *, tm):
    t, d = x.shape
    row = lambda a: a.reshape(1, -1)
    tok = lambda a: pl.BlockSpec((tm, a.shape[1]), lambda i: (i, 0))
    full = lambda a: pl.BlockSpec(a.shape, lambda i: (0,) * a.ndim)
    toks = (x, y, r, k, v, g, o, z)
    consts = (bd, row(r_k), row(gn_w), row(gn_b), wa, wb, wo, row(g_ffn), w_q)
    gate_spec = pl.BlockSpec((tm, 2 * d), lambda i: (i, gate_block))
    nq = w_q.shape[1]
    return pl.pallas_call(
        _combine_kernel,
        grid=(t // tm,),
        in_specs=[tok(a) for a in toks[:-1]] + [gate_spec] + [full(a) for a in consts],
        out_specs=[pl.BlockSpec((tm, d), lambda i: (i, 0)), pl.BlockSpec((tm, nq), lambda i: (i, 0))],
        out_shape=[jax.ShapeDtypeStruct((t, d), F32), jax.ShapeDtypeStruct((t, nq), BF16)],
        compiler_params=_cparams("parallel"),
        name="combine",
    )(*toks, *consts)


NOT_RANKED = 127.0
RANK_BLOCKS_PER_LOOP = 2
STAIR = tuple(PEER_TOPK // (ii + 1) for ii in range(PEER_TOPK))


def _top_ranks(s, k, exact_ties):
    n, t = s.shape
    key = lax.broadcasted_iota(jnp.int32, (n, LANES), 0)
    slot = lax.broadcasted_iota(jnp.int32, (k, LANES), 0)

    def one(r, s, rank, top):
        m = jnp.max(s, axis=0, keepdims=True)
        hit = s == m
        if exact_ties:
            hit = key == jnp.min(jnp.where(hit, key, n), axis=0, keepdims=True)
        rank = jnp.where(hit, jnp.asarray(r, F32), rank)
        s = jnp.where(hit, -jnp.inf, s)
        top = jnp.where(slot == r, m, top)
        return s, rank, top

    def body(r, carry):
        return tuple(one(r, *c) for c in carry)

    blocks = [s[:, c * LANES:(c + 1) * LANES] for c in range(t // LANES)]
    tops, ranks = [], []
    for g in range(0, len(blocks), RANK_BLOCKS_PER_LOOP):
        init = tuple((b, jnp.full((n, LANES), NOT_RANKED, F32), jnp.zeros((k, LANES), F32))
                     for b in blocks[g:g + RANK_BLOCKS_PER_LOOP])
        for _, rank, top in lax.fori_loop(0, k, body, init):
            tops.append(top)
            ranks.append(rank)
    return jnp.concatenate(tops, axis=1), jnp.concatenate(ranks, axis=1)


def _ranked_excess(rank, k):
    count = jnp.sum((rank < NOT_RANKED).astype(F32), axis=0, keepdims=True)
    return jnp.abs(count - k)


def _peer_route_kernel(q_ref, keys_ref, lam_ref, cc_ref, rho_ref, e1_ref):
    tm = q_ref.shape[0]
    k = PEER_TOPK
    neg = -jnp.inf

    def head(h, _):
        col = pl.multiple_of(h * 2 * HALF_Q, 2 * HALF_Q)
        s0 = _dot_nt(keys_ref[h, 0], q_ref[:, pl.ds(col, HALF_Q)])
        s1 = _dot_nt(keys_ref[h, 1], q_ref[:, pl.ds(col + HALF_Q, HALF_Q)])
        def rank_all(exact_ties):
            top0, rank0 = _top_ranks(s0, k, exact_ties)
            top1, rank1 = _top_ranks(s1, k, exact_ties)
            row8 = lax.broadcasted_iota(jnp.int32, (SUBLANES, tm), 0)
            groups = [top0[0:1] + top1[0:8], top0[0:1] + top1[8:16], top0[1:2] + top1[0:8]]
            for ii in range(2, 8):
                groups.append(jnp.where(row8 < STAIR[ii], top0[ii:ii + 1] + top1[0:8], neg))
            groups.append(top0[8:16] + top1[0:1])
            cand = jnp.concatenate(groups, axis=0)
            _, crank = _top_ranks(cand, k, exact_ties)
            return top0, rank0, top1, rank1, cand, crank

        quick = rank_all(False)
        excess = _ranked_excess(quick[1], k) + _ranked_excess(quick[3], k) + _ranked_excess(quick[5], k)
        top0, rank0, top1, rank1, cand, crank = lax.cond(
            jnp.max(excess) == 0.0, lambda: quick, lambda: rank_all(True))
        sel = crank < NOT_RANKED
        ex = jnp.where(sel, jnp.exp(cand - cand[0:1]), 0.0)
        z = jnp.sum(ex, axis=0, keepdims=True)
        self = sel.astype(F32)
        counts = [jnp.sum(self[0:16], axis=0, keepdims=True)]
        for g in range(2, 9):
            counts.append(jnp.sum(self[8 * g:8 * g + 8], axis=0, keepdims=True))
        lvec = jnp.concatenate(counts + [self[72:80]], axis=0)

        lam = jnp.zeros((N_KEYS, tm), F32)
        for ii in range(k):
            lam = jnp.where(rank0 == float(ii), lvec[ii:ii + 1], lam)
        lam_ref[h] = lam
        cc_ref[h] = jnp.exp(s0 - top0[0:1]) * (0.5 / z)
        rho_ref[h] = rank1.astype(BF16)
        e1_ref[h] = jnp.exp(s1 - top1[0:1]).astype(BF16)
        return 0

    lax.fori_loop(0, PEER_HEADS, head, 0)


def peer_route(qp, keys, *, tm):
    t = qp.shape[0]
    out = jax.ShapeDtypeStruct((PEER_HEADS, N_KEYS, t), F32)
    out_b = jax.ShapeDtypeStruct((PEER_HEADS, N_KEYS, t), BF16)
    ospec = pl.BlockSpec((PEER_HEADS, N_KEYS, tm), lambda i: (0, 0, i))
    return pl.pallas_call(
        _peer_route_kernel,
        grid=(t // tm,),
        in_specs=[pl.BlockSpec((tm, qp.shape[1]), lambda i: (i, 0)),
                  pl.BlockSpec(keys.shape, lambda i: (0, 0, 0, 0))],
        out_specs=[ospec] * 4,
        out_shape=[out, out, out_b, out_b],
        compiler_params=_cparams("parallel"),
        name="peer_route",
    )(qp, keys)


PEER_ROWS = 16
PEER_TOKEN_CHUNK = 256


def _erf(x):
    return lax.erf(x)


def _gelu_twice(x):
    return x * (1.0 + _erf(x * (1.0 / math.sqrt(2.0))))


def _peer_expert_kernel(x_ref, gn_ref, u_ref, vt_ref, lam_ref, cc_ref, rho_ref, e1_ref, out_ref,
                        xt_ref, acc_ref, pre0_ref, pre1_ref):
    j = pl.program_id(1)
    last = pl.num_programs(1) - 1
    tm = xt_ref.shape[1]
    chunks = [slice(c, c + PEER_TOKEN_CHUNK) for c in range(0, tm, PEER_TOKEN_CHUNK)]

    @pl.when(j == 0)
    def _():
        xt_ref[...] = _rms(x_ref[...], gn_ref[...]).T.astype(BF16)
        acc_ref[...] = jnp.zeros_like(acc_ref)

    def step(fill_ref, drain_ref):
        for cols in chunks if fill_ref is not None else ():
            fill_ref[:, cols] = _dot(u_ref[...], xt_ref[:, cols])
        for cols in chunks if drain_ref is not None else ():
            gs = []
            for ii in range(PEER_ROWS):
                rows = slice(ii * N_KEYS, (ii + 1) * N_KEYS)
                terms = []
                for h in range(PEER_HEADS):
                    lam = lam_ref[h, ii:ii + 1, cols].astype(BF16)
                    cc = cc_ref[h, ii:ii + 1, cols].astype(BF16)
                    terms.append(jnp.where(rho_ref[h, :, cols] < lam, e1_ref[h, :, cols] * cc,
                                           jnp.zeros((), BF16)))
                gs.append(_gelu_twice(drain_ref[rows, cols]).astype(BF16) * sum(terms[1:], terms[0]))
            acc_ref[:, cols] += _dot(vt_ref[0], jnp.concatenate(gs, axis=0))

    even = j % 2 == 0
    inner = jnp.logical_and(j > 0, j < last)
    pl.when(j == 0)(lambda: step(pre0_ref, None))
    pl.when(jnp.logical_and(inner, even))(lambda: step(pre0_ref, pre1_ref))
    pl.when(jnp.logical_and(inner, jnp.logical_not(even)))(lambda: step(pre1_ref, pre0_ref))
    pl.when(jnp.logical_and(j == last, even))(lambda: step(None, pre1_ref))
    pl.when(jnp.logical_and(j == last, jnp.logical_not(even)))(lambda: step(None, pre0_ref))

    @pl.when(j == last)
    def _():
        out_ref[...] = acc_ref[...].T


def peer_expert(x, gain, u, vt, lam, cc, rho, e1, *, tm):
    t, d = x.shape
    nrow = PEER_ROWS * N_KEYS
    nblk = u.shape[0] // nrow
    stage = lambda lag: (lambda j: jnp.clip(j - lag, 0, nblk - 1))
    rspec = pl.BlockSpec((PEER_HEADS, PEER_ROWS, tm), lambda i, j: (0, stage(1)(j), i))
    cspec = pl.BlockSpec((PEER_HEADS, N_KEYS, tm), lambda i, j: (0, 0, i))
    return pl.pallas_call(
        _peer_expert_kernel,
        grid=(t // tm, nblk + 1),
        in_specs=[pl.BlockSpec((tm, d), lambda i, j: (i, 0)),
                  pl.BlockSpec((1, d), lambda i, j: (0, 0)),
                  pl.BlockSpec((nrow, d), lambda i, j: (stage(0)(j), 0)),
                  pl.BlockSpec((1, d, nrow), lambda i, j: (stage(1)(j), 0, 0)),
                  rspec, rspec, cspec, cspec],
        out_specs=pl.BlockSpec((tm, d), lambda i, j: (i, 0)),
        out_shape=jax.ShapeDtypeStruct((t, d), F32),
        scratch_shapes=[pltpu.VMEM((d, tm), BF16), pltpu.VMEM((d, tm), F32),
                        pltpu.VMEM((nrow, tm), F32), pltpu.VMEM((nrow, tm), F32)],
        compiler_params=_cparams("parallel", "arbitrary"),
        name="peer_expert",
    )(x, gain.reshape(1, d), u, vt, lam, cc, rho, e1)


def _ple_final_kernel(x_ref, f_ref, p_ref, gp_ref, gf_ref, wg_ref, wp_ref, out_ref):
    x = x_ref[...] + f_ref[...]
    gate = _sigmoid(_dot(_rms(x, gp_ref[...]).astype(BF16), wg_ref[...]))
    x = x + gate * _dot(p_ref[...].astype(BF16), wp_ref[...])
    out_ref[...] = _rms(x, gf_ref[...])


def ple_final(x, f, p, g_ple, g_final, wg, wp, *, tm):
    t, d = x.shape
    tok = lambda a: pl.BlockSpec((tm, a.shape[1]), lambda i: (i, 0))
    full = lambda a: pl.BlockSpec(a.shape, lambda i: (0,) * a.ndim)
    consts = (g_ple.reshape(1, d), g_final.reshape(1, d), wg, wp)
    return pl.pallas_call(
        _ple_final_kernel,
        grid=(t // tm,),
        in_specs=[tok(x), tok(f), tok(p)] + [full(a) for a in consts],
        out_specs=tok(x),
        out_shape=jax.ShapeDtypeStruct((t, d), F32),
        compiler_params=_cparams("parallel"),
        name="ple_final",
    )(x, f, p, *consts)


def _place(cols, width, offset):
    return jnp.pad(cols, ((0, 0), (offset, width - offset - cols.shape[1])))


def _rw_in_weights(w_rw, mu):
    o3 = 3 * RW_DIM
    lw, la = 64, 64
    segs = [w_rw[:, :o3], _place(w_rw[:, o3:o3 + lw], LANES, 0),
            _place(w_rw[:, o3 + lw:o3 + lw + la], LANES, 0), _place(w_rw[:, o3 + lw + la:], 2 * LANES, 0)]
    mus = [mu[None, :o3], _place(mu[None, o3:o3 + lw], LANES, 0),
           _place(mu[None, o3 + lw:o3 + lw + la], LANES, 0), _place(mu[None, o3 + lw + la:], 2 * LANES, 0)]
    return jnp.concatenate(segs, axis=1), jnp.concatenate(mus, axis=1)[0]


def _mla_in_weights(w_mla):
    half = QK_ROPE // 2
    lat = Q_LORA + KV_LORA
    kr = w_mla[:, lat:]
    kr_sw = jnp.concatenate([-kr[:, half:], kr[:, :half]], axis=1)
    return jnp.concatenate([w_mla[:, :lat], _place(kr, LANES, QK_NOPE), _place(kr_sw, LANES, QK_NOPE),
                            jnp.zeros((w_mla.shape[0], LANES), w_mla.dtype)], axis=1)


def _mla_up_weights(w_uq, w_ukv):
    half = QK_ROPE // 2
    qd = QK_NOPE + QK_ROPE
    wq = w_uq.reshape(Q_LORA, MLA_HEADS, qd)
    rope = wq[:, :, QK_NOPE:]
    rope_sw = jnp.concatenate([-rope[:, :, half:], rope[:, :, :half]], axis=2)
    pad = lambda t, off: jnp.pad(t, ((0, 0), (0, 0), (off, LANES - off - t.shape[2])))
    wq_pad = pad(wq, 0).reshape(Q_LORA, MLA_HEADS * LANES)
    wq_sw = pad(rope_sw, QK_NOPE).reshape(Q_LORA, MLA_HEADS * LANES)
    lane = jnp.arange(MLA_HEADS * LANES) % LANES
    wk = jnp.where(lane < QK_NOPE, w_ukv, 0.0)
    wv = w_ukv.reshape(KV_LORA, MLA_HEADS, LANES)[:, :, QK_NOPE:].reshape(KV_LORA, MLA_HEADS * V_HEAD)
    return wq_pad, wq_sw, wk, wv


def kernel(x, p, positions, norm_mix, w_in, rw_mu, rw_w0, rw_w2, rw_a0, rw_a2, rw_g2, rw_k_k, rw_k_a, rw_r_k, rw_gn_w, rw_gn_b, rw_w_o, mla_q_norm, mla_w_uq, mla_kv_norm, mla_w_ukv, mla_w_o, w_out, norm_ffn, peer_w_q, peer_sub_keys, peer_u, peer_v, norm_ple, ple_w_gate, ple_w_proj, norm_final):
    bsz, seq, d = x.shape
    t = bsz * seq
    depth = p.shape[0]
    bf = lambda a: a.astype(BF16)
    rw_cols = 3 * RW_DIM + 64 + 64 + 128
    mla_cols = Q_LORA + KV_LORA + QK_ROPE

    head_of = jnp.arange(SEG_SUM_WIDTH) // RW_HEAD_DIM
    bd = bf(head_of[:, None] == head_of[None, :])
    inv_freq = ROPE_THETA ** (-jnp.arange(0, QK_ROPE, 2, dtype=F32) / QK_ROPE)
    f_lane = _place(jnp.concatenate([inv_freq, inv_freq])[None, :], LANES, QK_NOPE)
    pos = positions.reshape(t, 1)

    xf = x.reshape(t, d)
    assert depth == 1, "the final RMSNorm is fused into the layer's last kernel"
    for i in range(depth):
        w_rw, mu = _rw_in_weights(w_in[i][:, :rw_cols], rw_mu[i])
        w_mla = _mla_in_weights(w_in[i][:, rw_cols:rw_cols + mla_cols])
        w_gates = w_in[i][:, rw_cols + mla_cols:]
        tn = MLA_ZCOLS
        w_all = jnp.concatenate([w_rw, w_gates, w_mla], axis=1)
        gate_lo = RW_ZCOLS // tn
        gate_hi = gate_lo + w_gates.shape[1] // tn
        z = norm_matmul(xf, norm_mix[i], bf(w_all), tm=1024, tn=tn, sigmoid_blocks=(gate_lo, gate_hi))

        pad_rows = lambda w: jnp.pad(w, ((0, LANES - w.shape[0]), (0, 0)))
        r, wl, k, v, a, b, g = rwkv_prep(
            z.reshape(bsz, seq, -1), mu, rw_w0[i], pad_rows(rw_w2[i]), rw_a0[i],
            pad_rows(rw_a2[i]), rw_g2[i], rw_k_k[i], rw_k_a[i], bd, tm=256)
        y = rwkv_scan(r, wl, k, v, a, b, rows=math.gcd(bsz, 4))

        wq, wqs, wk, wv = _mla_up_weights(mla_w_uq[i], mla_w_ukv[i])
        q, kk, vv = mla_prep(z, gate_hi, pos, f_lane, mla_q_norm[i], mla_kv_norm[i],
                             bf(wq), bf(wqs), bf(wk), bf(wv), tm=512)
        n = MLA_HEADS * LANES
        nv = MLA_HEADS * V_HEAD
        o = attention(q.reshape(bsz, seq, n), kk.reshape(bsz, seq, n),
                      vv.reshape(bsz, seq // ATTN_TILE, nv, ATTN_TILE))

        flat = lambda a: a.reshape(t, -1)
        x1, qp = combine(xf, flat(y), flat(r), flat(k), flat(v), flat(g), flat(o), z, gate_lo * tn // (2 * d),
                         bd, rw_r_k[i].reshape(-1), rw_gn_w[i], rw_gn_b[i],
                         bf(rw_w_o[i]), bf(mla_w_o[i]), bf(w_out[i]), norm_ffn[i], bf(peer_w_q[i]), tm=256)
        lam, cc, rho, e1 = peer_route(qp, bf(peer_sub_keys[i]), tm=256)
        nrow = PEER_ROWS * N_KEYS
        vt = bf(peer_v[i]).reshape(-1, nrow, d).transpose(0, 2, 1)
        ffn = peer_expert(x1, norm_ffn[i], bf(peer_u[i]), vt, lam, cc, rho, e1, tm=512)

        xf = ple_final(x1, ffn, p[i].reshape(t, -1), norm_ple[i], norm_final,
                       bf(ple_w_gate[i]), bf(ple_w_proj[i]), tm=256)
    return xf.reshape(bsz, seq, d)
```

```python
import functools
import math

import jax
import jax.numpy as jnp
from jax import lax
from jax.experimental import pallas as pl
from jax.experimental.pallas import tpu as pltpu

F32 = jnp.float32
BF16 = jnp.bfloat16
HIGHEST = lax.Precision.HIGHEST

LANES = 128
SUBLANES = 8
VMEM_LIMIT = 56 * 1024 * 1024

EPS = 1e-6
RW_HEADS = 8
RW_HEAD_DIM = 64
RW_DIM = RW_HEADS * RW_HEAD_DIM
RW_GN_EPS = 64e-5
SCAN_CHUNK = 64
SEG_SUM_WIDTH = 256

MLA_HEADS = 8
QK_NOPE = 64
QK_ROPE = 32
V_HEAD = 64
Q_LORA = 384
KV_LORA = 256
ROPE_THETA = 10000.0
MASK_CHUNK = 64
NEG_INF = -1e30

PEER_HEADS = 8
N_KEYS = 128
PEER_TOPK = 16
HALF_Q = 128


def _cparams(*sem):
    return pltpu.CompilerParams(dimension_semantics=sem, vmem_limit_bytes=VMEM_LIMIT)


def _dot(a, b, precision=None):
    return jnp.dot(a, b, preferred_element_type=F32, precision=precision)


def _dot_nt(a, b, precision=None):
    return lax.dot_general(a, b, (((1,), (1,)), ((), ())),
                           preferred_element_type=F32, precision=precision)


def _rms(x, gain):
    return x * lax.rsqrt(jnp.mean(x * x, axis=-1, keepdims=True) + EPS) * gain


def _sigmoid(x):
    return 1.0 / (1.0 + jnp.exp(-x))


def _seg_sum(x, bd):
    hi = x.astype(BF16)
    lo = (x - hi.astype(F32)).astype(BF16)
    w = bd.shape[0]
    parts = [_dot(hi[:, c:c + w], bd) + _dot(lo[:, c:c + w], bd) for c in range(0, x.shape[1], w)]
    return jnp.concatenate(parts, axis=1)


def _norm_matmul_kernel(x_ref, g_ref, w_ref, o_ref, h_ref, *, sigmoid_blocks):
    j = pl.program_id(1)

    @pl.when(j == 0)
    def _():
        h_ref[...] = _rms(x_ref[...], g_ref[...]).astype(BF16)

    def project(act):
        o_ref[...] = act(_dot(h_ref[...], w_ref[...])).astype(o_ref.dtype)

    if sigmoid_blocks is None:
        project(lambda y: y)
    else:
        gated = jnp.logical_and(j >= sigmoid_blocks[0], j < sigmoid_blocks[1])
        pl.when(gated)(lambda: project(_sigmoid))
        pl.when(jnp.logical_not(gated))(lambda: project(lambda y: y))


def norm_matmul(x, gain, w, *, tm, tn, sigmoid_blocks=None, out_dtype=F32):
    t, d = x.shape
    n = w.shape[1]
    return pl.pallas_call(
        functools.partial(_norm_matmul_kernel, sigmoid_blocks=sigmoid_blocks),
        grid=(t // tm, n // tn),
        in_specs=[pl.BlockSpec((tm, d), lambda i, j: (i, 0)),
                  pl.BlockSpec((1, d), lambda i, j: (0, 0)),
                  pl.BlockSpec((d, tn), lambda i, j: (0, j))],
        out_specs=pl.BlockSpec((tm, tn), lambda i, j: (i, j)),
        out_shape=jax.ShapeDtypeStruct((t, n), out_dtype),
        scratch_shapes=[pltpu.VMEM((tm, d), BF16)],
        compiler_params=_cparams("parallel", "arbitrary"),
        name="norm_matmul",
    )(x, gain.reshape(1, d), w)


RW_ZCOLS = 3 * RW_DIM + 4 * LANES


def _rwkv_prep_kernel(z_ref, zp_ref, mu_ref, w0_ref, w2_ref, a0_ref, a2_ref, g2_ref,
                      kk_ref, ka_ref, bd_ref,
                      r_ref, wl_ref, k_ref, v_ref, a_ref, b_ref, g_ref):
    z = z_ref[0]
    tm = z.shape[0]
    prev_last = zp_ref[0][SUBLANES - 1:SUBLANES, :]
    prev_last = jnp.where(pl.program_id(1) == 0, 0.0, prev_last)
    rolled = pltpu.roll(z, 1, 0)
    row = lax.broadcasted_iota(jnp.int32, (tm, 1), 0)
    z_prev = jnp.where(row == 0, prev_last, rolled)
    z = z + mu_ref[...] * (z_prev - z)

    o1, o2, o3 = RW_DIM, 2 * RW_DIM, 3 * RW_DIM
    r, k, v = z[:, :o1], z[:, o1:o2], z[:, o2:o3]
    zw, za, zg = (z[:, o3 + c * LANES:o3 + (c + 1) * LANES] for c in range(3))

    wpre = w0_ref[...] + _dot(jnp.tanh(zw), w2_ref[...])
    nx = -wpre
    softplus = jnp.maximum(nx, 0.0) + jnp.log(1.0 + jnp.exp(-jnp.abs(nx)))
    w = -softplus - 0.5
    iclr = _sigmoid(a0_ref[...] + _dot(za, a2_ref[...]))
    g = _dot(_sigmoid(zg), g2_ref[...])

    kk = k * kk_ref[...]
    ss = _seg_sum(kk * kk, bd_ref[...])
    kk = kk / jnp.maximum(jnp.sqrt(ss), 1e-12)

    r_ref[0] = r
    wl_ref[0] = -jnp.exp(w)
    k_ref[0] = k * (1.0 + (iclr - 1.0) * ka_ref[...])
    v_ref[0] = v
    a_ref[0] = -kk
    b_ref[0] = kk * iclr
    g_ref[0] = g


def rwkv_prep(z, mu, w0, w2, a0, a2, g2, k_k, k_a, bd, *, tm):
    bsz, seq, _ = z.shape
    zc = RW_ZCOLS
    d = RW_DIM
    row = lambda a: a.reshape(1, -1)
    full = lambda a: pl.BlockSpec(a.shape, lambda b, i: (0,) * a.ndim)
    args = (row(mu), row(w0), w2, row(a0), a2, g2, row(k_k), row(k_a), bd)
    out = jax.ShapeDtypeStruct((bsz, seq, d), F32)
    ospec = pl.BlockSpec((1, tm, d), lambda b, i: (b, i, 0))
    return pl.pallas_call(
        _rwkv_prep_kernel,
        grid=(bsz, seq // tm),
        in_specs=[pl.BlockSpec((1, tm, zc), lambda b, i: (b, i, 0)),
                  pl.BlockSpec((1, SUBLANES, zc),
                               lambda b, i: (b, jnp.maximum(i * (tm // SUBLANES) - 1, 0), 0))]
                 + [full(a) for a in args],
        out_specs=[ospec] * 7,
        out_shape=[out] * 7,
        compiler_params=_cparams("parallel", "arbitrary"),
        name="rwkv_prep",
    )(z, z, *args)


def _rwkv_scan_kernel(r_ref, wl_ref, k_ref, v_ref, a_ref, b_ref, y_ref, g_ref):
    c = SCAN_CHUNK
    hd = RW_HEAD_DIM
    npair = g_ref.shape[0]

    @pl.when(pl.program_id(1) == 0)
    def _():
        g_ref[...] = jnp.zeros_like(g_ref)

    ri = lax.broadcasted_iota(jnp.int32, (c, c), 0)
    ci = lax.broadcasted_iota(jnp.int32, (c, c), 1)
    tril = (ri >= ci).astype(F32)
    lane = lax.broadcasted_iota(jnp.int32, (c, 2 * hd), 1)
    m0 = lane < hd
    r2 = lax.broadcasted_iota(jnp.int32, (2 * c, 2 * c), 0)
    c2 = lax.broadcasted_iota(jnp.int32, (2 * c, 2 * c), 1)
    same = (r2 >= c) == (c2 >= c)
    strict = jnp.logical_and(same, r2 > c2)
    incl = jnp.logical_and(same, r2 >= c2)
    eye = (r2 == c2).astype(F32)

    bf = lambda x: x.astype(BF16)
    stack = lambda x: bf(jnp.concatenate([jnp.where(m0, x, 0.0), jnp.where(m0, 0.0, x)], axis=0))
    twice = lambda x: bf(jnp.concatenate([x, x], axis=0))
    pick = lambda s: jnp.where(m0, s[:c], s[c:])

    pairs = range(npair)
    each = lambda f, *cols: [f(*args) for args in zip(*cols)]
    per_row = r_ref.shape[2] // LANES
    where = [(hp // per_row, slice((hp % per_row) * LANES, (hp % per_row + 1) * LANES)) for hp in pairs]
    load = lambda ref: [ref[bi, :, sl] for bi, sl in where]
    r, wl, k, v, a, b = (load(x) for x in (r_ref, wl_ref, k_ref, v_ref, a_ref, b_ref))

    cs = each(lambda w: _dot(tril, w, HIGHEST), wl)
    cs_last = each(lambda s: s[c - 1:c, :], cs)
    p_inv = each(lambda s: jnp.exp(-s), cs)
    at = each(lambda a, s, w: a * jnp.exp(s - w), a, cs, wl)
    rt = each(lambda r, s: r * jnp.exp(s), r, cs)
    at_s, rt_s = each(stack, at), each(stack, rt)
    bt_s = each(lambda b, p: stack(b * p), b, p_inv)
    kt_s = each(lambda k, p: stack(k * p), k, p_inv)

    ab = each(lambda x, y: jnp.where(strict, _dot_nt(x, y), 0.0), at_s, bt_s)
    ak = each(lambda x, y: jnp.where(strict, _dot_nt(x, y), 0.0), at_s, kt_s)
    rb = each(lambda x, y: jnp.where(incl, _dot_nt(x, y), 0.0), rt_s, bt_s)
    rk = each(lambda x, y: jnp.where(incl, _dot_nt(x, y), 0.0), rt_s, kt_s)

    tinv = each(lambda m: eye + m, ab)
    x = each(bf, ab)
    for _ in range(int(math.log2(c)) - 1):
        x = each(lambda m: bf(_dot(m, m)), x)
        tinv = each(lambda t, m: t + _dot(bf(t), m), tinv, x)

    gt = [g_ref[hp] for hp in pairs]
    gtb = each(bf, gt)
    vv = each(twice, v)
    rhs = each(lambda at, g, ak, vv: _dot_nt(bf(at), g) + pick(_dot(bf(ak), vv)), at, gtb, ak, vv)
    u = each(lambda t, x: pick(_dot(bf(t), twice(x))), tinv, rhs)
    y = each(lambda rt, g, rb, u, rk, vv:
             _dot_nt(bf(rt), g) + pick(_dot(bf(rb), twice(u)) + _dot(bf(rk), vv)),
             rt, gtb, rb, u, rk, vv)
    upd = each(lambda u, b, v, k, s, sl:
               _dot(bf(u.T), bf(b * jnp.exp(sl - s))) + _dot(bf(v.T), bf(k * jnp.exp(sl - s))),
               u, b, v, k, cs, cs_last)
    for hp in pairs:
        bi, sl = where[hp]
        y_ref[bi, :, sl] = y[hp]
        g_ref[hp] = gt[hp] * jnp.exp(cs_last[hp]) + jnp.where(same, upd[hp], 0.0)


def rwkv_scan(r, wl, k, v, a, b, *, rows):
    bsz, seq, d = r.shape
    c = SCAN_CHUNK
    spec = pl.BlockSpec((rows, c, d), lambda bi, ci: (bi, ci, 0))
    return pl.pallas_call(
        _rwkv_scan_kernel,
        grid=(bsz // rows, seq // c),
        in_specs=[spec] * 6,
        out_specs=spec,
        out_shape=jax.ShapeDtypeStruct((bsz, seq, d), F32),
        scratch_shapes=[pltpu.VMEM((rows * d // LANES, LANES, LANES), F32)],
        compiler_params=_cparams("parallel", "arbitrary"),
        name="rwkv_scan",
    )(r, wl, k, v, a, b)


MLA_ZCOLS = 1024
MLA_SCALE = math.log2(math.e) / math.sqrt(QK_NOPE + QK_ROPE)
ATTN_TILE = 256


def _mla_prep_kernel(z_ref, pos_ref, fl_ref, qn_ref, kvn_ref, wq_ref, wqs_ref, wk_ref, wv_ref,
                     q_ref, k_ref, v_ref):
    z = z_ref[...]
    c_q = _rms(z[:, :Q_LORA], qn_ref[...]).astype(BF16)
    c_kv = _rms(z[:, Q_LORA:Q_LORA + KV_LORA], kvn_ref[...]).astype(BF16)
    kr = z[:, Q_LORA + KV_LORA:Q_LORA + KV_LORA + LANES]
    krs = z[:, Q_LORA + KV_LORA + LANES:Q_LORA + KV_LORA + 2 * LANES]

    ang = pos_ref[...].astype(F32) * fl_ref[...]
    cos, sin = jnp.cos(ang), jnp.sin(ang)
    kr_rot = kr * cos + krs * sin

    q = _dot(c_q, wq_ref[...])
    qs = _dot(c_q, wqs_ref[...])
    kn = _dot(c_kv, wk_ref[...])
    v = _dot(c_kv, wv_ref[...])
    for c in range(v_ref.shape[0]):
        v_ref[c] = v[c * ATTN_TILE:(c + 1) * ATTN_TILE, :].T.astype(BF16)
    for h in range(MLA_HEADS):
        sl = slice(h * LANES, (h + 1) * LANES)
        q_ref[:, sl] = ((q[:, sl] * cos + qs[:, sl] * sin) * MLA_SCALE).astype(BF16)
        k_ref[:, sl] = (kn[:, sl] + kr_rot).astype(BF16)


def mla_prep(z, zblock, pos, f_lane, q_norm, kv_norm, wq, wqs, wk, wv, *, tm):
    t = z.shape[0]
    n = MLA_HEADS * LANES
    full = lambda a: pl.BlockSpec(a.shape, lambda i: (0,) * a.ndim)
    args = (f_lane, q_norm.reshape(1, -1), kv_norm.reshape(1, -1), wq, wqs, wk, wv)
    out = jax.ShapeDtypeStruct((t, n), BF16)
    nv = wv.shape[1]
    out_vt = jax.ShapeDtypeStruct((t // ATTN_TILE, nv, ATTN_TILE), BF16)
    ospec = pl.BlockSpec((tm, n), lambda i: (i, 0))
    vspec = pl.BlockSpec((tm // ATTN_TILE, nv, ATTN_TILE), lambda i: (i, 0, 0))
    return pl.pallas_call(
        _mla_prep_kernel,
        grid=(t // tm,),
        in_specs=[pl.BlockSpec((tm, MLA_ZCOLS), lambda i: (i, zblock)),
                  pl.BlockSpec((tm, 1), lambda i: (i, 0))] + [full(a) for a in args],
        out_specs=[ospec, ospec, vspec],
        out_shape=[out, out, out_vt],
        compiler_params=_cparams("parallel"),
        name="mla_prep",
    )(z, pos, *args)


ATTN_HEADS_PER_STEP = 8


def _attn_kernel(q_ref, k_ref, vt_ref, o_ref):
    tq = ATTN_TILE
    iq = pl.program_id(2)
    nh = q_ref.shape[2] // LANES
    heads = [slice(h * LANES, (h + 1) * LANES) for h in range(nh)]
    qs = [q_ref[0, :, sl] for sl in heads]

    def tile(j, carry, masked):
        start = pl.multiple_of(j * tq, tq)
        if masked:
            kc = lax.broadcasted_iota(jnp.int32, (tq, tq), 0) // MASK_CHUNK
            qc = lax.broadcasted_iota(jnp.int32, (tq, tq), 1) // MASK_CHUNK
            keep = kc <= qc
        hs = range(nh)
        s = [_dot_nt(k_ref[0, pl.ds(start, tq), heads[h]], qs[h]) for h in hs]
        if masked:
            s = [jnp.where(keep, x, NEG_INF) for x in s]
        m_new = [jnp.maximum(carry[h][0], jnp.max(s[h], axis=0, keepdims=True)) for h in hs]
        alpha = [jnp.exp2(carry[h][0] - m_new[h]) for h in hs]
        p = [jnp.exp2(s[h] - m_new[h]) for h in hs]
        l = [alpha[h] * carry[h][1] + jnp.sum(p[h], axis=0, keepdims=True) for h in hs]
        pv = [_dot(vt_ref[0, j, h * V_HEAD:(h + 1) * V_HEAD, :], p[h].astype(BF16))
              for h in hs]
        return tuple((m_new[h], l[h], alpha[h] * carry[h][2] + pv[h]) for h in hs)

    init = tuple((jnp.full((1, tq), NEG_INF, F32), jnp.zeros((1, tq), F32), jnp.zeros((V_HEAD, tq), F32))
                 for _ in heads)
    carry = lax.fori_loop(0, iq, lambda j, c: tile(j, c, False), init)
    carry = tile(iq, carry, True)
    out = [acc / l for _, l, acc in carry]
    for pair in range(nh // 2):
        both = jnp.concatenate(out[2 * pair:2 * pair + 2], axis=0)
        o_ref[0, :, pair * LANES:(pair + 1) * LANES] = both.T.astype(BF16)


def attention(q, k, vt):
    bsz, seq, n = q.shape
    tq = ATTN_TILE
    w = ATTN_HEADS_PER_STEP * LANES
    wv = ATTN_HEADS_PER_STEP * V_HEAD
    qspec = pl.BlockSpec((1, tq, w), lambda b, h, i: (b, i, h))
    kspec = pl.BlockSpec((1, seq, w), lambda b, h, i: (b, 0, h))
    vspec = pl.BlockSpec((1, seq // tq, wv, tq), lambda b, h, i: (b, 0, h, 0))
    return pl.pallas_call(
        _attn_kernel,
        grid=(bsz, n // w, seq // tq),
        in_specs=[qspec, kspec, vspec],
        out_specs=pl.BlockSpec((1, tq, wv), lambda b, h, i: (b, i, h)),
        out_shape=jax.ShapeDtypeStruct((bsz, seq, vt.shape[2]), BF16),
        compiler_params=_cparams("parallel", "parallel", "arbitrary"),
        name="attention",
    )(q, k, vt)


def _combine_kernel(x_ref, y_ref, r_ref, k_ref, v_ref, g_ref, o_ref, gates_ref,
                    bd_ref, rk_ref, gnw_ref, gnb_ref, wa_ref, wb_ref, wo_ref, gq_ref, wq_ref,
                    out_ref, qp_ref):
    bd = bd_ref[...]
    inv_n = 1.0 / RW_HEAD_DIM
    y = y_ref[...]
    v = v_ref[...]
    mean = _seg_sum(y, bd) * inv_n
    yc = y - mean
    var = _seg_sum(yc * yc, bd) * inv_n
    yn = yc * lax.rsqrt(var + RW_GN_EPS) * gnw_ref[...] + gnb_ref[...]
    bonus = _seg_sum(r_ref[...] * k_ref[...] * rk_ref[...], bd) * v
    ya = _dot(((yn + bonus) * g_ref[...]).astype(BF16), wa_ref[...])
    yb = _dot(o_ref[...], wb_ref[...])
    d = ya.shape[1]
    gates = gates_ref[...]
    mix = gates[:, :d] * ya + gates[:, d:] * yb
    x1 = x_ref[...] + _dot(mix.astype(BF16), wo_ref[...])
    out_ref[...] = x1
    qp_ref[...] = _dot(_rms(x1, gq_ref[...]).astype(BF16), wq_ref[...]).astype(BF16)


def combine(x, y, r, k, v, g, o, z, gate_block, bd, r_k, gn_w, gn_b, wa, wb, wo, g_ffn, w_q, *, tm):
    t, d = x.shape
    row = lambda a: a.reshape(1, -1)
    tok = lambda a: pl.BlockSpec((tm, a.shape[1]), lambda i: (i, 0))
    full = lambda a: pl.BlockSpec(a.shape, lambda i: (0,) * a.ndim)
    toks = (x, y, r, k, v, g, o, z)
    consts = (bd, row(r_k), row(gn_w), row(gn_b), wa, wb, wo, row(g_ffn), w_q)
    gate_spec = pl.BlockSpec((tm, 2 * d), lambda i: (i, gate_block))
    nq = w_q.shape[1]
    return pl.pallas_call(
        _combine_kernel,
        grid=(t // tm,),
        in_specs=[tok(a) for a in toks[:-1]] + [gate_spec] + [full(a) for a in consts],
        out_specs=[pl.BlockSpec((tm, d), lambda i: (i, 0)), pl.BlockSpec((tm, nq), lambda i: (i, 0))],
        out_shape=[jax.ShapeDtypeStruct((t, d), F32), jax.ShapeDtypeStruct((t, nq), BF16)],
        compiler_params=_cparams("parallel"),
        name="combine",
    )(*toks, *consts)


NOT_RANKED = 127.0
RANK_BLOCKS_PER_LOOP = 2
STAIR = tuple(PEER_TOPK // (ii + 1) for ii in range(PEER_TOPK))


def _top_ranks(s, k, exact_ties):
    n, t = s.shape
    key = lax.broadcasted_iota(jnp.int32, (n, LANES), 0)
    slot = lax.broadcasted_iota(jnp.int32, (k, LANES), 0)

    def one(r, s, rank, top):
        m = jnp.max(s, axis=0, keepdims=True)
        hit = s == m
        if exact_ties:
            hit = key == jnp.min(jnp.where(hit, key, n), axis=0, keepdims=True)
        rank = jnp.where(hit, jnp.asarray(r, F32), rank)
        s = jnp.where(hit, -jnp.inf, s)
        top = jnp.where(slot == r, m, top)
        return s, rank, top

    def body(r, carry):
        return tuple(one(r, *c) for c in carry)

    blocks = [s[:, c * LANES:(c + 1) * LANES] for c in range(t // LANES)]
    tops, ranks = [], []
    for g in range(0, len(blocks), RANK_BLOCKS_PER_LOOP):
        init = tuple((b, jnp.full((n, LANES), NOT_RANKED, F32), jnp.zeros((k, LANES), F32))
                     for b in blocks[g:g + RANK_BLOCKS_PER_LOOP])
        for _, rank, top in lax.fori_loop(0, k, body, init):
            tops.append(top)
            ranks.append(rank)
    return jnp.concatenate(tops, axis=1), jnp.concatenate(ranks, axis=1)


def _ranked_excess(rank, k):
    count = jnp.sum((rank < NOT_RANKED).astype(F32), axis=0, keepdims=True)
    return jnp.abs(count - k)


def _peer_route_kernel(q_ref, keys_ref, lam_ref, cc_ref, rho_ref, e1_ref):
    tm = q_ref.shape[0]
    k = PEER_TOPK
    neg = -jnp.inf

    def head(h, _):
        col = pl.multiple_of(h * 2 * HALF_Q, 2 * HALF_Q)
        s0 = _dot_nt(keys_ref[h, 0], q_ref[:, pl.ds(col, HALF_Q)])
        s1 = _dot_nt(keys_ref[h, 1], q_ref[:, pl.ds(col + HALF_Q, HALF_Q)])
        def rank_all(exact_ties):
            top0, rank0 = _top_ranks(s0, k, exact_ties)
            top1, rank1 = _top_ranks(s1, k, exact_ties)
            row8 = lax.broadcasted_iota(jnp.int32, (SUBLANES, tm), 0)
            groups = [top0[0:1] + top1[0:8], top0[0:1] + top1[8:16], top0[1:2] + top1[0:8]]
            for ii in range(2, 8):
                groups.append(jnp.where(row8 < STAIR[ii], top0[ii:ii + 1] + top1[0:8], neg))
            groups.append(top0[8:16] + top1[0:1])
            cand = jnp.concatenate(groups, axis=0)
            _, crank = _top_ranks(cand, k, exact_ties)
            return top0, rank0, top1, rank1, cand, crank

        quick = rank_all(False)
        excess = _ranked_excess(quick[1], k) + _ranked_excess(quick[3], k) + _ranked_excess(quick[5], k)
        top0, rank0, top1, rank1, cand, crank = lax.cond(
            jnp.max(excess) == 0.0, lambda: quick, lambda: rank_all(True))
        sel = crank < NOT_RANKED
        ex = jnp.where(sel, jnp.exp(cand - cand[0:1]), 0.0)
        z = jnp.sum(ex, axis=0, keepdims=True)
        self = sel.astype(F32)
        counts = [jnp.sum(self[0:16], axis=0, keepdims=True)]
        for g in range(2, 9):
            counts.append(jnp.sum(self[8 * g:8 * g + 8], axis=0, keepdims=True))
        lvec = jnp.concatenate(counts + [self[72:80]], axis=0)

        lam = jnp.zeros((N_KEYS, tm), F32)
        for ii in range(k):
            lam = jnp.where(rank0 == float(ii), lvec[ii:ii + 1], lam)
        lam_ref[h] = lam.astype(BF16)
        cc_ref[h] = (jnp.exp(s0 - top0[0:1]) * (0.5 / z)).astype(BF16)
        rho_ref[h] = rank1.astype(BF16)
        e1_ref[h] = jnp.exp(s1 - top1[0:1]).astype(BF16)
        return 0

    lax.fori_loop(0, PEER_HEADS, head, 0)


def peer_route(qp, keys, *, tm):
    t = qp.shape[0]
    out_b = jax.ShapeDtypeStruct((PEER_HEADS, N_KEYS, t), BF16)
    ospec = pl.BlockSpec((PEER_HEADS, N_KEYS, tm), lambda i: (0, 0, i))
    return pl.pallas_call(
        _peer_route_kernel,
        grid=(t // tm,),
        in_specs=[pl.BlockSpec((tm, qp.shape[1]), lambda i: (i, 0)),
                  pl.BlockSpec(keys.shape, lambda i: (0, 0, 0, 0))],
        out_specs=[ospec] * 4,
        out_shape=[out_b] * 4,
        compiler_params=_cparams("parallel"),
        name="peer_route",
    )(qp, keys)


PEER_ROWS = 16
PEER_TOKEN_CHUNK = 256


def _erf(x):
    return lax.erf(x)


def _gelu_twice(x):
    return x * (1.0 + _erf(x * (1.0 / math.sqrt(2.0))))


def _peer_expert_kernel(x_ref, gn_ref, u_ref, vt_ref, lam_ref, cc_ref, rho_ref, e1_ref, out_ref,
                        xt_ref, acc_ref, pre0_ref, pre1_ref):
    j = pl.program_id(1)
    last = pl.num_programs(1) - 1
    tm = xt_ref.shape[1]
    chunks = [slice(c, c + PEER_TOKEN_CHUNK) for c in range(0, tm, PEER_TOKEN_CHUNK)]

    @pl.when(j == 0)
    def _():
        xt_ref[...] = _rms(x_ref[...], gn_ref[...]).T.astype(BF16)
        acc_ref[...] = jnp.zeros_like(acc_ref)

    def step(fill_ref, drain_ref):
        for cols in chunks if fill_ref is not None else ():
            fill_ref[:, cols] = _dot(u_ref[...], xt_ref[:, cols])
        for cols in chunks if drain_ref is not None else ():
            gs = []
            for ii in range(PEER_ROWS):
                rows = slice(ii * N_KEYS, (ii + 1) * N_KEYS)
                terms = []
                for h in range(PEER_HEADS):
                    lam = lam_ref[h, ii:ii + 1, cols]
                    cc = cc_ref[h, ii:ii + 1, cols]
                    terms.append(jnp.where(rho_ref[h, :, cols] < lam, e1_ref[h, :, cols] * cc,
                                           jnp.zeros((), BF16)))
                gs.append(_gelu_twice(drain_ref[rows, cols]).astype(BF16) * sum(terms[1:], terms[0]))
            acc_ref[:, cols] += _dot(vt_ref[0], jnp.concatenate(gs, axis=0))

    even = j % 2 == 0
    inner = jnp.logical_and(j > 0, j < last)
    pl.when(j == 0)(lambda: step(pre0_ref, None))
    pl.when(jnp.logical_and(inner, even))(lambda: step(pre0_ref, pre1_ref))
    pl.when(jnp.logical_and(inner, jnp.logical_not(even)))(lambda: step(pre1_ref, pre0_ref))
    pl.when(jnp.logical_and(j == last, even))(lambda: step(None, pre1_ref))
    pl.when(jnp.logical_and(j == last, jnp.logical_not(even)))(lambda: step(None, pre0_ref))

    @pl.when(j == last)
    def _():
        out_ref[...] = acc_ref[...].T


def peer_expert(x, gain, u, vt, lam, cc, rho, e1, *, tm):
    t, d = x.shape
    nrow = PEER_ROWS * N_KEYS
    nblk = u.shape[0] // nrow
    stage = lambda lag: (lambda j: jnp.clip(j - lag, 0, nblk - 1))
    rspec = pl.BlockSpec((PEER_HEADS, PEER_ROWS, tm), lambda i, j: (0, stage(1)(j), i))
    cspec = pl.BlockSpec((PEER_HEADS, N_KEYS, tm), lambda i, j: (0, 0, i))
    return pl.pallas_call(
        _peer_expert_kernel,
        grid=(t // tm, nblk + 1),
        in_specs=[pl.BlockSpec((tm, d), lambda i, j: (i, 0)),
                  pl.BlockSpec((1, d), lambda i, j: (0, 0)),
                  pl.BlockSpec((nrow, d), lambda i, j: (stage(0)(j), 0)),
                  pl.BlockSpec((1, d, nrow), lambda i, j: (stage(1)(j), 0, 0)),
                  rspec, rspec, cspec, cspec],
        out_specs=pl.BlockSpec((tm, d), lambda i, j: (i, 0)),
        out_shape=jax.ShapeDtypeStruct((t, d), F32),
        scratch_shapes=[pltpu.VMEM((d, tm), BF16), pltpu.VMEM((d, tm), F32),
                        pltpu.VMEM((nrow, tm), F32), pltpu.VMEM((nrow, tm), F32)],
        compiler_params=_cparams("parallel", "arbitrary"),
        name="peer_expert",
    )(x, gain.reshape(1, d), u, vt, lam, cc, rho, e1)


def _ple_final_kernel(x_ref, f_ref, p_ref, gp_ref, gf_ref, wg_ref, wp_ref, out_ref):
    x = x_ref[...] + f_ref[...]
    gate = _sigmoid(_dot(_rms(x, gp_ref[...]).astype(BF16), wg_ref[...]))
    x = x + gate * _dot(p_ref[...].astype(BF16), wp_ref[...])
    out_ref[...] = _rms(x, gf_ref[...])


def ple_final(x, f, p, g_ple, g_final, wg, wp, *, tm):
    t, d = x.shape
    tok = lambda a: pl.BlockSpec((tm, a.shape[1]), lambda i: (i, 0))
    full = lambda a: pl.BlockSpec(a.shape, lambda i: (0,) * a.ndim)
    consts = (g_ple.reshape(1, d), g_final.reshape(1, d), wg, wp)
    return pl.pallas_call(
        _ple_final_kernel,
        grid=(t // tm,),
        in_specs=[tok(x), tok(f), tok(p)] + [full(a) for a in consts],
        out_specs=tok(x),
        out_shape=jax.ShapeDtypeStruct((t, d), F32),
        compiler_params=_cparams("parallel"),
        name="ple_final",
    )(x, f, p, *consts)


def _place(cols, width, offset):
    return jnp.pad(cols, ((0, 0), (offset, width - offset - cols.shape[1])))


def _rw_in_weights(w_rw, mu):
    o3 = 3 * RW_DIM
    lw, la = 64, 64
    segs = [w_rw[:, :o3], _place(w_rw[:, o3:o3 + lw], LANES, 0),
            _place(w_rw[:, o3 + lw:o3 + lw + la], LANES, 0), _place(w_rw[:, o3 + lw + la:], 2 * LANES, 0)]
    mus = [mu[None, :o3], _place(mu[None, o3:o3 + lw], LANES, 0),
           _place(mu[None, o3 + lw:o3 + lw + la], LANES, 0), _place(mu[None, o3 + lw + la:], 2 * LANES, 0)]
    return jnp.concatenate(segs, axis=1), jnp.concatenate(mus, axis=1)[0]


def _mla_in_weights(w_mla):
    half = QK_ROPE // 2
    lat = Q_LORA + KV_LORA
    kr = w_mla[:, lat:]
    kr_sw = jnp.concatenate([-kr[:, half:], kr[:, :half]], axis=1)
    return jnp.concatenate([w_mla[:, :lat], _place(kr, LANES, QK_NOPE), _place(kr_sw, LANES, QK_NOPE),
                            jnp.zeros((w_mla.shape[0], LANES), w_mla.dtype)], axis=1)


def _mla_up_weights(w_uq, w_ukv):
    half = QK_ROPE // 2
    qd = QK_NOPE + QK_ROPE
    wq = w_uq.reshape(Q_LORA, MLA_HEADS, qd)
    rope = wq[:, :, QK_NOPE:]
    rope_sw = jnp.concatenate([-rope[:, :, half:], rope[:, :, :half]], axis=2)
    pad = lambda t, off: jnp.pad(t, ((0, 0), (0, 0), (off, LANES - off - t.shape[2])))
    wq_pad = pad(wq, 0).reshape(Q_LORA, MLA_HEADS * LANES)
    wq_sw = pad(rope_sw, QK_NOPE).reshape(Q_LORA, MLA_HEADS * LANES)
    lane = jnp.arange(MLA_HEADS * LANES) % LANES
    wk = jnp.where(lane < QK_NOPE, w_ukv, 0.0)
    wv = w_ukv.reshape(KV_LORA, MLA_HEADS, LANES)[:, :, QK_NOPE:].reshape(KV_LORA, MLA_HEADS * V_HEAD)
    return wq_pad, wq_sw, wk, wv


def kernel(x, p, positions, norm_mix, w_in, rw_mu, rw_w0, rw_w2, rw_a0, rw_a2, rw_g2, rw_k_k, rw_k_a, rw_r_k, rw_gn_w, rw_gn_b, rw_w_o, mla_q_norm, mla_w_uq, mla_kv_norm, mla_w_ukv, mla_w_o, w_out, norm_ffn, peer_w_q, peer_sub_keys, peer_u, peer_v, norm_ple, ple_w_gate, ple_w_proj, norm_final):
    bsz, seq, d = x.shape
    t = bsz * seq
    depth = p.shape[0]
    bf = lambda a: a.astype(BF16)
    rw_cols = 3 * RW_DIM + 64 + 64 + 128
    mla_cols = Q_LORA + KV_LORA + QK_ROPE

    head_of = jnp.arange(SEG_SUM_WIDTH) // RW_HEAD_DIM
    bd = bf(head_of[:, None] == head_of[None, :])
    inv_freq = ROPE_THETA ** (-jnp.arange(0, QK_ROPE, 2, dtype=F32) / QK_ROPE)
    f_lane = _place(jnp.concatenate([inv_freq, inv_freq])[None, :], LANES, QK_NOPE)
    pos = positions.reshape(t, 1)

    xf = x.reshape(t, d)
    assert depth == 1, "the final RMSNorm is fused into the layer's last kernel"
    for i in range(depth):
        w_rw, mu = _rw_in_weights(w_in[i][:, :rw_cols], rw_mu[i])
        w_mla = _mla_in_weights(w_in[i][:, rw_cols:rw_cols + mla_cols])
        w_gates = w_in[i][:, rw_cols + mla_cols:]
        tn = MLA_ZCOLS
        w_all = jnp.concatenate([w_rw, w_gates, w_mla], axis=1)
        gate_lo = RW_ZCOLS // tn
        gate_hi = gate_lo + w_gates.shape[1] // tn
        z = norm_matmul(xf, norm_mix[i], bf(w_all), tm=1024, tn=tn, sigmoid_blocks=(gate_lo, gate_hi))

        pad_rows = lambda w: jnp.pad(w, ((0, LANES - w.shape[0]), (0, 0)))
        r, wl, k, v, a, b, g = rwkv_prep(
            z.reshape(bsz, seq, -1), mu, rw_w0[i], pad_rows(rw_w2[i]), rw_a0[i],
            pad_rows(rw_a2[i]), rw_g2[i], rw_k_k[i], rw_k_a[i], bd, tm=256)
        y = rwkv_scan(r, wl, k, v, a, b, rows=math.gcd(bsz, 4))

        wq, wqs, wk, wv = _mla_up_weights(mla_w_uq[i], mla_w_ukv[i])
        q, kk, vv = mla_prep(z, gate_hi, pos, f_lane, mla_q_norm[i], mla_kv_norm[i],
                             bf(wq), bf(wqs), bf(wk), bf(wv), tm=512)
        n = MLA_HEADS * LANES
        nv = MLA_HEADS * V_HEAD
        o = attention(q.reshape(bsz, seq, n), kk.reshape(bsz, seq, n),
                      vv.reshape(bsz, seq // ATTN_TILE, nv, ATTN_TILE))

        flat = lambda a: a.reshape(t, -1)
        x1, qp = combine(xf, flat(y), flat(r), flat(k), flat(v), flat(g), flat(o), z, gate_lo * tn // (2 * d),
                         bd, rw_r_k[i].reshape(-1), rw_gn_w[i], rw_gn_b[i],
                         bf(rw_w_o[i]), bf(mla_w_o[i]), bf(w_out[i]), norm_ffn[i], bf(peer_w_q[i]), tm=256)
        lam, cc, rho, e1 = peer_route(qp, bf(peer_sub_keys[i]), tm=256)
        nrow = PEER_ROWS * N_KEYS
        vt = bf(peer_v[i]).reshape(-1, nrow, d).transpose(0, 2, 1)
        ffn = peer_expert(x1, norm_ffn[i], bf(peer_u[i]), vt, lam, cc, rho, e1, tm=512)

        xf = ple_final(x1, ffn, p[i].reshape(t, -1), norm_ple[i], norm_final,
                       bf(ple_w_gate[i]), bf(ple_w_proj[i]), tm=256)
    return xf.reshape(bsz, seq, d)
```

```python
import functools
import math

import jax
import jax.numpy as jnp
from jax import lax
from jax.experimental import pallas as pl
from jax.experimental.pallas import tpu as pltpu

F32 = jnp.float32
BF16 = jnp.bfloat16
HIGHEST = lax.Precision.HIGHEST

LANES = 128
SUBLANES = 8
VMEM_LIMIT = 56 * 1024 * 1024

EPS = 1e-6
RW_HEADS = 8
RW_HEAD_DIM = 64
RW_DIM = RW_HEADS * RW_HEAD_DIM
RW_GN_EPS = 64e-5
SCAN_CHUNK = 64
SEG_SUM_WIDTH = 256

MLA_HEADS = 8
QK_NOPE = 64
QK_ROPE = 32
V_HEAD = 64
Q_LORA = 384
KV_LORA = 256
ROPE_THETA = 10000.0
MASK_CHUNK = 64
NEG_INF = -1e30

PEER_HEADS = 8
N_KEYS = 128
PEER_TOPK = 16
HALF_Q = 128


def _cparams(*sem):
    return pltpu.CompilerParams(dimension_semantics=sem, vmem_limit_bytes=VMEM_LIMIT)


def _dot(a, b, precision=None):
    return jnp.dot(a, b, preferred_element_type=F32, precision=precision)


def _dot_nt(a, b, precision=None):
    return lax.dot_general(a, b, (((1,), (1,)), ((), ())),
                           preferred_element_type=F32, precision=precision)


def _rms(x, gain):
    return x * lax.rsqrt(jnp.mean(x * x, axis=-1, keepdims=True) + EPS) * gain


def _sigmoid(x):
    return 1.0 / (1.0 + jnp.exp(-x))


def _seg_sum(x, bd):
    hi = x.astype(BF16)
    lo = (x - hi.astype(F32)).astype(BF16)
    w = bd.shape[0]
    parts = [_dot(hi[:, c:c + w], bd) + _dot(lo[:, c:c + w], bd) for c in range(0, x.shape[1], w)]
    return jnp.concatenate(parts, axis=1)


def _norm_matmul_kernel(x_ref, g_ref, w_ref, o_ref, h_ref, *, sigmoid_blocks):
    j = pl.program_id(1)

    @pl.when(j == 0)
    def _():
        h_ref[...] = _rms(x_ref[...], g_ref[...]).astype(BF16)

    def project(act):
        o_ref[...] = act(_dot(h_ref[...], w_ref[...])).astype(o_ref.dtype)

    if sigmoid_blocks is None:
        project(lambda y: y)
    else:
        gated = jnp.logical_and(j >= sigmoid_blocks[0], j < sigmoid_blocks[1])
        pl.when(gated)(lambda: project(_sigmoid))
        pl.when(jnp.logical_not(gated))(lambda: project(lambda y: y))


def norm_matmul(x, gain, w, *, tm, tn, sigmoid_blocks=None, out_dtype=F32):
    t, d = x.shape
    n = w.shape[1]
    return pl.pallas_call(
        functools.partial(_norm_matmul_kernel, sigmoid_blocks=sigmoid_blocks),
        grid=(t // tm, n // tn),
        in_specs=[pl.BlockSpec((tm, d), lambda i, j: (i, 0)),
                  pl.BlockSpec((1, d), lambda i, j: (0, 0)),
                  pl.BlockSpec((d, tn), lambda i, j: (0, j))],
        out_specs=pl.BlockSpec((tm, tn), lambda i, j: (i, j)),
        out_shape=jax.ShapeDtypeStruct((t, n), out_dtype),
        scratch_shapes=[pltpu.VMEM((tm, d), BF16)],
        compiler_params=_cparams("parallel", "arbitrary"),
        name="norm_matmul",
    )(x, gain.reshape(1, d), w)


RW_ZCOLS = 3 * RW_DIM + 4 * LANES


def _rwkv_prep_kernel(z_ref, zp_ref, mu_ref, w0_ref, w2_ref, a0_ref, a2_ref, g2_ref,
                      kk_ref, ka_ref, bd_ref,
                      r_ref, wl_ref, k_ref, v_ref, a_ref, b_ref, g_ref):
    z = z_ref[0]
    tm = z.shape[0]
    prev_last = zp_ref[0][SUBLANES - 1:SUBLANES, :]
    prev_last = jnp.where(pl.program_id(1) == 0, 0.0, prev_last)
    rolled = pltpu.roll(z, 1, 0)
    row = lax.broadcasted_iota(jnp.int32, (tm, 1), 0)
    z_prev = jnp.where(row == 0, prev_last, rolled)
    z = z + mu_ref[...] * (z_prev - z)

    o1, o2, o3 = RW_DIM, 2 * RW_DIM, 3 * RW_DIM
    r, k, v = z[:, :o1], z[:, o1:o2], z[:, o2:o3]
    zw, za, zg = (z[:, o3 + c * LANES:o3 + (c + 1) * LANES] for c in range(3))

    wpre = w0_ref[...] + _dot(jnp.tanh(zw), w2_ref[...])
    nx = -wpre
    softplus = jnp.maximum(nx, 0.0) + jnp.log(1.0 + jnp.exp(-jnp.abs(nx)))
    w = -softplus - 0.5
    iclr = _sigmoid(a0_ref[...] + _dot(za, a2_ref[...]))
    g = _dot(_sigmoid(zg), g2_ref[...])

    kk = k * kk_ref[...]
    ss = _seg_sum(kk * kk, bd_ref[...])
    kk = kk / jnp.maximum(jnp.sqrt(ss), 1e-12)

    r_ref[0] = r
    wl_ref[0] = -jnp.exp(w)
    k_ref[0] = k * (1.0 + (iclr - 1.0) * ka_ref[...])
    v_ref[0] = v
    a_ref[0] = -kk
    b_ref[0] = kk * iclr
    g_ref[0] = g


def rwkv_prep(z, mu, w0, w2, a0, a2, g2, k_k, k_a, bd, *, tm):
    bsz, seq, _ = z.shape
    zc = RW_ZCOLS
    d = RW_DIM
    row = lambda a: a.reshape(1, -1)
    full = lambda a: pl.BlockSpec(a.shape, lambda b, i: (0,) * a.ndim)
    args = (row(mu), row(w0), w2, row(a0), a2, g2, row(k_k), row(k_a), bd)
    out = jax.ShapeDtypeStruct((bsz, seq, d), F32)
    ospec = pl.BlockSpec((1, tm, d), lambda b, i: (b, i, 0))
    return pl.pallas_call(
        _rwkv_prep_kernel,
        grid=(bsz, seq // tm),
        in_specs=[pl.BlockSpec((1, tm, zc), lambda b, i: (b, i, 0)),
                  pl.BlockSpec((1, SUBLANES, zc),
                               lambda b, i: (b, jnp.maximum(i * (tm // SUBLANES) - 1, 0), 0))]
                 + [full(a) for a in args],
        out_specs=[ospec] * 7,
        out_shape=[out] * 7,
        compiler_params=_cparams("parallel", "arbitrary"),
        name="rwkv_prep",
    )(z, z, *args)


def _rwkv_scan_kernel(r_ref, wl_ref, k_ref, v_ref, a_ref, b_ref, y_ref, g_ref):
    c = SCAN_CHUNK
    hd = RW_HEAD_DIM
    npair = g_ref.shape[0]

    @pl.when(pl.program_id(1) == 0)
    def _():
        g_ref[...] = jnp.zeros_like(g_ref)

    ri = lax.broadcasted_iota(jnp.int32, (c, c), 0)
    ci = lax.broadcasted_iota(jnp.int32, (c, c), 1)
    tril = (ri >= ci).astype(F32)
    lane = lax.broadcasted_iota(jnp.int32, (c, 2 * hd), 1)
    m0 = lane < hd
    r2 = lax.broadcasted_iota(jnp.int32, (2 * c, 2 * c), 0)
    c2 = lax.broadcasted_iota(jnp.int32, (2 * c, 2 * c), 1)
    same = (r2 >= c) == (c2 >= c)
    strict = jnp.logical_and(same, r2 > c2)
    incl = jnp.logical_and(same, r2 >= c2)
    eye = (r2 == c2).astype(F32)

    bf = lambda x: x.astype(BF16)
    stack = lambda x: bf(jnp.concatenate([jnp.where(m0, x, 0.0), jnp.where(m0, 0.0, x)], axis=0))
    twice = lambda x: bf(jnp.concatenate([x, x], axis=0))
    pick = lambda s: jnp.where(m0, s[:c], s[c:])

    pairs = range(npair)
    each = lambda f, *cols: [f(*args) for args in zip(*cols)]
    per_row = r_ref.shape[2] // LANES
    where = [(hp // per_row, slice((hp % per_row) * LANES, (hp % per_row + 1) * LANES)) for hp in pairs]
    load = lambda ref: [ref[bi, :, sl] for bi, sl in where]
    r, wl, k, v, a, b = (load(x) for x in (r_ref, wl_ref, k_ref, v_ref, a_ref, b_ref))

    cs = each(lambda w: _dot(tril, w, HIGHEST), wl)
    cs_last = each(lambda s: s[c - 1:c, :], cs)
    p_inv = each(lambda s: jnp.exp(-s), cs)
    at = each(lambda a, s, w: a * jnp.exp(s - w), a, cs, wl)
    rt = each(lambda r, s: r * jnp.exp(s), r, cs)
    at_s, rt_s = each(stack, at), each(stack, rt)
    bt_s = each(lambda b, p: stack(b * p), b, p_inv)
    kt_s = each(lambda k, p: stack(k * p), k, p_inv)

    ab = each(lambda x, y: jnp.where(strict, _dot_nt(x, y), 0.0), at_s, bt_s)
    ak = each(lambda x, y: jnp.where(strict, _dot_nt(x, y), 0.0), at_s, kt_s)
    rb = each(lambda x, y: jnp.where(incl, _dot_nt(x, y), 0.0), rt_s, bt_s)
    rk = each(lambda x, y: jnp.where(incl, _dot_nt(x, y), 0.0), rt_s, kt_s)

    tinv = each(lambda m: eye + m, ab)
    x = each(bf, ab)
    for _ in range(int(math.log2(c)) - 1):
        x = each(lambda m: bf(_dot(m, m)), x)
        tinv = each(lambda t, m: t + _dot(bf(t), m), tinv, x)

    gt = [g_ref[hp] for hp in pairs]
    gtb = each(bf, gt)
    vv = each(twice, v)
    rhs = each(lambda at, g, ak, vv: _dot_nt(bf(at), g) + pick(_dot(bf(ak), vv)), at, gtb, ak, vv)
    u = each(lambda t, x: pick(_dot(bf(t), twice(x))), tinv, rhs)
    y = each(lambda rt, g, rb, u, rk, vv:
             _dot_nt(bf(rt), g) + pick(_dot(bf(rb), twice(u)) + _dot(bf(rk), vv)),
             rt, gtb, rb, u, rk, vv)
    upd = each(lambda u, b, v, k, s, sl:
               _dot(bf(u.T), bf(b * jnp.exp(sl - s))) + _dot(bf(v.T), bf(k * jnp.exp(sl - s))),
               u, b, v, k, cs, cs_last)
    for hp in pairs:
        bi, sl = where[hp]
        y_ref[bi, :, sl] = y[hp]
        g_ref[hp] = gt[hp] * jnp.exp(cs_last[hp]) + jnp.where(same, upd[hp], 0.0)


def rwkv_scan(r, wl, k, v, a, b, *, rows):
    bsz, seq, d = r.shape
    c = SCAN_CHUNK
    spec = pl.BlockSpec((rows, c, d), lambda bi, ci: (bi, ci, 0))
    return pl.pallas_call(
        _rwkv_scan_kernel,
        grid=(bsz // rows, seq // c),
        in_specs=[spec] * 6,
        out_specs=spec,
        out_shape=jax.ShapeDtypeStruct((bsz, seq, d), F32),
        scratch_shapes=[pltpu.VMEM((rows * d // LANES, LANES, LANES), F32)],
        compiler_params=_cparams("parallel", "arbitrary"),
        name="rwkv_scan",
    )(r, wl, k, v, a, b)


MLA_ZCOLS = 1024
MLA_SCALE = math.log2(math.e) / math.sqrt(QK_NOPE + QK_ROPE)
ATTN_TILE = 256


def _mla_prep_kernel(z_ref, pos_ref, fl_ref, qn_ref, kvn_ref, wq_ref, wqs_ref, wk_ref, wv_ref,
                     q_ref, k_ref, v_ref):
    z = z_ref[...]
    c_q = _rms(z[:, :Q_LORA], qn_ref[...]).astype(BF16)
    c_kv = _rms(z[:, Q_LORA:Q_LORA + KV_LORA], kvn_ref[...]).astype(BF16)
    kr = z[:, Q_LORA + KV_LORA:Q_LORA + KV_LORA + LANES]
    krs = z[:, Q_LORA + KV_LORA + LANES:Q_LORA + KV_LORA + 2 * LANES]

    ang = pos_ref[...].astype(F32) * fl_ref[...]
    cos, sin = jnp.cos(ang), jnp.sin(ang)
    kr_rot = kr * cos + krs * sin

    q = _dot(c_q, wq_ref[...])
    qs = _dot(c_q, wqs_ref[...])
    kn = _dot(c_kv, wk_ref[...])
    v = _dot(c_kv, wv_ref[...])
    for c in range(v_ref.shape[0]):
        v_ref[c] = v[c * ATTN_TILE:(c + 1) * ATTN_TILE, :].T.astype(BF16)
    for h in range(MLA_HEADS):
        sl = slice(h * LANES, (h + 1) * LANES)
        q_ref[:, sl] = ((q[:, sl] * cos + qs[:, sl] * sin) * MLA_SCALE).astype(BF16)
        k_ref[:, sl] = (kn[:, sl] + kr_rot).astype(BF16)


def mla_prep(z, zblock, pos, f_lane, q_norm, kv_norm, wq, wqs, wk, wv, *, tm):
    t = z.shape[0]
    n = MLA_HEADS * LANES
    full = lambda a: pl.BlockSpec(a.shape, lambda i: (0,) * a.ndim)
    args = (f_lane, q_norm.reshape(1, -1), kv_norm.reshape(1, -1), wq, wqs, wk, wv)
    out = jax.ShapeDtypeStruct((t, n), BF16)
    nv = wv.shape[1]
    out_vt = jax.ShapeDtypeStruct((t // ATTN_TILE, nv, ATTN_TILE), BF16)
    ospec = pl.BlockSpec((tm, n), lambda i: (i, 0))
    vspec = pl.BlockSpec((tm // ATTN_TILE, nv, ATTN_TILE), lambda i: (i, 0, 0))
    return pl.pallas_call(
        _mla_prep_kernel,
        grid=(t // tm,),
        in_specs=[pl.BlockSpec((tm, MLA_ZCOLS), lambda i: (i, zblock)),
                  pl.BlockSpec((tm, 1), lambda i: (i, 0))] + [full(a) for a in args],
        out_specs=[ospec, ospec, vspec],
        out_shape=[out, out, out_vt],
        compiler_params=_cparams("parallel"),
        name="mla_prep",
    )(z, pos, *args)


ATTN_HEADS_PER_STEP = 8


def _attn_kernel(q_ref, k_ref, vt_ref, o_ref):
    tq = ATTN_TILE
    iq = pl.program_id(2)
    nh = q_ref.shape[2] // LANES
    heads = [slice(h * LANES, (h + 1) * LANES) for h in range(nh)]
    qs = [q_ref[0, :, sl] for sl in heads]

    def tile(j, carry, masked):
        start = pl.multiple_of(j * tq, tq)
        if masked:
            kc = lax.broadcasted_iota(jnp.int32, (tq, tq), 0) // MASK_CHUNK
            qc = lax.broadcasted_iota(jnp.int32, (tq, tq), 1) // MASK_CHUNK
            keep = kc <= qc
        hs = range(nh)
        s = [_dot_nt(k_ref[0, pl.ds(start, tq), heads[h]], qs[h]) for h in hs]
        if masked:
            s = [jnp.where(keep, x, NEG_INF) for x in s]
        m_new = [jnp.maximum(carry[h][0], jnp.max(s[h], axis=0, keepdims=True)) for h in hs]
        alpha = [jnp.exp2(carry[h][0] - m_new[h]) for h in hs]
        p = [jnp.exp2(s[h] - m_new[h]) for h in hs]
        l = [alpha[h] * carry[h][1] + jnp.sum(p[h], axis=0, keepdims=True) for h in hs]
        pv = [_dot(vt_ref[0, j, h * V_HEAD:(h + 1) * V_HEAD, :], p[h].astype(BF16))
              for h in hs]
        return tuple((m_new[h], l[h], alpha[h] * carry[h][2] + pv[h]) for h in hs)

    init = tuple((jnp.full((1, tq), NEG_INF, F32), jnp.zeros((1, tq), F32), jnp.zeros((V_HEAD, tq), F32))
                 for _ in heads)
    carry = lax.fori_loop(0, iq, lambda j, c: tile(j, c, False), init)
    carry = tile(iq, carry, True)
    out = [acc / l for _, l, acc in carry]
    for pair in range(nh // 2):
        both = jnp.concatenate(out[2 * pair:2 * pair + 2], axis=0)
        o_ref[0, :, pair * LANES:(pair + 1) * LANES] = both.T.astype(BF16)


def attention(q, k, vt):
    bsz, seq, n = q.shape
    tq = ATTN_TILE
    w = ATTN_HEADS_PER_STEP * LANES
    wv = ATTN_HEADS_PER_STEP * V_HEAD
    qspec = pl.BlockSpec((1, tq, w), lambda b, h, i: (b, i, h))
    kspec = pl.BlockSpec((1, seq, w), lambda b, h, i: (b, 0, h))
    vspec = pl.BlockSpec((1, seq // tq, wv, tq), lambda b, h, i: (b, 0, h, 0))
    return pl.pallas_call(
        _attn_kernel,
        grid=(bsz, n // w, seq // tq),
        in_specs=[qspec, kspec, vspec],
        out_specs=pl.BlockSpec((1, tq, wv), lambda b, h, i: (b, i, h)),
        out_shape=jax.ShapeDtypeStruct((bsz, seq, vt.shape[2]), BF16),
        compiler_params=_cparams("parallel", "parallel", "arbitrary"),
        name="attention",
    )(q, k, vt)


def _combine_kernel(x_ref, y_ref, r_ref, k_ref, v_ref, g_ref, o_ref, gates_ref,
                    bd_ref, rk_ref, gnw_ref, gnb_ref, wa_ref, wb_ref, wo_ref, gq_ref, wq_ref,
                    out_ref, qp_ref):
    bd = bd_ref[...]
    inv_n = 1.0 / RW_HEAD_DIM
    y = y_ref[...]
    v = v_ref[...]
    mean = _seg_sum(y, bd) * inv_n
    yc = y - mean
    var = _seg_sum(yc * yc, bd) * inv_n
    yn = yc * lax.rsqrt(var + RW_GN_EPS) * gnw_ref[...] + gnb_ref[...]
    bonus = _seg_sum(r_ref[...] * k_ref[...] * rk_ref[...], bd) * v
    ya = _dot(((yn + bonus) * g_ref[...]).astype(BF16), wa_ref[...])
    yb = _dot(o_ref[...], wb_ref[...])
    d = ya.shape[1]
    gates = gates_ref[...]
    mix = gates[:, :d] * ya + gates[:, d:] * yb
    x1 = x_ref[...] + _dot(mix.astype(BF16), wo_ref[...])
    out_ref[...] = x1
    qp_ref[...] = _dot(_rms(x1, gq_ref[...]).astype(BF16), wq_ref[...]).astype(BF16)


def combine(x, y, r, k, v, g, o, z, gate_block, bd, r_k, gn_w, gn_b, wa, wb, wo, g_ffn, w_q, *, tm):
    t, d = x.shape
    row = lambda a: a.reshape(1, -1)
    tok = lambda a: pl.BlockSpec((tm, a.shape[1]), lambda i: (i, 0))
    full = lambda a: pl.BlockSpec(a.shape, lambda i: (0,) * a.ndim)
    toks = (x, y, r, k, v, g, o, z)
    consts = (bd, row(r_k), row(gn_w), row(gn_b), wa, wb, wo, row(g_ffn), w_q)
    gate_spec = pl.BlockSpec((tm, 2 * d), lambda i: (i, gate_block))
    nq = w_q.shape[1]
    return pl.pallas_call(
        _combine_kernel,
        grid=(t // tm,),
        in_specs=[tok(a) for a in toks[:-1]] + [gate_spec] + [full(a) for a in consts],
        out_specs=[pl.BlockSpec((tm, d), lambda i: (i, 0)), pl.BlockSpec((tm, nq), lambda i: (i, 0))],
        out_shape=[jax.ShapeDtypeStruct((t, d), F32), jax.ShapeDtypeStruct((t, nq), BF16)],
        compiler_params=_cparams("parallel"),
        name="combine",
    )(*toks, *consts)


NOT_RANKED = 127.0
RANK_BLOCKS_PER_LOOP = 2
STAIR = tuple(PEER_TOPK // (ii + 1) for ii in range(PEER_TOPK))


def _top_ranks(s, k, exact_ties):
    n, t = s.shape
    key = lax.broadcasted_iota(jnp.int32, (n, LANES), 0)
    slot = lax.broadcasted_iota(jnp.int32, (k, LANES), 0)

    def one(r, s, rank, top):
        m = jnp.max(s, axis=0, keepdims=True)
        hit = s == m
        if exact_ties:
            hit = key == jnp.min(jnp.where(hit, key, n), axis=0, keepdims=True)
        rank = jnp.where(hit, jnp.asarray(r, F32), rank)
        s = jnp.where(hit, -jnp.inf, s)
        top = jnp.where(slot == r, m, top)
        return s, rank, top

    def body(r, carry):
        return tuple(one(r, *c) for c in carry)

    blocks = [s[:, c * LANES:(c + 1) * LANES] for c in range(t // LANES)]
    tops, ranks = [], []
    for g in range(0, len(blocks), RANK_BLOCKS_PER_LOOP):
        init = tuple((b, jnp.full((n, LANES), NOT_RANKED, F32), jnp.zeros((k, LANES), F32))
                     for b in blocks[g:g + RANK_BLOCKS_PER_LOOP])
        for _, rank, top in lax.fori_loop(0, k, body, init):
            tops.append(top)
            ranks.append(rank)
    return jnp.concatenate(tops, axis=1), jnp.concatenate(ranks, axis=1)


def _ranked_excess(rank, k):
    count = jnp.sum((rank < NOT_RANKED).astype(F32), axis=0, keepdims=True)
    return jnp.abs(count - k)


def _peer_route_kernel(q_ref, keys_ref, lam_ref, cc_ref, rho_ref, e1_ref):
    tm = q_ref.shape[0]
    k = PEER_TOPK
    neg = -jnp.inf

    def head(h, _):
        col = pl.multiple_of(h * 2 * HALF_Q, 2 * HALF_Q)
        s0 = _dot_nt(keys_ref[h, 0], q_ref[:, pl.ds(col, HALF_Q)])
        s1 = _dot_nt(keys_ref[h, 1], q_ref[:, pl.ds(col + HALF_Q, HALF_Q)])
        def rank_all(exact_ties):
            top0, rank0 = _top_ranks(s0, k, exact_ties)
            top1, rank1 = _top_ranks(s1, k, exact_ties)
            row8 = lax.broadcasted_iota(jnp.int32, (SUBLANES, tm), 0)
            groups = [top0[0:1] + top1[0:8], top0[0:1] + top1[8:16], top0[1:2] + top1[0:8]]
            for ii in range(2, 8):
                groups.append(jnp.where(row8 < STAIR[ii], top0[ii:ii + 1] + top1[0:8], neg))
            groups.append(top0[8:16] + top1[0:1])
            cand = jnp.concatenate(groups, axis=0)
            _, crank = _top_ranks(cand, k, exact_ties)
            return top0, rank0, top1, rank1, cand, crank

        quick = rank_all(False)
        excess = _ranked_excess(quick[1], k) + _ranked_excess(quick[3], k) + _ranked_excess(quick[5], k)
        top0, rank0, top1, rank1, cand, crank = lax.cond(
            jnp.max(excess) == 0.0, lambda: quick, lambda: rank_all(True))
        sel = crank < NOT_RANKED
        ex = jnp.where(sel, jnp.exp(cand - cand[0:1]), 0.0)
        z = jnp.sum(ex, axis=0, keepdims=True)
        self = sel.astype(F32)
        counts = [jnp.sum(self[0:16], axis=0, keepdims=True)]
        for g in range(2, 9):
            counts.append(jnp.sum(self[8 * g:8 * g + 8], axis=0, keepdims=True))
        lvec = jnp.concatenate(counts + [self[72:80]], axis=0)

        lam = jnp.zeros((N_KEYS, tm), F32)
        for ii in range(k):
            lam = jnp.where(rank0 == float(ii), lvec[ii:ii + 1], lam)
        lam_ref[h] = lam.astype(BF16)
        cc_ref[h] = (jnp.exp(s0 - top0[0:1]) * (0.5 / z)).astype(BF16)
        rho_ref[h] = rank1.astype(BF16)
        e1_ref[h] = jnp.exp(s1 - top1[0:1]).astype(BF16)
        return 0

    lax.fori_loop(0, PEER_HEADS, head, 0)


def peer_route(qp, keys, *, tm):
    t = qp.shape[0]
    out_b = jax.ShapeDtypeStruct((PEER_HEADS, N_KEYS, t), BF16)
    ospec = pl.BlockSpec((PEER_HEADS, N_KEYS, tm), lambda i: (0, 0, i))
    return pl.pallas_call(
        _peer_route_kernel,
        grid=(t // tm,),
        in_specs=[pl.BlockSpec((tm, qp.shape[1]), lambda i: (i, 0)),
                  pl.BlockSpec(keys.shape, lambda i: (0, 0, 0, 0))],
        out_specs=[ospec] * 4,
        out_shape=[out_b] * 4,
        compiler_params=_cparams("parallel"),
        name="peer_route",
    )(qp, keys)


PEER_ROWS = 16
PEER_TOKEN_CHUNK = 256


def _erf(x):
    return lax.erf(x)


def _gelu_twice(x):
    return x * (1.0 + _erf(x * (1.0 / math.sqrt(2.0))))


def _peer_expert_kernel(x_ref, gn_ref, u_ref, vt_ref, lam_ref, cc_ref, rho_ref, e1_ref, out_ref,
                        xt_ref, acc_ref, pre0_ref, pre1_ref):
    j = pl.program_id(1)
    last = pl.num_programs(1) - 1
    tm = xt_ref.shape[1]
    chunks = [slice(c, c + PEER_TOKEN_CHUNK) for c in range(0, tm, PEER_TOKEN_CHUNK)]

    @pl.when(j == 0)
    def _():
        xt_ref[...] = _rms(x_ref[...], gn_ref[...]).T.astype(BF16)
        acc_ref[...] = jnp.zeros_like(acc_ref)

    def step(fill_ref, drain_ref):
        for cols in chunks if fill_ref is not None else ():
            fill_ref[:, cols] = _dot(u_ref[...], xt_ref[:, cols])
        for cols in chunks if drain_ref is not None else ():
            gs = []
            for ii in range(PEER_ROWS):
                rows = slice(ii * N_KEYS, (ii + 1) * N_KEYS)
                terms = []
                for h in range(PEER_HEADS):
                    lam = lam_ref[h, ii:ii + 1, cols]
                    cc = cc_ref[h, ii:ii + 1, cols]
                    terms.append(jnp.where(rho_ref[h, :, cols] < lam, e1_ref[h, :, cols] * cc,
                                           jnp.zeros((), BF16)))
                gs.append(_gelu_twice(drain_ref[rows, cols]).astype(BF16) * sum(terms[1:], terms[0]))
            acc_ref[:, cols] += _dot(vt_ref[0], jnp.concatenate(gs, axis=0))

    even = j % 2 == 0
    inner = jnp.logical_and(j > 0, j < last)
    pl.when(j == 0)(lambda: step(pre0_ref, None))
    pl.when(jnp.logical_and(inner, even))(lambda: step(pre0_ref, pre1_ref))
    pl.when(jnp.logical_and(inner, jnp.logical_not(even)))(lambda: step(pre1_ref, pre0_ref))
    pl.when(jnp.logical_and(j == last, even))(lambda: step(None, pre1_ref))
    pl.when(jnp.logical_and(j == last, jnp.logical_not(even)))(lambda: step(None, pre0_ref))

    @pl.when(j == last)
    def _():
        out_ref[...] = acc_ref[...].T


def peer_expert(x, gain, u, vt, lam, cc, rho, e1, *, tm):
    t, d = x.shape
    nrow = PEER_ROWS * N_KEYS
    nblk = u.shape[0] // nrow
    stage = lambda lag: (lambda j: jnp.clip(j - lag, 0, nblk - 1))
    rspec = pl.BlockSpec((PEER_HEADS, PEER_ROWS, tm), lambda i, j: (0, stage(1)(j), i))
    cspec = pl.BlockSpec((PEER_HEADS, N_KEYS, tm), lambda i, j: (0, 0, i))
    return pl.pallas_call(
        _peer_expert_kernel,
        grid=(t // tm, nblk + 1),
        in_specs=[pl.BlockSpec((tm, d), lambda i, j: (i, 0)),
                  pl.BlockSpec((1, d), lambda i, j: (0, 0)),
                  pl.BlockSpec((nrow, d), lambda i, j: (stage(0)(j), 0)),
                  pl.BlockSpec((1, d, nrow), lambda i, j: (stage(1)(j), 0, 0)),
                  rspec, rspec, cspec, cspec],
        out_specs=pl.BlockSpec((tm, d), lambda i, j: (i, 0)),
        out_shape=jax.ShapeDtypeStruct((t, d), F32),
        scratch_shapes=[pltpu.VMEM((d, tm), BF16), pltpu.VMEM((d, tm), F32),
                        pltpu.VMEM((nrow, tm), F32), pltpu.VMEM((nrow, tm), F32)],
        compiler_params=_cparams("parallel", "arbitrary"),
        name="peer_expert",
    )(x, gain.reshape(1, d), u, vt, lam, cc, rho, e1)


def _ple_final_kernel(x_ref, f_ref, p_ref, gp_ref, gf_ref, wg_ref, wp_ref, out_ref):
    x = x_ref[...] + f_ref[...]
    gate = _sigmoid(_dot(_rms(x, gp_ref[...]).astype(BF16), wg_ref[...]))
    x = x + gate * _dot(p_ref[...].astype(BF16), wp_ref[...])
    out_ref[...] = _rms(x, gf_ref[...])


def ple_final(x, f, p, g_ple, g_final, wg, wp, *, tm):
    t, d = x.shape
    tok = lambda a: pl.BlockSpec((tm, a.shape[1]), lambda i: (i, 0))
    full = lambda a: pl.BlockSpec(a.shape, lambda i: (0,) * a.ndim)
    consts = (g_ple.reshape(1, d), g_final.reshape(1, d), wg, wp)
    return pl.pallas_call(
        _ple_final_kernel,
        grid=(t // tm,),
        in_specs=[tok(x), tok(f), tok(p)] + [full(a) for a in consts],
        out_specs=tok(x),
        out_shape=jax.ShapeDtypeStruct((t, d), F32),
        compiler_params=_cparams("parallel"),
        name="ple_final",
    )(x, f, p, *consts)


def _place(cols, width, offset):
    return jnp.pad(cols, ((0, 0), (offset, width - offset - cols.shape[1])))


def _rw_in_weights(w_rw, mu):
    o3 = 3 * RW_DIM
    lw, la = 64, 64
    segs = [w_rw[:, :o3], _place(w_rw[:, o3:o3 + lw], LANES, 0),
            _place(w_rw[:, o3 + lw:o3 + lw + la], LANES, 0), _place(w_rw[:, o3 + lw + la:], 2 * LANES, 0)]
    mus = [mu[None, :o3], _place(mu[None, o3:o3 + lw], LANES, 0),
           _place(mu[None, o3 + lw:o3 + lw + la], LANES, 0), _place(mu[None, o3 + lw + la:], 2 * LANES, 0)]
    return jnp.concatenate(segs, axis=1), jnp.concatenate(mus, axis=1)[0]


def _mla_in_weights(w_mla):
    half = QK_ROPE // 2
    lat = Q_LORA + KV_LORA
    kr = w_mla[:, lat:]
    kr_sw = jnp.concatenate([-kr[:, half:], kr[:, :half]], axis=1)
    return jnp.concatenate([w_mla[:, :lat], _place(kr, LANES, QK_NOPE), _place(kr_sw, LANES, QK_NOPE),
                            jnp.zeros((w_mla.shape[0], LANES), w_mla.dtype)], axis=1)


def _mla_up_weights(w_uq, w_ukv):
    half = QK_ROPE // 2
    qd = QK_NOPE + QK_ROPE
    wq = w_uq.reshape(Q_LORA, MLA_HEADS, qd)
    rope = wq[:, :, QK_NOPE:]
    rope_sw = jnp.concatenate([-rope[:, :, half:], rope[:, :, :half]], axis=2)
    pad = lambda t, off: jnp.pad(t, ((0, 0), (0, 0), (off, LANES - off - t.shape[2])))
    wq_pad = pad(wq, 0).reshape(Q_LORA, MLA_HEADS * LANES)
    wq_sw = pad(rope_sw, QK_NOPE).reshape(Q_LORA, MLA_HEADS * LANES)
    lane = jnp.arange(MLA_HEADS * LANES) % LANES
    wk = jnp.where(lane < QK_NOPE, w_ukv, 0.0)
    wv = w_ukv.reshape(KV_LORA, MLA_HEADS, LANES)[:, :, QK_NOPE:].reshape(KV_LORA, MLA_HEADS * V_HEAD)
    return wq_pad, wq_sw, wk, wv


def kernel(x, p, positions, norm_mix, w_in, rw_mu, rw_w0, rw_w2, rw_a0, rw_a2, rw_g2, rw_k_k, rw_k_a, rw_r_k, rw_gn_w, rw_gn_b, rw_w_o, mla_q_norm, mla_w_uq, mla_kv_norm, mla_w_ukv, mla_w_o, w_out, norm_ffn, peer_w_q, peer_sub_keys, peer_u, peer_v, norm_ple, ple_w_gate, ple_w_proj, norm_final):
    bsz, seq, d = x.shape
    t = bsz * seq
    depth = p.shape[0]
    bf = lambda a: a.astype(BF16)
    rw_cols = 3 * RW_DIM + 64 + 64 + 128
    mla_cols = Q_LORA + KV_LORA + QK_ROPE

    head_of = jnp.arange(SEG_SUM_WIDTH) // RW_HEAD_DIM
    bd = bf(head_of[:, None] == head_of[None, :])
    inv_freq = ROPE_THETA ** (-jnp.arange(0, QK_ROPE, 2, dtype=F32) / QK_ROPE)
    f_lane = _place(jnp.concatenate([inv_freq, inv_freq])[None, :], LANES, QK_NOPE)
    pos = positions.reshape(t, 1)

    xf = x.reshape(t, d)
    assert depth == 1, "the final RMSNorm is fused into the layer's last kernel"
    for i in range(depth):
        w_rw, mu = _rw_in_weights(w_in[i][:, :rw_cols], rw_mu[i])
        w_mla = _mla_in_weights(w_in[i][:, rw_cols:rw_cols + mla_cols])
        w_gates = w_in[i][:, rw_cols + mla_cols:]
        tn = MLA_ZCOLS
        w_all = jnp.concatenate([w_rw, w_gates, w_mla], axis=1)
        gate_lo = RW_ZCOLS // tn
        gate_hi = gate_lo + w_gates.shape[1] // tn
        z = norm_matmul(xf, norm_mix[i], bf(w_all), tm=1024, tn=tn, sigmoid_blocks=(gate_lo, gate_hi))

        pad_rows = lambda w: jnp.pad(w, ((0, LANES - w.shape[0]), (0, 0)))
        r, wl, k, v, a, b, g = rwkv_prep(
            z.reshape(bsz, seq, -1), mu, rw_w0[i], pad_rows(rw_w2[i]), rw_a0[i],
            pad_rows(rw_a2[i]), rw_g2[i], rw_k_k[i], rw_k_a[i], bd, tm=512)
        y = rwkv_scan(r, wl, k, v, a, b, rows=math.gcd(bsz, 4))

        wq, wqs, wk, wv = _mla_up_weights(mla_w_uq[i], mla_w_ukv[i])
        q, kk, vv = mla_prep(z, gate_hi, pos, f_lane, mla_q_norm[i], mla_kv_norm[i],
                             bf(wq), bf(wqs), bf(wk), bf(wv), tm=512)
        n = MLA_HEADS * LANES
        nv = MLA_HEADS * V_HEAD
        o = attention(q.reshape(bsz, seq, n), kk.reshape(bsz, seq, n),
                      vv.reshape(bsz, seq // ATTN_TILE, nv, ATTN_TILE))

        flat = lambda a: a.reshape(t, -1)
        x1, qp = combine(xf, flat(y), flat(r), flat(k), flat(v), flat(g), flat(o), z, gate_lo * tn // (2 * d),
                         bd, rw_r_k[i].reshape(-1), rw_gn_w[i], rw_gn_b[i],
                         bf(rw_w_o[i]), bf(mla_w_o[i]), bf(w_out[i]), norm_ffn[i], bf(peer_w_q[i]), tm=256)
        lam, cc, rho, e1 = peer_route(qp, bf(peer_sub_keys[i]), tm=256)
        nrow = PEER_ROWS * N_KEYS
        vt = bf(peer_v[i]).reshape(-1, nrow, d).transpose(0, 2, 1)
        ffn = peer_expert(x1, norm_ffn[i], bf(peer_u[i]), vt, lam, cc, rho, e1, tm=512)

        xf = ple_final(x1, ffn, p[i].reshape(t, -1), norm_ple[i], norm_final,
                       bf(ple_w_gate[i]), bf(ple_w_proj[i]), tm=512)
    return xf.reshape(bsz, seq, d)
```

```python
import functools
import math

import jax
import jax.numpy as jnp
from jax import lax
from jax.experimental import pallas as pl
from jax.experimental.pallas import tpu as pltpu

F32 = jnp.float32
BF16 = jnp.bfloat16
HIGHEST = lax.Precision.HIGHEST

LANES = 128
SUBLANES = 8
VMEM_LIMIT = 56 * 1024 * 1024

EPS = 1e-6
RW_HEADS = 8
RW_HEAD_DIM = 64
RW_DIM = RW_HEADS * RW_HEAD_DIM
RW_GN_EPS = 64e-5
SCAN_CHUNK = 64
SEG_SUM_WIDTH = 256

MLA_HEADS = 8
QK_NOPE = 64
QK_ROPE = 32
V_HEAD = 64
Q_LORA = 384
KV_LORA = 256
ROPE_THETA = 10000.0
MASK_CHUNK = 64
NEG_INF = -1e30

PEER_HEADS = 8
N_KEYS = 128
PEER_TOPK = 16
HALF_Q = 128


def _cparams(*sem):
    return pltpu.CompilerParams(dimension_semantics=sem, vmem_limit_bytes=VMEM_LIMIT)


def _dot(a, b, precision=None):
    return jnp.dot(a, b, preferred_element_type=F32, precision=precision)


def _dot_nt(a, b, precision=None):
    return lax.dot_general(a, b, (((1,), (1,)), ((), ())),
                           preferred_element_type=F32, precision=precision)


def _rms(x, gain):
    return x * lax.rsqrt(jnp.mean(x * x, axis=-1, keepdims=True) + EPS) * gain


def _sigmoid(x):
    return 1.0 / (1.0 + jnp.exp(-x))


def _seg_sum(x, bd):
    hi = x.astype(BF16)
    lo = (x - hi.astype(F32)).astype(BF16)
    w = bd.shape[0]
    parts = [_dot(hi[:, c:c + w], bd) + _dot(lo[:, c:c + w], bd) for c in range(0, x.shape[1], w)]
    return jnp.concatenate(parts, axis=1)


def _norm_matmul_kernel(x_ref, g_ref, w_ref, o_ref, h_ref, *, sigmoid_blocks):
    j = pl.program_id(1)

    @pl.when(j == 0)
    def _():
        h_ref[...] = _rms(x_ref[...], g_ref[...]).astype(BF16)

    def project(act):
        o_ref[...] = act(_dot(h_ref[...], w_ref[...])).astype(o_ref.dtype)

    if sigmoid_blocks is None:
        project(lambda y: y)
    else:
        gated = jnp.logical_and(j >= sigmoid_blocks[0], j < sigmoid_blocks[1])
        pl.when(gated)(lambda: project(_sigmoid))
        pl.when(jnp.logical_not(gated))(lambda: project(lambda y: y))


def norm_matmul(x, gain, w, *, tm, tn, sigmoid_blocks=None, out_dtype=F32):
    t, d = x.shape
    n = w.shape[1]
    return pl.pallas_call(
        functools.partial(_norm_matmul_kernel, sigmoid_blocks=sigmoid_blocks),
        grid=(t // tm, n // tn),
        in_specs=[pl.BlockSpec((tm, d), lambda i, j: (i, 0)),
                  pl.BlockSpec((1, d), lambda i, j: (0, 0)),
                  pl.BlockSpec((d, tn), lambda i, j: (0, j))],
        out_specs=pl.BlockSpec((tm, tn), lambda i, j: (i, j)),
        out_shape=jax.ShapeDtypeStruct((t, n), out_dtype),
        scratch_shapes=[pltpu.VMEM((tm, d), BF16)],
        compiler_params=_cparams("parallel", "arbitrary"),
        name="norm_matmul",
    )(x, gain.reshape(1, d), w)


RW_ZCOLS = 3 * RW_DIM + 4 * LANES


def _rwkv_prep_kernel(z_ref, zp_ref, mu_ref, w0_ref, w2_ref, a0_ref, a2_ref, g2_ref,
                      kk_ref, ka_ref, bd_ref,
                      r_ref, wl_ref, k_ref, v_ref, a_ref, b_ref, g_ref):
    z = z_ref[0]
    tm = z.shape[0]
    prev_last = zp_ref[0][SUBLANES - 1:SUBLANES, :]
    prev_last = jnp.where(pl.program_id(1) == 0, 0.0, prev_last)
    rolled = pltpu.roll(z, 1, 0)
    row = lax.broadcasted_iota(jnp.int32, (tm, 1), 0)
    z_prev = jnp.where(row == 0, prev_last, rolled)
    z = z + mu_ref[...] * (z_prev - z)

    o1, o2, o3 = RW_DIM, 2 * RW_DIM, 3 * RW_DIM
    r, k, v = z[:, :o1], z[:, o1:o2], z[:, o2:o3]
    zw, za, zg = (z[:, o3 + c * LANES:o3 + (c + 1) * LANES] for c in range(3))

    wpre = w0_ref[...] + _dot(jnp.tanh(zw), w2_ref[...])
    nx = -wpre
    softplus = jnp.maximum(nx, 0.0) + jnp.log(1.0 + jnp.exp(-jnp.abs(nx)))
    w = -softplus - 0.5
    iclr = _sigmoid(a0_ref[...] + _dot(za, a2_ref[...]))
    g = _dot(_sigmoid(zg), g2_ref[...])

    kk = k * kk_ref[...]
    ss = _seg_sum(kk * kk, bd_ref[...])
    kk = kk / jnp.maximum(jnp.sqrt(ss), 1e-12)

    r_ref[0] = r
    wl_ref[0] = -jnp.exp(w)
    k_ref[0] = k * (1.0 + (iclr - 1.0) * ka_ref[...])
    v_ref[0] = v
    a_ref[0] = -kk
    b_ref[0] = kk * iclr
    g_ref[0] = g


def rwkv_prep(z, mu, w0, w2, a0, a2, g2, k_k, k_a, bd, *, tm):
    bsz, seq, _ = z.shape
    zc = RW_ZCOLS
    d = RW_DIM
    row = lambda a: a.reshape(1, -1)
    full = lambda a: pl.BlockSpec(a.shape, lambda b, i: (0,) * a.ndim)
    args = (row(mu), row(w0), w2, row(a0), a2, g2, row(k_k), row(k_a), bd)
    out = jax.ShapeDtypeStruct((bsz, seq, d), F32)
    ospec = pl.BlockSpec((1, tm, d), lambda b, i: (b, i, 0))
    return pl.pallas_call(
        _rwkv_prep_kernel,
        grid=(bsz, seq // tm),
        in_specs=[pl.BlockSpec((1, tm, zc), lambda b, i: (b, i, 0)),
                  pl.BlockSpec((1, SUBLANES, zc),
                               lambda b, i: (b, jnp.maximum(i * (tm // SUBLANES) - 1, 0), 0))]
                 + [full(a) for a in args],
        out_specs=[ospec] * 7,
        out_shape=[out] * 7,
        compiler_params=_cparams("parallel", "arbitrary"),
        name="rwkv_prep",
    )(z, z, *args)


def _rwkv_scan_kernel(r_ref, wl_ref, k_ref, v_ref, a_ref, b_ref, y_ref, g_ref):
    c = SCAN_CHUNK
    hd = RW_HEAD_DIM
    npair = g_ref.shape[0]

    @pl.when(pl.program_id(1) == 0)
    def _():
        g_ref[...] = jnp.zeros_like(g_ref)

    ri = lax.broadcasted_iota(jnp.int32, (c, c), 0)
    ci = lax.broadcasted_iota(jnp.int32, (c, c), 1)
    tril = (ri >= ci).astype(F32)
    lane = lax.broadcasted_iota(jnp.int32, (c, 2 * hd), 1)
    m0 = lane < hd
    r2 = lax.broadcasted_iota(jnp.int32, (2 * c, 2 * c), 0)
    c2 = lax.broadcasted_iota(jnp.int32, (2 * c, 2 * c), 1)
    same = (r2 >= c) == (c2 >= c)
    strict = jnp.logical_and(same, r2 > c2)
    incl = jnp.logical_and(same, r2 >= c2)
    eye = (r2 == c2).astype(F32)

    bf = lambda x: x.astype(BF16)
    stack = lambda x: bf(jnp.concatenate([jnp.where(m0, x, 0.0), jnp.where(m0, 0.0, x)], axis=0))
    twice = lambda x: bf(jnp.concatenate([x, x], axis=0))
    pick = lambda s: jnp.where(m0, s[:c], s[c:])

    pairs = range(npair)
    each = lambda f, *cols: [f(*args) for args in zip(*cols)]
    per_row = r_ref.shape[2] // LANES
    where = [(hp // per_row, slice((hp % per_row) * LANES, (hp % per_row + 1) * LANES)) for hp in pairs]
    load = lambda ref: [ref[bi, :, sl] for bi, sl in where]
    r, wl, k, v, a, b = (load(x) for x in (r_ref, wl_ref, k_ref, v_ref, a_ref, b_ref))

    cs = each(lambda w: _dot(tril, w, HIGHEST), wl)
    cs_last = each(lambda s: s[c - 1:c, :], cs)
    p_inv = each(lambda s: jnp.exp(-s), cs)
    at = each(lambda a, s, w: a * jnp.exp(s - w), a, cs, wl)
    rt = each(lambda r, s: r * jnp.exp(s), r, cs)
    at_s, rt_s = each(stack, at), each(stack, rt)
    bt_s = each(lambda b, p: stack(b * p), b, p_inv)
    kt_s = each(lambda k, p: stack(k * p), k, p_inv)

    ab = each(lambda x, y: jnp.where(strict, _dot_nt(x, y), 0.0), at_s, bt_s)
    ak = each(lambda x, y: jnp.where(strict, _dot_nt(x, y), 0.0), at_s, kt_s)
    rb = each(lambda x, y: jnp.where(incl, _dot_nt(x, y), 0.0), rt_s, bt_s)
    rk = each(lambda x, y: jnp.where(incl, _dot_nt(x, y), 0.0), rt_s, kt_s)

    tinv = each(lambda m: eye + m, ab)
    x = each(bf, ab)
    for _ in range(int(math.log2(c)) - 1):
        x = each(lambda m: bf(_dot(m, m)), x)
        tinv = each(lambda t, m: t + _dot(bf(t), m), tinv, x)

    gt = [g_ref[hp] for hp in pairs]
    gtb = each(bf, gt)
    vv = each(twice, v)
    rhs = each(lambda at, g, ak, vv: _dot_nt(bf(at), g) + pick(_dot(bf(ak), vv)), at, gtb, ak, vv)
    u = each(lambda t, x: pick(_dot(bf(t), twice(x))), tinv, rhs)
    y = each(lambda rt, g, rb, u, rk, vv:
             _dot_nt(bf(rt), g) + pick(_dot(bf(rb), twice(u)) + _dot(bf(rk), vv)),
             rt, gtb, rb, u, rk, vv)
    upd = each(lambda u, b, v, k, s, sl:
               _dot(bf(u.T), bf(b * jnp.exp(sl - s))) + _dot(bf(v.T), bf(k * jnp.exp(sl - s))),
               u, b, v, k, cs, cs_last)
    for hp in pairs:
        bi, sl = where[hp]
        y_ref[bi, :, sl] = y[hp]
        g_ref[hp] = gt[hp] * jnp.exp(cs_last[hp]) + jnp.where(same, upd[hp], 0.0)


def rwkv_scan(r, wl, k, v, a, b, *, rows):
    bsz, seq, d = r.shape
    c = SCAN_CHUNK
    spec = pl.BlockSpec((rows, c, d), lambda bi, ci: (bi, ci, 0))
    return pl.pallas_call(
        _rwkv_scan_kernel,
        grid=(bsz // rows, seq // c),
        in_specs=[spec] * 6,
        out_specs=spec,
        out_shape=jax.ShapeDtypeStruct((bsz, seq, d), F32),
        scratch_shapes=[pltpu.VMEM((rows * d // LANES, LANES, LANES), F32)],
        compiler_params=_cparams("parallel", "arbitrary"),
        name="rwkv_scan",
    )(r, wl, k, v, a, b)


MLA_ZCOLS = 1024
MLA_SCALE = math.log2(math.e) / math.sqrt(QK_NOPE + QK_ROPE)
ATTN_TILE = 256


def _mla_prep_kernel(z_ref, pos_ref, fl_ref, qn_ref, kvn_ref, wq_ref, wqs_ref, wk_ref, wv_ref,
                     q_ref, k_ref, v_ref):
    z = z_ref[...]
    c_q = _rms(z[:, :Q_LORA], qn_ref[...]).astype(BF16)
    c_kv = _rms(z[:, Q_LORA:Q_LORA + KV_LORA], kvn_ref[...]).astype(BF16)
    kr = z[:, Q_LORA + KV_LORA:Q_LORA + KV_LORA + LANES]
    krs = z[:, Q_LORA + KV_LORA + LANES:Q_LORA + KV_LORA + 2 * LANES]

    ang = pos_ref[...].astype(F32) * fl_ref[...]
    cos, sin = jnp.cos(ang), jnp.sin(ang)
    kr_rot = kr * cos + krs * sin

    q = _dot(c_q, wq_ref[...])
    qs = _dot(c_q, wqs_ref[...])
    kn = _dot(c_kv, wk_ref[...])
    v = _dot(c_kv, wv_ref[...])
    for c in range(v_ref.shape[0]):
        v_ref[c] = v[c * ATTN_TILE:(c + 1) * ATTN_TILE, :].T.astype(BF16)
    for h in range(MLA_HEADS):
        sl = slice(h * LANES, (h + 1) * LANES)
        q_ref[:, sl] = ((q[:, sl] * cos + qs[:, sl] * sin) * MLA_SCALE).astype(BF16)
        k_ref[:, sl] = (kn[:, sl] + kr_rot).astype(BF16)


def mla_prep(z, zblock, pos, f_lane, q_norm, kv_norm, wq, wqs, wk, wv, *, tm):
    t = z.shape[0]
    n = MLA_HEADS * LANES
    full = lambda a: pl.BlockSpec(a.shape, lambda i: (0,) * a.ndim)
    args = (f_lane, q_norm.reshape(1, -1), kv_norm.reshape(1, -1), wq, wqs, wk, wv)
    out = jax.ShapeDtypeStruct((t, n), BF16)
    nv = wv.shape[1]
    out_vt = jax.ShapeDtypeStruct((t // ATTN_TILE, nv, ATTN_TILE), BF16)
    ospec = pl.BlockSpec((tm, n), lambda i: (i, 0))
    vspec = pl.BlockSpec((tm // ATTN_TILE, nv, ATTN_TILE), lambda i: (i, 0, 0))
    return pl.pallas_call(
        _mla_prep_kernel,
        grid=(t // tm,),
        in_specs=[pl.BlockSpec((tm, MLA_ZCOLS), lambda i: (i, zblock)),
                  pl.BlockSpec((tm, 1), lambda i: (i, 0))] + [full(a) for a in args],
        out_specs=[ospec, ospec, vspec],
        out_shape=[out, out, out_vt],
        compiler_params=_cparams("parallel"),
        name="mla_prep",
    )(z, pos, *args)


ATTN_HEADS_PER_STEP = 8


def _attn_kernel(q_ref, k_ref, vt_ref, o_ref):
    tq = ATTN_TILE
    iq = pl.program_id(2)
    nh = q_ref.shape[2] // LANES
    heads = [slice(h * LANES, (h + 1) * LANES) for h in range(nh)]
    qs = [q_ref[0, :, sl] for sl in heads]

    def tile(j, carry, masked):
        start = pl.multiple_of(j * tq, tq)
        if masked:
            kc = lax.broadcasted_iota(jnp.int32, (tq, tq), 0) // MASK_CHUNK
            qc = lax.broadcasted_iota(jnp.int32, (tq, tq), 1) // MASK_CHUNK
            keep = kc <= qc
        hs = range(nh)
        s = [_dot_nt(k_ref[0, pl.ds(start, tq), heads[h]], qs[h]) for h in hs]
        if masked:
            s = [jnp.where(keep, x, NEG_INF) for x in s]
        m_new = [jnp.maximum(carry[h][0], jnp.max(s[h], axis=0, keepdims=True)) for h in hs]
        alpha = [jnp.exp2(carry[h][0] - m_new[h]) for h in hs]
        p = [jnp.exp2(s[h] - m_new[h]) for h in hs]
        l = [alpha[h] * carry[h][1] + jnp.sum(p[h], axis=0, keepdims=True) for h in hs]
        pv = [_dot(vt_ref[0, j, h * V_HEAD:(h + 1) * V_HEAD, :], p[h].astype(BF16))
              for h in hs]
        return tuple((m_new[h], l[h], alpha[h] * carry[h][2] + pv[h]) for h in hs)

    init = tuple((jnp.full((1, tq), NEG_INF, F32), jnp.zeros((1, tq), F32), jnp.zeros((V_HEAD, tq), F32))
                 for _ in heads)
    carry = lax.fori_loop(0, iq, lambda j, c: tile(j, c, False), init)
    carry = tile(iq, carry, True)
    out = [acc / l for _, l, acc in carry]
    for pair in range(nh // 2):
        both = jnp.concatenate(out[2 * pair:2 * pair + 2], axis=0)
        o_ref[0, :, pair * LANES:(pair + 1) * LANES] = both.T.astype(BF16)


def attention(q, k, vt):
    bsz, seq, n = q.shape
    tq = ATTN_TILE
    w = ATTN_HEADS_PER_STEP * LANES
    wv = ATTN_HEADS_PER_STEP * V_HEAD
    qspec = pl.BlockSpec((1, tq, w), lambda b, h, i: (b, i, h))
    kspec = pl.BlockSpec((1, seq, w), lambda b, h, i: (b, 0, h))
    vspec = pl.BlockSpec((1, seq // tq, wv, tq), lambda b, h, i: (b, 0, h, 0))
    return pl.pallas_call(
        _attn_kernel,
        grid=(bsz, n // w, seq // tq),
        in_specs=[qspec, kspec, vspec],
        out_specs=pl.BlockSpec((1, tq, wv), lambda b, h, i: (b, i, h)),
        out_shape=jax.ShapeDtypeStruct((bsz, seq, vt.shape[2]), BF16),
        compiler_params=_cparams("parallel", "parallel", "arbitrary"),
        name="attention",
    )(q, k, vt)


def _combine_kernel(x_ref, y_ref, r_ref, k_ref, v_ref, g_ref, o_ref, gates_ref,
                    bd_ref, rk_ref, gnw_ref, gnb_ref, wa_ref, wb_ref, wo_ref, gq_ref, wq_ref,
                    out_ref, qp_ref):
    bd = bd_ref[...]
    inv_n = 1.0 / RW_HEAD_DIM
    y = y_ref[...]
    v = v_ref[...]
    mean = _seg_sum(y, bd) * inv_n
    yc = y - mean
    var = _seg_sum(yc * yc, bd) * inv_n
    yn = yc * lax.rsqrt(var + RW_GN_EPS) * gnw_ref[...] + gnb_ref[...]
    bonus = _seg_sum(r_ref[...] * k_ref[...] * rk_ref[...], bd) * v
    ya = _dot(((yn + bonus) * g_ref[...]).astype(BF16), wa_ref[...])
    yb = _dot(o_ref[...], wb_ref[...])
    d = ya.shape[1]
    gates = gates_ref[...]
    mix = gates[:, :d] * ya + gates[:, d:] * yb
    x1 = x_ref[...] + _dot(mix.astype(BF16), wo_ref[...])
    out_ref[...] = x1
    qp_ref[...] = _dot(_rms(x1, gq_ref[...]).astype(BF16), wq_ref[...]).astype(BF16)


def combine(x, y, r, k, v, g, o, z, gate_block, bd, r_k, gn_w, gn_b, wa, wb, wo, g_ffn, w_q, *, tm):
    t, d = x.shape
    row = lambda a: a.reshape(1, -1)
    tok = lambda a: pl.BlockSpec((tm, a.shape[1]), lambda i: (i, 0))
    full = lambda a: pl.BlockSpec(a.shape, lambda i: (0,) * a.ndim)
    toks = (x, y, r, k, v, g, o, z)
    consts = (bd, row(r_k), row(gn_w), row(gn_b), wa, wb, wo, row(g_ffn), w_q)
    gate_spec = pl.BlockSpec((tm, 2 * d), lambda i: (i, gate_block))
    nq = w_q.shape[1]
    return pl.pallas_call(
        _combine_kernel,
        grid=(t // tm,),
        in_specs=[tok(a) for a in toks[:-1]] + [gate_spec] + [full(a) for a in consts],
        out_specs=[pl.BlockSpec((tm, d), lambda i: (i, 0)), pl.BlockSpec((tm, nq), lambda i: (i, 0))],
        out_shape=[jax.ShapeDtypeStruct((t, d), F32), jax.ShapeDtypeStruct((t, nq), BF16)],
        compiler_params=_cparams("parallel"),
        name="combine",
    )(*toks, *consts)


NOT_RANKED = 127.0
RANK_BLOCKS_PER_LOOP = 2
STAIR = tuple(PEER_TOPK // (ii + 1) for ii in range(PEER_TOPK))


def _top_ranks(s, k, exact_ties):
    n, t = s.shape
    key = lax.broadcasted_iota(jnp.int32, (n, LANES), 0)
    slot = lax.broadcasted_iota(jnp.int32, (k, LANES), 0)

    def one(r, s, rank, top):
        m = jnp.max(s, axis=0, keepdims=True)
        hit = s == m
        if exact_ties:
            hit = key == jnp.min(jnp.where(hit, key, n), axis=0, keepdims=True)
        rank = jnp.where(hit, jnp.asarray(r, F32), rank)
        s = jnp.where(hit, -jnp.inf, s)
        top = jnp.where(slot == r, m, top)
        return s, rank, top

    def body(r, carry):
        return tuple(one(r, *c) for c in carry)

    blocks = [s[:, c * LANES:(c + 1) * LANES] for c in range(t // LANES)]
    tops, ranks = [], []
    for g in range(0, len(blocks), RANK_BLOCKS_PER_LOOP):
        init = tuple((b, jnp.full((n, LANES), NOT_RANKED, F32), jnp.zeros((k, LANES), F32))
                     for b in blocks[g:g + RANK_BLOCKS_PER_LOOP])
        for _, rank, top in lax.fori_loop(0, k, body, init):
            tops.append(top)
            ranks.append(rank)
    return jnp.concatenate(tops, axis=1), jnp.concatenate(ranks, axis=1)


def _ranked_excess(rank, k):
    count = jnp.sum((rank < NOT_RANKED).astype(F32), axis=0, keepdims=True)
    return jnp.abs(count - k)


def _peer_route_kernel(q_ref, keys_ref, lam_ref, cc_ref, rho_ref, e1_ref):
    tm = q_ref.shape[0]
    k = PEER_TOPK
    neg = -jnp.inf

    def head(h, _):
        col = pl.multiple_of(h * 2 * HALF_Q, 2 * HALF_Q)
        s0 = _dot_nt(keys_ref[h, 0], q_ref[:, pl.ds(col, HALF_Q)])
        s1 = _dot_nt(keys_ref[h, 1], q_ref[:, pl.ds(col + HALF_Q, HALF_Q)])
        def rank_all(exact_ties):
            top0, rank0 = _top_ranks(s0, k, exact_ties)
            top1, rank1 = _top_ranks(s1, k, exact_ties)
            row8 = lax.broadcasted_iota(jnp.int32, (SUBLANES, tm), 0)
            groups = [top0[0:1] + top1[0:8], top0[0:1] + top1[8:16], top0[1:2] + top1[0:8]]
            for ii in range(2, 8):
                groups.append(jnp.where(row8 < STAIR[ii], top0[ii:ii + 1] + top1[0:8], neg))
            groups.append(top0[8:16] + top1[0:1])
            cand = jnp.concatenate(groups, axis=0)
            _, crank = _top_ranks(cand, k, exact_ties)
            return top0, rank0, top1, rank1, cand, crank

        quick = rank_all(False)
        excess = _ranked_excess(quick[1], k) + _ranked_excess(quick[3], k) + _ranked_excess(quick[5], k)
        top0, rank0, top1, rank1, cand, crank = lax.cond(
            jnp.max(excess) == 0.0, lambda: quick, lambda: rank_all(True))
        sel = crank < NOT_RANKED
        ex = jnp.where(sel, jnp.exp(cand - cand[0:1]), 0.0)
        z = jnp.sum(ex, axis=0, keepdims=True)
        self = sel.astype(F32)
        counts = [jnp.sum(self[0:16], axis=0, keepdims=True)]
        for g in range(2, 9):
            counts.append(jnp.sum(self[8 * g:8 * g + 8], axis=0, keepdims=True))
        lvec = jnp.concatenate(counts + [self[72:80]], axis=0)

        lam = jnp.zeros((N_KEYS, tm), F32)
        for ii in range(k):
            lam = jnp.where(rank0 == float(ii), lvec[ii:ii + 1], lam)
        lam_ref[h] = lam.astype(BF16)
        cc_ref[h] = (jnp.exp(s0 - top0[0:1]) * (0.5 / z)).astype(BF16)
        rho_ref[h] = rank1.astype(BF16)
        e1_ref[h] = jnp.exp(s1 - top1[0:1]).astype(BF16)
        return 0

    lax.fori_loop(0, PEER_HEADS, head, 0)


def peer_route(qp, keys, *, tm):
    t = qp.shape[0]
    out_b = jax.ShapeDtypeStruct((PEER_HEADS, N_KEYS, t), BF16)
    ospec = pl.BlockSpec((PEER_HEADS, N_KEYS, tm), lambda i: (0, 0, i))
    return pl.pallas_call(
        _peer_route_kernel,
        grid=(t // tm,),
        in_specs=[pl.BlockSpec((tm, qp.shape[1]), lambda i: (i, 0)),
                  pl.BlockSpec(keys.shape, lambda i: (0, 0, 0, 0))],
        out_specs=[ospec] * 4,
        out_shape=[out_b] * 4,
        compiler_params=_cparams("parallel"),
        name="peer_route",
    )(qp, keys)


PEER_ROWS = 16
PEER_TOKEN_CHUNK = 256


def _erf(x):
    return lax.erf(x)


def _gelu_twice(x):
    return x * (1.0 + _erf(x * (1.0 / math.sqrt(2.0))))


def _peer_expert_kernel(x_ref, gn_ref, u_ref, vt_ref, lam_ref, cc_ref, rho_ref, e1_ref, out_ref,
                        xt_ref, acc_ref, pre0_ref, pre1_ref):
    j = pl.program_id(1)
    last = pl.num_programs(1) - 1
    tm = xt_ref.shape[1]
    chunks = [slice(c, c + PEER_TOKEN_CHUNK) for c in range(0, tm, PEER_TOKEN_CHUNK)]

    @pl.when(j == 0)
    def _():
        xt_ref[...] = _rms(x_ref[...], gn_ref[...]).T.astype(BF16)
        acc_ref[...] = jnp.zeros_like(acc_ref)

    def step(fill_ref, drain_ref):
        for cols in chunks if fill_ref is not None else ():
            fill_ref[:, cols] = _dot(u_ref[...], xt_ref[:, cols])
        for cols in chunks if drain_ref is not None else ():
            gs = []
            for ii in range(PEER_ROWS):
                rows = slice(ii * N_KEYS, (ii + 1) * N_KEYS)
                terms = []
                for h in range(PEER_HEADS):
                    lam = lam_ref[h, ii:ii + 1, cols]
                    cc = cc_ref[h, ii:ii + 1, cols]
                    terms.append(jnp.where(rho_ref[h, :, cols] < lam, e1_ref[h, :, cols] * cc,
                                           jnp.zeros((), BF16)))
                gs.append(_gelu_twice(drain_ref[rows, cols]).astype(BF16) * sum(terms[1:], terms[0]))
            acc_ref[:, cols] += _dot(vt_ref[0], jnp.concatenate(gs, axis=0))

    even = j % 2 == 0
    inner = jnp.logical_and(j > 0, j < last)
    pl.when(j == 0)(lambda: step(pre0_ref, None))
    pl.when(jnp.logical_and(inner, even))(lambda: step(pre0_ref, pre1_ref))
    pl.when(jnp.logical_and(inner, jnp.logical_not(even)))(lambda: step(pre1_ref, pre0_ref))
    pl.when(jnp.logical_and(j == last, even))(lambda: step(None, pre1_ref))
    pl.when(jnp.logical_and(j == last, jnp.logical_not(even)))(lambda: step(None, pre0_ref))

    @pl.when(j == last)
    def _():
        out_ref[...] = acc_ref[...].T


def peer_expert(x, gain, u, vt, lam, cc, rho, e1, *, tm):
    t, d = x.shape
    nrow = PEER_ROWS * N_KEYS
    nblk = u.shape[0] // nrow
    stage = lambda lag: (lambda j: jnp.clip(j - lag, 0, nblk - 1))
    rspec = pl.BlockSpec((PEER_HEADS, PEER_ROWS, tm), lambda i, j: (0, stage(1)(j), i))
    cspec = pl.BlockSpec((PEER_HEADS, N_KEYS, tm), lambda i, j: (0, 0, i))
    return pl.pallas_call(
        _peer_expert_kernel,
        grid=(t // tm, nblk + 1),
        in_specs=[pl.BlockSpec((tm, d), lambda i, j: (i, 0)),
                  pl.BlockSpec((1, d), lambda i, j: (0, 0)),
                  pl.BlockSpec((nrow, d), lambda i, j: (stage(0)(j), 0)),
                  pl.BlockSpec((1, d, nrow), lambda i, j: (stage(1)(j), 0, 0)),
                  rspec, rspec, cspec, cspec],
        out_specs=pl.BlockSpec((tm, d), lambda i, j: (i, 0)),
        out_shape=jax.ShapeDtypeStruct((t, d), F32),
        scratch_shapes=[pltpu.VMEM((d, tm), BF16), pltpu.VMEM((d, tm), F32),
                        pltpu.VMEM((nrow, tm), F32), pltpu.VMEM((nrow, tm), F32)],
        compiler_params=_cparams("parallel", "arbitrary"),
        name="peer_expert",
    )(x, gain.reshape(1, d), u, vt, lam, cc, rho, e1)


def _ple_final_kernel(x_ref, f_ref, p_ref, gp_ref, gf_ref, wg_ref, wp_ref, out_ref):
    x = x_ref[...] + f_ref[...]
    gate = _sigmoid(_dot(_rms(x, gp_ref[...]).astype(BF16), wg_ref[...]))
    x = x + gate * _dot(p_ref[...].astype(BF16), wp_ref[...])
    out_ref[...] = _rms(x, gf_ref[...])


def ple_final(x, f, p, g_ple, g_final, wg, wp, *, tm):
    t, d = x.shape
    tok = lambda a: pl.BlockSpec((tm, a.shape[1]), lambda i: (i, 0))
    full = lambda a: pl.BlockSpec(a.shape, lambda i: (0,) * a.ndim)
    consts = (g_ple.reshape(1, d), g_final.reshape(1, d), wg, wp)
    return pl.pallas_call(
        _ple_final_kernel,
        grid=(t // tm,),
        in_specs=[tok(x), tok(f), tok(p)] + [full(a) for a in consts],
        out_specs=tok(x),
        out_shape=jax.ShapeDtypeStruct((t, d), F32),
        compiler_params=_cparams("parallel"),
        name="ple_final",
    )(x, f, p, *consts)


def _place(cols, width, offset):
    return jnp.pad(cols, ((0, 0), (offset, width - offset - cols.shape[1])))


def _rw_in_weights(w_rw, mu):
    o3 = 3 * RW_DIM
    lw, la = 64, 64
    segs = [w_rw[:, :o3], _place(w_rw[:, o3:o3 + lw], LANES, 0),
            _place(w_rw[:, o3 + lw:o3 + lw + la], LANES, 0), _place(w_rw[:, o3 + lw + la:], 2 * LANES, 0)]
    mus = [mu[None, :o3], _place(mu[None, o3:o3 + lw], LANES, 0),
           _place(mu[None, o3 + lw:o3 + lw + la], LANES, 0), _place(mu[None, o3 + lw + la:], 2 * LANES, 0)]
    return jnp.concatenate(segs, axis=1), jnp.concatenate(mus, axis=1)[0]


def _mla_in_weights(w_mla):
    half = QK_ROPE // 2
    lat = Q_LORA + KV_LORA
    kr = w_mla[:, lat:]
    kr_sw = jnp.concatenate([-kr[:, half:], kr[:, :half]], axis=1)
    return jnp.concatenate([w_mla[:, :lat], _place(kr, LANES, QK_NOPE), _place(kr_sw, LANES, QK_NOPE),
                            jnp.zeros((w_mla.shape[0], LANES), w_mla.dtype)], axis=1)


def _mla_up_weights(w_uq, w_ukv):
    half = QK_ROPE // 2
    qd = QK_NOPE + QK_ROPE
    wq = w_uq.reshape(Q_LORA, MLA_HEADS, qd)
    rope = wq[:, :, QK_NOPE:]
    rope_sw = jnp.concatenate([-rope[:, :, half:], rope[:, :, :half]], axis=2)
    pad = lambda t, off: jnp.pad(t, ((0, 0), (0, 0), (off, LANES - off - t.shape[2])))
    wq_pad = pad(wq, 0).reshape(Q_LORA, MLA_HEADS * LANES)
    wq_sw = pad(rope_sw, QK_NOPE).reshape(Q_LORA, MLA_HEADS * LANES)
    lane = jnp.arange(MLA_HEADS * LANES) % LANES
    wk = jnp.where(lane < QK_NOPE, w_ukv, 0.0)
    wv = w_ukv.reshape(KV_LORA, MLA_HEADS, LANES)[:, :, QK_NOPE:].reshape(KV_LORA, MLA_HEADS * V_HEAD)
    return wq_pad, wq_sw, wk, wv


def kernel(x, p, positions, norm_mix, w_in, rw_mu, rw_w0, rw_w2, rw_a0, rw_a2, rw_g2, rw_k_k, rw_k_a, rw_r_k, rw_gn_w, rw_gn_b, rw_w_o, mla_q_norm, mla_w_uq, mla_kv_norm, mla_w_ukv, mla_w_o, w_out, norm_ffn, peer_w_q, peer_sub_keys, peer_u, peer_v, norm_ple, ple_w_gate, ple_w_proj, norm_final):
    bsz, seq, d = x.shape
    t = bsz * seq
    depth = p.shape[0]
    bf = lambda a: a.astype(BF16)
    rw_cols = 3 * RW_DIM + 64 + 64 + 128
    mla_cols = Q_LORA + KV_LORA + QK_ROPE

    head_of = jnp.arange(SEG_SUM_WIDTH) // RW_HEAD_DIM
    bd = bf(head_of[:, None] == head_of[None, :])
    inv_freq = ROPE_THETA ** (-jnp.arange(0, QK_ROPE, 2, dtype=F32) / QK_ROPE)
    f_lane = _place(jnp.concatenate([inv_freq, inv_freq])[None, :], LANES, QK_NOPE)
    pos = positions.reshape(t, 1)

    xf = x.reshape(t, d)
    assert depth == 1, "the final RMSNorm is fused into the layer's last kernel"
    for i in range(depth):
        w_rw, mu = _rw_in_weights(w_in[i][:, :rw_cols], rw_mu[i])
        w_mla = _mla_in_weights(w_in[i][:, rw_cols:rw_cols + mla_cols])
        w_gates = w_in[i][:, rw_cols + mla_cols:]
        tn = MLA_ZCOLS
        w_all = jnp.concatenate([w_rw, w_gates, w_mla], axis=1)
        gate_lo = RW_ZCOLS // tn
        gate_hi = gate_lo + w_gates.shape[1] // tn
        z = norm_matmul(xf, norm_mix[i], bf(w_all), tm=1024, tn=tn, sigmoid_blocks=(gate_lo, gate_hi))

        pad_rows = lambda w: jnp.pad(w, ((0, LANES - w.shape[0]), (0, 0)))
        r, wl, k, v, a, b, g = rwkv_prep(
            z.reshape(bsz, seq, -1), mu, rw_w0[i], pad_rows(rw_w2[i]), rw_a0[i],
            pad_rows(rw_a2[i]), rw_g2[i], rw_k_k[i], rw_k_a[i], bd, tm=512)
        y = rwkv_scan(r, wl, k, v, a, b, rows=math.gcd(bsz, 4))

        wq, wqs, wk, wv = _mla_up_weights(mla_w_uq[i], mla_w_ukv[i])
        q, kk, vv = mla_prep(z, gate_hi, pos, f_lane, mla_q_norm[i], mla_kv_norm[i],
                             bf(wq), bf(wqs), bf(wk), bf(wv), tm=512)
        n = MLA_HEADS * LANES
        nv = MLA_HEADS * V_HEAD
        o = attention(q.reshape(bsz, seq, n), kk.reshape(bsz, seq, n),
                      vv.reshape(bsz, seq // ATTN_TILE, nv, ATTN_TILE))

        flat = lambda a: a.reshape(t, -1)
        x1, qp = combine(xf, flat(y), flat(r), flat(k), flat(v), flat(g), flat(o), z, gate_lo * tn // (2 * d),
                         bd, rw_r_k[i].reshape(-1), rw_gn_w[i], rw_gn_b[i],
                         bf(rw_w_o[i]), bf(mla_w_o[i]), bf(w_out[i]), norm_ffn[i], bf(peer_w_q[i]), tm=512)
        lam, cc, rho, e1 = peer_route(qp, bf(peer_sub_keys[i]), tm=256)
        nrow = PEER_ROWS * N_KEYS
        vt = bf(peer_v[i]).reshape(-1, nrow, d).transpose(0, 2, 1)
        ffn = peer_expert(x1, norm_ffn[i], bf(peer_u[i]), vt, lam, cc, rho, e1, tm=512)

        xf = ple_final(x1, ffn, p[i].reshape(t, -1), norm_ple[i], norm_final,
                       bf(ple_w_gate[i]), bf(ple_w_proj[i]), tm=512)
    return xf.reshape(bsz, seq, d)
```
